```python
import math
import jax, jax.numpy as jnp
from jax import lax
import numpy as np

D_MODEL = 1024
BATCH = 16
SEQ = 4096
DEPTH = 2
DEC_BATCH = 32
DEC_SEQ = 16
PAST_LEN = 1024

CHUNK = 64
N_EVEN = (DEPTH + 1) // 2
N_ODD = DEPTH // 2
EPS = 1e-6
ROPE_THETA = 500000.0
ROT_FRACTION = 4
A_HEADS = 8
A_KV_HEADS = 2
HEAD_DIM = 64
IDX_HEADS = 4
IDX_DIM = 64
TOPK_MAX = 256
Q_BLOCK = 128
B_HEADS = 4
B_QK_DIM = 64
B_V_DIM = 128
C_HEADS = 4
C_DIM = 128
CONV_W = 4
D_HEADS = 4
D_EXPAND = 128
D_V_DIM = 128
D_FF = 3584
N_EXPERTS = 8
TOP_K_EXPERTS = 2

EVEN_SPLITS = (A_HEADS * HEAD_DIM, A_KV_HEADS * HEAD_DIM, A_KV_HEADS * HEAD_DIM,
               IDX_HEADS * IDX_DIM, IDX_DIM, IDX_HEADS,
               B_HEADS * B_QK_DIM, B_HEADS * B_QK_DIM, B_HEADS * B_V_DIM,
               B_HEADS, B_HEADS, B_HEADS * B_V_DIM)
EVEN_TOTAL = sum(EVEN_SPLITS)
ODD_SPLITS = (3 * C_HEADS * C_DIM, C_HEADS, C_HEADS, C_HEADS * C_DIM,
              D_HEADS * D_EXPAND, D_HEADS * D_EXPAND, D_HEADS * D_V_DIM, D_HEADS * D_V_DIM)
ODD_TOTAL = sum(ODD_SPLITS)

kernel_name = 'hybrid_stream_dsa_mlstm_gdn_hgrn2_step'


def rms_norm(x, gain):
    xf = x.astype(jnp.float32)
    y = xf * lax.rsqrt(jnp.mean(xf * xf, axis=-1, keepdims=True) + EPS) * gain.astype(jnp.float32)
    return y.astype(x.dtype)


def l2_norm(x):
    return x * lax.rsqrt(jnp.sum(x * x, axis=-1, keepdims=True) + EPS)


def split_cols(p, widths):
    cuts = [int(c) for c in np.cumsum(widths)[:-1]]
    return jnp.split(p, cuts, axis=-1)


def rope_partial(x, pos):
    rot = x.shape[-1] // ROT_FRACTION
    half = rot // 2
    inv_freq = ROPE_THETA ** (-jnp.arange(half, dtype=jnp.float32) * 2.0 / rot)
    ang = pos.astype(jnp.float32)[:, None] * inv_freq[None, :]
    cos = jnp.cos(ang)[:, None, :]
    sin = jnp.sin(ang)[:, None, :]
    xf = x.astype(jnp.float32)
    x1 = xf[..., :half]
    x2 = xf[..., half:rot]
    out = jnp.concatenate([x1 * cos - x2 * sin, x1 * sin + x2 * cos, xf[..., rot:]], axis=-1)
    return out.astype(x.dtype)


def time_chunks(a, chunk):
    b, t = a.shape[:2]
    return jnp.moveaxis(a.reshape(b, t // chunk, chunk, *a.shape[2:]), 1, 0)


def from_time_chunks(a):
    nc, b, l = a.shape[:3]
    return jnp.moveaxis(a, 0, 1).reshape(b, nc * l, *a.shape[3:])


def dsa_attend(q, qi, wi, limit, k, v, ki, n_sel):
    bsz, tq = q.shape[:2]
    idx_logits = jnp.einsum('btjd,bsd->btjs', qi, ki).astype(jnp.float32)
    score = jnp.einsum('btj,btjs->bts', wi.astype(jnp.float32), jax.nn.relu(idx_logits))
    visible = jnp.arange(k.shape[1])[None, :] < limit[:, None]
    score = jnp.where(visible[None], score, -jnp.inf)
    _, sel = lax.top_k(score, n_sel)
    valid = sel < limit[None, :, None]
    b_idx = jnp.arange(bsz)[:, None, None]
    k_sel = k[b_idx, sel]
    v_sel = v[b_idx, sel]
    qg = q.reshape(bsz, tq, A_KV_HEADS, A_HEADS // A_KV_HEADS, HEAD_DIM)
    logits = jnp.einsum('btkgd,btnkd->btkgn', qg, k_sel).astype(jnp.float32) * HEAD_DIM ** -0.5
    logits = jnp.where(valid[:, :, None, None, :], logits, -jnp.inf)
    probs = jax.nn.softmax(logits, axis=-1).astype(v.dtype)
    out = jnp.einsum('btkgn,btnkd->btkgd', probs, v_sel)
    return out.reshape(bsz, tq, A_HEADS * HEAD_DIM)


def dsa_prompt(q, qi, wi, k, v, ki, pos):
    bsz, t = q.shape[:2]
    nb = t // Q_BLOCK
    n_sel = min(TOPK_MAX, t // 4)
    limit = (pos // CHUNK + 1) * CHUNK

    def blocks(a):
        return jnp.moveaxis(a.reshape(bsz, nb, Q_BLOCK, *a.shape[2:]), 1, 0)

    out = lax.map(lambda xs: dsa_attend(xs[0], xs[1], xs[2], xs[3], k, v, ki, n_sel),
                  (blocks(q), blocks(qi), blocks(wi), limit.reshape(nb, Q_BLOCK)))
    return jnp.moveaxis(out, 0, 1).reshape(bsz, t, A_HEADS * HEAD_DIM)


def mlstm_chunked(q, k, v, ig, lf, c0, n0, m0, chunk):
    causal = jnp.tril(jnp.ones((chunk, chunk), bool))

    def step(carry, xs):
        c, n, m = carry
        qc, kc, vc, ic, fc = xs
        b = jnp.swapaxes(jnp.cumsum(fc, axis=1), 1, 2)
        ih = jnp.swapaxes(ic, 1, 2)
        dmat = jnp.where(causal, b[..., :, None] - b[..., None, :] + ih[..., None, :], -jnp.inf)
        inter = b + m[..., None]
        mrow = jnp.maximum(inter, dmat.max(-1))
        w_intra = jnp.exp(dmat - mrow[..., None])
        w_state = jnp.exp(inter - mrow)
        scores = jnp.einsum('blhd,bshd->bhls', qc, kc) * w_intra
        num = (jnp.einsum('bhls,bshv->bhlv', scores, vc)
               + w_state[..., None] * jnp.einsum('blhd,bhdv->bhlv', qc, c))
        den = scores.sum(-1) + w_state * jnp.einsum('blhd,bhd->bhl', qc, n)
        h = num / jnp.maximum(jnp.abs(den), jnp.exp(-mrow))[..., None]
        b_end = b[..., -1]
        g = b_end[..., None] - b + ih
        m_new = jnp.maximum(b_end + m, g.max(-1))
        w_k = jnp.exp(g - m_new[..., None])
        keep = jnp.exp(b_end + m - m_new)
        c_new = keep[..., None, None] * c + jnp.einsum('bhs,bshd,bshv->bhdv', w_k, kc, vc)
        n_new = keep[..., None] * n + jnp.einsum('bhs,bshd->bhd', w_k, kc)
        return (c_new, n_new, m_new), jnp.swapaxes(h, 1, 2)

    xs = (time_chunks(q, chunk), time_chunks(k, chunk), time_chunks(v, chunk),
          time_chunks(ig, chunk), time_chunks(lf, chunk))
    (c, n, m), hs = lax.scan(step, (c0, n0, m0), xs)
    return from_time_chunks(hs), c, n, m


def gdn_chunked(q, k, v, g, beta, s0, chunk):
    qs, ks, vs, gs, bs = (jnp.swapaxes(time_chunks(a, chunk), 2, 3) for a in (q, k, v, g, beta))
    strict = jnp.tril(jnp.ones((chunk, chunk), bool), -1)
    incl = jnp.tril(jnp.ones((chunk, chunk), bool))
    gc = jnp.cumsum(gs, axis=-1)
    diff = gc[..., :, None] - gc[..., None, :]
    kk = jnp.einsum('nbhld,nbhsd->nbhls', ks, ks)
    a_mat = jnp.where(strict, bs[..., :, None] * kk * jnp.exp(jnp.where(strict, diff, -jnp.inf)), 0.0)
    lhs = a_mat + jnp.eye(chunk, dtype=a_mat.dtype)
    w_v = lax.linalg.triangular_solve(lhs, bs[..., None] * vs, left_side=True, lower=True, unit_diagonal=True)
    w_k = lax.linalg.triangular_solve(lhs, (bs * jnp.exp(gc))[..., None] * ks, left_side=True, lower=True,
                                      unit_diagonal=True)
    qk = jnp.where(incl, jnp.einsum('nbhld,nbhsd->nbhls', qs, ks) * jnp.exp(jnp.where(incl, diff, -jnp.inf)), 0.0)

    def step(s, xs):
        wv, wk, qc, kc, qkc, gcc = xs
        delta = wv - jnp.einsum('bhld,bhdv->bhlv', wk, s)
        o = (jnp.einsum('bhld,bhdv->bhlv', qc * jnp.exp(gcc)[..., None], s)
             + jnp.einsum('bhls,bhsv->bhlv', qkc, delta))
        g_end = gcc[..., -1]
        s_new = (jnp.exp(g_end)[..., None, None] * s
                 + jnp.einsum('bhsd,bhsv->bhdv', kc * jnp.exp(g_end[..., None] - gcc)[..., None], delta))
        return s_new, o

    s_end, outs = lax.scan(step, s0, (w_v, w_k, qs, ks, qk, gc))
    return from_time_chunks(jnp.swapaxes(outs, 2, 3)), s_end


def gla_chunked(q, k, v, lf, s0, chunk):
    incl = jnp.tril(jnp.ones((chunk, chunk), bool))[None, :, :, None, None]

    def step(s, xs):
        qc, kc, vc, fc = xs
        b = jnp.cumsum(fc, axis=1)
        decay = jnp.exp(jnp.where(incl, b[:, :, None] - b[:, None, :], -jnp.inf))
        attn = jnp.einsum('blhd,blshd->bhls', qc, decay * kc[:, None])
        o = (jnp.einsum('blhd,bhdv->blhv', qc * jnp.exp(b), s)
             + jnp.einsum('bhls,bshv->blhv', attn, vc))
        b_end = b[:, -1]
        s_new = (jnp.exp(b_end)[..., None] * s
                 + jnp.einsum('bshd,bshv->bhdv', kc * jnp.exp(b_end[:, None] - b), vc))
        return s_new, o

    xs = (time_chunks(q, chunk), time_chunks(k, chunk), time_chunks(v, chunk), time_chunks(lf, chunk))
    s_end, outs = lax.scan(step, s0, xs)
    return from_time_chunks(outs), s_end


def mixer_ab(hn, pos, w_in, w_out, q_gain, k_gain, gate_bias, hb_gain, cache):
    f32 = jnp.float32
    bsz, t, _ = hn.shape
    qa, ka, va, qi, ki, wi, qb, kb, vb, ib, fb, ob = split_cols(hn @ w_in, EVEN_SPLITS)
    qa = rope_partial(rms_norm(qa.reshape(bsz, t, A_HEADS, HEAD_DIM), q_gain), pos)
    ka = rope_partial(rms_norm(ka.reshape(bsz, t, A_KV_HEADS, HEAD_DIM), k_gain), pos)
    va = va.reshape(bsz, t, A_KV_HEADS, HEAD_DIM)
    qi = rope_partial(qi.reshape(bsz, t, IDX_HEADS, IDX_DIM), pos)
    ki = rope_partial(ki.reshape(bsz, t, 1, IDX_DIM), pos)[:, :, 0]
    wi = wi * (IDX_HEADS ** -0.5 * IDX_DIM ** -0.5)
    if cache is None:
        a_out = dsa_prompt(qa, qi, wi, ka, va, ki, pos)
        c0 = jnp.zeros((bsz, B_HEADS, B_QK_DIM, B_V_DIM), f32)
        n0 = jnp.zeros((bsz, B_HEADS, B_QK_DIM), f32)
        m0 = jnp.zeros((bsz, B_HEADS), f32)
        chunk = CHUNK
    else:
        k_c, v_c, ki_c, c0, n0, m0 = cache
        k_all = jnp.concatenate([k_c.astype(ka.dtype), ka], axis=1)
        v_all = jnp.concatenate([v_c.astype(va.dtype), va], axis=1)
        ki_all = jnp.concatenate([ki_c.astype(ki.dtype), ki], axis=1)
        n_keys = k_all.shape[1]
        limit = jnp.full((t,), n_keys, jnp.int32)
        a_out = dsa_attend(qa, qi, wi, limit, k_all, v_all, ki_all, min(TOPK_MAX, n_keys // 4))
        chunk = t
    qb = qb.reshape(bsz, t, B_HEADS, B_QK_DIM).astype(f32)
    kb = kb.reshape(bsz, t, B_HEADS, B_QK_DIM).astype(f32) * B_QK_DIM ** -0.5
    vb = vb.reshape(bsz, t, B_HEADS, B_V_DIM).astype(f32)
    ig = ib.astype(f32) + gate_bias[0].astype(f32)
    lf = jax.nn.log_sigmoid(fb.astype(f32) + gate_bias[1].astype(f32))
    h, c, n, m = mlstm_chunked(qb, kb, vb, ig, lf, c0.astype(f32), n0.astype(f32), m0.astype(f32), chunk)
    h = rms_norm(h, hb_gain) * jax.nn.sigmoid(ob.reshape(bsz, t, B_HEADS, B_V_DIM).astype(f32))
    mixed = jnp.concatenate([a_out, h.reshape(bsz, t, -1).astype(hn.dtype)], axis=-1)
    dt = hn.dtype
    return mixed @ w_out, (ka, va, ki, c.astype(dt), n.astype(dt), m.astype(dt))


def mixer_cd(hn, w_in, w_out, conv_w, a_log, dt_bias, c_gain, lb, d_gain, cache):
    f32 = jnp.float32
    bsz, t, _ = hn.shape
    qkv, bc, ac, zc, qd, fd, vd, gd = split_cols(hn @ w_in, ODD_SPLITS)
    if cache is None:
        conv_prev = jnp.zeros((bsz, CONV_W - 1, qkv.shape[-1]), qkv.dtype)
        sc0 = jnp.zeros((bsz, C_HEADS, C_DIM, C_DIM), f32)
        sd0 = jnp.zeros((bsz, D_HEADS, D_EXPAND, D_V_DIM), f32)
        chunk = CHUNK
    else:
        sc0, conv_prev, sd0 = cache
        chunk = t
    xc = jnp.concatenate([conv_prev.astype(qkv.dtype), qkv], axis=1)
    conv = xc[:, 0:t] * conv_w[0]
    for w in range(1, CONV_W):
        conv = conv + xc[:, w:w + t] * conv_w[w]
    conv = jax.nn.silu(conv.astype(f32))
    qc, kc, vc = jnp.split(conv, 3, axis=-1)
    qc = l2_norm(qc.reshape(bsz, t, C_HEADS, C_DIM)) * C_DIM ** -0.5
    kc = l2_norm(kc.reshape(bsz, t, C_HEADS, C_DIM))
    vc = vc.reshape(bsz, t, C_HEADS, C_DIM)
    beta = jax.nn.sigmoid(bc.astype(f32))
    g = -jnp.exp(a_log.astype(f32)) * jax.nn.softplus(ac.astype(f32) + dt_bias.astype(f32))
    oc, sc = gdn_chunked(qc, kc, vc, g, beta, sc0.astype(f32), chunk)
    oc = rms_norm(oc, c_gain) * jax.nn.silu(zc.reshape(bsz, t, C_HEADS, C_DIM).astype(f32))
    lb = lb.reshape(D_HEADS, D_EXPAND)
    zf = fd.reshape(bsz, t, D_HEADS, D_EXPAND).astype(f32)
    lf = jnp.logaddexp(jnp.log(lb), jnp.log1p(-lb) + jax.nn.log_sigmoid(zf))
    kd = (1.0 - lb) * jax.nn.sigmoid(-zf)
    qd = jax.nn.silu(qd.reshape(bsz, t, D_HEADS, D_EXPAND).astype(f32))
    vd = vd.reshape(bsz, t, D_HEADS, D_V_DIM).astype(f32)
    od, sd = gla_chunked(qd, kd, vd, lf, sd0.astype(f32), chunk)
    od = rms_norm(od, d_gain) * jax.nn.silu(gd.reshape(bsz, t, D_HEADS, D_V_DIM).astype(f32))
    mixed = jnp.concatenate([oc.reshape(bsz, t, -1), od.reshape(bsz, t, -1)], axis=-1).astype(hn.dtype)
    dt = hn.dtype
    return mixed @ w_out, (sc.astype(dt), xc[:, xc.shape[1] - (CONV_W - 1):], sd.astype(dt))


def swiglu(h, w1, w3, w2):
    return (jax.nn.silu(h @ w1) * (h @ w3)) @ w2


def moe_swiglu(h, router, w1, w3, w2):
    logits = (h @ router).astype(jnp.float32)
    top_val, top_idx = lax.top_k(logits, TOP_K_EXPERTS)
    gates = jax.nn.softmax(top_val, axis=-1)
    combine = jnp.sum(jax.nn.one_hot(top_idx, N_EXPERTS, dtype=jnp.float32) * gates[..., None], axis=-2)
    combine = combine.astype(h.dtype)
    out = jnp.zeros_like(h)
    for e in range(N_EXPERTS):
        out = out + combine[..., e:e + 1] * swiglu(h, w1[e], w3[e], w2[e])
    return out


def trunk(x, pos_offset, cache, prm):
    t = x.shape[1]
    pos = pos_offset + jnp.arange(t, dtype=jnp.int32)
    probs = jax.nn.softmax(prm['d_lb_logits'].astype(jnp.float32), axis=0)
    lower_bounds = jnp.cumsum(probs, axis=0) - probs[0]
    even_states = ([], [], [], [], [], [])
    odd_states = ([], [], [])
    for layer in range(DEPTH):
        i = layer // 2
        hn = rms_norm(x, prm['norm_mix'][layer])
        if layer % 2 == 0:
            lc = None if cache is None else tuple(c[i] for c in cache[:6])
            out, st = mixer_ab(hn, pos, prm['w_in_ab'][i], prm['w_out_ab'][i], prm['a_q_gain'][i],
                               prm['a_k_gain'][i], prm['b_gate_bias'][i], prm['b_norm_gain'][i], lc)
            x = x + out
            hn = rms_norm(x, prm['norm_ffn'][layer])
            x = x + swiglu(hn, prm['ffn_w1'][i], prm['ffn_w3'][i], prm['ffn_w2'][i])
            for acc, s in zip(even_states, st):
                acc.append(s)
        else:
            lc = None if cache is None else tuple(c[i] for c in cache[6:])
            out, st = mixer_cd(hn, prm['w_in_cd'][i], prm['w_out_cd'][i], prm['c_conv_w'][i], prm['c_a_log'][i],
                               prm['c_dt_bias'][i], prm['c_norm_gain'][i], lower_bounds[layer],
                               prm['d_norm_gain'][i], lc)
            x = x + out
            hn = rms_norm(x, prm['norm_ffn'][layer])
            x = x + moe_swiglu(hn, prm['moe_router'][i], prm['moe_w1'][i], prm['moe_w3'][i], prm['moe_w2'][i])
            for acc, s in zip(odd_states, st):
                acc.append(s)
    new_state = tuple(jnp.stack(a, axis=0) for a in even_states + odd_states)
    return x, new_state


def setup_inputs(seed: int = 0) -> dict:
    key = jax.random.key(seed)
    ks = iter(jax.random.split(key, 48))
    f32 = jnp.float32

    def nrm(shape, scale):
        return jax.random.normal(next(ks), shape, f32) * scale

    def gain(shape):
        return 1.0 + nrm(shape, 0.01)

    dt = jnp.exp(jax.random.uniform(next(ks), (N_ODD, C_HEADS), f32, math.log(1e-3), math.log(1e-1)))
    return {
        'x_prompt': nrm((BATCH, SEQ, D_MODEL), 1.0),
        'x_sample': nrm((DEC_BATCH, DEC_SEQ, D_MODEL), 1.0),
        'cache_a_k': nrm((N_EVEN, DEC_BATCH, PAST_LEN, A_KV_HEADS, HEAD_DIM), 1.0),
        'cache_a_v': nrm((N_EVEN, DEC_BATCH, PAST_LEN, A_KV_HEADS, HEAD_DIM), 1.0),
        'cache_a_kidx': nrm((N_EVEN, DEC_BATCH, PAST_LEN, IDX_DIM), 1.0),
        'state_b_c': nrm((N_EVEN, DEC_BATCH, B_HEADS, B_QK_DIM, B_V_DIM), 0.1),
        'state_b_n': nrm((N_EVEN, DEC_BATCH, B_HEADS, B_QK_DIM), 0.1),
        'state_b_m': nrm((N_EVEN, DEC_BATCH, B_HEADS), 1.0),
        'state_c_s': nrm((N_ODD, DEC_BATCH, C_HEADS, C_DIM, C_DIM), 0.1),
        'state_c_conv': nrm((N_ODD, DEC_BATCH, CONV_W - 1, 3 * C_HEADS * C_DIM), 1.0),
        'state_d_s': nrm((N_ODD, DEC_BATCH, D_HEADS, D_EXPAND, D_V_DIM), 0.3),
        'norm_mix': gain((DEPTH, D_MODEL)),
        'norm_ffn': gain((DEPTH, D_MODEL)),
        'w_in_ab': nrm((N_EVEN, D_MODEL, EVEN_TOTAL), D_MODEL ** -0.5),
        'w_out_ab': nrm((N_EVEN, D_MODEL, D_MODEL), D_MODEL ** -0.5),
        'a_q_gain': gain((N_EVEN, HEAD_DIM)),
        'a_k_gain': gain((N_EVEN, HEAD_DIM)),
        'b_gate_bias': jnp.stack([nrm((N_EVEN, B_HEADS), 0.1),
                                  3.0 + nrm((N_EVEN, B_HEADS), 0.5)], axis=1),
        'b_norm_gain': gain((N_EVEN, B_V_DIM)),
        'w_in_cd': nrm((N_ODD, D_MODEL, ODD_TOTAL), D_MODEL ** -0.5),
        'w_out_cd': nrm((N_ODD, D_MODEL, D_MODEL), D_MODEL ** -0.5),
        'c_conv_w': nrm((N_ODD, CONV_W, 3 * C_HEADS * C_DIM), CONV_W ** -0.5),
        'c_a_log': jnp.log(jax.random.uniform(next(ks), (N_ODD, C_HEADS), f32, 1.0, 16.0)),
        'c_dt_bias': dt + jnp.log(-jnp.expm1(-dt)),
        'c_norm_gain': gain((N_ODD, C_DIM)),
        'd_lb_logits': nrm((DEPTH, D_HEADS * D_EXPAND), 0.5),
        'd_norm_gain': gain((N_ODD, D_V_DIM)),
        'ffn_w1': nrm((N_EVEN, D_MODEL, D_FF), D_MODEL ** -0.5),
        'ffn_w3': nrm((N_EVEN, D_MODEL, D_FF), D_MODEL ** -0.5),
        'ffn_w2': nrm((N_EVEN, D_FF, D_MODEL), D_FF ** -0.5),
        'moe_router': nrm((N_ODD, D_MODEL, N_EXPERTS), D_MODEL ** -0.5),
        'moe_w1': nrm((N_ODD, N_EXPERTS, D_MODEL, D_FF), D_MODEL ** -0.5),
        'moe_w3': nrm((N_ODD, N_EXPERTS, D_MODEL, D_FF), D_MODEL ** -0.5),
        'moe_w2': nrm((N_ODD, N_EXPERTS, D_FF, D_MODEL), D_FF ** -0.5),
    }


def reference(x_prompt, x_sample, cache_a_k, cache_a_v, cache_a_kidx, state_b_c, state_b_n, state_b_m,
              state_c_s, state_c_conv, state_d_s, norm_mix, norm_ffn, w_in_ab, w_out_ab, a_q_gain, a_k_gain,
              b_gate_bias, b_norm_gain, w_in_cd, w_out_cd, c_conv_w, c_a_log, c_dt_bias, c_norm_gain,
              d_lb_logits, d_norm_gain, ffn_w1, ffn_w3, ffn_w2, moe_router, moe_w1, moe_w3, moe_w2):
    prm = dict(norm_mix=norm_mix, norm_ffn=norm_ffn, w_in_ab=w_in_ab, w_out_ab=w_out_ab, a_q_gain=a_q_gain,
               a_k_gain=a_k_gain, b_gate_bias=b_gate_bias, b_norm_gain=b_norm_gain, w_in_cd=w_in_cd,
               w_out_cd=w_out_cd, c_conv_w=c_conv_w, c_a_log=c_a_log, c_dt_bias=c_dt_bias,
               c_norm_gain=c_norm_gain, d_lb_logits=d_lb_logits, d_norm_gain=d_norm_gain, ffn_w1=ffn_w1,
               ffn_w3=ffn_w3, ffn_w2=ffn_w2, moe_router=moe_router, moe_w1=moe_w1, moe_w3=moe_w3, moe_w2=moe_w2)
    cache = (cache_a_k, cache_a_v, cache_a_kidx, state_b_c, state_b_n, state_b_m, state_c_s, state_c_conv, state_d_s)
    y_prompt, st_p = trunk(x_prompt, 0, None, prm)
    y_sample, st_s = trunk(x_sample, cache_a_k.shape[2], cache, prm)
    a_k_p, a_v_p, a_kidx_p, b_c_p, b_n_p, b_m_p, c_s_p, c_conv_p, d_s_p = st_p
    a_k_s, a_v_s, a_kidx_s, b_c_s, b_n_s, b_m_s, c_s_s, c_conv_s, d_s_s = st_s
    return (y_prompt, y_sample, a_k_p, a_v_p, a_kidx_p, b_c_p, b_n_p, b_m_p, c_s_p, c_conv_p, d_s_p,
            a_k_s, a_v_s, a_kidx_s, b_c_s, b_n_s, b_m_s, c_s_s, c_conv_s, d_s_s)
```

```python
import functools
import math

import jax
import jax.numpy as jnp
import numpy as np
from jax import lax
from jax.experimental import pallas as pl
from jax.experimental.pallas import tpu as pltpu

F32 = jnp.float32
BF16 = jnp.bfloat16

EPS = 1e-6
ROPE_THETA = 500000.0
ROT_FRACTION = 4
CHUNK = 64
A_HEADS, A_KV_HEADS, HEAD_DIM = 8, 2, 64
IDX_HEADS, IDX_DIM = 4, 64
TOPK_MAX, Q_BLOCK = 256, 128
B_HEADS, B_QK_DIM, B_V_DIM = 4, 64, 128
C_HEADS, C_DIM, CONV_W = 4, 128, 4
D_HEADS, D_EXPAND, D_V_DIM = 4, 128, 128
N_EXPERTS, TOP_K_EXPERTS = 8, 2

LANE = 128
VMEM_LIMIT = 48 * 1024 * 1024

EVEN_SPLITS = (A_HEADS * HEAD_DIM, A_KV_HEADS * HEAD_DIM, A_KV_HEADS * HEAD_DIM,
               IDX_HEADS * IDX_DIM, IDX_DIM, IDX_HEADS,
               B_HEADS * B_QK_DIM, B_HEADS * B_QK_DIM, B_HEADS * B_V_DIM,
               B_HEADS, B_HEADS, B_HEADS * B_V_DIM)
ODD_SPLITS = (3 * C_HEADS * C_DIM, C_HEADS, C_HEADS, C_HEADS * C_DIM,
              D_HEADS * D_EXPAND, D_HEADS * D_EXPAND, D_HEADS * D_V_DIM, D_HEADS * D_V_DIM)


def _split_cols(p, widths):
    cuts = [int(c) for c in np.cumsum(widths)[:-1]]
    return jnp.split(p, cuts, axis=-1)


def _row_tile(n, target):
    t = min(n, target)
    while n % t:
        t //= 2
    return t


def _col_tile(n, target):
    best = LANE
    for k in range(1, n // LANE + 1):
        c = k * LANE
        if n % c == 0 and c <= target:
            best = c
    return best


def _rms_rows(x, gain):
    return x * lax.rsqrt(jnp.mean(x * x, axis=-1, keepdims=True) + EPS) * gain


def _norm_matmul_kernel(x_ref, g_ref, w_ref, o_ref, xn_ref):
    @pl.when(pl.program_id(1) == 0)
    def _():
        xn_ref[...] = _rms_rows(x_ref[...], g_ref[...]).astype(BF16)

    o_ref[...] = jnp.dot(xn_ref[...], w_ref[...], preferred_element_type=F32)


def norm_matmul(x, gain, w):
    n, d = x.shape
    m = w.shape[1]
    tm = _row_tile(n, 1024)
    tn = _col_tile(m, 1536)
    return pl.pallas_call(
        _norm_matmul_kernel,
        grid=(n // tm, m // tn),
        in_specs=[pl.BlockSpec((tm, d), lambda i, j: (i, 0)),
                  pl.BlockSpec((1, d), lambda i, j: (0, 0)),
                  pl.BlockSpec((d, tn), lambda i, j: (0, j))],
        out_specs=pl.BlockSpec((tm, tn), lambda i, j: (i, j)),
        out_shape=jax.ShapeDtypeStruct((n, m), F32),
        scratch_shapes=[pltpu.VMEM((tm, d), BF16)],
        compiler_params=pltpu.CompilerParams(
            dimension_semantics=("parallel", "arbitrary"), vmem_limit_bytes=VMEM_LIMIT),
        name="norm_matmul",
    )(x, gain.reshape(1, d), w)


def _matmul_res_kernel(a_ref, w_ref, r_ref, o_ref):
    o_ref[...] = r_ref[...] + jnp.dot(a_ref[...].astype(BF16), w_ref[...], preferred_element_type=F32)


def matmul_residual(a, w, res):
    n, k = a.shape
    m = w.shape[1]
    tm = _row_tile(n, 1024)
    return pl.pallas_call(
        _matmul_res_kernel,
        grid=(n // tm,),
        in_specs=[pl.BlockSpec((tm, k), lambda i: (i, 0)),
                  pl.BlockSpec((k, m), lambda i: (0, 0)),
                  pl.BlockSpec((tm, m), lambda i: (i, 0))],
        out_specs=pl.BlockSpec((tm, m), lambda i: (i, 0)),
        out_shape=jax.ShapeDtypeStruct((n, m), F32),
        compiler_params=pltpu.CompilerParams(
            dimension_semantics=("parallel",), vmem_limit_bytes=VMEM_LIMIT),
        name="matmul_residual",
    )(a, w, res)


def _swiglu_tile(xn, w1, w3, w2):
    h1 = jnp.dot(xn, w1, preferred_element_type=F32)
    h3 = jnp.dot(xn, w3, preferred_element_type=F32)
    act = h1 * jax.nn.sigmoid(h1) * h3
    return act


def _ffn_kernel(x_ref, g_ref, w1_ref, w3_ref, w2_ref, o_ref, xn_ref):
    @pl.when(pl.program_id(1) == 0)
    def _():
        x = x_ref[...]
        xn_ref[...] = _rms_rows(x, g_ref[...]).astype(BF16)
        o_ref[...] = x

    act = _swiglu_tile(xn_ref[...], w1_ref[...], w3_ref[...], None)
    o_ref[...] += jnp.dot(act.astype(BF16), w2_ref[...], preferred_element_type=F32)


def ffn_residual(x, gain, w1, w3, w2):
    n, d = x.shape
    f = w1.shape[1]
    tm = _row_tile(n, 1024)
    tf = _col_tile(f, 512)
    return pl.pallas_call(
        _ffn_kernel,
        grid=(n // tm, f // tf),
        in_specs=[pl.BlockSpec((tm, d), lambda i, j: (i, 0)),
                  pl.BlockSpec((1, d), lambda i, j: (0, 0)),
                  pl.BlockSpec((d, tf), lambda i, j: (0, j)),
                  pl.BlockSpec((d, tf), lambda i, j: (0, j)),
                  pl.BlockSpec((tf, d), lambda i, j: (j, 0))],
        out_specs=pl.BlockSpec((tm, d), lambda i, j: (i, 0)),
        out_shape=jax.ShapeDtypeStruct((n, d), F32),
        scratch_shapes=[pltpu.VMEM((tm, d), BF16)],
        compiler_params=pltpu.CompilerParams(
            dimension_semantics=("parallel", "arbitrary"), vmem_limit_bytes=VMEM_LIMIT),
        name="ffn_residual",
    )(x, gain.reshape(1, d), w1, w3, w2)


def _moe_kernel(x_ref, g_ref, r_ref, w1_ref, w3_ref, w2_ref, o_ref, xn_ref, comb_ref):
    e = pl.program_id(1)
    j = pl.program_id(2)

    @pl.when(jnp.logical_and(e == 0, j == 0))
    def _():
        x = x_ref[...]
        xn = _rms_rows(x, g_ref[...])
        xn_ref[...] = xn.astype(BF16)
        o_ref[...] = x
        logits = jnp.dot(xn, r_ref[...], preferred_element_type=F32, precision=lax.Precision.HIGHEST)
        lane = lax.broadcasted_iota(jnp.int32, logits.shape, 1)
        logits = jnp.where(lane < N_EXPERTS, logits, -jnp.inf)
        m1 = jnp.max(logits, axis=-1, keepdims=True)
        i1 = jnp.min(jnp.where(logits == m1, lane, LANE), axis=-1, keepdims=True)
        rest = jnp.where(lane == i1, -jnp.inf, logits)
        m2 = jnp.max(rest, axis=-1, keepdims=True)
        i2 = jnp.min(jnp.where(rest == m2, lane, LANE), axis=-1, keepdims=True)
        e2 = jnp.exp(m2 - m1)
        den = 1.0 + e2
        comb_ref[...] = jnp.where(lane == i1, 1.0 / den, 0.0) + jnp.where(lane == i2, e2 / den, 0.0)

    comb = comb_ref[...]
    lane = lax.broadcasted_iota(jnp.int32, comb.shape, 1)
    c = jnp.sum(jnp.where(lane == e, comb, 0.0), axis=-1, keepdims=True)
    act = _swiglu_tile(xn_ref[...], w1_ref[0], w3_ref[0], None)
    o_ref[...] += c * jnp.dot(act.astype(BF16), w2_ref[0], preferred_element_type=F32)


def moe_residual(x, gain, router, w1, w3, w2):
    n, d = x.shape
    ne, _, f = w1.shape
    tm = _row_tile(n, 1024)
    tf = _col_tile(f, 512)
    return pl.pallas_call(
        _moe_kernel,
        grid=(n // tm, ne, f // tf),
        in_specs=[pl.BlockSpec((tm, d), lambda i, e, j: (i, 0)),
                  pl.BlockSpec((1, d), lambda i, e, j: (0, 0)),
                  pl.BlockSpec((d, LANE), lambda i, e, j: (0, 0)),
                  pl.BlockSpec((1, d, tf), lambda i, e, j: (e, 0, j)),
                  pl.BlockSpec((1, d, tf), lambda i, e, j: (e, 0, j)),
                  pl.BlockSpec((1, tf, d), lambda i, e, j: (e, j, 0))],
        out_specs=pl.BlockSpec((tm, d), lambda i, e, j: (i, 0)),
        out_shape=jax.ShapeDtypeStruct((n, d), F32),
        scratch_shapes=[pltpu.VMEM((tm, d), BF16), pltpu.VMEM((tm, LANE), F32)],
        compiler_params=pltpu.CompilerParams(
            dimension_semantics=("parallel", "arbitrary", "arbitrary"), vmem_limit_bytes=VMEM_LIMIT),
        name="moe_residual",
    )(x, gain.reshape(1, d), router, w1, w3, w2)


def _rms_norm(x, gain):
    return x * lax.rsqrt(jnp.mean(x * x, axis=-1, keepdims=True) + EPS) * gain


def _l2_norm(x):
    return x * lax.rsqrt(jnp.sum(x * x, axis=-1, keepdims=True) + EPS)


def _rope_partial(x, pos):
    rot = x.shape[-1] // ROT_FRACTION
    half = rot // 2
    inv_freq = ROPE_THETA ** (-jnp.arange(half, dtype=F32) * 2.0 / rot)
    ang = pos.astype(F32)[:, None] * inv_freq[None, :]
    cos = jnp.cos(ang)[:, None, :]
    sin = jnp.sin(ang)[:, None, :]
    x1 = x[..., :half]
    x2 = x[..., half:rot]
    return jnp.concatenate([x1 * cos - x2 * sin, x1 * sin + x2 * cos, x[..., rot:]], axis=-1)


def _time_chunks(a, chunk):
    b, t = a.shape[:2]
    return jnp.moveaxis(a.reshape(b, t // chunk, chunk, *a.shape[2:]), 1, 0)


def _from_time_chunks(a):
    nc, b, l = a.shape[:3]
    return jnp.moveaxis(a, 0, 1).reshape(b, nc * l, *a.shape[3:])


def _dsa_attend(q, qi, wi, limit, k, v, ki, n_sel):
    bsz, tq = q.shape[:2]
    idx_logits = jnp.einsum('btjd,bsd->btjs', qi, ki)
    score = jnp.einsum('btj,btjs->bts', wi, jax.nn.relu(idx_logits))
    visible = jnp.arange(k.shape[1])[None, :] < limit[:, None]
    score = jnp.where(visible[None], score, -jnp.inf)
    _, sel = lax.top_k(score, n_sel)
    valid = sel < limit[None, :, None]
    b_idx = jnp.arange(bsz)[:, None, None]
    k_sel = k[b_idx, sel]
    v_sel = v[b_idx, sel]
    qg = q.reshape(bsz, tq, A_KV_HEADS, A_HEADS // A_KV_HEADS, HEAD_DIM)
    logits = jnp.einsum('btkgd,btnkd->btkgn', qg, k_sel) * HEAD_DIM ** -0.5
    logits = jnp.where(valid[:, :, None, None, :], logits, -jnp.inf)
    probs = jax.nn.softmax(logits, axis=-1)
    out = jnp.einsum('btkgn,btnkd->btkgd', probs, v_sel)
    return out.reshape(bsz, tq, A_HEADS * HEAD_DIM)


def _dsa_prompt(q, qi, wi, k, v, ki, pos):
    bsz, t = q.shape[:2]
    nb = t // Q_BLOCK
    n_sel = min(TOPK_MAX, t // 4)
    limit = (pos // CHUNK + 1) * CHUNK

    def blocks(a):
        return jnp.moveaxis(a.reshape(bsz, nb, Q_BLOCK, *a.shape[2:]), 1, 0)

    out = lax.map(lambda xs: _dsa_attend(xs[0], xs[1], xs[2], xs[3], k, v, ki, n_sel),
                  (blocks(q), blocks(qi), blocks(wi), limit.reshape(nb, Q_BLOCK)))
    return jnp.moveaxis(out, 0, 1).reshape(bsz, t, A_HEADS * HEAD_DIM)


def _mlstm_chunked(q, k, v, ig, lf, c0, n0, m0, chunk):
    causal = jnp.tril(jnp.ones((chunk, chunk), bool))

    def step(carry, xs):
        c, n, m = carry
        qc, kc, vc, ic, fc = xs
        b = jnp.swapaxes(jnp.cumsum(fc, axis=1), 1, 2)
        ih = jnp.swapaxes(ic, 1, 2)
        dmat = jnp.where(causal, b[..., :, None] - b[..., None, :] + ih[..., None, :], -jnp.inf)
        inter = b + m[..., None]
        mrow = jnp.maximum(inter, dmat.max(-1))
        w_intra = jnp.exp(dmat - mrow[..., None])
        w_state = jnp.exp(inter - mrow)
        scores = jnp.einsum('blhd,bshd->bhls', qc, kc) * w_intra
        num = (jnp.einsum('bhls,bshv->bhlv', scores, vc)
               + w_state[..., None] * jnp.einsum('blhd,bhdv->bhlv', qc, c))
        den = scores.sum(-1) + w_state * jnp.einsum('blhd,bhd->bhl', qc, n)
        h = num / jnp.maximum(jnp.abs(den), jnp.exp(-mrow))[..., None]
        b_end = b[..., -1]
        g = b_end[..., None] - b + ih
        m_new = jnp.maximum(b_end + m, g.max(-1))
        w_k = jnp.exp(g - m_new[..., None])
        keep = jnp.exp(b_end + m - m_new)
        c_new = keep[..., None, None] * c + jnp.einsum('bhs,bshd,bshv->bhdv', w_k, kc, vc)
        n_new = keep[..., None] * n + jnp.einsum('bhs,bshd->bhd', w_k, kc)
        return (c_new, n_new, m_new), jnp.swapaxes(h, 1, 2)

    xs = (_time_chunks(q, chunk), _time_chunks(k, chunk), _time_chunks(v, chunk),
          _time_chunks(ig, chunk), _time_chunks(lf, chunk))
    (c, n, m), hs = lax.scan(step, (c0, n0, m0), xs)
    return _from_time_chunks(hs), c, n, m


def _gdn_chunked(q, k, v, g, beta, s0, chunk):
    qs, ks, vs, gs, bs = (jnp.swapaxes(_time_chunks(a, chunk), 2, 3) for a in (q, k, v, g, beta))
    strict = jnp.tril(jnp.ones((chunk, chunk), bool), -1)
    incl = jnp.tril(jnp.ones((chunk, chunk), bool))
    gc = jnp.cumsum(gs, axis=-1)
    diff = gc[..., :, None] - gc[..., None, :]
    kk = jnp.einsum('nbhld,nbhsd->nbhls', ks, ks)
    a_mat = jnp.where(strict, bs[..., :, None] * kk * jnp.exp(jnp.where(strict, diff, -jnp.inf)), 0.0)
    lhs = a_mat + jnp.eye(chunk, dtype=a_mat.dtype)
    w_v = lax.linalg.triangular_solve(lhs, bs[..., None] * vs, left_side=True, lower=True, unit_diagonal=True)
    w_k = lax.linalg.triangular_solve(lhs, (bs * jnp.exp(gc))[..., None] * ks, left_side=True, lower=True,
                                      unit_diagonal=True)
    qk = jnp.where(incl, jnp.einsum('nbhld,nbhsd->nbhls', qs, ks) * jnp.exp(jnp.where(incl, diff, -jnp.inf)), 0.0)

    def step(s, xs):
        wv, wk, qc, kc, qkc, gcc = xs
        delta = wv - jnp.einsum('bhld,bhdv->bhlv', wk, s)
        o = (jnp.einsum('bhld,bhdv->bhlv', qc * jnp.exp(gcc)[..., None], s)
             + jnp.einsum('bhls,bhsv->bhlv', qkc, delta))
        g_end = gcc[..., -1]
        s_new = (jnp.exp(g_end)[..., None, None] * s
                 + jnp.einsum('bhsd,bhsv->bhdv', kc * jnp.exp(g_end[..., None] - gcc)[..., None], delta))
        return s_new, o

    s_end, outs = lax.scan(step, s0, (w_v, w_k, qs, ks, qk, gc))
    return _from_time_chunks(jnp.swapaxes(outs, 2, 3)), s_end


def _gla_chunked(q, k, v, lf, s0, chunk):
    incl = jnp.tril(jnp.ones((chunk, chunk), bool))[None, :, :, None, None]

    def step(s, xs):
        qc, kc, vc, fc = xs
        b = jnp.cumsum(fc, axis=1)
        decay = jnp.exp(jnp.where(incl, b[:, :, None] - b[:, None, :], -jnp.inf))
        attn = jnp.einsum('blhd,blshd->bhls', qc, decay * kc[:, None])
        o = (jnp.einsum('blhd,bhdv->blhv', qc * jnp.exp(b), s)
             + jnp.einsum('bhls,bshv->blhv', attn, vc))
        b_end = b[:, -1]
        s_new = (jnp.exp(b_end)[..., None] * s
                 + jnp.einsum('bshd,bshv->bhdv', kc * jnp.exp(b_end[:, None] - b), vc))
        return s_new, o

    xs = (_time_chunks(q, chunk), _time_chunks(k, chunk), _time_chunks(v, chunk), _time_chunks(lf, chunk))
    s_end, outs = lax.scan(step, s0, xs)
    return _from_time_chunks(outs), s_end


def _mixer_ab(proj, pos, q_gain, k_gain, gate_bias, hb_gain, cache):
    bsz, t, _ = proj.shape
    qa, ka, va, qi, ki, wi, qb, kb, vb, ib, fb, ob = _split_cols(proj, EVEN_SPLITS)
    qa = _rope_partial(_rms_norm(qa.reshape(bsz, t, A_HEADS, HEAD_DIM), q_gain), pos)
    ka = _rope_partial(_rms_norm(ka.reshape(bsz, t, A_KV_HEADS, HEAD_DIM), k_gain), pos)
    va = va.reshape(bsz, t, A_KV_HEADS, HEAD_DIM)
    qi = _rope_partial(qi.reshape(bsz, t, IDX_HEADS, IDX_DIM), pos)
    ki = _rope_partial(ki.reshape(bsz, t, 1, IDX_DIM), pos)[:, :, 0]
    wi = wi * (IDX_HEADS ** -0.5 * IDX_DIM ** -0.5)
    if cache is None:
        a_out = _dsa_prompt(qa, qi, wi, ka, va, ki, pos)
        c0 = jnp.zeros((bsz, B_HEADS, B_QK_DIM, B_V_DIM), F32)
        n0 = jnp.zeros((bsz, B_HEADS, B_QK_DIM), F32)
        m0 = jnp.zeros((bsz, B_HEADS), F32)
        chunk = CHUNK
    else:
        k_c, v_c, ki_c, c0, n0, m0 = cache
        k_all = jnp.concatenate([k_c, ka], axis=1)
        v_all = jnp.concatenate([v_c, va], axis=1)
        ki_all = jnp.concatenate([ki_c, ki], axis=1)
        n_keys = k_all.shape[1]
        limit = jnp.full((t,), n_keys, jnp.int32)
        a_out = _dsa_attend(qa, qi, wi, limit, k_all, v_all, ki_all, min(TOPK_MAX, n_keys // 4))
        chunk = t
    qb = qb.reshape(bsz, t, B_HEADS, B_QK_DIM)
    kb = kb.reshape(bsz, t, B_HEADS, B_QK_DIM) * B_QK_DIM ** -0.5
    vb = vb.reshape(bsz, t, B_HEADS, B_V_DIM)
    ig = ib + gate_bias[0]
    lf = jax.nn.log_sigmoid(fb + gate_bias[1])
    h, c, n, m = _mlstm_chunked(qb, kb, vb, ig, lf, c0, n0, m0, chunk)
    h = _rms_norm(h, hb_gain) * jax.nn.sigmoid(ob.reshape(bsz, t, B_HEADS, B_V_DIM))
    mixed = jnp.concatenate([a_out, h.reshape(bsz, t, -1)], axis=-1)
    return mixed, (ka, va, ki, c, n, m)


def _mixer_cd(proj, conv_w, a_log, dt_bias, c_gain, lb, d_gain, cache):
    bsz, t, _ = proj.shape
    qkv, bc, ac, zc, qd, fd, vd, gd = _split_cols(proj, ODD_SPLITS)
    if cache is None:
        conv_prev = jnp.zeros((bsz, CONV_W - 1, qkv.shape[-1]), F32)
        sc0 = jnp.zeros((bsz, C_HEADS, C_DIM, C_DIM), F32)
        sd0 = jnp.zeros((bsz, D_HEADS, D_EXPAND, D_V_DIM), F32)
        chunk = CHUNK
    else:
        sc0, conv_prev, sd0 = cache
        chunk = t
    xc = jnp.concatenate([conv_prev, qkv], axis=1)
    conv = xc[:, 0:t] * conv_w[0]
    for w in range(1, CONV_W):
        conv = conv + xc[:, w:w + t] * conv_w[w]
    conv = jax.nn.silu(conv)
    qc, kc, vc = jnp.split(conv, 3, axis=-1)
    qc = _l2_norm(qc.reshape(bsz, t, C_HEADS, C_DIM)) * C_DIM ** -0.5
    kc = _l2_norm(kc.reshape(bsz, t, C_HEADS, C_DIM))
    vc = vc.reshape(bsz, t, C_HEADS, C_DIM)
    beta = jax.nn.sigmoid(bc)
    g = -jnp.exp(a_log) * jax.nn.softplus(ac + dt_bias)
    oc, sc = _gdn_chunked(qc, kc, vc, g, beta, sc0, chunk)
    oc = _rms_norm(oc, c_gain) * jax.nn.silu(zc.reshape(bsz, t, C_HEADS, C_DIM))
    lb = lb.reshape(D_HEADS, D_EXPAND)
    zf = fd.reshape(bsz, t, D_HEADS, D_EXPAND)
    lf = jnp.logaddexp(jnp.log(lb), jnp.log1p(-lb) + jax.nn.log_sigmoid(zf))
    kd = (1.0 - lb) * jax.nn.sigmoid(-zf)
    qd = jax.nn.silu(qd.reshape(bsz, t, D_HEADS, D_EXPAND))
    vd = vd.reshape(bsz, t, D_HEADS, D_V_DIM)
    od, sd = _gla_chunked(qd, kd, vd, lf, sd0, chunk)
    od = _rms_norm(od, d_gain) * jax.nn.silu(gd.reshape(bsz, t, D_HEADS, D_V_DIM))
    mixed = jnp.concatenate([oc.reshape(bsz, t, -1), od.reshape(bsz, t, -1)], axis=-1)
    return mixed, (sc, xc[:, xc.shape[1] - (CONV_W - 1):], sd)


def _pad_cols(w, mult=LANE):
    pad = (-w.shape[-1]) % mult
    return jnp.pad(w, [(0, 0)] * (w.ndim - 1) + [(0, pad)])


def _trunk(x, pos_offset, cache, prm, wts):
    bsz, t, d = x.shape
    n = bsz * t
    pos = pos_offset + jnp.arange(t, dtype=jnp.int32)
    probs = jax.nn.softmax(prm['d_lb_logits'], axis=0)
    lower_bounds = jnp.cumsum(probs, axis=0) - probs[0]
    xf = x.reshape(n, d)

    lc = None if cache is None else tuple(c[0] for c in cache[:6])
    proj = norm_matmul(xf, prm['norm_mix'][0], wts['w_in_ab'])[:, :sum(EVEN_SPLITS)]
    mixed, st_even = _mixer_ab(proj.reshape(bsz, t, -1), pos, prm['a_q_gain'][0], prm['a_k_gain'][0],
                               prm['b_gate_bias'][0], prm['b_norm_gain'][0], lc)
    xf = matmul_residual(mixed.reshape(n, d), wts['w_out_ab'], xf)
    xf = ffn_residual(xf, prm['norm_ffn'][0], wts['ffn_w1'], wts['ffn_w3'], wts['ffn_w2'])

    lc = None if cache is None else tuple(c[0] for c in cache[6:])
    proj = norm_matmul(xf, prm['norm_mix'][1], wts['w_in_cd'])[:, :sum(ODD_SPLITS)]
    mixed, st_odd = _mixer_cd(proj.reshape(bsz, t, -1), prm['c_conv_w'][0], prm['c_a_log'][0],
                              prm['c_dt_bias'][0], prm['c_norm_gain'][0], lower_bounds[1],
                              prm['d_norm_gain'][0], lc)
    xf = matmul_residual(mixed.reshape(n, d), wts['w_out_cd'], xf)
    xf = moe_residual(xf, prm['norm_ffn'][1], wts['moe_router'], wts['moe_w1'], wts['moe_w3'], wts['moe_w2'])

    new_state = tuple(s[None] for s in st_even + st_odd)
    return xf.reshape(bsz, t, d), new_state


def kernel(x_prompt, x_sample, cache_a_k, cache_a_v, cache_a_kidx, state_b_c, state_b_n, state_b_m,
           state_c_s, state_c_conv, state_d_s, norm_mix, norm_ffn, w_in_ab, w_out_ab, a_q_gain, a_k_gain,
           b_gate_bias, b_norm_gain, w_in_cd, w_out_cd, c_conv_w, c_a_log, c_dt_bias, c_norm_gain,
           d_lb_logits, d_norm_gain, ffn_w1, ffn_w3, ffn_w2, moe_router, moe_w1, moe_w3, moe_w2):
    prm = dict(norm_mix=norm_mix, norm_ffn=norm_ffn, a_q_gain=a_q_gain, a_k_gain=a_k_gain,
               b_gate_bias=b_gate_bias, b_norm_gain=b_norm_gain, c_conv_w=c_conv_w, c_a_log=c_a_log,
               c_dt_bias=c_dt_bias, c_norm_gain=c_norm_gain, d_lb_logits=d_lb_logits, d_norm_gain=d_norm_gain)
    wts = dict(w_in_ab=_pad_cols(w_in_ab[0]).astype(BF16), w_out_ab=w_out_ab[0].astype(BF16),
               w_in_cd=_pad_cols(w_in_cd[0]).astype(BF16), w_out_cd=w_out_cd[0].astype(BF16),
               ffn_w1=ffn_w1[0].astype(BF16), ffn_w3=ffn_w3[0].astype(BF16), ffn_w2=ffn_w2[0].astype(BF16),
               moe_router=_pad_cols(moe_router[0]),
               moe_w1=moe_w1[0].astype(BF16), moe_w3=moe_w3[0].astype(BF16), moe_w2=moe_w2[0].astype(BF16))
    cache = (cache_a_k, cache_a_v, cache_a_kidx, state_b_c, state_b_n, state_b_m, state_c_s, state_c_conv, state_d_s)
    y_prompt, st_p = _trunk(x_prompt, 0, None, prm, wts)
    y_sample, st_s = _trunk(x_sample, cache_a_k.shape[2], cache, prm, wts)
    return (y_prompt, y_sample) + st_p + st_s
```

```python
import functools
import math

import jax
import jax.numpy as jnp
import numpy as np
from jax import lax
from jax.experimental import pallas as pl
from jax.experimental.pallas import tpu as pltpu

F32 = jnp.float32
BF16 = jnp.bfloat16

EPS = 1e-6
ROPE_THETA = 500000.0
ROT_FRACTION = 4
CHUNK = 64
A_HEADS, A_KV_HEADS, HEAD_DIM = 8, 2, 64
IDX_HEADS, IDX_DIM = 4, 64
TOPK_MAX, Q_BLOCK = 256, 128
B_HEADS, B_QK_DIM, B_V_DIM = 4, 64, 128
C_HEADS, C_DIM, CONV_W = 4, 128, 4
D_HEADS, D_EXPAND, D_V_DIM = 4, 128, 128
N_EXPERTS, TOP_K_EXPERTS = 8, 2

LANE = 128
VMEM_LIMIT = 48 * 1024 * 1024

EVEN_SPLITS = (A_HEADS * HEAD_DIM, A_KV_HEADS * HEAD_DIM, A_KV_HEADS * HEAD_DIM,
               IDX_HEADS * IDX_DIM, IDX_DIM, IDX_HEADS,
               B_HEADS * B_QK_DIM, B_HEADS * B_QK_DIM, B_HEADS * B_V_DIM,
               B_HEADS, B_HEADS, B_HEADS * B_V_DIM)
ODD_SPLITS = (3 * C_HEADS * C_DIM, C_HEADS, C_HEADS, C_HEADS * C_DIM,
              D_HEADS * D_EXPAND, D_HEADS * D_EXPAND, D_HEADS * D_V_DIM, D_HEADS * D_V_DIM)


def _split_cols(p, widths):
    cuts = [int(c) for c in np.cumsum(widths)[:-1]]
    return jnp.split(p, cuts, axis=-1)


def _row_tile(n, target):
    t = min(n, target)
    while n % t:
        t //= 2
    return t


def _col_tile(n, target):
    best = LANE
    for k in range(1, n // LANE + 1):
        c = k * LANE
        if n % c == 0 and c <= target:
            best = c
    return best


def _rms_rows(x, gain):
    return x * lax.rsqrt(jnp.mean(x * x, axis=-1, keepdims=True) + EPS) * gain


def _norm_matmul_kernel(x_ref, g_ref, w_ref, o_ref, xn_ref):
    @pl.when(pl.program_id(1) == 0)
    def _():
        xn_ref[...] = _rms_rows(x_ref[...], g_ref[...]).astype(BF16)

    o_ref[...] = jnp.dot(xn_ref[...], w_ref[...], preferred_element_type=F32)


def norm_matmul(x, gain, w):
    n, d = x.shape
    m = w.shape[1]
    tm = _row_tile(n, 1024)
    tn = _col_tile(m, 1536)
    return pl.pallas_call(
        _norm_matmul_kernel,
        grid=(n // tm, m // tn),
        in_specs=[pl.BlockSpec((tm, d), lambda i, j: (i, 0)),
                  pl.BlockSpec((1, d), lambda i, j: (0, 0)),
                  pl.BlockSpec((d, tn), lambda i, j: (0, j))],
        out_specs=pl.BlockSpec((tm, tn), lambda i, j: (i, j)),
        out_shape=jax.ShapeDtypeStruct((n, m), F32),
        scratch_shapes=[pltpu.VMEM((tm, d), BF16)],
        compiler_params=pltpu.CompilerParams(
            dimension_semantics=("parallel", "arbitrary"), vmem_limit_bytes=VMEM_LIMIT),
        name="norm_matmul",
    )(x, gain.reshape(1, d), w)


def _matmul_res_kernel(a_ref, w_ref, r_ref, o_ref):
    o_ref[...] = r_ref[...] + jnp.dot(a_ref[...].astype(BF16), w_ref[...], preferred_element_type=F32)


def matmul_residual(a, w, res):
    n, k = a.shape
    m = w.shape[1]
    tm = _row_tile(n, 1024)
    return pl.pallas_call(
        _matmul_res_kernel,
        grid=(n // tm,),
        in_specs=[pl.BlockSpec((tm, k), lambda i: (i, 0)),
                  pl.BlockSpec((k, m), lambda i: (0, 0)),
                  pl.BlockSpec((tm, m), lambda i: (i, 0))],
        out_specs=pl.BlockSpec((tm, m), lambda i: (i, 0)),
        out_shape=jax.ShapeDtypeStruct((n, m), F32),
        compiler_params=pltpu.CompilerParams(
            dimension_semantics=("parallel",), vmem_limit_bytes=VMEM_LIMIT),
        name="matmul_residual",
    )(a, w, res)


def _swiglu_tile(xn, w1, w3, w2):
    h1 = jnp.dot(xn, w1, preferred_element_type=F32)
    h3 = jnp.dot(xn, w3, preferred_element_type=F32)
    act = h1 * jax.nn.sigmoid(h1) * h3
    return act


def _ffn_kernel(x_ref, g_ref, w1_ref, w3_ref, w2_ref, o_ref, xn_ref):
    @pl.when(pl.program_id(1) == 0)
    def _():
        x = x_ref[...]
        xn_ref[...] = _rms_rows(x, g_ref[...]).astype(BF16)
        o_ref[...] = x

    act = _swiglu_tile(xn_ref[...], w1_ref[...], w3_ref[...], None)
    o_ref[...] += jnp.dot(act.astype(BF16), w2_ref[...], preferred_element_type=F32)


def ffn_residual(x, gain, w1, w3, w2):
    n, d = x.shape
    f = w1.shape[1]
    tm = _row_tile(n, 1024)
    tf = _col_tile(f, 512)
    return pl.pallas_call(
        _ffn_kernel,
        grid=(n // tm, f // tf),
        in_specs=[pl.BlockSpec((tm, d), lambda i, j: (i, 0)),
                  pl.BlockSpec((1, d), lambda i, j: (0, 0)),
                  pl.BlockSpec((d, tf), lambda i, j: (0, j)),
                  pl.BlockSpec((d, tf), lambda i, j: (0, j)),
                  pl.BlockSpec((tf, d), lambda i, j: (j, 0))],
        out_specs=pl.BlockSpec((tm, d), lambda i, j: (i, 0)),
        out_shape=jax.ShapeDtypeStruct((n, d), F32),
        scratch_shapes=[pltpu.VMEM((tm, d), BF16)],
        compiler_params=pltpu.CompilerParams(
            dimension_semantics=("parallel", "arbitrary"), vmem_limit_bytes=VMEM_LIMIT),
        name="ffn_residual",
    )(x, gain.reshape(1, d), w1, w3, w2)


def _moe_kernel(x_ref, g_ref, r_ref, w1_ref, w3_ref, w2_ref, o_ref, xn_ref, comb_ref):
    e = pl.program_id(1)
    j = pl.program_id(2)

    @pl.when(jnp.logical_and(e == 0, j == 0))
    def _():
        x = x_ref[...]
        xn = _rms_rows(x, g_ref[...])
        xn_ref[...] = xn.astype(BF16)
        o_ref[...] = x
        logits = jnp.dot(xn, r_ref[...], preferred_element_type=F32, precision=lax.Precision.HIGHEST)
        lane = lax.broadcasted_iota(jnp.int32, logits.shape, 1)
        logits = jnp.where(lane < N_EXPERTS, logits, -jnp.inf)
        m1 = jnp.max(logits, axis=-1, keepdims=True)
        i1 = jnp.min(jnp.where(logits == m1, lane, LANE), axis=-1, keepdims=True)
        rest = jnp.where(lane == i1, -jnp.inf, logits)
        m2 = jnp.max(rest, axis=-1, keepdims=True)
        i2 = jnp.min(jnp.where(rest == m2, lane, LANE), axis=-1, keepdims=True)
        e2 = jnp.exp(m2 - m1)
        den = 1.0 + e2
        comb_ref[...] = jnp.where(lane == i1, 1.0 / den, 0.0) + jnp.where(lane == i2, e2 / den, 0.0)

    comb = comb_ref[...]
    lane = lax.broadcasted_iota(jnp.int32, comb.shape, 1)
    c = jnp.sum(jnp.where(lane == e, comb, 0.0), axis=-1, keepdims=True)
    act = _swiglu_tile(xn_ref[...], w1_ref[0], w3_ref[0], None)
    o_ref[...] += c * jnp.dot(act.astype(BF16), w2_ref[0], preferred_element_type=F32)


def moe_residual(x, gain, router, w1, w3, w2):
    n, d = x.shape
    ne, _, f = w1.shape
    tm = _row_tile(n, 1024)
    tf = _col_tile(f, 512)
    return pl.pallas_call(
        _moe_kernel,
        grid=(n // tm, ne, f // tf),
        in_specs=[pl.BlockSpec((tm, d), lambda i, e, j: (i, 0)),
                  pl.BlockSpec((1, d), lambda i, e, j: (0, 0)),
                  pl.BlockSpec((d, LANE), lambda i, e, j: (0, 0)),
                  pl.BlockSpec((1, d, tf), lambda i, e, j: (e, 0, j)),
                  pl.BlockSpec((1, d, tf), lambda i, e, j: (e, 0, j)),
                  pl.BlockSpec((1, tf, d), lambda i, e, j: (e, j, 0))],
        out_specs=pl.BlockSpec((tm, d), lambda i, e, j: (i, 0)),
        out_shape=jax.ShapeDtypeStruct((n, d), F32),
        scratch_shapes=[pltpu.VMEM((tm, d), BF16), pltpu.VMEM((tm, LANE), F32)],
        compiler_params=pltpu.CompilerParams(
            dimension_semantics=("parallel", "arbitrary", "arbitrary"), vmem_limit_bytes=VMEM_LIMIT),
        name="moe_residual",
    )(x, gain.reshape(1, d), router, w1, w3, w2)


AB_QA, AB_KA, AB_VA, AB_QI, AB_MISC, AB_QB, AB_KB, AB_VB, AB_OB, AB_TOTAL = (
    0, 512, 640, 768, 1024, 1152, 1408, 1664, 2176, 2688)
MISC_WI, MISC_IB, MISC_FB = 64, 68, 72
HALF = LANE // 2
KEY_CHUNK = 512
MASKED = -1e30
KEY_OF_NEG_INF = -2139095041
INT_MIN = -2147483648


def _permute_w_in_ab(w):
    qa, ka, va, qi, ki, wi, qb, kb, vb, ib, fb, ob = _split_cols(w, EVEN_SPLITS)
    pad = jnp.zeros((w.shape[0], LANE - IDX_DIM - 3 * IDX_HEADS), w.dtype)
    return jnp.concatenate([qa, ka, va, qi, ki, wi, ib, fb, pad, qb, kb, vb, ob], axis=1)


def _rope_tables(pos):
    rot = HEAD_DIM // ROT_FRACTION
    half = rot // 2
    inv_freq = ROPE_THETA ** (-jnp.arange(half, dtype=F32) * 2.0 / rot)
    ang = pos.astype(F32)[:, None] * inv_freq[None, :]
    cos, sin = jnp.cos(ang), jnp.sin(ang)
    t = pos.shape[0]
    one = jnp.ones((t, HEAD_DIM - rot), F32)
    zero_r = jnp.zeros((t, HEAD_DIM - rot), F32)
    zero_h = jnp.zeros((t, half), F32)
    c = jnp.concatenate([cos, cos, one], axis=1)
    s_up = jnp.concatenate([-sin, zero_h, zero_r], axis=1)
    s_dn = jnp.concatenate([zero_h, sin, zero_r], axis=1)
    return tuple(jnp.concatenate([a, a], axis=1) for a in (c, s_up, s_dn))


def _rope_tile(x, c, s_up, s_dn):
    half = HEAD_DIM // ROT_FRACTION // 2
    return x * c + pltpu.roll(x, LANE - half, 1) * s_up + pltpu.roll(x, half, 1) * s_dn


def _head_norm_tile(x, gain, same_head):
    sq = x * x
    hi = sq.astype(BF16)
    lo = (sq - hi.astype(F32)).astype(BF16)
    ss = (jnp.dot(hi, same_head, preferred_element_type=F32)
          + jnp.dot(lo, same_head, preferred_element_type=F32))
    return x * lax.rsqrt(ss * (1.0 / HEAD_DIM) + EPS) * gain


def _aprep_kernel(p_ref, c_ref, su_ref, sd_ref, qg_ref, kg_ref,
                  qpad_ref, qipad_ref, k16_ref, v16_ref, ki16_ref, k32_ref, v32_ref, ki32_ref):
    c, su, sd = c_ref[...], su_ref[...], sd_ref[...]
    tm = c.shape[0]
    row = lax.broadcasted_iota(jnp.int32, (LANE, LANE), 0)
    col = lax.broadcasted_iota(jnp.int32, (LANE, LANE), 1)
    same_head = jnp.where(row // HALF == col // HALF, 1.0, 0.0).astype(BF16)
    lane = lax.broadcasted_iota(jnp.int32, (tm, LANE), 1)
    low = lane < HALF

    def tile(off):
        return p_ref[:, off:off + LANE]

    heads_per_group = A_HEADS // A_KV_HEADS
    for p in range(A_HEADS // 2):
        y = _rope_tile(_head_norm_tile(tile(AB_QA + p * LANE), qg_ref[...], same_head), c, su, sd)
        y = y * HEAD_DIM ** -0.5
        y_sw = pltpu.roll(y, HALF, 1)
        for o in range(2):
            h = 2 * p + o
            g = h // heads_per_group
            src = y if o == g else y_sw
            qpad_ref[:, h * LANE:(h + 1) * LANE] = jnp.where(low if g == 0 else ~low, src, 0.0).astype(BF16)
    k = _rope_tile(_head_norm_tile(tile(AB_KA), kg_ref[...], same_head), c, su, sd)
    k32_ref[...] = k
    k16_ref[...] = k.astype(BF16)
    v = tile(AB_VA)
    v32_ref[...] = v
    v16_ref[...] = v.astype(BF16)
    for p in range(IDX_HEADS // 2):
        y = _rope_tile(tile(AB_QI + p * LANE), c, su, sd)
        y_sw = pltpu.roll(y, HALF, 1)
        qipad_ref[:, (2 * p) * LANE:(2 * p + 1) * LANE] = jnp.where(low, y, 0.0).astype(BF16)
        qipad_ref[:, (2 * p + 1) * LANE:(2 * p + 2) * LANE] = jnp.where(low, y_sw, 0.0).astype(BF16)
    ki = _rope_tile(tile(AB_MISC), c, su, sd)
    ki32_ref[...] = ki[:, :IDX_DIM]
    ki16_ref[...] = jnp.where(low, ki, 0.0).astype(BF16)


def dsa_prep(proj, pos, q_gain, k_gain, t):
    n = proj.shape[0]
    tm = _row_tile(n, 512)
    tabs = _rope_tables(pos)
    if t < tm:
        tabs = tuple(jnp.tile(a, (tm // t, 1)) for a in tabs)
    nt = tabs[0].shape[0] // tm
    a_cols = AB_QB
    tab_spec = pl.BlockSpec((tm, LANE), lambda i: (i % nt, 0))
    gain_spec = pl.BlockSpec((1, LANE), lambda i: (0, 0))

    def out(width, dtype):
        return (jax.ShapeDtypeStruct((n, width), dtype), pl.BlockSpec((tm, width), lambda i: (i, 0)))

    outs = [out(A_HEADS * LANE, BF16), out(IDX_HEADS * LANE, BF16), out(LANE, BF16), out(LANE, BF16),
            out(LANE, BF16), out(LANE, F32), out(LANE, F32), out(IDX_DIM, F32)]
    return pl.pallas_call(
        _aprep_kernel,
        grid=(n // tm,),
        in_specs=[pl.BlockSpec((tm, a_cols), lambda i: (i, 0)), tab_spec, tab_spec, tab_spec,
                  gain_spec, gain_spec],
        out_specs=[o[1] for o in outs],
        out_shape=[o[0] for o in outs],
        compiler_params=pltpu.CompilerParams(dimension_semantics=("parallel",), vmem_limit_bytes=VMEM_LIMIT),
        name="dsa_prep",
    )(proj, *tabs, jnp.tile(q_gain, 2).reshape(1, LANE), jnp.tile(k_gain, 2).reshape(1, LANE))


def _dsa_kernel(q_ref, qi_ref, misc_ref, lim_ref, k_ref, v_ref, ki_ref, o_ref, key_ref, bias_ref,
                *, causal, n_sel):
    qb = q_ref.shape[0]
    kc = KEY_CHUNK
    n_idx = IDX_HEADS
    hpg = A_HEADS // A_KV_HEADS
    if causal:
        nch = ((pl.program_id(1) + 1) * qb + kc - 1) // kc
    else:
        nch = k_ref.shape[0] // kc
    limit = lim_ref[...]
    misc = misc_ref[...]
    wscale = IDX_HEADS ** -0.5 * IDX_DIM ** -0.5
    w = [misc[:, MISC_WI + j:MISC_WI + j + 1] * wscale for j in range(n_idx)]
    qi = jnp.concatenate([qi_ref[:, j * LANE:(j + 1) * LANE] for j in range(n_idx)], axis=0)
    nt = (((1,), (1,)), ((), ()))

    def score_body(c, carry):
        off = pl.multiple_of(c * kc, kc)
        lg = lax.dot_general(qi, ki_ref[pl.ds(off, kc), :], nt, preferred_element_type=F32)
        lg = jnp.maximum(lg, 0.0).reshape(n_idx, qb, kc)
        s = w[0] * lg[0]
        for j in range(1, n_idx):
            s = s + w[j] * lg[j]
        kidx = off + lax.broadcasted_iota(jnp.int32, (qb, kc), 1)
        s = jnp.where(kidx < limit, s, -jnp.inf)
        bits = lax.bitcast_convert_type(s, jnp.int32)
        key_ref[:, pl.ds(off, kc)] = jnp.where(bits < 0, bits ^ 0x7FFFFFFF, bits)
        return carry

    lax.fori_loop(0, nch, score_body, 0)

    def count_ge(cand):
        def body(c, acc):
            off = pl.multiple_of(c * kc, kc)
            m = jnp.where(key_ref[:, pl.ds(off, kc)] >= cand, 1.0, 0.0)
            for t in range(kc // LANE):
                acc = acc + m[:, t * LANE:(t + 1) * LANE]
            return acc

        acc = lax.fori_loop(0, nch, body, jnp.zeros((qb, LANE), F32))
        return jnp.sum(acc, axis=1, keepdims=True)

    want = float(n_sel)
    tau = jnp.where(count_ge(jnp.zeros((qb, 1), jnp.int32)) >= want, 0, INT_MIN).astype(jnp.int32)

    def bisect(i, tau):
        cand = tau | jnp.left_shift(jnp.int32(1), 30 - i)
        return jnp.where(count_ge(cand) >= want, cand, tau)

    tau = lax.fori_loop(0, 31, bisect, tau)

    room = want - count_ge(tau + 1)
    r_i = lax.broadcasted_iota(jnp.int32, (LANE, LANE), 0)
    c_i = lax.broadcasted_iota(jnp.int32, (LANE, LANE), 1)
    prefix_ones = jnp.where(r_i <= c_i, 1.0, 0.0).astype(BF16)

    def bias_body(c, seen):
        off = pl.multiple_of(c * kc, kc)
        for t in range(kc // LANE):
            x = key_ref[:, pl.ds(off + t * LANE, LANE)]
            eq = x == tau
            eqf = jnp.where(eq, 1.0, 0.0)
            rank = jnp.dot(eqf.astype(BF16), prefix_ones, preferred_element_type=F32) + seen
            sel = ((x > tau) | (eq & (rank <= room))) & (x != KEY_OF_NEG_INF)
            bias_ref[:, pl.ds(off + t * LANE, LANE)] = jnp.where(sel, 0.0, MASKED)
            seen = seen + jnp.sum(eqf, axis=1, keepdims=True)
        return seen

    lax.fori_loop(0, nch, bias_body, jnp.zeros((qb, 1), F32))

    lane = lax.broadcasted_iota(jnp.int32, (qb, LANE), 1)
    outs = []
    for g in range(A_KV_HEADS):
        qg = jnp.concatenate([q_ref[:, (hpg * g + h) * LANE:(hpg * g + h + 1) * LANE] for h in range(hpg)], axis=0)

        def att_body(c, carry, qg=qg):
            m, l, acc = carry
            off = pl.multiple_of(c * kc, kc)
            s = lax.dot_general(qg, k_ref[pl.ds(off, kc), :], nt, preferred_element_type=F32)
            s = (s.reshape(hpg, qb, kc) + bias_ref[:, pl.ds(off, kc)][None]).reshape(hpg * qb, kc)
            m_new = jnp.maximum(m, jnp.max(s, axis=1, keepdims=True))
            alpha = jnp.exp(m - m_new)
            p = jnp.exp(s - m_new)
            l = alpha * l + jnp.sum(p, axis=1, keepdims=True)
            acc = alpha * acc + jnp.dot(p.astype(BF16), v_ref[pl.ds(off, kc), :], preferred_element_type=F32)
            return m_new, l, acc

        init = (jnp.full((hpg * qb, 1), MASKED, F32), jnp.zeros((hpg * qb, 1), F32),
                jnp.zeros((hpg * qb, LANE), F32))
        _, l, acc = lax.fori_loop(0, nch, att_body, init)
        og = acc / l
        for h in range(hpg):
            oh = og[h * qb:(h + 1) * qb]
            outs.append(oh if (h % 2) == g else pltpu.roll(oh, HALF, 1))
    for p in range(A_HEADS // 2):
        o_ref[:, p * LANE:(p + 1) * LANE] = jnp.where(lane < HALF, outs[2 * p], outs[2 * p + 1])


def dsa_attention(qpad, qipad, proj, limit, k16, v16, ki16, *, bsz, tq, tk, causal, n_sel):
    qb = min(Q_BLOCK, tq)
    nqb = tq // qb
    assert tk % KEY_CHUNK == 0 and tk >= n_sel
    kern = functools.partial(_dsa_kernel, causal=causal, n_sel=n_sel)

    def qspec(width, col=0):
        return pl.BlockSpec((qb, width), lambda b, i: (b * nqb + i, col))

    def kspec():
        return pl.BlockSpec((tk, LANE), lambda b, i: (b, 0))

    return pl.pallas_call(
        kern,
        grid=(bsz, nqb),
        in_specs=[qspec(A_HEADS * LANE), qspec(IDX_HEADS * LANE), qspec(LANE, AB_MISC // LANE), qspec(1),
                  kspec(), kspec(), kspec()],
        out_specs=qspec(A_HEADS * HEAD_DIM),
        out_shape=jax.ShapeDtypeStruct((bsz * tq, A_HEADS * HEAD_DIM), F32),
        scratch_shapes=[pltpu.VMEM((qb, tk), jnp.int32), pltpu.VMEM((qb, tk), F32)],
        compiler_params=pltpu.CompilerParams(
            dimension_semantics=("parallel", "arbitrary"), vmem_limit_bytes=VMEM_LIMIT),
        name="dsa_attention",
    )(qpad, qipad, proj, limit, k16, v16, ki16)


def _rms_norm(x, gain):
    return x * lax.rsqrt(jnp.mean(x * x, axis=-1, keepdims=True) + EPS) * gain


def _l2_norm(x):
    return x * lax.rsqrt(jnp.sum(x * x, axis=-1, keepdims=True) + EPS)


def _rope_partial(x, pos):
    rot = x.shape[-1] // ROT_FRACTION
    half = rot // 2
    inv_freq = ROPE_THETA ** (-jnp.arange(half, dtype=F32) * 2.0 / rot)
    ang = pos.astype(F32)[:, None] * inv_freq[None, :]
    cos = jnp.cos(ang)[:, None, :]
    sin = jnp.sin(ang)[:, None, :]
    x1 = x[..., :half]
    x2 = x[..., half:rot]
    return jnp.concatenate([x1 * cos - x2 * sin, x1 * sin + x2 * cos, x[..., rot:]], axis=-1)


def _time_chunks(a, chunk):
    b, t = a.shape[:2]
    return jnp.moveaxis(a.reshape(b, t // chunk, chunk, *a.shape[2:]), 1, 0)


def _from_time_chunks(a):
    nc, b, l = a.shape[:3]
    return jnp.moveaxis(a, 0, 1).reshape(b, nc * l, *a.shape[3:])


def _dsa_attend(q, qi, wi, limit, k, v, ki, n_sel):
    bsz, tq = q.shape[:2]
    idx_logits = jnp.einsum('btjd,bsd->btjs', qi, ki)
    score = jnp.einsum('btj,btjs->bts', wi, jax.nn.relu(idx_logits))
    visible = jnp.arange(k.shape[1])[None, :] < limit[:, None]
    score = jnp.where(visible[None], score, -jnp.inf)
    _, sel = lax.top_k(score, n_sel)
    valid = sel < limit[None, :, None]
    b_idx = jnp.arange(bsz)[:, None, None]
    k_sel = k[b_idx, sel]
    v_sel = v[b_idx, sel]
    qg = q.reshape(bsz, tq, A_KV_HEADS, A_HEADS // A_KV_HEADS, HEAD_DIM)
    logits = jnp.einsum('btkgd,btnkd->btkgn', qg, k_sel) * HEAD_DIM ** -0.5
    logits = jnp.where(valid[:, :, None, None, :], logits, -jnp.inf)
    probs = jax.nn.softmax(logits, axis=-1)
    out = jnp.einsum('btkgn,btnkd->btkgd', probs, v_sel)
    return out.reshape(bsz, tq, A_HEADS * HEAD_DIM)


def _dsa_prompt(q, qi, wi, k, v, ki, pos):
    bsz, t = q.shape[:2]
    nb = t // Q_BLOCK
    n_sel = min(TOPK_MAX, t // 4)
    limit = (pos // CHUNK + 1) * CHUNK

    def blocks(a):
        return jnp.moveaxis(a.reshape(bsz, nb, Q_BLOCK, *a.shape[2:]), 1, 0)

    out = lax.map(lambda xs: _dsa_attend(xs[0], xs[1], xs[2], xs[3], k, v, ki, n_sel),
                  (blocks(q), blocks(qi), blocks(wi), limit.reshape(nb, Q_BLOCK)))
    return jnp.moveaxis(out, 0, 1).reshape(bsz, t, A_HEADS * HEAD_DIM)


def _mlstm_chunked(q, k, v, ig, lf, c0, n0, m0, chunk):
    causal = jnp.tril(jnp.ones((chunk, chunk), bool))

    def step(carry, xs):
        c, n, m = carry
        qc, kc, vc, ic, fc = xs
        b = jnp.swapaxes(jnp.cumsum(fc, axis=1), 1, 2)
        ih = jnp.swapaxes(ic, 1, 2)
        dmat = jnp.where(causal, b[..., :, None] - b[..., None, :] + ih[..., None, :], -jnp.inf)
        inter = b + m[..., None]
        mrow = jnp.maximum(inter, dmat.max(-1))
        w_intra = jnp.exp(dmat - mrow[..., None])
        w_state = jnp.exp(inter - mrow)
        scores = jnp.einsum('blhd,bshd->bhls', qc, kc) * w_intra
        num = (jnp.einsum('bhls,bshv->bhlv', scores, vc)
               + w_state[..., None] * jnp.einsum('blhd,bhdv->bhlv', qc, c))
        den = scores.sum(-1) + w_state * jnp.einsum('blhd,bhd->bhl', qc, n)
        h = num / jnp.maximum(jnp.abs(den), jnp.exp(-mrow))[..., None]
        b_end = b[..., -1]
        g = b_end[..., None] - b + ih
        m_new = jnp.maximum(b_end + m, g.max(-1))
        w_k = jnp.exp(g - m_new[..., None])
        keep = jnp.exp(b_end + m - m_new)
        c_new = keep[..., None, None] * c + jnp.einsum('bhs,bshd,bshv->bhdv', w_k, kc, vc)
        n_new = keep[..., None] * n + jnp.einsum('bhs,bshd->bhd', w_k, kc)
        return (c_new, n_new, m_new), jnp.swapaxes(h, 1, 2)

    xs = (_time_chunks(q, chunk), _time_chunks(k, chunk), _time_chunks(v, chunk),
          _time_chunks(ig, chunk), _time_chunks(lf, chunk))
    (c, n, m), hs = lax.scan(step, (c0, n0, m0), xs)
    return _from_time_chunks(hs), c, n, m


def _gdn_chunked(q, k, v, g, beta, s0, chunk):
    qs, ks, vs, gs, bs = (jnp.swapaxes(_time_chunks(a, chunk), 2, 3) for a in (q, k, v, g, beta))
    strict = jnp.tril(jnp.ones((chunk, chunk), bool), -1)
    incl = jnp.tril(jnp.ones((chunk, chunk), bool))
    gc = jnp.cumsum(gs, axis=-1)
    diff = gc[..., :, None] - gc[..., None, :]
    kk = jnp.einsum('nbhld,nbhsd->nbhls', ks, ks)
    a_mat = jnp.where(strict, bs[..., :, None] * kk * jnp.exp(jnp.where(strict, diff, -jnp.inf)), 0.0)
    lhs = a_mat + jnp.eye(chunk, dtype=a_mat.dtype)
    w_v = lax.linalg.triangular_solve(lhs, bs[..., None] * vs, left_side=True, lower=True, unit_diagonal=True)
    w_k = lax.linalg.triangular_solve(lhs, (bs * jnp.exp(gc))[..., None] * ks, left_side=True, lower=True,
                                      unit_diagonal=True)
    qk = jnp.where(incl, jnp.einsum('nbhld,nbhsd->nbhls', qs, ks) * jnp.exp(jnp.where(incl, diff, -jnp.inf)), 0.0)

    def step(s, xs):
        wv, wk, qc, kc, qkc, gcc = xs
        delta = wv - jnp.einsum('bhld,bhdv->bhlv', wk, s)
        o = (jnp.einsum('bhld,bhdv->bhlv', qc * jnp.exp(gcc)[..., None], s)
             + jnp.einsum('bhls,bhsv->bhlv', qkc, delta))
        g_end = gcc[..., -1]
        s_new = (jnp.exp(g_end)[..., None, None] * s
                 + jnp.einsum('bhsd,bhsv->bhdv', kc * jnp.exp(g_end[..., None] - gcc)[..., None], delta))
        return s_new, o

    s_end, outs = lax.scan(step, s0, (w_v, w_k, qs, ks, qk, gc))
    return _from_time_chunks(jnp.swapaxes(outs, 2, 3)), s_end


def _gla_chunked(q, k, v, lf, s0, chunk):
    incl = jnp.tril(jnp.ones((chunk, chunk), bool))[None, :, :, None, None]

    def step(s, xs):
        qc, kc, vc, fc = xs
        b = jnp.cumsum(fc, axis=1)
        decay = jnp.exp(jnp.where(incl, b[:, :, None] - b[:, None, :], -jnp.inf))
        attn = jnp.einsum('blhd,blshd->bhls', qc, decay * kc[:, None])
        o = (jnp.einsum('blhd,bhdv->blhv', qc * jnp.exp(b), s)
             + jnp.einsum('bhls,bshv->blhv', attn, vc))
        b_end = b[:, -1]
        s_new = (jnp.exp(b_end)[..., None] * s
                 + jnp.einsum('bshd,bshv->bhdv', kc * jnp.exp(b_end[:, None] - b), vc))
        return s_new, o

    xs = (_time_chunks(q, chunk), _time_chunks(k, chunk), _time_chunks(v, chunk), _time_chunks(lf, chunk))
    s_end, outs = lax.scan(step, s0, xs)
    return _from_time_chunks(outs), s_end


def _mixer_ab(proj, bsz, t, pos, q_gain, k_gain, gate_bias, hb_gain, cache):
    n = bsz * t
    qpad, qipad, k16, v16, ki16, k32, v32, ki32 = dsa_prep(proj, pos, q_gain, k_gain, t)
    if cache is None:
        limit = jnp.tile((pos // CHUNK + 1) * CHUNK, bsz).reshape(n, 1)
        a_out = dsa_attention(qpad, qipad, proj, limit, k16, v16, ki16, bsz=bsz, tq=t, tk=t,
                              causal=True, n_sel=min(TOPK_MAX, t // 4))
        c0 = jnp.zeros((bsz, B_HEADS, B_QK_DIM, B_V_DIM), F32)
        n0 = jnp.zeros((bsz, B_HEADS, B_QK_DIM), F32)
        m0 = jnp.zeros((bsz, B_HEADS), F32)
        chunk = CHUNK
    else:
        k_c, v_c, ki_c, c0, n0, m0 = cache
        past = k_c.shape[1]
        n_keys = past + t
        tk = -(-n_keys // KEY_CHUNK) * KEY_CHUNK

        def with_cache(c, new):
            c = c.reshape(bsz, past, -1).astype(BF16)
            c = jnp.pad(c, ((0, 0), (0, 0), (0, LANE - c.shape[-1])))
            return jnp.concatenate([c, new.reshape(bsz, t, LANE),
                                    jnp.zeros((bsz, tk - n_keys, LANE), BF16)], axis=1).reshape(bsz * tk, LANE)

        limit = jnp.full((n, 1), n_keys, jnp.int32)
        a_out = dsa_attention(qpad, qipad, proj, limit, with_cache(k_c, k16), with_cache(v_c, v16),
                              with_cache(ki_c, ki16), bsz=bsz, tq=t, tk=tk, causal=False,
                              n_sel=min(TOPK_MAX, n_keys // 4))
        chunk = t
    p3 = proj.reshape(bsz, t, -1)
    qb = p3[..., AB_QB:AB_KB].reshape(bsz, t, B_HEADS, B_QK_DIM)
    kb = p3[..., AB_KB:AB_VB].reshape(bsz, t, B_HEADS, B_QK_DIM) * B_QK_DIM ** -0.5
    vb = p3[..., AB_VB:AB_OB].reshape(bsz, t, B_HEADS, B_V_DIM)
    ob = p3[..., AB_OB:AB_TOTAL]
    ib = p3[..., AB_MISC + MISC_IB:AB_MISC + MISC_IB + B_HEADS]
    fb = p3[..., AB_MISC + MISC_FB:AB_MISC + MISC_FB + B_HEADS]
    ig = ib + gate_bias[0]
    lf = jax.nn.log_sigmoid(fb + gate_bias[1])
    h, c, n_, m = _mlstm_chunked(qb, kb, vb, ig, lf, c0, n0, m0, chunk)
    h = _rms_norm(h, hb_gain) * jax.nn.sigmoid(ob.reshape(bsz, t, B_HEADS, B_V_DIM))
    mixed = jnp.concatenate([a_out, h.reshape(n, -1)], axis=-1)
    st = (k32.reshape(bsz, t, A_KV_HEADS, HEAD_DIM), v32.reshape(bsz, t, A_KV_HEADS, HEAD_DIM),
          ki32.reshape(bsz, t, IDX_DIM), c, n_, m)
    return mixed, st


def _mixer_cd(proj, conv_w, a_log, dt_bias, c_gain, lb, d_gain, cache):
    bsz, t, _ = proj.shape
    qkv, bc, ac, zc, qd, fd, vd, gd = _split_cols(proj, ODD_SPLITS)
    if cache is None:
        conv_prev = jnp.zeros((bsz, CONV_W - 1, qkv.shape[-1]), F32)
        sc0 = jnp.zeros((bsz, C_HEADS, C_DIM, C_DIM), F32)
        sd0 = jnp.zeros((bsz, D_HEADS, D_EXPAND, D_V_DIM), F32)
        chunk = CHUNK
    else:
        sc0, conv_prev, sd0 = cache
        chunk = t
    xc = jnp.concatenate([conv_prev, qkv], axis=1)
    conv = xc[:, 0:t] * conv_w[0]
    for w in range(1, CONV_W):
        conv = conv + xc[:, w:w + t] * conv_w[w]
    conv = jax.nn.silu(conv)
    qc, kc, vc = jnp.split(conv, 3, axis=-1)
    qc = _l2_norm(qc.reshape(bsz, t, C_HEADS, C_DIM)) * C_DIM ** -0.5
    kc = _l2_norm(kc.reshape(bsz, t, C_HEADS, C_DIM))
    vc = vc.reshape(bsz, t, C_HEADS, C_DIM)
    beta = jax.nn.sigmoid(bc)
    g = -jnp.exp(a_log) * jax.nn.softplus(ac + dt_bias)
    oc, sc = _gdn_chunked(qc, kc, vc, g, beta, sc0, chunk)
    oc = _rms_norm(oc, c_gain) * jax.nn.silu(zc.reshape(bsz, t, C_HEADS, C_DIM))
    lb = lb.reshape(D_HEADS, D_EXPAND)
    zf = fd.reshape(bsz, t, D_HEADS, D_EXPAND)
    lf = jnp.logaddexp(jnp.log(lb), jnp.log1p(-lb) + jax.nn.log_sigmoid(zf))
    kd = (1.0 - lb) * jax.nn.sigmoid(-zf)
    qd = jax.nn.silu(qd.reshape(bsz, t, D_HEADS, D_EXPAND))
    vd = vd.reshape(bsz, t, D_HEADS, D_V_DIM)
    od, sd = _gla_chunked(qd, kd, vd, lf, sd0, chunk)
    od = _rms_norm(od, d_gain) * jax.nn.silu(gd.reshape(bsz, t, D_HEADS, D_V_DIM))
    mixed = jnp.concatenate([oc.reshape(bsz, t, -1), od.reshape(bsz, t, -1)], axis=-1)
    return mixed, (sc, xc[:, xc.shape[1] - (CONV_W - 1):], sd)


def _pad_cols(w, mult=LANE):
    pad = (-w.shape[-1]) % mult
    return jnp.pad(w, [(0, 0)] * (w.ndim - 1) + [(0, pad)])


def _trunk(x, pos_offset, cache, prm, wts):
    bsz, t, d = x.shape
    n = bsz * t
    pos = pos_offset + jnp.arange(t, dtype=jnp.int32)
    probs = jax.nn.softmax(prm['d_lb_logits'], axis=0)
    lower_bounds = jnp.cumsum(probs, axis=0) - probs[0]
    xf = x.reshape(n, d)

    lc = None if cache is None else tuple(c[0] for c in cache[:6])
    proj = norm_matmul(xf, prm['norm_mix'][0], wts['w_in_ab'])
    mixed, st_even = _mixer_ab(proj, bsz, t, pos, prm['a_q_gain'][0], prm['a_k_gain'][0],
                               prm['b_gate_bias'][0], prm['b_norm_gain'][0], lc)
    xf = matmul_residual(mixed.reshape(n, d), wts['w_out_ab'], xf)
    xf = ffn_residual(xf, prm['norm_ffn'][0], wts['ffn_w1'], wts['ffn_w3'], wts['ffn_w2'])

    lc = None if cache is None else tuple(c[0] for c in cache[6:])
    proj = norm_matmul(xf, prm['norm_mix'][1], wts['w_in_cd'])[:, :sum(ODD_SPLITS)]
    mixed, st_odd = _mixer_cd(proj.reshape(bsz, t, -1), prm['c_conv_w'][0], prm['c_a_log'][0],
                              prm['c_dt_bias'][0], prm['c_norm_gain'][0], lower_bounds[1],
                              prm['d_norm_gain'][0], lc)
    xf = matmul_residual(mixed.reshape(n, d), wts['w_out_cd'], xf)
    xf = moe_residual(xf, prm['norm_ffn'][1], wts['moe_router'], wts['moe_w1'], wts['moe_w3'], wts['moe_w2'])

    new_state = tuple(s[None] for s in st_even + st_odd)
    return xf.reshape(bsz, t, d), new_state


def kernel(x_prompt, x_sample, cache_a_k, cache_a_v, cache_a_kidx, state_b_c, state_b_n, state_b_m,
           state_c_s, state_c_conv, state_d_s, norm_mix, norm_ffn, w_in_ab, w_out_ab, a_q_gain, a_k_gain,
           b_gate_bias, b_norm_gain, w_in_cd, w_out_cd, c_conv_w, c_a_log, c_dt_bias, c_norm_gain,
           d_lb_logits, d_norm_gain, ffn_w1, ffn_w3, ffn_w2, moe_router, moe_w1, moe_w3, moe_w2):
    prm = dict(norm_mix=norm_mix, norm_ffn=norm_ffn, a_q_gain=a_q_gain, a_k_gain=a_k_gain,
               b_gate_bias=b_gate_bias, b_norm_gain=b_norm_gain, c_conv_w=c_conv_w, c_a_log=c_a_log,
               c_dt_bias=c_dt_bias, c_norm_gain=c_norm_gain, d_lb_logits=d_lb_logits, d_norm_gain=d_norm_gain)
    wts = dict(w_in_ab=_permute_w_in_ab(w_in_ab[0]).astype(BF16), w_out_ab=w_out_ab[0].astype(BF16),
               w_in_cd=_pad_cols(w_in_cd[0]).astype(BF16), w_out_cd=w_out_cd[0].astype(BF16),
               ffn_w1=ffn_w1[0].astype(BF16), ffn_w3=ffn_w3[0].astype(BF16), ffn_w2=ffn_w2[0].astype(BF16),
               moe_router=_pad_cols(moe_router[0]),
               moe_w1=moe_w1[0].astype(BF16), moe_w3=moe_w3[0].astype(BF16), moe_w2=moe_w2[0].astype(BF16))
    cache = (cache_a_k, cache_a_v, cache_a_kidx, state_b_c, state_b_n, state_b_m, state_c_s, state_c_conv, state_d_s)
    y_prompt, st_p = _trunk(x_prompt, 0, None, prm, wts)
    y_sample, st_s = _trunk(x_sample, cache_a_k.shape[2], cache, prm, wts)
    return (y_prompt, y_sample) + st_p + st_s
```

```python
import functools
import math

import jax
import jax.numpy as jnp
import numpy as np
from jax import lax
from jax.experimental import pallas as pl
from jax.experimental.pallas import tpu as pltpu

F32 = jnp.float32
BF16 = jnp.bfloat16

EPS = 1e-6
ROPE_THETA = 500000.0
ROT_FRACTION = 4
CHUNK = 64
A_HEADS, A_KV_HEADS, HEAD_DIM = 8, 2, 64
IDX_HEADS, IDX_DIM = 4, 64
TOPK_MAX, Q_BLOCK = 256, 128
B_HEADS, B_QK_DIM, B_V_DIM = 4, 64, 128
C_HEADS, C_DIM, CONV_W = 4, 128, 4
D_HEADS, D_EXPAND, D_V_DIM = 4, 128, 128
N_EXPERTS, TOP_K_EXPERTS = 8, 2

LANE = 128
VMEM_LIMIT = 48 * 1024 * 1024

EVEN_SPLITS = (A_HEADS * HEAD_DIM, A_KV_HEADS * HEAD_DIM, A_KV_HEADS * HEAD_DIM,
               IDX_HEADS * IDX_DIM, IDX_DIM, IDX_HEADS,
               B_HEADS * B_QK_DIM, B_HEADS * B_QK_DIM, B_HEADS * B_V_DIM,
               B_HEADS, B_HEADS, B_HEADS * B_V_DIM)
ODD_SPLITS = (3 * C_HEADS * C_DIM, C_HEADS, C_HEADS, C_HEADS * C_DIM,
              D_HEADS * D_EXPAND, D_HEADS * D_EXPAND, D_HEADS * D_V_DIM, D_HEADS * D_V_DIM)


def _split_cols(p, widths):
    cuts = [int(c) for c in np.cumsum(widths)[:-1]]
    return jnp.split(p, cuts, axis=-1)


def _row_tile(n, target):
    t = min(n, target)
    while n % t:
        t //= 2
    return t


def _col_tile(n, target):
    best = LANE
    for k in range(1, n // LANE + 1):
        c = k * LANE
        if n % c == 0 and c <= target:
            best = c
    return best


def _rms_rows(x, gain):
    return x * lax.rsqrt(jnp.mean(x * x, axis=-1, keepdims=True) + EPS) * gain


def _norm_matmul_kernel(x_ref, g_ref, w_ref, o_ref, xn_ref):
    @pl.when(pl.program_id(1) == 0)
    def _():
        xn_ref[...] = _rms_rows(x_ref[...], g_ref[...]).astype(BF16)

    o_ref[...] = jnp.dot(xn_ref[...], w_ref[...], preferred_element_type=F32)


def norm_matmul(x, gain, w):
    n, d = x.shape
    m = w.shape[1]
    tm = _row_tile(n, 1024)
    tn = _col_tile(m, 1536)
    return pl.pallas_call(
        _norm_matmul_kernel,
        grid=(n // tm, m // tn),
        in_specs=[pl.BlockSpec((tm, d), lambda i, j: (i, 0)),
                  pl.BlockSpec((1, d), lambda i, j: (0, 0)),
                  pl.BlockSpec((d, tn), lambda i, j: (0, j))],
        out_specs=pl.BlockSpec((tm, tn), lambda i, j: (i, j)),
        out_shape=jax.ShapeDtypeStruct((n, m), F32),
        scratch_shapes=[pltpu.VMEM((tm, d), BF16)],
        compiler_params=pltpu.CompilerParams(
            dimension_semantics=("parallel", "arbitrary"), vmem_limit_bytes=VMEM_LIMIT),
        name="norm_matmul",
    )(x, gain.reshape(1, d), w)


def _matmul_res_kernel(a_ref, b_ref, w_ref, r_ref, o_ref):
    ka = a_ref.shape[1]
    o_ref[...] = (r_ref[...] + jnp.dot(a_ref[...].astype(BF16), w_ref[0:ka, :], preferred_element_type=F32)
                  + jnp.dot(b_ref[...].astype(BF16), w_ref[ka:, :], preferred_element_type=F32))


def matmul_residual(a, b, w, res):
    n, ka = a.shape
    kb = b.shape[1]
    m = w.shape[1]
    tm = _row_tile(n, 1024)
    return pl.pallas_call(
        _matmul_res_kernel,
        grid=(n // tm,),
        in_specs=[pl.BlockSpec((tm, ka), lambda i: (i, 0)),
                  pl.BlockSpec((tm, kb), lambda i: (i, 0)),
                  pl.BlockSpec((ka + kb, m), lambda i: (0, 0)),
                  pl.BlockSpec((tm, m), lambda i: (i, 0))],
        out_specs=pl.BlockSpec((tm, m), lambda i: (i, 0)),
        out_shape=jax.ShapeDtypeStruct((n, m), F32),
        compiler_params=pltpu.CompilerParams(
            dimension_semantics=("parallel",), vmem_limit_bytes=VMEM_LIMIT),
        name="matmul_residual",
    )(a, b, w, res)


def _swiglu_tile(xn, w1, w3):
    h1 = jnp.dot(xn, w1, preferred_element_type=F32)
    h3 = jnp.dot(xn, w3, preferred_element_type=F32)
    return h1 * jax.nn.sigmoid(h1) * h3


def _ffn_kernel(x_ref, g_ref, w1_ref, w3_ref, w2_ref, o_ref, xn_ref):
    @pl.when(pl.program_id(1) == 0)
    def _():
        x = x_ref[...]
        xn_ref[...] = _rms_rows(x, g_ref[...]).astype(BF16)
        o_ref[...] = x

    act = _swiglu_tile(xn_ref[...], w1_ref[...], w3_ref[...])
    o_ref[...] += jnp.dot(act.astype(BF16), w2_ref[...], preferred_element_type=F32)


def ffn_residual(x, gain, w1, w3, w2):
    n, d = x.shape
    f = w1.shape[1]
    tm = _row_tile(n, 1024)
    tf = _col_tile(f, 512)
    return pl.pallas_call(
        _ffn_kernel,
        grid=(n // tm, f // tf),
        in_specs=[pl.BlockSpec((tm, d), lambda i, j: (i, 0)),
                  pl.BlockSpec((1, d), lambda i, j: (0, 0)),
                  pl.BlockSpec((d, tf), lambda i, j: (0, j)),
                  pl.BlockSpec((d, tf), lambda i, j: (0, j)),
                  pl.BlockSpec((tf, d), lambda i, j: (j, 0))],
        out_specs=pl.BlockSpec((tm, d), lambda i, j: (i, 0)),
        out_shape=jax.ShapeDtypeStruct((n, d), F32),
        scratch_shapes=[pltpu.VMEM((tm, d), BF16)],
        compiler_params=pltpu.CompilerParams(
            dimension_semantics=("parallel", "arbitrary"), vmem_limit_bytes=VMEM_LIMIT),
        name="ffn_residual",
    )(x, gain.reshape(1, d), w1, w3, w2)


def _moe_kernel(x_ref, g_ref, r_ref, w1_ref, w3_ref, w2_ref, o_ref, xn_ref, comb_ref):
    e = pl.program_id(1)
    j = pl.program_id(2)

    @pl.when(jnp.logical_and(e == 0, j == 0))
    def _():
        x = x_ref[...]
        xn = _rms_rows(x, g_ref[...])
        xn_ref[...] = xn.astype(BF16)
        o_ref[...] = x
        logits = jnp.dot(xn, r_ref[...], preferred_element_type=F32, precision=lax.Precision.HIGHEST)
        lane = lax.broadcasted_iota(jnp.int32, logits.shape, 1)
        logits = jnp.where(lane < N_EXPERTS, logits, -jnp.inf)
        m1 = jnp.max(logits, axis=-1, keepdims=True)
        i1 = jnp.min(jnp.where(logits == m1, lane, LANE), axis=-1, keepdims=True)
        rest = jnp.where(lane == i1, -jnp.inf, logits)
        m2 = jnp.max(rest, axis=-1, keepdims=True)
        i2 = jnp.min(jnp.where(rest == m2, lane, LANE), axis=-1, keepdims=True)
        e2 = jnp.exp(m2 - m1)
        den = 1.0 + e2
        comb_ref[...] = jnp.where(lane == i1, 1.0 / den, 0.0) + jnp.where(lane == i2, e2 / den, 0.0)

    comb = comb_ref[...]
    lane = lax.broadcasted_iota(jnp.int32, comb.shape, 1)
    c = jnp.sum(jnp.where(lane == e, comb, 0.0), axis=-1, keepdims=True)
    act = _swiglu_tile(xn_ref[...], w1_ref[0], w3_ref[0])
    o_ref[...] += c * jnp.dot(act.astype(BF16), w2_ref[0], preferred_element_type=F32)


def moe_residual(x, gain, router, w1, w3, w2):
    n, d = x.shape
    ne, _, f = w1.shape
    tm = _row_tile(n, 1024)
    tf = _col_tile(f, 512)
    return pl.pallas_call(
        _moe_kernel,
        grid=(n // tm, ne, f // tf),
        in_specs=[pl.BlockSpec((tm, d), lambda i, e, j: (i, 0)),
                  pl.BlockSpec((1, d), lambda i, e, j: (0, 0)),
                  pl.BlockSpec((d, LANE), lambda i, e, j: (0, 0)),
                  pl.BlockSpec((1, d, tf), lambda i, e, j: (e, 0, j)),
                  pl.BlockSpec((1, d, tf), lambda i, e, j: (e, 0, j)),
                  pl.BlockSpec((1, tf, d), lambda i, e, j: (e, j, 0))],
        out_specs=pl.BlockSpec((tm, d), lambda i, e, j: (i, 0)),
        out_shape=jax.ShapeDtypeStruct((n, d), F32),
        scratch_shapes=[pltpu.VMEM((tm, d), BF16), pltpu.VMEM((tm, LANE), F32)],
        compiler_params=pltpu.CompilerParams(
            dimension_semantics=("parallel", "arbitrary", "arbitrary"), vmem_limit_bytes=VMEM_LIMIT),
        name="moe_residual",
    )(x, gain.reshape(1, d), router, w1, w3, w2)


AB_QA, AB_VB, AB_OB, AB_QI, AB_QB, AB_KB, AB_KA, AB_VA, AB_MISC, AB_TOTAL = (
    0, 512, 1024, 1536, 1792, 2048, 2304, 2432, 2560, 2688)
MISC_WI, MISC_IB, MISC_FB = 64, 68, 72
HALF = LANE // 2
KEY_CHUNK = 512
MASKED = -1e30
KEY_OF_NEG_INF = -2139095041
INT_MIN = -2147483648


def _permute_w_in_ab(w):
    qa, ka, va, qi, ki, wi, qb, kb, vb, ib, fb, ob = _split_cols(w, EVEN_SPLITS)
    pad = jnp.zeros((w.shape[0], LANE - IDX_DIM - 3 * IDX_HEADS), w.dtype)
    return jnp.concatenate([qa, vb, ob, qi, qb, kb, ka, va, ki, wi, ib, fb, pad], axis=1)


def _rope_tables(pos):
    rot = HEAD_DIM // ROT_FRACTION
    half = rot // 2
    inv_freq = ROPE_THETA ** (-jnp.arange(half, dtype=F32) * 2.0 / rot)
    ang = pos.astype(F32)[:, None] * inv_freq[None, :]
    cos, sin = jnp.cos(ang), jnp.sin(ang)
    t = pos.shape[0]
    one = jnp.ones((t, HEAD_DIM - rot), F32)
    zero_r = jnp.zeros((t, HEAD_DIM - rot), F32)
    zero_h = jnp.zeros((t, half), F32)
    c = jnp.concatenate([cos, cos, one], axis=1)
    s_up = jnp.concatenate([-sin, zero_h, zero_r], axis=1)
    s_dn = jnp.concatenate([zero_h, sin, zero_r], axis=1)
    return tuple(jnp.concatenate([a, a], axis=1) for a in (c, s_up, s_dn))


def _rope_tile(x, c, s_up, s_dn):
    half = HEAD_DIM // ROT_FRACTION // 2
    return x * c + pltpu.roll(x, LANE - half, 1) * s_up + pltpu.roll(x, half, 1) * s_dn


def _head_norm_tile(x, gain, same_head):
    sq = x * x
    hi = sq.astype(BF16)
    lo = (sq - hi.astype(F32)).astype(BF16)
    ss = (jnp.dot(hi, same_head, preferred_element_type=F32)
          + jnp.dot(lo, same_head, preferred_element_type=F32))
    return x * lax.rsqrt(ss * (1.0 / HEAD_DIM) + EPS) * gain


def _aprep_kernel(qa_ref, ka_ref, va_ref, qi_ref, misc_ref, c_ref, su_ref, sd_ref, qg_ref, kg_ref,
                  qpad_ref, qipad_ref, k16_ref, v16_ref, ki16_ref, k32_ref, v32_ref, ki32_ref):
    c, su, sd = c_ref[...], su_ref[...], sd_ref[...]
    tm = c.shape[0]
    row = lax.broadcasted_iota(jnp.int32, (LANE, LANE), 0)
    col = lax.broadcasted_iota(jnp.int32, (LANE, LANE), 1)
    same_head = jnp.where(row // HALF == col // HALF, 1.0, 0.0).astype(BF16)
    lane = lax.broadcasted_iota(jnp.int32, (tm, LANE), 1)
    low = lane < HALF

    heads_per_group = A_HEADS // A_KV_HEADS
    for p in range(A_HEADS // 2):
        y = _rope_tile(_head_norm_tile(qa_ref[:, p * LANE:(p + 1) * LANE], qg_ref[...], same_head), c, su, sd)
        y = y * HEAD_DIM ** -0.5
        y_sw = pltpu.roll(y, HALF, 1)
        for o in range(2):
            h = 2 * p + o
            g = h // heads_per_group
            src = y if o == g else y_sw
            qpad_ref[:, h * LANE:(h + 1) * LANE] = jnp.where(low if g == 0 else ~low, src, 0.0).astype(BF16)
    k = _rope_tile(_head_norm_tile(ka_ref[...], kg_ref[...], same_head), c, su, sd)
    k32_ref[...] = k
    k16_ref[...] = k.astype(BF16)
    v = va_ref[...]
    v32_ref[...] = v
    v16_ref[...] = v.astype(BF16)
    for p in range(IDX_HEADS // 2):
        y = _rope_tile(qi_ref[:, p * LANE:(p + 1) * LANE], c, su, sd)
        y_sw = pltpu.roll(y, HALF, 1)
        qipad_ref[:, (2 * p) * LANE:(2 * p + 1) * LANE] = jnp.where(low, y, 0.0).astype(BF16)
        qipad_ref[:, (2 * p + 1) * LANE:(2 * p + 2) * LANE] = jnp.where(low, y_sw, 0.0).astype(BF16)
    ki = _rope_tile(misc_ref[...], c, su, sd)
    ki32_ref[...] = ki[:, :IDX_DIM]
    ki16_ref[...] = jnp.where(low, ki, 0.0).astype(BF16)


def dsa_prep(proj, pos, q_gain, k_gain, t):
    n = proj.shape[0]
    tm = _row_tile(n, 512)
    tabs = _rope_tables(pos)
    if t < tm:
        tabs = tuple(jnp.tile(a, (tm // t, 1)) for a in tabs)
    nt = tabs[0].shape[0] // tm
    tab_spec = pl.BlockSpec((tm, LANE), lambda i: (i % nt, 0))
    gain_spec = pl.BlockSpec((1, LANE), lambda i: (0, 0))

    def col(width, offset):
        return pl.BlockSpec((tm, width), lambda i: (i, offset // width))

    def out(width, dtype):
        return (jax.ShapeDtypeStruct((n, width), dtype), pl.BlockSpec((tm, width), lambda i: (i, 0)))

    outs = [out(A_HEADS * LANE, BF16), out(IDX_HEADS * LANE, BF16), out(LANE, BF16), out(LANE, BF16),
            out(LANE, BF16), out(LANE, F32), out(LANE, F32), out(IDX_DIM, F32)]
    return pl.pallas_call(
        _aprep_kernel,
        grid=(n // tm,),
        in_specs=[col(A_HEADS * HEAD_DIM, AB_QA), col(LANE, AB_KA), col(LANE, AB_VA),
                  col(IDX_HEADS * IDX_DIM, AB_QI), col(LANE, AB_MISC), tab_spec, tab_spec, tab_spec,
                  gain_spec, gain_spec],
        out_specs=[o[1] for o in outs],
        out_shape=[o[0] for o in outs],
        compiler_params=pltpu.CompilerParams(dimension_semantics=("parallel",), vmem_limit_bytes=VMEM_LIMIT),
        name="dsa_prep",
    )(proj, proj, proj, proj, proj, *tabs, jnp.tile(q_gain, 2).reshape(1, LANE), jnp.tile(k_gain, 2).reshape(1, LANE))


def _dsa_kernel(q_ref, qi_ref, misc_ref, lim_ref, k_ref, v_ref, ki_ref, o_ref, key_ref, bias_ref,
                *, causal, n_sel):
    qb = q_ref.shape[0]
    kc = KEY_CHUNK
    n_idx = IDX_HEADS
    hpg = A_HEADS // A_KV_HEADS
    if causal:
        nch = ((pl.program_id(1) + 1) * qb + kc - 1) // kc
    else:
        nch = k_ref.shape[0] // kc
    limit = lim_ref[...]
    misc = misc_ref[...]
    wscale = IDX_HEADS ** -0.5 * IDX_DIM ** -0.5
    w = [misc[:, MISC_WI + j:MISC_WI + j + 1] * wscale for j in range(n_idx)]
    qi = jnp.concatenate([qi_ref[:, j * LANE:(j + 1) * LANE] for j in range(n_idx)], axis=0)
    nt = (((1,), (1,)), ((), ()))

    def score_body(c, carry):
        off = pl.multiple_of(c * kc, kc)
        lg = lax.dot_general(qi, ki_ref[pl.ds(off, kc), :], nt, preferred_element_type=F32)
        lg = jnp.maximum(lg, 0.0).reshape(n_idx, qb, kc)
        s = w[0] * lg[0]
        for j in range(1, n_idx):
            s = s + w[j] * lg[j]
        kidx = off + lax.broadcasted_iota(jnp.int32, (qb, kc), 1)
        s = jnp.where(kidx < limit, s, -jnp.inf)
        bits = lax.bitcast_convert_type(s, jnp.int32)
        key_ref[:, pl.ds(off, kc)] = jnp.where(bits < 0, bits ^ 0x7FFFFFFF, bits)
        return carry

    lax.fori_loop(0, nch, score_body, 0)

    def count_ge(cand):
        def body(c, acc):
            off = pl.multiple_of(c * kc, kc)
            m = jnp.where(key_ref[:, pl.ds(off, kc)] >= cand, 1.0, 0.0)
            for t in range(kc // LANE):
                acc = acc + m[:, t * LANE:(t + 1) * LANE]
            return acc

        acc = lax.fori_loop(0, nch, body, jnp.zeros((qb, LANE), F32))
        return jnp.sum(acc, axis=1, keepdims=True)

    want = float(n_sel)
    tau = jnp.where(count_ge(jnp.zeros((qb, 1), jnp.int32)) >= want, 0, INT_MIN).astype(jnp.int32)

    def bisect(i, tau):
        cand = tau | jnp.left_shift(jnp.int32(1), 30 - i)
        return jnp.where(count_ge(cand) >= want, cand, tau)

    tau = lax.fori_loop(0, 31, bisect, tau)

    room = want - count_ge(tau + 1)
    r_i = lax.broadcasted_iota(jnp.int32, (LANE, LANE), 0)
    c_i = lax.broadcasted_iota(jnp.int32, (LANE, LANE), 1)
    prefix_ones = jnp.where(r_i <= c_i, 1.0, 0.0).astype(BF16)

    def bias_body(c, seen):
        off = pl.multiple_of(c * kc, kc)
        for t in range(kc // LANE):
            x = key_ref[:, pl.ds(off + t * LANE, LANE)]
            eq = x == tau
            eqf = jnp.where(eq, 1.0, 0.0)
            rank = jnp.dot(eqf.astype(BF16), prefix_ones, preferred_element_type=F32) + seen
            sel = ((x > tau) | (eq & (rank <= room))) & (x != KEY_OF_NEG_INF)
            bias_ref[:, pl.ds(off + t * LANE, LANE)] = jnp.where(sel, 0.0, MASKED)
            seen = seen + jnp.sum(eqf, axis=1, keepdims=True)
        return seen

    lax.fori_loop(0, nch, bias_body, jnp.zeros((qb, 1), F32))

    lane = lax.broadcasted_iota(jnp.int32, (qb, LANE), 1)
    outs = []
    for g in range(A_KV_HEADS):
        qg = jnp.concatenate([q_ref[:, (hpg * g + h) * LANE:(hpg * g + h + 1) * LANE] for h in range(hpg)], axis=0)

        def att_body(c, carry, qg=qg):
            m, l, acc = carry
            off = pl.multiple_of(c * kc, kc)
            s = lax.dot_general(qg, k_ref[pl.ds(off, kc), :], nt, preferred_element_type=F32)
            s = (s.reshape(hpg, qb, kc) + bias_ref[:, pl.ds(off, kc)][None]).reshape(hpg * qb, kc)
            m_new = jnp.maximum(m, jnp.max(s, axis=1, keepdims=True))
            alpha = jnp.exp(m - m_new)
            p = jnp.exp(s - m_new)
            l = alpha * l + jnp.sum(p, axis=1, keepdims=True)
            acc = alpha * acc + jnp.dot(p.astype(BF16), v_ref[pl.ds(off, kc), :], preferred_element_type=F32)
            return m_new, l, acc

        init = (jnp.full((hpg * qb, 1), MASKED, F32), jnp.zeros((hpg * qb, 1), F32),
                jnp.zeros((hpg * qb, LANE), F32))
        _, l, acc = lax.fori_loop(0, nch, att_body, init)
        og = acc / l
        for h in range(hpg):
            oh = og[h * qb:(h + 1) * qb]
            outs.append(oh if (h % 2) == g else pltpu.roll(oh, HALF, 1))
    for p in range(A_HEADS // 2):
        o_ref[:, p * LANE:(p + 1) * LANE] = jnp.where(lane < HALF, outs[2 * p], outs[2 * p + 1])


def dsa_attention(qpad, qipad, proj, limit, k16, v16, ki16, *, bsz, tq, tk, causal, n_sel):
    qb = min(Q_BLOCK, tq)
    nqb = tq // qb
    assert tk % KEY_CHUNK == 0 and tk >= n_sel
    kern = functools.partial(_dsa_kernel, causal=causal, n_sel=n_sel)

    def qspec(width, col=0):
        return pl.BlockSpec((qb, width), lambda b, i: (b * nqb + i, col))

    def kspec():
        return pl.BlockSpec((tk, LANE), lambda b, i: (b, 0))

    return pl.pallas_call(
        kern,
        grid=(bsz, nqb),
        in_specs=[qspec(A_HEADS * LANE), qspec(IDX_HEADS * LANE), qspec(LANE, AB_MISC // LANE), qspec(1),
                  kspec(), kspec(), kspec()],
        out_specs=qspec(A_HEADS * HEAD_DIM),
        out_shape=jax.ShapeDtypeStruct((bsz * tq, A_HEADS * HEAD_DIM), F32),
        scratch_shapes=[pltpu.VMEM((qb, tk), jnp.int32), pltpu.VMEM((qb, tk), F32)],
        compiler_params=pltpu.CompilerParams(
            dimension_semantics=("parallel", "arbitrary"), vmem_limit_bytes=VMEM_LIMIT),
        name="dsa_attention",
    )(qpad, qipad, proj, limit, k16, v16, ki16)


SUB = 16


def _dot(a, b):
    return jnp.dot(a, b, preferred_element_type=F32)


def _dot_nt(a, b):
    return lax.dot_general(a, b, (((1,), (1,)), ((), ())), preferred_element_type=F32)


def _dot_tn(a, b):
    return lax.dot_general(a, b, (((0,), (0,)), ((), ())), preferred_element_type=F32)


def _split2(x):
    hi = x.astype(BF16)
    return hi, (x - hi.astype(F32)).astype(BF16)


def _split3(x):
    hi = x.astype(BF16)
    r = x - hi.astype(F32)
    mid = r.astype(BF16)
    return hi, mid, (r - mid.astype(F32)).astype(BF16)


def _cumsum_rows(x, tril16):
    hi, mid, lo = _split3(x)
    return _dot(tril16, hi) + _dot(tril16, mid) + _dot(tril16, lo)


def _dot_f32(a, b):
    ah, al = _split2(a)
    bh, bl = _split2(b)
    return _dot(ah, bh) + (_dot(ah, bl) + _dot(al, bh))


def _tri_mask(n, strict=False):
    r = lax.broadcasted_iota(jnp.int32, (n, n), 0)
    c = lax.broadcasted_iota(jnp.int32, (n, n), 1)
    return r > c if strict else r >= c


def _rows_to_lanes(x):
    rows = x.shape[0]
    if rows < LANE:
        x = jnp.concatenate([x, jnp.zeros((LANE - rows, LANE), x.dtype)], axis=0)
    return x.T


def _chunk_call(kern, *, bsz, nc, rows, ins, outs, scratch, name):
    def spec(a, kind, width, offset):
        if kind == 'rows':
            return pl.BlockSpec((rows, width), lambda b, c: (b * nc + c, offset // width))
        if kind == 'batch':
            return pl.BlockSpec((1,) + tuple(a.shape[1:]), lambda b, c: (b,) + (0,) * (len(a.shape) - 1))
        return pl.BlockSpec(tuple(a.shape), lambda b, c: (0,) * len(a.shape))

    return pl.pallas_call(
        kern,
        grid=(bsz, nc),
        in_specs=[spec(*i) for i in ins],
        out_specs=[spec(*o) for o in outs],
        out_shape=[o[0] for o in outs],
        scratch_shapes=scratch,
        compiler_params=pltpu.CompilerParams(
            dimension_semantics=("parallel", "arbitrary"), vmem_limit_bytes=VMEM_LIMIT),
        name=name,
    )(*[i[0] for i in ins])


def _mlstm_kernel(q_ref, k_ref, v_ref, og_ref, misc_ref, gb_ref, gain_ref, c0_ref, n0_ref, m0_ref,
                  h_ref, c_out_ref, n_out_ref, m_out_ref, c_scr, n_scr, m_scr):
    ci = pl.program_id(1)

    @pl.when(ci == 0)
    def _():
        c_scr[...] = c0_ref[0]
        n_scr[...] = n0_ref[0]
        m_scr[...] = m0_ref[0]

    rows = q_ref.shape[0]
    incl = _tri_mask(rows)
    tril16 = jnp.where(incl, 1.0, 0.0).astype(BF16)
    gates = misc_ref[...] + gb_ref[...]
    bcum = _cumsum_rows(jax.nn.log_sigmoid(gates), tril16)
    gates_t = _rows_to_lanes(gates)
    bcum_t = _rows_to_lanes(bcum)
    lane = lax.broadcasted_iota(jnp.int32, (rows, LANE), 1)
    lane1 = lax.broadcasted_iota(jnp.int32, (1, LANE), 1)
    row_sq = lax.broadcasted_iota(jnp.int32, (LANE, LANE), 0)
    m_all = m_scr[...]
    m_next = m_all
    gain = gain_ref[...]
    for p in range(B_HEADS // 2):
        cs = c_scr[p]
        cs16 = cs.astype(BF16)
        n_row = n_scr[:, p * LANE:(p + 1) * LANE]
        qt = q_ref[:, p * LANE:(p + 1) * LANE]
        kt = k_ref[:, p * LANE:(p + 1) * LANE]
        upd = jnp.zeros((LANE, LANE), F32)
        ksum = jnp.zeros((1, LANE), F32)
        keeps = []
        for o in range(2):
            h = 2 * p + o
            mine = (lane >= o * HALF) & (lane < (o + 1) * HALF)
            qm = jnp.where(mine, qt, 0.0)
            km = jnp.where(mine, kt, 0.0) * B_QK_DIM ** -0.5
            qm16 = qm.astype(BF16)
            b_col = bcum[:, MISC_FB + h:MISC_FB + h + 1]
            i_col = gates[:, MISC_IB + h:MISC_IB + h + 1]
            b_row = bcum_t[MISC_FB + h:MISC_FB + h + 1, 0:rows]
            i_row = gates_t[MISC_IB + h:MISC_IB + h + 1, 0:rows]
            m_h = m_all[:, h:h + 1]
            dmat = jnp.where(incl, b_col - b_row + i_row, -jnp.inf)
            inter = b_col + m_h
            mrow = jnp.maximum(inter, jnp.max(dmat, axis=1, keepdims=True))
            w_state = jnp.exp(inter - mrow)
            scores = _dot_nt(qm16, km.astype(BF16)) * jnp.exp(dmat - mrow)
            vh = v_ref[:, h * LANE:(h + 1) * LANE]
            vh16 = vh.astype(BF16)
            num = _dot(scores.astype(BF16), vh16) + w_state * _dot(qm16, cs16)
            den = (jnp.sum(scores, axis=1, keepdims=True)
                   + w_state * jnp.sum(qm * n_row, axis=1, keepdims=True))
            hh = num / jnp.maximum(jnp.abs(den), jnp.exp(-mrow))
            h_ref[:, h * LANE:(h + 1) * LANE] = _rms_rows(hh, gain) * jax.nn.sigmoid(og_ref[:, h * LANE:(h + 1) * LANE])
            b_end = b_col[rows - 1:rows]
            g_col = b_end - b_col + i_col
            m_new = jnp.maximum(b_end + m_h, jnp.max(g_col, axis=0, keepdims=True))
            kw = km * jnp.exp(g_col - m_new)
            upd = upd + _dot_tn(kw.astype(BF16), vh16)
            ksum = ksum + jnp.sum(kw, axis=0, keepdims=True)
            keeps.append(jnp.exp(b_end + m_h - m_new))
            m_next = jnp.where(lane1 == h, m_new, m_next)
        c_scr[p] = jnp.where(row_sq < HALF, keeps[0], keeps[1]) * cs + upd
        n_scr[:, p * LANE:(p + 1) * LANE] = jnp.where(lane1 < HALF, keeps[0], keeps[1]) * n_row + ksum
    m_scr[...] = m_next

    @pl.when(ci == pl.num_programs(1) - 1)
    def _():
        c_out_ref[0] = c_scr[...]
        n_out_ref[0] = n_scr[...]
        m_out_ref[0] = m_scr[...]


def mlstm_mixer(proj, gate_bias, gain, c0, n0, m0, *, bsz, t, chunk):
    nc = t // chunk
    pairs = B_HEADS // 2
    gb = jnp.zeros((1, LANE), F32)
    gb = gb.at[0, MISC_IB:MISC_IB + B_HEADS].set(gate_bias[0]).at[0, MISC_FB:MISC_FB + B_HEADS].set(gate_bias[1])
    c0 = c0.reshape(bsz, pairs, LANE, B_V_DIM)
    n0 = n0.reshape(bsz, 1, B_HEADS * B_QK_DIM)
    m0 = jnp.pad(m0, ((0, 0), (0, LANE - B_HEADS))).reshape(bsz, 1, LANE)
    wq = B_HEADS * B_QK_DIM
    wv = B_HEADS * B_V_DIM
    h, c, n, m = _chunk_call(
        _mlstm_kernel, bsz=bsz, nc=nc, rows=chunk,
        ins=[(proj, 'rows', wq, AB_QB), (proj, 'rows', wq, AB_KB), (proj, 'rows', wv, AB_VB),
             (proj, 'rows', wv, AB_OB), (proj, 'rows', LANE, AB_MISC), (gb, 'const', 0, 0),
             (gain.reshape(1, B_V_DIM), 'const', 0, 0), (c0, 'batch', 0, 0), (n0, 'batch', 0, 0),
             (m0, 'batch', 0, 0)],
        outs=[(jax.ShapeDtypeStruct((bsz * t, wv), F32), 'rows', wv, 0),
              (jax.ShapeDtypeStruct(c0.shape, F32), 'batch', 0, 0),
              (jax.ShapeDtypeStruct(n0.shape, F32), 'batch', 0, 0),
              (jax.ShapeDtypeStruct(m0.shape, F32), 'batch', 0, 0)],
        scratch=[pltpu.VMEM((pairs, LANE, B_V_DIM), F32), pltpu.VMEM((1, wq), F32), pltpu.VMEM((1, LANE), F32)],
        name="mlstm_mixer")
    return (h, c.reshape(bsz, B_HEADS, B_QK_DIM, B_V_DIM), n.reshape(bsz, B_HEADS, B_QK_DIM),
            m.reshape(bsz, LANE)[:, :B_HEADS])


CD_QKV, CD_ZC, CD_QD, CD_FD, CD_VD, CD_GD, CD_MISC, CD_TOTAL = 0, 1536, 2048, 2560, 3072, 3584, 4096, 4224
MISC_BC, MISC_AC = 0, 4
TAIL = 8


def _permute_w_in_cd(w):
    qkv, bc, ac, zc, qd, fd, vd, gd = _split_cols(w, ODD_SPLITS)
    pad = jnp.zeros((w.shape[0], LANE - 2 * C_HEADS), w.dtype)
    return jnp.concatenate([qkv, zc, qd, fd, vd, gd, bc, ac, pad], axis=1)


def _gdn_kernel(qkv_ref, z_ref, misc_ref, cw_ref, alog_ref, dt_ref, gain_ref, s0_ref, tail0_ref,
                o_ref, s_out_ref, st_scr, tail_scr):
    ci = pl.program_id(1)

    @pl.when(ci == 0)
    def _():
        for h in range(C_HEADS):
            st_scr[h] = s0_ref[0, h].T
        tail_scr[...] = tail0_ref[0]

    rows = qkv_ref.shape[0]
    width = qkv_ref.shape[1]
    x = qkv_ref[...]
    tail = tail_scr[...]
    row8 = lax.broadcasted_iota(jnp.int32, (TAIL, width), 0)
    acc = x * cw_ref[CONV_W - 1:CONV_W, :]
    for back in range(1, CONV_W):
        rolled = pltpu.roll(x, back, 0)
        first = jnp.where(row8 < back, pltpu.roll(tail, back, 0), rolled[0:TAIL])
        shifted = first if rows == TAIL else jnp.concatenate([first, rolled[TAIL:]], axis=0)
        acc = acc + shifted * cw_ref[CONV_W - 1 - back:CONV_W - back, :]
    tail_scr[...] = x[rows - TAIL:rows]
    conv = acc * jax.nn.sigmoid(acc)

    incl = _tri_mask(rows)
    strict = _tri_mask(rows, strict=True)
    tril16 = jnp.where(incl, 1.0, 0.0).astype(BF16)
    eye = jnp.where(incl & ~strict, 1.0, 0.0)
    misc = misc_ref[...]
    beta_t = jax.nn.sigmoid(misc)
    g_t = -jnp.exp(alog_ref[...]) * jax.nn.softplus(misc + dt_ref[...])
    gcum = _cumsum_rows(g_t, tril16)
    gcum_t = _rows_to_lanes(gcum)
    gain = gain_ref[...]
    hd = C_HEADS * C_DIM
    n_square = int(math.log2(rows)) - 1
    for h in range(C_HEADS):
        qh = conv[:, h * C_DIM:(h + 1) * C_DIM]
        kh = conv[:, hd + h * C_DIM:hd + (h + 1) * C_DIM]
        vh = conv[:, 2 * hd + h * C_DIM:2 * hd + (h + 1) * C_DIM]
        qh = qh * lax.rsqrt(jnp.sum(qh * qh, axis=-1, keepdims=True) + EPS) * C_DIM ** -0.5
        kh = kh * lax.rsqrt(jnp.sum(kh * kh, axis=-1, keepdims=True) + EPS)
        beta = beta_t[:, MISC_BC + h:MISC_BC + h + 1]
        gc = gcum[:, MISC_AC + h:MISC_AC + h + 1]
        gc_row = gcum_t[MISC_AC + h:MISC_AC + h + 1, 0:rows]
        decay = jnp.exp(jnp.where(incl, gc - gc_row, -jnp.inf))
        k16 = kh.astype(BF16)
        a_mat = jnp.where(strict, beta * _dot_nt(k16, k16) * decay, 0.0)
        power = -a_mat
        inv = eye + power
        for _ in range(n_square):
            power = _dot_f32(power, power)
            inv = inv + _dot_f32(inv, power)
        inv_hi, inv_lo = _split2(inv)
        rhs_v = (beta * vh).astype(BF16)
        rhs_k = (beta * jnp.exp(gc) * kh).astype(BF16)
        w_v = _dot(inv_hi, rhs_v) + _dot(inv_lo, rhs_v)
        w_k = _dot(inv_hi, rhs_k) + _dot(inv_lo, rhs_k)
        qk = _dot_nt(qh.astype(BF16), k16) * decay
        st = st_scr[h]
        st16 = st.astype(BF16)
        delta = w_v - _dot_nt(w_k.astype(BF16), st16)
        d16 = delta.astype(BF16)
        out = _dot_nt((qh * jnp.exp(gc)).astype(BF16), st16) + _dot(qk.astype(BF16), d16)
        g_end = gc[rows - 1:rows]
        st_scr[h] = jnp.exp(g_end) * st + _dot_tn(d16, (kh * jnp.exp(g_end - gc)).astype(BF16))
        z = z_ref[:, h * C_DIM:(h + 1) * C_DIM]
        o_ref[:, h * C_DIM:(h + 1) * C_DIM] = _rms_rows(out, gain) * (z * jax.nn.sigmoid(z))

    @pl.when(ci == pl.num_programs(1) - 1)
    def _():
        for h in range(C_HEADS):
            s_out_ref[0, h] = st_scr[h].T


def gdn_mixer(proj, conv_w, a_log, dt_bias, gain, s0, conv_prev, *, bsz, t, chunk):
    nc = t // chunk
    hd = C_HEADS * C_DIM
    lanes = jnp.zeros((1, LANE), F32)
    alog = lanes.at[0, MISC_AC:MISC_AC + C_HEADS].set(a_log)
    dt = lanes.at[0, MISC_AC:MISC_AC + C_HEADS].set(dt_bias)
    tail0 = jnp.pad(conv_prev, ((0, 0), (TAIL - (CONV_W - 1), 0), (0, 0)))
    out, s = _chunk_call(
        _gdn_kernel, bsz=bsz, nc=nc, rows=chunk,
        ins=[(proj, 'rows', 3 * hd, CD_QKV), (proj, 'rows', hd, CD_ZC), (proj, 'rows', LANE, CD_MISC),
             (conv_w, 'const', 0, 0), (alog, 'const', 0, 0), (dt, 'const', 0, 0),
             (gain.reshape(1, C_DIM), 'const', 0, 0), (s0, 'batch', 0, 0), (tail0, 'batch', 0, 0)],
        outs=[(jax.ShapeDtypeStruct((bsz * t, hd), F32), 'rows', hd, 0),
              (jax.ShapeDtypeStruct(s0.shape, F32), 'batch', 0, 0)],
        scratch=[pltpu.VMEM((C_HEADS, C_DIM, C_DIM), F32), pltpu.VMEM((TAIL, 3 * hd), F32)],
        name="gdn_mixer")
    return out, s


def _hgrn2_kernel(q_ref, f_ref, v_ref, g_ref, lb_ref, gain_ref, s0_ref, o_ref, s_out_ref, st_scr):
    ci = pl.program_id(1)

    @pl.when(ci == 0)
    def _():
        for h in range(D_HEADS):
            st_scr[h] = s0_ref[0, h].T

    rows = q_ref.shape[0]
    tril16 = jnp.where(_tri_mask(rows), 1.0, 0.0).astype(BF16)
    lb = lb_ref[...]
    zf = f_ref[...]
    logf = jnp.logaddexp(jnp.log(lb), jnp.log1p(-lb) + jax.nn.log_sigmoid(zf))
    kd = (1.0 - lb) * jax.nn.sigmoid(-zf)
    qx = q_ref[...]
    qd = qx * jax.nn.sigmoid(qx)
    bcum = _cumsum_rows(logf, tril16)
    gain = gain_ref[...]
    row_sub = lax.broadcasted_iota(jnp.int32, (SUB, 1), 0)
    for h in range(D_HEADS):
        sl = slice(h * D_EXPAND, (h + 1) * D_EXPAND)
        q, k, b = qd[:, sl], kd[:, sl], bcum[:, sl]
        v = v_ref[:, h * D_V_DIM:(h + 1) * D_V_DIM]
        v16 = v.astype(BF16)
        st = st_scr[h]
        inter = _dot_nt((q * jnp.exp(b)).astype(BF16), st.astype(BF16))
        blocks = []
        for i in range(rows // SUB):
            r0 = i * SUB
            qi, bi = q[r0:r0 + SUB], b[r0:r0 + SUB]
            oi = inter[r0:r0 + SUB]
            if i > 0:
                ref = b[r0 - 1:r0]
                att = _dot_nt((qi * jnp.exp(bi - ref)).astype(BF16),
                              (k[0:r0] * jnp.exp(ref - b[0:r0])).astype(BF16))
                oi = oi + _dot(att.astype(BF16), v16[0:r0])
            for s in range(SUB):
                r = r0 + s
                a = jnp.sum(qi * jnp.exp(bi - b[r:r + 1]) * k[r:r + 1], axis=1, keepdims=True)
                oi = oi + jnp.where(row_sub >= s, a, 0.0) * v[r:r + 1]
            blocks.append(oi)
        out = blocks[0] if len(blocks) == 1 else jnp.concatenate(blocks, axis=0)
        b_end = b[rows - 1:rows]
        st_scr[h] = jnp.exp(b_end) * st + _dot_tn(v16, (k * jnp.exp(b_end - b)).astype(BF16))
        g = g_ref[:, h * D_V_DIM:(h + 1) * D_V_DIM]
        o_ref[:, h * D_V_DIM:(h + 1) * D_V_DIM] = _rms_rows(out, gain) * (g * jax.nn.sigmoid(g))

    @pl.when(ci == pl.num_programs(1) - 1)
    def _():
        for h in range(D_HEADS):
            s_out_ref[0, h] = st_scr[h].T


def hgrn2_mixer(proj, lower_bound, gain, s0, *, bsz, t, chunk):
    nc = t // chunk
    wk = D_HEADS * D_EXPAND
    wv = D_HEADS * D_V_DIM
    out, s = _chunk_call(
        _hgrn2_kernel, bsz=bsz, nc=nc, rows=chunk,
        ins=[(proj, 'rows', wk, CD_QD), (proj, 'rows', wk, CD_FD), (proj, 'rows', wv, CD_VD),
             (proj, 'rows', wv, CD_GD), (lower_bound.reshape(1, wk), 'const', 0, 0),
             (gain.reshape(1, D_V_DIM), 'const', 0, 0), (s0, 'batch', 0, 0)],
        outs=[(jax.ShapeDtypeStruct((bsz * t, wv), F32), 'rows', wv, 0),
              (jax.ShapeDtypeStruct(s0.shape, F32), 'batch', 0, 0)],
        scratch=[pltpu.VMEM((D_HEADS, D_V_DIM, D_EXPAND), F32)],
        name="hgrn2_mixer")
    return out, s


def _pad_cols(w, mult=LANE):
    pad = (-w.shape[-1]) % mult
    return jnp.pad(w, [(0, 0)] * (w.ndim - 1) + [(0, pad)])


def _mixer_ab(proj, bsz, t, pos, prm, cache):
    n = bsz * t
    qpad, qipad, k16, v16, ki16, k32, v32, ki32 = dsa_prep(proj, pos, prm['a_q_gain'][0], prm['a_k_gain'][0], t)
    if cache is None:
        limit = jnp.tile((pos // CHUNK + 1) * CHUNK, bsz).reshape(n, 1)
        a_out = dsa_attention(qpad, qipad, proj, limit, k16, v16, ki16, bsz=bsz, tq=t, tk=t,
                              causal=True, n_sel=min(TOPK_MAX, t // 4))
        c0 = jnp.zeros((bsz, B_HEADS, B_QK_DIM, B_V_DIM), F32)
        n0 = jnp.zeros((bsz, B_HEADS, B_QK_DIM), F32)
        m0 = jnp.zeros((bsz, B_HEADS), F32)
        chunk = CHUNK
    else:
        k_c, v_c, ki_c, c0, n0, m0 = cache
        past = k_c.shape[1]
        n_keys = past + t
        tk = -(-n_keys // KEY_CHUNK) * KEY_CHUNK

        def with_cache(c, new):
            c = c.reshape(bsz, past, -1).astype(BF16)
            c = jnp.pad(c, ((0, 0), (0, 0), (0, LANE - c.shape[-1])))
            return jnp.concatenate([c, new.reshape(bsz, t, LANE),
                                    jnp.zeros((bsz, tk - n_keys, LANE), BF16)], axis=1).reshape(bsz * tk, LANE)

        limit = jnp.full((n, 1), n_keys, jnp.int32)
        a_out = dsa_attention(qpad, qipad, proj, limit, with_cache(k_c, k16), with_cache(v_c, v16),
                              with_cache(ki_c, ki16), bsz=bsz, tq=t, tk=tk, causal=False,
                              n_sel=min(TOPK_MAX, n_keys // 4))
        chunk = t
    h, c, n_, m = mlstm_mixer(proj, prm['b_gate_bias'][0], prm['b_norm_gain'][0], c0, n0, m0,
                              bsz=bsz, t=t, chunk=chunk)
    st = (k32.reshape(bsz, t, A_KV_HEADS, HEAD_DIM), v32.reshape(bsz, t, A_KV_HEADS, HEAD_DIM),
          ki32.reshape(bsz, t, IDX_DIM), c, n_, m)
    return a_out, h, st


def _mixer_cd(proj, bsz, t, prm, lower_bound, cache):
    hd = C_HEADS * C_DIM
    if cache is None:
        sc0 = jnp.zeros((bsz, C_HEADS, C_DIM, C_DIM), F32)
        conv_prev = jnp.zeros((bsz, CONV_W - 1, 3 * hd), F32)
        sd0 = jnp.zeros((bsz, D_HEADS, D_EXPAND, D_V_DIM), F32)
        chunk = CHUNK
    else:
        sc0, conv_prev, sd0 = cache
        chunk = t
    oc, sc = gdn_mixer(proj, prm['c_conv_w'][0], prm['c_a_log'][0], prm['c_dt_bias'][0], prm['c_norm_gain'][0],
                       sc0, conv_prev, bsz=bsz, t=t, chunk=chunk)
    od, sd = hgrn2_mixer(proj, lower_bound, prm['d_norm_gain'][0], sd0, bsz=bsz, t=t, chunk=chunk)
    qkv = proj.reshape(bsz, t, -1)[:, :, CD_QKV:CD_QKV + 3 * hd]
    conv_new = jnp.concatenate([conv_prev, qkv[:, t - (CONV_W - 1):]], axis=1)[:, -(CONV_W - 1):]
    return oc, od, (sc, conv_new, sd)


def _trunk(x, pos_offset, cache, prm, wts):
    bsz, t, d = x.shape
    n = bsz * t
    pos = pos_offset + jnp.arange(t, dtype=jnp.int32)
    probs = jax.nn.softmax(prm['d_lb_logits'], axis=0)
    lower_bounds = jnp.cumsum(probs, axis=0) - probs[0]
    xf = x.reshape(n, d)

    lc = None if cache is None else tuple(c[0] for c in cache[:6])
    proj = norm_matmul(xf, prm['norm_mix'][0], wts['w_in_ab'])
    a_out, b_out, st_even = _mixer_ab(proj, bsz, t, pos, prm, lc)
    xf = matmul_residual(a_out, b_out, wts['w_out_ab'], xf)
    xf = ffn_residual(xf, prm['norm_ffn'][0], wts['ffn_w1'], wts['ffn_w3'], wts['ffn_w2'])

    lc = None if cache is None else tuple(c[0] for c in cache[6:])
    proj = norm_matmul(xf, prm['norm_mix'][1], wts['w_in_cd'])
    c_out, d_out, st_odd = _mixer_cd(proj, bsz, t, prm, lower_bounds[1], lc)
    xf = matmul_residual(c_out, d_out, wts['w_out_cd'], xf)
    xf = moe_residual(xf, prm['norm_ffn'][1], wts['moe_router'], wts['moe_w1'], wts['moe_w3'], wts['moe_w2'])

    new_state = tuple(s[None] for s in st_even + st_odd)
    return xf.reshape(bsz, t, d), new_state


def kernel(x_prompt, x_sample, cache_a_k, cache_a_v, cache_a_kidx, state_b_c, state_b_n, state_b_m,
           state_c_s, state_c_conv, state_d_s, norm_mix, norm_ffn, w_in_ab, w_out_ab, a_q_gain, a_k_gain,
           b_gate_bias, b_norm_gain, w_in_cd, w_out_cd, c_conv_w, c_a_log, c_dt_bias, c_norm_gain,
           d_lb_logits, d_norm_gain, ffn_w1, ffn_w3, ffn_w2, moe_router, moe_w1, moe_w3, moe_w2):
    prm = dict(norm_mix=norm_mix, norm_ffn=norm_ffn, a_q_gain=a_q_gain, a_k_gain=a_k_gain,
               b_gate_bias=b_gate_bias, b_norm_gain=b_norm_gain, c_conv_w=c_conv_w, c_a_log=c_a_log,
               c_dt_bias=c_dt_bias, c_norm_gain=c_norm_gain, d_lb_logits=d_lb_logits, d_norm_gain=d_norm_gain)
    wts = dict(w_in_ab=_permute_w_in_ab(w_in_ab[0]).astype(BF16), w_out_ab=w_out_ab[0].astype(BF16),
               w_in_cd=_permute_w_in_cd(w_in_cd[0]).astype(BF16), w_out_cd=w_out_cd[0].astype(BF16),
               ffn_w1=ffn_w1[0].astype(BF16), ffn_w3=ffn_w3[0].astype(BF16), ffn_w2=ffn_w2[0].astype(BF16),
               moe_router=_pad_cols(moe_router[0]),
               moe_w1=moe_w1[0].astype(BF16), moe_w3=moe_w3[0].astype(BF16), moe_w2=moe_w2[0].astype(BF16))
    cache = (cache_a_k, cache_a_v, cache_a_kidx, state_b_c, state_b_n, state_b_m, state_c_s, state_c_conv, state_d_s)
    y_prompt, st_p = _trunk(x_prompt, 0, None, prm, wts)
    y_sample, st_s = _trunk(x_sample, cache_a_k.shape[2], cache, prm, wts)
    return (y_prompt, y_sample) + st_p + st_s
```

```python
import functools
import math

import jax
import jax.numpy as jnp
import numpy as np
from jax import lax
from jax.experimental import pallas as pl
from jax.experimental.pallas import tpu as pltpu

F32 = jnp.float32
BF16 = jnp.bfloat16

EPS = 1e-6
ROPE_THETA = 500000.0
ROT_FRACTION = 4
CHUNK = 64
A_HEADS, A_KV_HEADS, HEAD_DIM = 8, 2, 64
IDX_HEADS, IDX_DIM = 4, 64
TOPK_MAX, Q_BLOCK = 256, 128
B_HEADS, B_QK_DIM, B_V_DIM = 4, 64, 128
C_HEADS, C_DIM, CONV_W = 4, 128, 4
D_HEADS, D_EXPAND, D_V_DIM = 4, 128, 128
N_EXPERTS, TOP_K_EXPERTS = 8, 2

LANE = 128
VMEM_LIMIT = 48 * 1024 * 1024

EVEN_SPLITS = (A_HEADS * HEAD_DIM, A_KV_HEADS * HEAD_DIM, A_KV_HEADS * HEAD_DIM,
               IDX_HEADS * IDX_DIM, IDX_DIM, IDX_HEADS,
               B_HEADS * B_QK_DIM, B_HEADS * B_QK_DIM, B_HEADS * B_V_DIM,
               B_HEADS, B_HEADS, B_HEADS * B_V_DIM)
ODD_SPLITS = (3 * C_HEADS * C_DIM, C_HEADS, C_HEADS, C_HEADS * C_DIM,
              D_HEADS * D_EXPAND, D_HEADS * D_EXPAND, D_HEADS * D_V_DIM, D_HEADS * D_V_DIM)


def _split_cols(p, widths):
    cuts = [int(c) for c in np.cumsum(widths)[:-1]]
    return jnp.split(p, cuts, axis=-1)


def _row_tile(n, target):
    t = min(n, target)
    while n % t:
        t //= 2
    return t


def _col_tile(n, target):
    best = LANE
    for k in range(1, n // LANE + 1):
        c = k * LANE
        if n % c == 0 and c <= target:
            best = c
    return best


def _rms_rows(x, gain):
    return x * lax.rsqrt(jnp.mean(x * x, axis=-1, keepdims=True) + EPS) * gain


def _norm_matmul_kernel(x_ref, g_ref, w_ref, o_ref, xn_ref):
    @pl.when(pl.program_id(1) == 0)
    def _():
        xn_ref[...] = _rms_rows(x_ref[...], g_ref[...]).astype(BF16)

    o_ref[...] = jnp.dot(xn_ref[...], w_ref[...], preferred_element_type=F32)


def norm_matmul(x, gain, w):
    n, d = x.shape
    m = w.shape[1]
    tm = _row_tile(n, 1024)
    tn = _col_tile(m, 1536)
    return pl.pallas_call(
        _norm_matmul_kernel,
        grid=(n // tm, m // tn),
        in_specs=[pl.BlockSpec((tm, d), lambda i, j: (i, 0)),
                  pl.BlockSpec((1, d), lambda i, j: (0, 0)),
                  pl.BlockSpec((d, tn), lambda i, j: (0, j))],
        out_specs=pl.BlockSpec((tm, tn), lambda i, j: (i, j)),
        out_shape=jax.ShapeDtypeStruct((n, m), F32),
        scratch_shapes=[pltpu.VMEM((tm, d), BF16)],
        compiler_params=pltpu.CompilerParams(
            dimension_semantics=("parallel", "arbitrary"), vmem_limit_bytes=VMEM_LIMIT),
        name="norm_matmul",
    )(x, gain.reshape(1, d), w)


def _matmul_res_kernel(a_ref, b_ref, w_ref, r_ref, o_ref):
    ka = a_ref.shape[1]
    o_ref[...] = (r_ref[...] + jnp.dot(a_ref[...].astype(BF16), w_ref[0:ka, :], preferred_element_type=F32)
                  + jnp.dot(b_ref[...].astype(BF16), w_ref[ka:, :], preferred_element_type=F32))


def matmul_residual(a, b, w, res):
    n, ka = a.shape
    kb = b.shape[1]
    m = w.shape[1]
    tm = _row_tile(n, 1024)
    return pl.pallas_call(
        _matmul_res_kernel,
        grid=(n // tm,),
        in_specs=[pl.BlockSpec((tm, ka), lambda i: (i, 0)),
                  pl.BlockSpec((tm, kb), lambda i: (i, 0)),
                  pl.BlockSpec((ka + kb, m), lambda i: (0, 0)),
                  pl.BlockSpec((tm, m), lambda i: (i, 0))],
        out_specs=pl.BlockSpec((tm, m), lambda i: (i, 0)),
        out_shape=jax.ShapeDtypeStruct((n, m), F32),
        compiler_params=pltpu.CompilerParams(
            dimension_semantics=("parallel",), vmem_limit_bytes=VMEM_LIMIT),
        name="matmul_residual",
    )(a, b, w, res)


def _swiglu_tile(xn, w1, w3):
    h1 = jnp.dot(xn, w1, preferred_element_type=F32)
    h3 = jnp.dot(xn, w3, preferred_element_type=F32)
    return h1 * jax.nn.sigmoid(h1) * h3


def _ffn_kernel(x_ref, g_ref, w1_ref, w3_ref, w2_ref, o_ref, xn_ref):
    @pl.when(pl.program_id(1) == 0)
    def _():
        x = x_ref[...]
        xn_ref[...] = _rms_rows(x, g_ref[...]).astype(BF16)
        o_ref[...] = x

    act = _swiglu_tile(xn_ref[...], w1_ref[...], w3_ref[...])
    o_ref[...] += jnp.dot(act.astype(BF16), w2_ref[...], preferred_element_type=F32)


def ffn_residual(x, gain, w1, w3, w2):
    n, d = x.shape
    f = w1.shape[1]
    tm = _row_tile(n, 1024)
    tf = _col_tile(f, 512)
    return pl.pallas_call(
        _ffn_kernel,
        grid=(n // tm, f // tf),
        in_specs=[pl.BlockSpec((tm, d), lambda i, j: (i, 0)),
                  pl.BlockSpec((1, d), lambda i, j: (0, 0)),
                  pl.BlockSpec((d, tf), lambda i, j: (0, j)),
                  pl.BlockSpec((d, tf), lambda i, j: (0, j)),
                  pl.BlockSpec((tf, d), lambda i, j: (j, 0))],
        out_specs=pl.BlockSpec((tm, d), lambda i, j: (i, 0)),
        out_shape=jax.ShapeDtypeStruct((n, d), F32),
        scratch_shapes=[pltpu.VMEM((tm, d), BF16)],
        compiler_params=pltpu.CompilerParams(
            dimension_semantics=("parallel", "arbitrary"), vmem_limit_bytes=VMEM_LIMIT),
        name="ffn_residual",
    )(x, gain.reshape(1, d), w1, w3, w2)


MOE_CAPS = (256, 384, 512)


def _moe_route_kernel(x_ref, g_ref, r_ref, xn_ref, comb_ref, post_ref, cnt_ref):
    x = x_ref[...]
    tm = x.shape[0]
    xn = _rms_rows(x, g_ref[...])
    xn_ref[...] = xn.astype(BF16)
    logits = jnp.dot(xn, r_ref[...], preferred_element_type=F32, precision=lax.Precision.HIGHEST)
    lane = lax.broadcasted_iota(jnp.int32, logits.shape, 1)
    logits = jnp.where(lane < N_EXPERTS, logits, -jnp.inf)
    m1 = jnp.max(logits, axis=-1, keepdims=True)
    i1 = jnp.min(jnp.where(logits == m1, lane, LANE), axis=-1, keepdims=True)
    rest = jnp.where(lane == i1, -jnp.inf, logits)
    m2 = jnp.max(rest, axis=-1, keepdims=True)
    i2 = jnp.min(jnp.where(rest == m2, lane, LANE), axis=-1, keepdims=True)
    e2 = jnp.exp(m2 - m1)
    den = 1.0 + e2
    comb_ref[...] = jnp.where(lane == i1, 1.0 / den, 0.0) + jnp.where(lane == i2, e2 / den, 0.0)
    chosen = (lane == i1) | (lane == i2)
    sel = jnp.where(chosen, 1.0, 0.0)
    tril16 = jnp.where(_tri_mask(LANE), 1.0, 0.0).astype(BF16)
    seen = jnp.zeros((1, LANE), F32)
    ranks = []
    for blk in range(tm // LANE):
        sb = sel[blk * LANE:(blk + 1) * LANE]
        ranks.append(_dot(tril16, sb.astype(BF16)) + seen - 1.0)
        seen = seen + jnp.sum(sb, axis=0, keepdims=True)
    rank = jnp.where(chosen, jnp.concatenate(ranks, axis=0), -1.0)
    post_ref[0] = rank.T[0:N_EXPERTS, :]
    cnt_ref[0] = seen


def _moe_expert_kernel(cnt_ref, xn_ref, comb_ref, post_ref, x_ref, w1_ref, w3_ref, w2_ref, o_ref,
                       xe_scr, y_scr, *, caps):
    i = pl.program_id(0)
    e = pl.program_id(1)
    j = pl.program_id(2)
    last = pl.num_programs(2) - 1
    tm = x_ref.shape[0]

    @pl.when(jnp.logical_and(e == 0, j == 0))
    def _():
        o_ref[...] = x_ref[...]

    cnt = cnt_ref[i * N_EXPERTS + e]

    def expert_step(cap):
        def pick():
            rank_row = post_ref[0, pl.ds(e, 1), :].astype(jnp.int32)
            slot = lax.broadcasted_iota(jnp.int32, (cap, tm), 0)
            return jnp.where(rank_row == slot, 1.0, 0.0).astype(BF16)

        @pl.when(j == 0)
        def _():
            xe_scr[0:cap, :] = _dot(pick(), xn_ref[...]).astype(BF16)

        act = _swiglu_tile(xe_scr[0:cap, :], w1_ref[0], w3_ref[0])
        yj = _dot(act.astype(BF16), w2_ref[0])

        @pl.when(j == 0)
        def _():
            y_scr[0:cap, :] = yj

        @pl.when(j > 0)
        def _():
            y_scr[0:cap, :] += yj

        @pl.when(j == last)
        def _():
            comb = comb_ref[...]
            lane = lax.broadcasted_iota(jnp.int32, comb.shape, 1)
            gate = jnp.sum(jnp.where(lane == e, comb, 0.0), axis=-1, keepdims=True)
            hi, lo = _split2(y_scr[0:cap, :])
            p = pick()
            o_ref[...] += gate * (_dot_tn(p, hi) + _dot_tn(p, lo))

    lo = 0
    for cap in caps:
        @pl.when(jnp.logical_and(cnt > lo, cnt <= cap))
        def _(cap=cap):
            expert_step(cap)
        lo = cap


def moe_residual(x, gain, router, w1, w3, w2):
    n, d = x.shape
    ne, _, f = w1.shape
    tm = _row_tile(n, 1024)
    tf = _col_tile(f, 512)
    nt = n // tm
    xn, comb, post, cnt = pl.pallas_call(
        _moe_route_kernel,
        grid=(nt,),
        in_specs=[pl.BlockSpec((tm, d), lambda i: (i, 0)),
                  pl.BlockSpec((1, d), lambda i: (0, 0)),
                  pl.BlockSpec((d, LANE), lambda i: (0, 0))],
        out_specs=[pl.BlockSpec((tm, d), lambda i: (i, 0)),
                   pl.BlockSpec((tm, LANE), lambda i: (i, 0)),
                   pl.BlockSpec((1, ne, tm), lambda i: (i, 0, 0)),
                   pl.BlockSpec((1, 1, LANE), lambda i: (i, 0, 0))],
        out_shape=[jax.ShapeDtypeStruct((n, d), BF16), jax.ShapeDtypeStruct((n, LANE), F32),
                   jax.ShapeDtypeStruct((nt, ne, tm), F32), jax.ShapeDtypeStruct((nt, 1, LANE), F32)],
        compiler_params=pltpu.CompilerParams(dimension_semantics=("parallel",), vmem_limit_bytes=VMEM_LIMIT),
        name="moe_route",
    )(x, gain.reshape(1, d), router)
    counts = cnt[:, 0, :ne].astype(jnp.int32).reshape(nt * ne)
    caps = tuple(c for c in MOE_CAPS if c < tm) + (tm,)
    grid_spec = pltpu.PrefetchScalarGridSpec(
        num_scalar_prefetch=1,
        grid=(nt, ne, f // tf),
        in_specs=[pl.BlockSpec((tm, d), lambda i, e, j, c: (i, 0)),
                  pl.BlockSpec((tm, LANE), lambda i, e, j, c: (i, 0)),
                  pl.BlockSpec((1, ne, tm), lambda i, e, j, c: (i, 0, 0)),
                  pl.BlockSpec((tm, d), lambda i, e, j, c: (i, 0)),
                  pl.BlockSpec((1, d, tf), lambda i, e, j, c: (e, 0, j)),
                  pl.BlockSpec((1, d, tf), lambda i, e, j, c: (e, 0, j)),
                  pl.BlockSpec((1, tf, d), lambda i, e, j, c: (e, j, 0))],
        out_specs=pl.BlockSpec((tm, d), lambda i, e, j, c: (i, 0)),
        scratch_shapes=[pltpu.VMEM((tm, d), BF16), pltpu.VMEM((tm, d), F32)])
    return pl.pallas_call(
        functools.partial(_moe_expert_kernel, caps=caps),
        grid_spec=grid_spec,
        out_shape=jax.ShapeDtypeStruct((n, d), F32),
        compiler_params=pltpu.CompilerParams(
            dimension_semantics=("parallel", "arbitrary", "arbitrary"), vmem_limit_bytes=VMEM_LIMIT),
        name="moe_experts",
    )(counts, xn, comb, post, x, w1, w3, w2)


AB_QA, AB_VB, AB_OB, AB_QI, AB_QB, AB_KB, AB_KA, AB_VA, AB_MISC, AB_TOTAL = (
    0, 512, 1024, 1536, 1792, 2048, 2304, 2432, 2560, 2688)
MISC_WI, MISC_IB, MISC_FB = 64, 68, 72
HALF = LANE // 2
KEY_CHUNK = 512
MASKED = -1e30
KEY_OF_NEG_INF = -2139095041
INT_MIN = -2147483648


def _permute_w_in_ab(w):
    qa, ka, va, qi, ki, wi, qb, kb, vb, ib, fb, ob = _split_cols(w, EVEN_SPLITS)
    pad = jnp.zeros((w.shape[0], LANE - IDX_DIM - 3 * IDX_HEADS), w.dtype)
    return jnp.concatenate([qa, vb, ob, qi, qb, kb, ka, va, ki, wi, ib, fb, pad], axis=1)


def _rope_tables(pos):
    rot = HEAD_DIM // ROT_FRACTION
    half = rot // 2
    inv_freq = ROPE_THETA ** (-jnp.arange(half, dtype=F32) * 2.0 / rot)
    ang = pos.astype(F32)[:, None] * inv_freq[None, :]
    cos, sin = jnp.cos(ang), jnp.sin(ang)
    t = pos.shape[0]
    one = jnp.ones((t, HEAD_DIM - rot), F32)
    zero_r = jnp.zeros((t, HEAD_DIM - rot), F32)
    zero_h = jnp.zeros((t, half), F32)
    c = jnp.concatenate([cos, cos, one], axis=1)
    s_up = jnp.concatenate([-sin, zero_h, zero_r], axis=1)
    s_dn = jnp.concatenate([zero_h, sin, zero_r], axis=1)
    return tuple(jnp.concatenate([a, a], axis=1) for a in (c, s_up, s_dn))


def _rope_tile(x, c, s_up, s_dn):
    half = HEAD_DIM // ROT_FRACTION // 2
    return x * c + pltpu.roll(x, LANE - half, 1) * s_up + pltpu.roll(x, half, 1) * s_dn


def _head_norm_tile(x, gain, same_head):
    sq = x * x
    hi = sq.astype(BF16)
    lo = (sq - hi.astype(F32)).astype(BF16)
    ss = (jnp.dot(hi, same_head, preferred_element_type=F32)
          + jnp.dot(lo, same_head, preferred_element_type=F32))
    return x * lax.rsqrt(ss * (1.0 / HEAD_DIM) + EPS) * gain


def _aprep_kernel(qa_ref, ka_ref, va_ref, qi_ref, misc_ref, c_ref, su_ref, sd_ref, qg_ref, kg_ref,
                  qpad_ref, qipad_ref, k16_ref, v16_ref, ki16_ref, k32_ref, v32_ref, ki32_ref):
    c, su, sd = c_ref[...], su_ref[...], sd_ref[...]
    tm = c.shape[0]
    row = lax.broadcasted_iota(jnp.int32, (LANE, LANE), 0)
    col = lax.broadcasted_iota(jnp.int32, (LANE, LANE), 1)
    same_head = jnp.where(row // HALF == col // HALF, 1.0, 0.0).astype(BF16)
    lane = lax.broadcasted_iota(jnp.int32, (tm, LANE), 1)
    low = lane < HALF

    heads_per_group = A_HEADS // A_KV_HEADS
    for p in range(A_HEADS // 2):
        y = _rope_tile(_head_norm_tile(qa_ref[:, p * LANE:(p + 1) * LANE], qg_ref[...], same_head), c, su, sd)
        y = y * HEAD_DIM ** -0.5
        y_sw = pltpu.roll(y, HALF, 1)
        for o in range(2):
            h = 2 * p + o
            g = h // heads_per_group
            src = y if o == g else y_sw
            qpad_ref[:, h * LANE:(h + 1) * LANE] = jnp.where(low if g == 0 else ~low, src, 0.0).astype(BF16)
    k = _rope_tile(_head_norm_tile(ka_ref[...], kg_ref[...], same_head), c, su, sd)
    k32_ref[...] = k
    k16_ref[...] = k.astype(BF16)
    v = va_ref[...]
    v32_ref[...] = v
    v16_ref[...] = v.astype(BF16)
    for p in range(IDX_HEADS // 2):
        y = _rope_tile(qi_ref[:, p * LANE:(p + 1) * LANE], c, su, sd)
        y_sw = pltpu.roll(y, HALF, 1)
        qipad_ref[:, (2 * p) * LANE:(2 * p + 1) * LANE] = jnp.where(low, y, 0.0).astype(BF16)
        qipad_ref[:, (2 * p + 1) * LANE:(2 * p + 2) * LANE] = jnp.where(low, y_sw, 0.0).astype(BF16)
    ki = _rope_tile(misc_ref[...], c, su, sd)
    ki32_ref[...] = ki[:, :IDX_DIM]
    ki16_ref[...] = jnp.where(low, ki, 0.0).astype(BF16)


def dsa_prep(proj, pos, q_gain, k_gain, t):
    n = proj.shape[0]
    tm = _row_tile(n, 512)
    tabs = _rope_tables(pos)
    if t < tm:
        tabs = tuple(jnp.tile(a, (tm // t, 1)) for a in tabs)
    nt = tabs[0].shape[0] // tm
    tab_spec = pl.BlockSpec((tm, LANE), lambda i: (i % nt, 0))
    gain_spec = pl.BlockSpec((1, LANE), lambda i: (0, 0))

    def col(width, offset):
        return pl.BlockSpec((tm, width), lambda i: (i, offset // width))

    def out(width, dtype):
        return (jax.ShapeDtypeStruct((n, width), dtype), pl.BlockSpec((tm, width), lambda i: (i, 0)))

    outs = [out(A_HEADS * LANE, BF16), out(IDX_HEADS * LANE, BF16), out(LANE, BF16), out(LANE, BF16),
            out(LANE, BF16), out(LANE, F32), out(LANE, F32), out(IDX_DIM, F32)]
    return pl.pallas_call(
        _aprep_kernel,
        grid=(n // tm,),
        in_specs=[col(A_HEADS * HEAD_DIM, AB_QA), col(LANE, AB_KA), col(LANE, AB_VA),
                  col(IDX_HEADS * IDX_DIM, AB_QI), col(LANE, AB_MISC), tab_spec, tab_spec, tab_spec,
                  gain_spec, gain_spec],
        out_specs=[o[1] for o in outs],
        out_shape=[o[0] for o in outs],
        compiler_params=pltpu.CompilerParams(dimension_semantics=("parallel",), vmem_limit_bytes=VMEM_LIMIT),
        name="dsa_prep",
    )(proj, proj, proj, proj, proj, *tabs, jnp.tile(q_gain, 2).reshape(1, LANE), jnp.tile(k_gain, 2).reshape(1, LANE))


def _dsa_kernel(q_ref, qi_ref, misc_ref, lim_ref, k_ref, v_ref, ki_ref, o_ref, key_ref, bias_ref,
                *, causal, n_sel):
    qb = q_ref.shape[0]
    kc = KEY_CHUNK
    n_idx = IDX_HEADS
    hpg = A_HEADS // A_KV_HEADS
    if causal:
        nch = ((pl.program_id(1) + 1) * qb + kc - 1) // kc
    else:
        nch = k_ref.shape[0] // kc
    limit = lim_ref[...]
    misc = misc_ref[...]
    wscale = IDX_HEADS ** -0.5 * IDX_DIM ** -0.5
    w = [misc[:, MISC_WI + j:MISC_WI + j + 1] * wscale for j in range(n_idx)]
    qi = jnp.concatenate([qi_ref[:, j * LANE:(j + 1) * LANE] for j in range(n_idx)], axis=0)
    nt = (((1,), (1,)), ((), ()))

    def score_body(c, carry):
        off = pl.multiple_of(c * kc, kc)
        lg = lax.dot_general(qi, ki_ref[pl.ds(off, kc), :], nt, preferred_element_type=F32)
        lg = jnp.maximum(lg, 0.0).reshape(n_idx, qb, kc)
        s = w[0] * lg[0]
        for j in range(1, n_idx):
            s = s + w[j] * lg[j]
        kidx = off + lax.broadcasted_iota(jnp.int32, (qb, kc), 1)
        s = jnp.where(kidx < limit, s, -jnp.inf)
        bits = lax.bitcast_convert_type(s, jnp.int32)
        key_ref[:, pl.ds(off, kc)] = jnp.where(bits < 0, bits ^ 0x7FFFFFFF, bits)
        return carry

    lax.fori_loop(0, nch, score_body, 0)

    def count_ge(cand):
        def body(c, acc):
            off = pl.multiple_of(c * kc, kc)
            m = jnp.where(key_ref[:, pl.ds(off, kc)] >= cand, 1.0, 0.0)
            for t in range(kc // LANE):
                acc = acc + m[:, t * LANE:(t + 1) * LANE]
            return acc

        acc = lax.fori_loop(0, nch, body, jnp.zeros((qb, LANE), F32))
        return jnp.sum(acc, axis=1, keepdims=True)

    want = float(n_sel)
    tau = jnp.where(count_ge(jnp.zeros((qb, 1), jnp.int32)) >= want, 0, INT_MIN).astype(jnp.int32)

    def bisect(i, tau):
        cand = tau | jnp.left_shift(jnp.int32(1), 30 - i)
        return jnp.where(count_ge(cand) >= want, cand, tau)

    tau = lax.fori_loop(0, 31, bisect, tau)

    room = want - count_ge(tau + 1)
    r_i = lax.broadcasted_iota(jnp.int32, (LANE, LANE), 0)
    c_i = lax.broadcasted_iota(jnp.int32, (LANE, LANE), 1)
    prefix_ones = jnp.where(r_i <= c_i, 1.0, 0.0).astype(BF16)

    def bias_body(c, seen):
        off = pl.multiple_of(c * kc, kc)
        for t in range(kc // LANE):
            x = key_ref[:, pl.ds(off + t * LANE, LANE)]
            eq = x == tau
            eqf = jnp.where(eq, 1.0, 0.0)
            rank = jnp.dot(eqf.astype(BF16), prefix_ones, preferred_element_type=F32) + seen
            sel = ((x > tau) | (eq & (rank <= room))) & (x != KEY_OF_NEG_INF)
            bias_ref[:, pl.ds(off + t * LANE, LANE)] = jnp.where(sel, 0.0, MASKED)
            seen = seen + jnp.sum(eqf, axis=1, keepdims=True)
        return seen

    lax.fori_loop(0, nch, bias_body, jnp.zeros((qb, 1), F32))

    lane = lax.broadcasted_iota(jnp.int32, (qb, LANE), 1)
    qgs = [jnp.concatenate([q_ref[:, (hpg * g + h) * LANE:(hpg * g + h + 1) * LANE] for h in range(hpg)], axis=0)
           for g in range(A_KV_HEADS)]

    def att_body(c, carry):
        off = pl.multiple_of(c * kc, kc)
        kch = k_ref[pl.ds(off, kc), :]
        vch = v_ref[pl.ds(off, kc), :]
        bias = bias_ref[:, pl.ds(off, kc)][None]
        new = []
        for g in range(A_KV_HEADS):
            m, l, acc = carry[g]
            s = lax.dot_general(qgs[g], kch, nt, preferred_element_type=F32)
            s = (s.reshape(hpg, qb, kc) + bias).reshape(hpg * qb, kc)
            m_new = jnp.maximum(m, jnp.max(s, axis=1, keepdims=True))
            alpha = jnp.exp(m - m_new)
            p = jnp.exp(s - m_new)
            l = alpha * l + jnp.sum(p, axis=1, keepdims=True)
            acc = alpha * acc + jnp.dot(p.astype(BF16), vch, preferred_element_type=F32)
            new.append((m_new, l, acc))
        return tuple(new)

    init = tuple((jnp.full((hpg * qb, 1), MASKED, F32), jnp.zeros((hpg * qb, 1), F32),
                  jnp.zeros((hpg * qb, LANE), F32)) for _ in range(A_KV_HEADS))
    res = lax.fori_loop(0, nch, att_body, init)
    outs = []
    for g in range(A_KV_HEADS):
        _, l, acc = res[g]
        og = acc / l
        for h in range(hpg):
            oh = og[h * qb:(h + 1) * qb]
            outs.append(oh if (h % 2) == g else pltpu.roll(oh, HALF, 1))
    for p in range(A_HEADS // 2):
        o_ref[:, p * LANE:(p + 1) * LANE] = jnp.where(lane < HALF, outs[2 * p], outs[2 * p + 1])


def dsa_attention(qpad, qipad, proj, limit, k16, v16, ki16, *, bsz, tq, tk, causal, n_sel):
    qb = min(Q_BLOCK, tq)
    nqb = tq // qb
    assert tk % KEY_CHUNK == 0 and tk >= n_sel
    kern = functools.partial(_dsa_kernel, causal=causal, n_sel=n_sel)

    def qspec(width, col=0):
        return pl.BlockSpec((qb, width), lambda b, i: (b * nqb + i, col))

    def kspec():
        return pl.BlockSpec((tk, LANE), lambda b, i: (b, 0))

    return pl.pallas_call(
        kern,
        grid=(bsz, nqb),
        in_specs=[qspec(A_HEADS * LANE), qspec(IDX_HEADS * LANE), qspec(LANE, AB_MISC // LANE), qspec(1),
                  kspec(), kspec(), kspec()],
        out_specs=qspec(A_HEADS * HEAD_DIM),
        out_shape=jax.ShapeDtypeStruct((bsz * tq, A_HEADS * HEAD_DIM), F32),
        scratch_shapes=[pltpu.VMEM((qb, tk), jnp.int32), pltpu.VMEM((qb, tk), F32)],
        compiler_params=pltpu.CompilerParams(
            dimension_semantics=("parallel", "arbitrary"), vmem_limit_bytes=VMEM_LIMIT),
        name="dsa_attention",
    )(qpad, qipad, proj, limit, k16, v16, ki16)


SUB = 16


def _dot(a, b):
    return jnp.dot(a, b, preferred_element_type=F32)


def _dot_nt(a, b):
    return lax.dot_general(a, b, (((1,), (1,)), ((), ())), preferred_element_type=F32)


def _dot_tn(a, b):
    return lax.dot_general(a, b, (((0,), (0,)), ((), ())), preferred_element_type=F32)


def _split2(x):
    hi = x.astype(BF16)
    return hi, (x - hi.astype(F32)).astype(BF16)


def _split3(x):
    hi = x.astype(BF16)
    r = x - hi.astype(F32)
    mid = r.astype(BF16)
    return hi, mid, (r - mid.astype(F32)).astype(BF16)


def _cumsum_rows(x, tril16):
    hi, mid, lo = _split3(x)
    return _dot(tril16, hi) + _dot(tril16, mid) + _dot(tril16, lo)


def _dot_f32(a, b):
    ah, al = _split2(a)
    bh, bl = _split2(b)
    return _dot(ah, bh) + (_dot(ah, bl) + _dot(al, bh))


def _tri_mask(n, strict=False):
    r = lax.broadcasted_iota(jnp.int32, (n, n), 0)
    c = lax.broadcasted_iota(jnp.int32, (n, n), 1)
    return r > c if strict else r >= c


def _rows_to_lanes(x):
    rows = x.shape[0]
    if rows < LANE:
        x = jnp.concatenate([x, jnp.zeros((LANE - rows, LANE), x.dtype)], axis=0)
    return x.T


def _chunk_call(kern, *, bsz, nc, rows, ins, outs, scratch, name):
    def spec(a, kind, width, offset):
        if kind == 'rows':
            return pl.BlockSpec((rows, width), lambda b, c: (b * nc + c, offset // width))
        if kind == 'batch':
            return pl.BlockSpec((1,) + tuple(a.shape[1:]), lambda b, c: (b,) + (0,) * (len(a.shape) - 1))
        return pl.BlockSpec(tuple(a.shape), lambda b, c: (0,) * len(a.shape))

    return pl.pallas_call(
        kern,
        grid=(bsz, nc),
        in_specs=[spec(*i) for i in ins],
        out_specs=[spec(*o) for o in outs],
        out_shape=[o[0] for o in outs],
        scratch_shapes=scratch,
        compiler_params=pltpu.CompilerParams(
            dimension_semantics=("parallel", "arbitrary"), vmem_limit_bytes=VMEM_LIMIT),
        name=name,
    )(*[i[0] for i in ins])


def _mlstm_kernel(q_ref, k_ref, v_ref, og_ref, misc_ref, gb_ref, gain_ref, c0_ref, n0_ref, m0_ref,
                  h_ref, c_out_ref, n_out_ref, m_out_ref, c_scr, n_scr, m_scr):
    ci = pl.program_id(1)

    @pl.when(ci == 0)
    def _():
        c_scr[...] = c0_ref[0]
        n_scr[...] = n0_ref[0]
        m_scr[...] = m0_ref[0]

    rows = q_ref.shape[0]
    incl = _tri_mask(rows)
    tril16 = jnp.where(incl, 1.0, 0.0).astype(BF16)
    gates = misc_ref[...] + gb_ref[...]
    bcum = _cumsum_rows(jax.nn.log_sigmoid(gates), tril16)
    gates_t = _rows_to_lanes(gates)
    bcum_t = _rows_to_lanes(bcum)
    lane = lax.broadcasted_iota(jnp.int32, (rows, LANE), 1)
    lane1 = lax.broadcasted_iota(jnp.int32, (1, LANE), 1)
    row_sq = lax.broadcasted_iota(jnp.int32, (LANE, LANE), 0)
    m_all = m_scr[...]
    m_next = m_all
    gain = gain_ref[...]
    for p in range(B_HEADS // 2):
        cs = c_scr[p]
        cs16 = cs.astype(BF16)
        n_row = n_scr[:, p * LANE:(p + 1) * LANE]
        qt = q_ref[:, p * LANE:(p + 1) * LANE]
        kt = k_ref[:, p * LANE:(p + 1) * LANE]
        upd = jnp.zeros((LANE, LANE), F32)
        ksum = jnp.zeros((1, LANE), F32)
        keeps = []
        for o in range(2):
            h = 2 * p + o
            mine = (lane >= o * HALF) & (lane < (o + 1) * HALF)
            qm = jnp.where(mine, qt, 0.0)
            km = jnp.where(mine, kt, 0.0) * B_QK_DIM ** -0.5
            qm16 = qm.astype(BF16)
            b_col = bcum[:, MISC_FB + h:MISC_FB + h + 1]
            i_col = gates[:, MISC_IB + h:MISC_IB + h + 1]
            b_row = bcum_t[MISC_FB + h:MISC_FB + h + 1, 0:rows]
            i_row = gates_t[MISC_IB + h:MISC_IB + h + 1, 0:rows]
            m_h = m_all[:, h:h + 1]
            dmat = jnp.where(incl, b_col - b_row + i_row, -jnp.inf)
            inter = b_col + m_h
            mrow = jnp.maximum(inter, jnp.max(dmat, axis=1, keepdims=True))
            w_state = jnp.exp(inter - mrow)
            scores = _dot_nt(qm16, km.astype(BF16)) * jnp.exp(dmat - mrow)
            vh = v_ref[:, h * LANE:(h + 1) * LANE]
            vh16 = vh.astype(BF16)
            num = _dot(scores.astype(BF16), vh16) + w_state * _dot(qm16, cs16)
            den = (jnp.sum(scores, axis=1, keepdims=True)
                   + w_state * jnp.sum(qm * n_row, axis=1, keepdims=True))
            hh = num / jnp.maximum(jnp.abs(den), jnp.exp(-mrow))
            h_ref[:, h * LANE:(h + 1) * LANE] = _rms_rows(hh, gain) * jax.nn.sigmoid(og_ref[:, h * LANE:(h + 1) * LANE])
            b_end = b_col[rows - 1:rows]
            g_col = b_end - b_col + i_col
            m_new = jnp.maximum(b_end + m_h, jnp.max(g_col, axis=0, keepdims=True))
            kw = km * jnp.exp(g_col - m_new)
            upd = upd + _dot_tn(kw.astype(BF16), vh16)
            ksum = ksum + jnp.sum(kw, axis=0, keepdims=True)
            keeps.append(jnp.exp(b_end + m_h - m_new))
            m_next = jnp.where(lane1 == h, m_new, m_next)
        c_scr[p] = jnp.where(row_sq < HALF, keeps[0], keeps[1]) * cs + upd
        n_scr[:, p * LANE:(p + 1) * LANE] = jnp.where(lane1 < HALF, keeps[0], keeps[1]) * n_row + ksum
    m_scr[...] = m_next

    @pl.when(ci == pl.num_programs(1) - 1)
    def _():
        c_out_ref[0] = c_scr[...]
        n_out_ref[0] = n_scr[...]
        m_out_ref[0] = m_scr[...]


def mlstm_mixer(proj, gate_bias, gain, c0, n0, m0, *, bsz, t, chunk):
    nc = t // chunk
    pairs = B_HEADS // 2
    gb = jnp.zeros((1, LANE), F32)
    gb = gb.at[0, MISC_IB:MISC_IB + B_HEADS].set(gate_bias[0]).at[0, MISC_FB:MISC_FB + B_HEADS].set(gate_bias[1])
    c0 = c0.reshape(bsz, pairs, LANE, B_V_DIM)
    n0 = n0.reshape(bsz, 1, B_HEADS * B_QK_DIM)
    m0 = jnp.pad(m0, ((0, 0), (0, LANE - B_HEADS))).reshape(bsz, 1, LANE)
    wq = B_HEADS * B_QK_DIM
    wv = B_HEADS * B_V_DIM
    h, c, n, m = _chunk_call(
        _mlstm_kernel, bsz=bsz, nc=nc, rows=chunk,
        ins=[(proj, 'rows', wq, AB_QB), (proj, 'rows', wq, AB_KB), (proj, 'rows', wv, AB_VB),
             (proj, 'rows', wv, AB_OB), (proj, 'rows', LANE, AB_MISC), (gb, 'const', 0, 0),
             (gain.reshape(1, B_V_DIM), 'const', 0, 0), (c0, 'batch', 0, 0), (n0, 'batch', 0, 0),
             (m0, 'batch', 0, 0)],
        outs=[(jax.ShapeDtypeStruct((bsz * t, wv), F32), 'rows', wv, 0),
              (jax.ShapeDtypeStruct(c0.shape, F32), 'batch', 0, 0),
              (jax.ShapeDtypeStruct(n0.shape, F32), 'batch', 0, 0),
              (jax.ShapeDtypeStruct(m0.shape, F32), 'batch', 0, 0)],
        scratch=[pltpu.VMEM((pairs, LANE, B_V_DIM), F32), pltpu.VMEM((1, wq), F32), pltpu.VMEM((1, LANE), F32)],
        name="mlstm_mixer")
    return (h, c.reshape(bsz, B_HEADS, B_QK_DIM, B_V_DIM), n.reshape(bsz, B_HEADS, B_QK_DIM),
            m.reshape(bsz, LANE)[:, :B_HEADS])


CD_QKV, CD_ZC, CD_QD, CD_FD, CD_VD, CD_GD, CD_MISC, CD_TOTAL = 0, 1536, 2048, 2560, 3072, 3584, 4096, 4224
MISC_BC, MISC_AC = 0, 4
TAIL = 8


def _permute_w_in_cd(w):
    qkv, bc, ac, zc, qd, fd, vd, gd = _split_cols(w, ODD_SPLITS)
    pad = jnp.zeros((w.shape[0], LANE - 2 * C_HEADS), w.dtype)
    return jnp.concatenate([qkv, zc, qd, fd, vd, gd, bc, ac, pad], axis=1)


def _gdn_kernel(qkv_ref, z_ref, misc_ref, cw_ref, alog_ref, dt_ref, gain_ref, s0_ref, tail0_ref,
                o_ref, s_out_ref, st_scr, tail_scr):
    ci = pl.program_id(1)

    @pl.when(ci == 0)
    def _():
        for h in range(C_HEADS):
            st_scr[:, h * C_DIM:(h + 1) * C_DIM] = s0_ref[0, h].T
        tail_scr[...] = tail0_ref[0]

    rows = qkv_ref.shape[0]
    width = qkv_ref.shape[1]
    x = qkv_ref[...]
    tail = tail_scr[...]
    row8 = lax.broadcasted_iota(jnp.int32, (TAIL, width), 0)
    acc = x * cw_ref[CONV_W - 1:CONV_W, :]
    for back in range(1, CONV_W):
        rolled = pltpu.roll(x, back, 0)
        first = jnp.where(row8 < back, pltpu.roll(tail, back, 0), rolled[0:TAIL])
        shifted = first if rows == TAIL else jnp.concatenate([first, rolled[TAIL:]], axis=0)
        acc = acc + shifted * cw_ref[CONV_W - 1 - back:CONV_W - back, :]
    tail_scr[...] = x[rows - TAIL:rows]
    conv = acc * jax.nn.sigmoid(acc)

    tril16 = jnp.where(_tri_mask(rows), 1.0, 0.0).astype(BF16)
    misc = misc_ref[...]
    beta_t = jax.nn.sigmoid(misc)
    g_t = -jnp.exp(alog_ref[...]) * jax.nn.softplus(misc + dt_ref[...])
    gcum = _cumsum_rows(g_t, tril16)

    hd = C_HEADS * C_DIM
    hr = C_HEADS * rows

    def stack(f):
        return jnp.concatenate([f(h) for h in range(C_HEADS)], axis=0)

    def l2n(v):
        return v * lax.rsqrt(jnp.sum(v * v, axis=-1, keepdims=True) + EPS)

    q_all = stack(lambda h: l2n(conv[:, h * C_DIM:(h + 1) * C_DIM])) * C_DIM ** -0.5
    k_all = stack(lambda h: l2n(conv[:, hd + h * C_DIM:hd + (h + 1) * C_DIM]))
    v_all = stack(lambda h: conv[:, 2 * hd + h * C_DIM:2 * hd + (h + 1) * C_DIM])
    beta = stack(lambda h: beta_t[:, MISC_BC + h:MISC_BC + h + 1])
    gc = stack(lambda h: gcum[:, MISC_AC + h:MISC_AC + h + 1])
    g_end = stack(lambda h: jnp.broadcast_to(gcum[rows - 1:rows, MISC_AC + h:MISC_AC + h + 1], (rows, 1)))
    gc_row = jnp.broadcast_to(gc, (hr, LANE)).T[0:1, :]

    r = lax.broadcasted_iota(jnp.int32, (hr, hr), 0)
    c = lax.broadcasted_iota(jnp.int32, (hr, hr), 1)
    same = (r // rows) == (c // rows)
    incl = same & (r >= c)
    strict = same & (r > c)
    decay = jnp.exp(jnp.where(incl, gc - gc_row, -jnp.inf))
    k16 = k_all.astype(BF16)
    a_mat = jnp.where(strict, beta * _dot_nt(k16, k16) * decay, 0.0)
    power = -a_mat
    inv = jnp.where(r == c, 1.0, 0.0) + power
    for _ in range(int(math.log2(rows)) - 1):
        power = _dot_f32(power, power)
        inv = inv + _dot_f32(inv, power)
    inv_hi, inv_lo = _split2(inv)
    rhs = jnp.concatenate([beta * v_all, beta * jnp.exp(gc) * k_all], axis=1).astype(BF16)
    w = _dot(inv_hi, rhs) + _dot(inv_lo, rhs)
    w_v, w_k = w[:, 0:C_DIM], w[:, C_DIM:2 * C_DIM]
    qk = _dot_nt(q_all.astype(BF16), k16) * decay

    head_of_row = lax.broadcasted_iota(jnp.int32, (hr, C_DIM), 0) // rows

    def per_head_lanes(m):
        return jnp.concatenate([jnp.where(head_of_row == h, m, 0.0) for h in range(C_HEADS)], axis=1).astype(BF16)

    st = st_scr[...]
    st16 = st.astype(BF16)
    delta = w_v - _dot_nt(per_head_lanes(w_k), st16)
    d16 = delta.astype(BF16)
    out = _dot_nt(per_head_lanes(q_all * jnp.exp(gc)), st16) + _dot(qk.astype(BF16), d16)
    keep = jnp.concatenate([jnp.broadcast_to(jnp.exp(gcum[rows - 1:rows, MISC_AC + h:MISC_AC + h + 1]), (1, C_DIM))
                            for h in range(C_HEADS)], axis=1)
    st_scr[...] = keep * st + _dot_tn(d16, per_head_lanes(k_all * jnp.exp(g_end - gc)))
    gain = gain_ref[...]
    for h in range(C_HEADS):
        z = z_ref[:, h * C_DIM:(h + 1) * C_DIM]
        o_ref[:, h * C_DIM:(h + 1) * C_DIM] = _rms_rows(out[h * rows:(h + 1) * rows], gain) * (z * jax.nn.sigmoid(z))

    @pl.when(ci == pl.num_programs(1) - 1)
    def _():
        for h in range(C_HEADS):
            s_out_ref[0, h] = st_scr[:, h * C_DIM:(h + 1) * C_DIM].T


def gdn_mixer(proj, conv_w, a_log, dt_bias, gain, s0, conv_prev, *, bsz, t, chunk):
    nc = t // chunk
    hd = C_HEADS * C_DIM
    lanes = jnp.zeros((1, LANE), F32)
    alog = lanes.at[0, MISC_AC:MISC_AC + C_HEADS].set(a_log)
    dt = lanes.at[0, MISC_AC:MISC_AC + C_HEADS].set(dt_bias)
    tail0 = jnp.pad(conv_prev, ((0, 0), (TAIL - (CONV_W - 1), 0), (0, 0)))
    out, s = _chunk_call(
        _gdn_kernel, bsz=bsz, nc=nc, rows=chunk,
        ins=[(proj, 'rows', 3 * hd, CD_QKV), (proj, 'rows', hd, CD_ZC), (proj, 'rows', LANE, CD_MISC),
             (conv_w, 'const', 0, 0), (alog, 'const', 0, 0), (dt, 'const', 0, 0),
             (gain.reshape(1, C_DIM), 'const', 0, 0), (s0, 'batch', 0, 0), (tail0, 'batch', 0, 0)],
        outs=[(jax.ShapeDtypeStruct((bsz * t, hd), F32), 'rows', hd, 0),
              (jax.ShapeDtypeStruct(s0.shape, F32), 'batch', 0, 0)],
        scratch=[pltpu.VMEM((C_DIM, hd), F32), pltpu.VMEM((TAIL, 3 * hd), F32)],
        name="gdn_mixer")
    return out, s


def _hgrn2_kernel(q_ref, f_ref, v_ref, g_ref, lb_ref, gain_ref, s0_ref, o_ref, s_out_ref, st_scr):
    ci = pl.program_id(1)

    @pl.when(ci == 0)
    def _():
        for h in range(D_HEADS):
            st_scr[h] = s0_ref[0, h].T

    rows = q_ref.shape[0]
    tril16 = jnp.where(_tri_mask(rows), 1.0, 0.0).astype(BF16)
    lb = lb_ref[...]
    zf = f_ref[...]
    logf = jnp.logaddexp(jnp.log(lb), jnp.log1p(-lb) + jax.nn.log_sigmoid(zf))
    kd = (1.0 - lb) * jax.nn.sigmoid(-zf)
    qx = q_ref[...]
    qd = qx * jax.nn.sigmoid(qx)
    bcum = _cumsum_rows(logf, tril16)
    gain = gain_ref[...]
    row_sub = lax.broadcasted_iota(jnp.int32, (SUB, 1), 0)
    for h in range(D_HEADS):
        sl = slice(h * D_EXPAND, (h + 1) * D_EXPAND)
        q, k, b = qd[:, sl], kd[:, sl], bcum[:, sl]
        v = v_ref[:, h * D_V_DIM:(h + 1) * D_V_DIM]
        v16 = v.astype(BF16)
        st = st_scr[h]
        inter = _dot_nt((q * jnp.exp(b)).astype(BF16), st.astype(BF16))
        blocks = []
        for i in range(rows // SUB):
            r0 = i * SUB
            qi, bi = q[r0:r0 + SUB], b[r0:r0 + SUB]
            oi = inter[r0:r0 + SUB]
            if i > 0:
                ref = b[r0 - 1:r0]
                att = _dot_nt((qi * jnp.exp(bi - ref)).astype(BF16),
                              (k[0:r0] * jnp.exp(ref - b[0:r0])).astype(BF16))
                oi = oi + _dot(att.astype(BF16), v16[0:r0])
            for s in range(SUB):
                r = r0 + s
                a = jnp.sum(qi * jnp.exp(bi - b[r:r + 1]) * k[r:r + 1], axis=1, keepdims=True)
                oi = oi + jnp.where(row_sub >= s, a, 0.0) * v[r:r + 1]
            blocks.append(oi)
        out = blocks[0] if len(blocks) == 1 else jnp.concatenate(blocks, axis=0)
        b_end = b[rows - 1:rows]
        st_scr[h] = jnp.exp(b_end) * st + _dot_tn(v16, (k * jnp.exp(b_end - b)).astype(BF16))
        g = g_ref[:, h * D_V_DIM:(h + 1) * D_V_DIM]
        o_ref[:, h * D_V_DIM:(h + 1) * D_V_DIM] = _rms_rows(out, gain) * (g * jax.nn.sigmoid(g))

    @pl.when(ci == pl.num_programs(1) - 1)
    def _():
        for h in range(D_HEADS):
            s_out_ref[0, h] = st_scr[h].T


def hgrn2_mixer(proj, lower_bound, gain, s0, *, bsz, t, chunk):
    nc = t // chunk
    wk = D_HEADS * D_EXPAND
    wv = D_HEADS * D_V_DIM
    out, s = _chunk_call(
        _hgrn2_kernel, bsz=bsz, nc=nc, rows=chunk,
        ins=[(proj, 'rows', wk, CD_QD), (proj, 'rows', wk, CD_FD), (proj, 'rows', wv, CD_VD),
             (proj, 'rows', wv, CD_GD), (lower_bound.reshape(1, wk), 'const', 0, 0),
             (gain.reshape(1, D_V_DIM), 'const', 0, 0), (s0, 'batch', 0, 0)],
        outs=[(jax.ShapeDtypeStruct((bsz * t, wv), F32), 'rows', wv, 0),
              (jax.ShapeDtypeStruct(s0.shape, F32), 'batch', 0, 0)],
        scratch=[pltpu.VMEM((D_HEADS, D_V_DIM, D_EXPAND), F32)],
        name="hgrn2_mixer")
    return out, s


def _pad_cols(w, mult=LANE):
    pad = (-w.shape[-1]) % mult
    return jnp.pad(w, [(0, 0)] * (w.ndim - 1) + [(0, pad)])


def _mixer_ab(proj, bsz, t, pos, prm, cache):
    n = bsz * t
    qpad, qipad, k16, v16, ki16, k32, v32, ki32 = dsa_prep(proj, pos, prm['a_q_gain'][0], prm['a_k_gain'][0], t)
    if cache is None:
        limit = jnp.tile((pos // CHUNK + 1) * CHUNK, bsz).reshape(n, 1)
        a_out = dsa_attention(qpad, qipad, proj, limit, k16, v16, ki16, bsz=bsz, tq=t, tk=t,
                              causal=True, n_sel=min(TOPK_MAX, t // 4))
        c0 = jnp.zeros((bsz, B_HEADS, B_QK_DIM, B_V_DIM), F32)
        n0 = jnp.zeros((bsz, B_HEADS, B_QK_DIM), F32)
        m0 = jnp.zeros((bsz, B_HEADS), F32)
        chunk = CHUNK
    else:
        k_c, v_c, ki_c, c0, n0, m0 = cache
        past = k_c.shape[1]
        n_keys = past + t
        tk = -(-n_keys // KEY_CHUNK) * KEY_CHUNK

        def with_cache(c, new):
            c = c.reshape(bsz, past, -1).astype(BF16)
            c = jnp.pad(c, ((0, 0), (0, 0), (0, LANE - c.shape[-1])))
            return jnp.concatenate([c, new.reshape(bsz, t, LANE),
                                    jnp.zeros((bsz, tk - n_keys, LANE), BF16)], axis=1).reshape(bsz * tk, LANE)

        limit = jnp.full((n, 1), n_keys, jnp.int32)
        a_out = dsa_attention(qpad, qipad, proj, limit, with_cache(k_c, k16), with_cache(v_c, v16),
                              with_cache(ki_c, ki16), bsz=bsz, tq=t, tk=tk, causal=False,
                              n_sel=min(TOPK_MAX, n_keys // 4))
        chunk = t
    h, c, n_, m = mlstm_mixer(proj, prm['b_gate_bias'][0], prm['b_norm_gain'][0], c0, n0, m0,
                              bsz=bsz, t=t, chunk=chunk)
    st = (k32.reshape(bsz, t, A_KV_HEADS, HEAD_DIM), v32.reshape(bsz, t, A_KV_HEADS, HEAD_DIM),
          ki32.reshape(bsz, t, IDX_DIM), c, n_, m)
    return a_out, h, st


def _mixer_cd(proj, bsz, t, prm, lower_bound, cache):
    hd = C_HEADS * C_DIM
    if cache is None:
        sc0 = jnp.zeros((bsz, C_HEADS, C_DIM, C_DIM), F32)
        conv_prev = jnp.zeros((bsz, CONV_W - 1, 3 * hd), F32)
        sd0 = jnp.zeros((bsz, D_HEADS, D_EXPAND, D_V_DIM), F32)
        chunk = CHUNK
    else:
        sc0, conv_prev, sd0 = cache
        chunk = t
    oc, sc = gdn_mixer(proj, prm['c_conv_w'][0], prm['c_a_log'][0], prm['c_dt_bias'][0], prm['c_norm_gain'][0],
                       sc0, conv_prev, bsz=bsz, t=t, chunk=chunk)
    od, sd = hgrn2_mixer(proj, lower_bound, prm['d_norm_gain'][0], sd0, bsz=bsz, t=t, chunk=chunk)
    qkv = proj.reshape(bsz, t, -1)[:, :, CD_QKV:CD_QKV + 3 * hd]
    conv_new = jnp.concatenate([conv_prev, qkv[:, t - (CONV_W - 1):]], axis=1)[:, -(CONV_W - 1):]
    return oc, od, (sc, conv_new, sd)


def _trunk(x, pos_offset, cache, prm, wts):
    bsz, t, d = x.shape
    n = bsz * t
    pos = pos_offset + jnp.arange(t, dtype=jnp.int32)
    probs = jax.nn.softmax(prm['d_lb_logits'], axis=0)
    lower_bounds = jnp.cumsum(probs, axis=0) - probs[0]
    xf = x.reshape(n, d)

    lc = None if cache is None else tuple(c[0] for c in cache[:6])
    proj = norm_matmul(xf, prm['norm_mix'][0], wts['w_in_ab'])
    a_out, b_out, st_even = _mixer_ab(proj, bsz, t, pos, prm, lc)
    xf = matmul_residual(a_out, b_out, wts['w_out_ab'], xf)
    xf = ffn_residual(xf, prm['norm_ffn'][0], wts['ffn_w1'], wts['ffn_w3'], wts['ffn_w2'])

    lc = None if cache is None else tuple(c[0] for c in cache[6:])
    proj = norm_matmul(xf, prm['norm_mix'][1], wts['w_in_cd'])
    c_out, d_out, st_odd = _mixer_cd(proj, bsz, t, prm, lower_bounds[1], lc)
    xf = matmul_residual(c_out, d_out, wts['w_out_cd'], xf)
    xf = moe_residual(xf, prm['norm_ffn'][1], wts['moe_router'], wts['moe_w1'], wts['moe_w3'], wts['moe_w2'])

    new_state = tuple(s[None] for s in st_even + st_odd)
    return xf.reshape(bsz, t, d), new_state


def kernel(x_prompt, x_sample, cache_a_k, cache_a_v, cache_a_kidx, state_b_c, state_b_n, state_b_m,
           state_c_s, state_c_conv, state_d_s, norm_mix, norm_ffn, w_in_ab, w_out_ab, a_q_gain, a_k_gain,
           b_gate_bias, b_norm_gain, w_in_cd, w_out_cd, c_conv_w, c_a_log, c_dt_bias, c_norm_gain,
           d_lb_logits, d_norm_gain, ffn_w1, ffn_w3, ffn_w2, moe_router, moe_w1, moe_w3, moe_w2):
    prm = dict(norm_mix=norm_mix, norm_ffn=norm_ffn, a_q_gain=a_q_gain, a_k_gain=a_k_gain,
               b_gate_bias=b_gate_bias, b_norm_gain=b_norm_gain, c_conv_w=c_conv_w, c_a_log=c_a_log,
               c_dt_bias=c_dt_bias, c_norm_gain=c_norm_gain, d_lb_logits=d_lb_logits, d_norm_gain=d_norm_gain)
    wts = dict(w_in_ab=_permute_w_in_ab(w_in_ab[0]).astype(BF16), w_out_ab=w_out_ab[0].astype(BF16),
               w_in_cd=_permute_w_in_cd(w_in_cd[0]).astype(BF16), w_out_cd=w_out_cd[0].astype(BF16),
               ffn_w1=ffn_w1[0].astype(BF16), ffn_w3=ffn_w3[0].astype(BF16), ffn_w2=ffn_w2[0].astype(BF16),
               moe_router=_pad_cols(moe_router[0]),
               moe_w1=moe_w1[0].astype(BF16), moe_w3=moe_w3[0].astype(BF16), moe_w2=moe_w2[0].astype(BF16))
    cache = (cache_a_k, cache_a_v, cache_a_kidx, state_b_c, state_b_n, state_b_m, state_c_s, state_c_conv, state_d_s)
    y_prompt, st_p = _trunk(x_prompt, 0, None, prm, wts)
    y_sample, st_s = _trunk(x_sample, cache_a_k.shape[2], cache, prm, wts)
    return (y_prompt, y_sample) + st_p + st_s
```

```python
import functools
import math

import jax
import jax.numpy as jnp
import numpy as np
from jax import lax
from jax.experimental import pallas as pl
from jax.experimental.pallas import tpu as pltpu

F32 = jnp.float32
BF16 = jnp.bfloat16

EPS = 1e-6
ROPE_THETA = 500000.0
ROT_FRACTION = 4
CHUNK = 64
A_HEADS, A_KV_HEADS, HEAD_DIM = 8, 2, 64
IDX_HEADS, IDX_DIM = 4, 64
TOPK_MAX, Q_BLOCK = 256, 128
B_HEADS, B_QK_DIM, B_V_DIM = 4, 64, 128
C_HEADS, C_DIM, CONV_W = 4, 128, 4
D_HEADS, D_EXPAND, D_V_DIM = 4, 128, 128
N_EXPERTS, TOP_K_EXPERTS = 8, 2

LANE = 128
VMEM_LIMIT = 48 * 1024 * 1024

EVEN_SPLITS = (A_HEADS * HEAD_DIM, A_KV_HEADS * HEAD_DIM, A_KV_HEADS * HEAD_DIM,
               IDX_HEADS * IDX_DIM, IDX_DIM, IDX_HEADS,
               B_HEADS * B_QK_DIM, B_HEADS * B_QK_DIM, B_HEADS * B_V_DIM,
               B_HEADS, B_HEADS, B_HEADS * B_V_DIM)
ODD_SPLITS = (3 * C_HEADS * C_DIM, C_HEADS, C_HEADS, C_HEADS * C_DIM,
              D_HEADS * D_EXPAND, D_HEADS * D_EXPAND, D_HEADS * D_V_DIM, D_HEADS * D_V_DIM)


def _split_cols(p, widths):
    cuts = [int(c) for c in np.cumsum(widths)[:-1]]
    return jnp.split(p, cuts, axis=-1)


def _row_tile(n, target):
    t = min(n, target)
    while n % t:
        t //= 2
    return t


def _col_tile(n, target):
    best = LANE
    for k in range(1, n // LANE + 1):
        c = k * LANE
        if n % c == 0 and c <= target:
            best = c
    return best


def _rms_rows(x, gain):
    return x * lax.rsqrt(jnp.mean(x * x, axis=-1, keepdims=True) + EPS) * gain


def _norm_matmul_kernel(x_ref, g_ref, w_ref, o_ref, xn_ref):
    @pl.when(pl.program_id(1) == 0)
    def _():
        xn_ref[...] = _rms_rows(x_ref[...], g_ref[...]).astype(BF16)

    o_ref[...] = jnp.dot(xn_ref[...], w_ref[...], preferred_element_type=F32)


def norm_matmul(x, gain, w):
    n, d = x.shape
    m = w.shape[1]
    tm = _row_tile(n, 1024)
    tn = _col_tile(m, 1536)
    return pl.pallas_call(
        _norm_matmul_kernel,
        grid=(n // tm, m // tn),
        in_specs=[pl.BlockSpec((tm, d), lambda i, j: (i, 0)),
                  pl.BlockSpec((1, d), lambda i, j: (0, 0)),
                  pl.BlockSpec((d, tn), lambda i, j: (0, j))],
        out_specs=pl.BlockSpec((tm, tn), lambda i, j: (i, j)),
        out_shape=jax.ShapeDtypeStruct((n, m), F32),
        scratch_shapes=[pltpu.VMEM((tm, d), BF16)],
        compiler_params=pltpu.CompilerParams(
            dimension_semantics=("parallel", "arbitrary"), vmem_limit_bytes=VMEM_LIMIT),
        name="norm_matmul",
    )(x, gain.reshape(1, d), w)


def _matmul_res_kernel(a_ref, b_ref, w_ref, r_ref, o_ref):
    ka = a_ref.shape[1]
    o_ref[...] = (r_ref[...] + jnp.dot(a_ref[...].astype(BF16), w_ref[0:ka, :], preferred_element_type=F32)
                  + jnp.dot(b_ref[...].astype(BF16), w_ref[ka:, :], preferred_element_type=F32))


def matmul_residual(a, b, w, res):
    n, ka = a.shape
    kb = b.shape[1]
    m = w.shape[1]
    tm = _row_tile(n, 1024)
    return pl.pallas_call(
        _matmul_res_kernel,
        grid=(n // tm,),
        in_specs=[pl.BlockSpec((tm, ka), lambda i: (i, 0)),
                  pl.BlockSpec((tm, kb), lambda i: (i, 0)),
                  pl.BlockSpec((ka + kb, m), lambda i: (0, 0)),
                  pl.BlockSpec((tm, m), lambda i: (i, 0))],
        out_specs=pl.BlockSpec((tm, m), lambda i: (i, 0)),
        out_shape=jax.ShapeDtypeStruct((n, m), F32),
        compiler_params=pltpu.CompilerParams(
            dimension_semantics=("parallel",), vmem_limit_bytes=VMEM_LIMIT),
        name="matmul_residual",
    )(a, b, w, res)


def _swiglu_tile(xn, w1, w3):
    h1 = jnp.dot(xn, w1, preferred_element_type=F32)
    h3 = jnp.dot(xn, w3, preferred_element_type=F32)
    return h1 * jax.nn.sigmoid(h1) * h3


def _ffn_kernel(x_ref, g_ref, w1_ref, w3_ref, w2_ref, o_ref, xn_ref):
    @pl.when(pl.program_id(1) == 0)
    def _():
        x = x_ref[...]
        xn_ref[...] = _rms_rows(x, g_ref[...]).astype(BF16)
        o_ref[...] = x

    act = _swiglu_tile(xn_ref[...], w1_ref[...], w3_ref[...])
    o_ref[...] += jnp.dot(act.astype(BF16), w2_ref[...], preferred_element_type=F32)


def ffn_residual(x, gain, w1, w3, w2):
    n, d = x.shape
    f = w1.shape[1]
    tm = _row_tile(n, 1024)
    tf = _col_tile(f, 512)
    return pl.pallas_call(
        _ffn_kernel,
        grid=(n // tm, f // tf),
        in_specs=[pl.BlockSpec((tm, d), lambda i, j: (i, 0)),
                  pl.BlockSpec((1, d), lambda i, j: (0, 0)),
                  pl.BlockSpec((d, tf), lambda i, j: (0, j)),
                  pl.BlockSpec((d, tf), lambda i, j: (0, j)),
                  pl.BlockSpec((tf, d), lambda i, j: (j, 0))],
        out_specs=pl.BlockSpec((tm, d), lambda i, j: (i, 0)),
        out_shape=jax.ShapeDtypeStruct((n, d), F32),
        scratch_shapes=[pltpu.VMEM((tm, d), BF16)],
        compiler_params=pltpu.CompilerParams(
            dimension_semantics=("parallel", "arbitrary"), vmem_limit_bytes=VMEM_LIMIT),
        name="ffn_residual",
    )(x, gain.reshape(1, d), w1, w3, w2)


MOE_VMEM_LIMIT = 58 * 1024 * 1024
MOE_CAPS = (256, 384, 512)


def _moe_route_kernel(x_ref, g_ref, r_ref, xn_ref, comb_ref, post_ref, cnt_ref):
    x = x_ref[...]
    tm = x.shape[0]
    xn = _rms_rows(x, g_ref[...])
    xn_ref[...] = xn.astype(BF16)
    logits = jnp.dot(xn, r_ref[...], preferred_element_type=F32, precision=lax.Precision.HIGHEST)
    lane = lax.broadcasted_iota(jnp.int32, logits.shape, 1)
    logits = jnp.where(lane < N_EXPERTS, logits, -jnp.inf)
    m1 = jnp.max(logits, axis=-1, keepdims=True)
    i1 = jnp.min(jnp.where(logits == m1, lane, LANE), axis=-1, keepdims=True)
    rest = jnp.where(lane == i1, -jnp.inf, logits)
    m2 = jnp.max(rest, axis=-1, keepdims=True)
    i2 = jnp.min(jnp.where(rest == m2, lane, LANE), axis=-1, keepdims=True)
    e2 = jnp.exp(m2 - m1)
    den = 1.0 + e2
    comb_ref[...] = jnp.where(lane == i1, 1.0 / den, 0.0) + jnp.where(lane == i2, e2 / den, 0.0)
    chosen = (lane == i1) | (lane == i2)
    sel = jnp.where(chosen, 1.0, 0.0)
    tril16 = jnp.where(_tri_mask(LANE), 1.0, 0.0).astype(BF16)
    seen = jnp.zeros((1, LANE), F32)
    ranks = []
    for blk in range(tm // LANE):
        sb = sel[blk * LANE:(blk + 1) * LANE]
        ranks.append(_dot(tril16, sb.astype(BF16)) + seen - 1.0)
        seen = seen + jnp.sum(sb, axis=0, keepdims=True)
    rank = jnp.where(chosen, jnp.concatenate(ranks, axis=0), -1.0)
    post_ref[0] = rank.T[0:N_EXPERTS, :]
    cnt_ref[0] = seen


def _moe_expert_kernel(cnt_ref, xn_ref, comb_ref, post_ref, x_ref, w1_ref, w3_ref, w2_ref, o_ref,
                       xe_scr, y_scr, *, caps):
    i = pl.program_id(0)
    e = pl.program_id(1)
    j = pl.program_id(2)
    last = pl.num_programs(2) - 1
    tm = x_ref.shape[0]

    @pl.when(jnp.logical_and(e == 0, j == 0))
    def _():
        o_ref[...] = x_ref[...]

    cnt = cnt_ref[i * N_EXPERTS + e]

    def expert_step(cap):
        def pick():
            rank_row = post_ref[0, pl.ds(e, 1), :].astype(jnp.int32)
            slot = lax.broadcasted_iota(jnp.int32, (cap, tm), 0)
            return jnp.where(rank_row == slot, 1.0, 0.0).astype(BF16)

        @pl.when(j == 0)
        def _():
            xe_scr[0:cap, :] = _dot(pick(), xn_ref[...]).astype(BF16)

        act = _swiglu_tile(xe_scr[0:cap, :], w1_ref[0], w3_ref[0])
        yj = _dot(act.astype(BF16), w2_ref[0])

        @pl.when(j == 0)
        def _():
            y_scr[0:cap, :] = yj

        @pl.when(j > 0)
        def _():
            y_scr[0:cap, :] += yj

        @pl.when(j == last)
        def _():
            comb = comb_ref[...]
            lane = lax.broadcasted_iota(jnp.int32, comb.shape, 1)
            gate = jnp.sum(jnp.where(lane == e, comb, 0.0), axis=-1, keepdims=True)
            hi, lo = _split2(y_scr[0:cap, :])
            p = pick()
            o_ref[...] += gate * (_dot_tn(p, hi) + _dot_tn(p, lo))

    lo = 0
    for cap in caps:
        @pl.when(jnp.logical_and(cnt > lo, cnt <= cap))
        def _(cap=cap):
            expert_step(cap)
        lo = cap


def moe_residual(x, gain, router, w1, w3, w2):
    n, d = x.shape
    ne, _, f = w1.shape
    tm = _row_tile(n, 1024)
    tf = _col_tile(f, 896)
    nt = n // tm
    xn, comb, post, cnt = pl.pallas_call(
        _moe_route_kernel,
        grid=(nt,),
        in_specs=[pl.BlockSpec((tm, d), lambda i: (i, 0)),
                  pl.BlockSpec((1, d), lambda i: (0, 0)),
                  pl.BlockSpec((d, LANE), lambda i: (0, 0))],
        out_specs=[pl.BlockSpec((tm, d), lambda i: (i, 0)),
                   pl.BlockSpec((tm, LANE), lambda i: (i, 0)),
                   pl.BlockSpec((1, ne, tm), lambda i: (i, 0, 0)),
                   pl.BlockSpec((1, 1, LANE), lambda i: (i, 0, 0))],
        out_shape=[jax.ShapeDtypeStruct((n, d), BF16), jax.ShapeDtypeStruct((n, LANE), F32),
                   jax.ShapeDtypeStruct((nt, ne, tm), F32), jax.ShapeDtypeStruct((nt, 1, LANE), F32)],
        compiler_params=pltpu.CompilerParams(dimension_semantics=("parallel",), vmem_limit_bytes=VMEM_LIMIT),
        name="moe_route",
    )(x, gain.reshape(1, d), router)
    counts = cnt[:, 0, :ne].astype(jnp.int32).reshape(nt * ne)
    caps = tuple(c for c in MOE_CAPS if c < tm) + (tm,)
    grid_spec = pltpu.PrefetchScalarGridSpec(
        num_scalar_prefetch=1,
        grid=(nt, ne, f // tf),
        in_specs=[pl.BlockSpec((tm, d), lambda i, e, j, c: (i, 0)),
                  pl.BlockSpec((tm, LANE), lambda i, e, j, c: (i, 0)),
                  pl.BlockSpec((1, ne, tm), lambda i, e, j, c: (i, 0, 0)),
                  pl.BlockSpec((tm, d), lambda i, e, j, c: (i, 0)),
                  pl.BlockSpec((1, d, tf), lambda i, e, j, c: (e, 0, j)),
                  pl.BlockSpec((1, d, tf), lambda i, e, j, c: (e, 0, j)),
                  pl.BlockSpec((1, tf, d), lambda i, e, j, c: (e, j, 0))],
        out_specs=pl.BlockSpec((tm, d), lambda i, e, j, c: (i, 0)),
        scratch_shapes=[pltpu.VMEM((tm, d), BF16), pltpu.VMEM((tm, d), F32)])
    return pl.pallas_call(
        functools.partial(_moe_expert_kernel, caps=caps),
        grid_spec=grid_spec,
        out_shape=jax.ShapeDtypeStruct((n, d), F32),
        compiler_params=pltpu.CompilerParams(
            dimension_semantics=("parallel", "arbitrary", "arbitrary"), vmem_limit_bytes=MOE_VMEM_LIMIT),
        name="moe_experts",
    )(counts, xn, comb, post, x, w1, w3, w2)


AB_QA, AB_VB, AB_OB, AB_QI, AB_QB, AB_KB, AB_KA, AB_VA, AB_MISC, AB_TOTAL = (
    0, 512, 1024, 1536, 1792, 2048, 2304, 2432, 2560, 2688)
MISC_WI, MISC_IB, MISC_FB = 64, 68, 72
HALF = LANE // 2
KEY_CHUNK = 512
MASKED = -1e30
KEY_OF_NEG_INF = -2139095041
I16_MIN, I16_MAX = -32768, 32767


def _permute_w_in_ab(w):
    qa, ka, va, qi, ki, wi, qb, kb, vb, ib, fb, ob = _split_cols(w, EVEN_SPLITS)
    pad = jnp.zeros((w.shape[0], LANE - IDX_DIM - 3 * IDX_HEADS), w.dtype)
    return jnp.concatenate([qa, vb, ob, qi, qb, kb, ka, va, ki, wi, ib, fb, pad], axis=1)


def _rope_tables(pos):
    rot = HEAD_DIM // ROT_FRACTION
    half = rot // 2
    inv_freq = ROPE_THETA ** (-jnp.arange(half, dtype=F32) * 2.0 / rot)
    ang = pos.astype(F32)[:, None] * inv_freq[None, :]
    cos, sin = jnp.cos(ang), jnp.sin(ang)
    t = pos.shape[0]
    one = jnp.ones((t, HEAD_DIM - rot), F32)
    zero_r = jnp.zeros((t, HEAD_DIM - rot), F32)
    zero_h = jnp.zeros((t, half), F32)
    c = jnp.concatenate([cos, cos, one], axis=1)
    s_up = jnp.concatenate([-sin, zero_h, zero_r], axis=1)
    s_dn = jnp.concatenate([zero_h, sin, zero_r], axis=1)
    return tuple(jnp.concatenate([a, a], axis=1) for a in (c, s_up, s_dn))


def _rope_tile(x, c, s_up, s_dn):
    half = HEAD_DIM // ROT_FRACTION // 2
    return x * c + pltpu.roll(x, LANE - half, 1) * s_up + pltpu.roll(x, half, 1) * s_dn


def _head_norm_tile(x, gain, same_head):
    sq = x * x
    hi = sq.astype(BF16)
    lo = (sq - hi.astype(F32)).astype(BF16)
    ss = (jnp.dot(hi, same_head, preferred_element_type=F32)
          + jnp.dot(lo, same_head, preferred_element_type=F32))
    return x * lax.rsqrt(ss * (1.0 / HEAD_DIM) + EPS) * gain


def _aprep_kernel(qa_ref, ka_ref, va_ref, qi_ref, misc_ref, c_ref, su_ref, sd_ref, qg_ref, kg_ref,
                  qpad_ref, qipad_ref, k16_ref, v16_ref, ki16_ref, k32_ref, v32_ref, ki32_ref):
    c, su, sd = c_ref[...], su_ref[...], sd_ref[...]
    tm = c.shape[0]
    row = lax.broadcasted_iota(jnp.int32, (LANE, LANE), 0)
    col = lax.broadcasted_iota(jnp.int32, (LANE, LANE), 1)
    same_head = jnp.where(row // HALF == col // HALF, 1.0, 0.0).astype(BF16)
    lane = lax.broadcasted_iota(jnp.int32, (tm, LANE), 1)
    low = lane < HALF

    heads_per_group = A_HEADS // A_KV_HEADS
    for p in range(A_HEADS // 2):
        y = _rope_tile(_head_norm_tile(qa_ref[:, p * LANE:(p + 1) * LANE], qg_ref[...], same_head), c, su, sd)
        y = y * HEAD_DIM ** -0.5
        y_sw = pltpu.roll(y, HALF, 1)
        for o in range(2):
            h = 2 * p + o
            g = h // heads_per_group
            src = y if o == g else y_sw
            qpad_ref[:, h * LANE:(h + 1) * LANE] = jnp.where(low if g == 0 else ~low, src, 0.0).astype(BF16)
    k = _rope_tile(_head_norm_tile(ka_ref[...], kg_ref[...], same_head), c, su, sd)
    k32_ref[...] = k
    k16_ref[...] = k.astype(BF16)
    v = va_ref[...]
    v32_ref[...] = v
    v16_ref[...] = v.astype(BF16)
    for p in range(IDX_HEADS // 2):
        y = _rope_tile(qi_ref[:, p * LANE:(p + 1) * LANE], c, su, sd)
        y_sw = pltpu.roll(y, HALF, 1)
        qipad_ref[:, (2 * p) * LANE:(2 * p + 1) * LANE] = jnp.where(low, y, 0.0).astype(BF16)
        qipad_ref[:, (2 * p + 1) * LANE:(2 * p + 2) * LANE] = jnp.where(low, y_sw, 0.0).astype(BF16)
    ki = _rope_tile(misc_ref[...], c, su, sd)
    ki32_ref[...] = ki[:, :IDX_DIM]
    ki16_ref[...] = jnp.where(low, ki, 0.0).astype(BF16)


def dsa_prep(proj, pos, q_gain, k_gain, t):
    n = proj.shape[0]
    tm = _row_tile(n, 512)
    tabs = _rope_tables(pos)
    if t < tm:
        tabs = tuple(jnp.tile(a, (tm // t, 1)) for a in tabs)
    nt = tabs[0].shape[0] // tm
    tab_spec = pl.BlockSpec((tm, LANE), lambda i: (i % nt, 0))
    gain_spec = pl.BlockSpec((1, LANE), lambda i: (0, 0))

    def col(width, offset):
        return pl.BlockSpec((tm, width), lambda i: (i, offset // width))

    def out(width, dtype):
        return (jax.ShapeDtypeStruct((n, width), dtype), pl.BlockSpec((tm, width), lambda i: (i, 0)))

    outs = [out(A_HEADS * LANE, BF16), out(IDX_HEADS * LANE, BF16), out(LANE, BF16), out(LANE, BF16),
            out(LANE, BF16), out(LANE, F32), out(LANE, F32), out(IDX_DIM, F32)]
    return pl.pallas_call(
        _aprep_kernel,
        grid=(n // tm,),
        in_specs=[col(A_HEADS * HEAD_DIM, AB_QA), col(LANE, AB_KA), col(LANE, AB_VA),
                  col(IDX_HEADS * IDX_DIM, AB_QI), col(LANE, AB_MISC), tab_spec, tab_spec, tab_spec,
                  gain_spec, gain_spec],
        out_specs=[o[1] for o in outs],
        out_shape=[o[0] for o in outs],
        compiler_params=pltpu.CompilerParams(dimension_semantics=("parallel",), vmem_limit_bytes=VMEM_LIMIT),
        name="dsa_prep",
    )(proj, proj, proj, proj, proj, *tabs, jnp.tile(q_gain, 2).reshape(1, LANE), jnp.tile(k_gain, 2).reshape(1, LANE))


def _dsa_kernel(q_ref, qi_ref, misc_ref, lim_ref, k_ref, v_ref, ki_ref, o_ref, key_ref, bias_ref, hi_ref, lo_ref,
                *, causal, n_sel):
    qb = q_ref.shape[0]
    kc = KEY_CHUNK
    n_idx = IDX_HEADS
    hpg = A_HEADS // A_KV_HEADS
    if causal:
        nch = ((pl.program_id(1) + 1) * qb + kc - 1) // kc
    else:
        nch = k_ref.shape[0] // kc
    limit = lim_ref[...]
    misc = misc_ref[...]
    wscale = IDX_HEADS ** -0.5 * IDX_DIM ** -0.5
    w = [misc[:, MISC_WI + j:MISC_WI + j + 1] * wscale for j in range(n_idx)]
    qi = jnp.concatenate([qi_ref[:, j * LANE:(j + 1) * LANE] for j in range(n_idx)], axis=0)
    nt = (((1,), (1,)), ((), ()))

    def score_body(c, carry):
        off = pl.multiple_of(c * kc, kc)
        lg = lax.dot_general(qi, ki_ref[pl.ds(off, kc), :], nt, preferred_element_type=F32)
        lg = jnp.maximum(lg, 0.0).reshape(n_idx, qb, kc)
        s = w[0] * lg[0]
        for j in range(1, n_idx):
            s = s + w[j] * lg[j]
        kidx = off + lax.broadcasted_iota(jnp.int32, (qb, kc), 1)
        s = jnp.where(kidx < limit, s, -jnp.inf)
        bits = lax.bitcast_convert_type(s, jnp.int32)
        key = jnp.where(bits < 0, bits ^ 0x7FFFFFFF, bits)
        key_ref[:, pl.ds(off, kc)] = key
        hi_ref[:, pl.ds(off, kc)] = jnp.right_shift(key, 16).astype(jnp.int16)
        return carry

    lax.fori_loop(0, nch, score_body, 0)

    def count_ge(cand):
        def body(c, acc):
            off = pl.multiple_of(c * kc, kc)
            m = jnp.where(key_ref[:, pl.ds(off, kc)] >= cand, 1.0, 0.0)
            for t in range(kc // LANE):
                acc = acc + m[:, t * LANE:(t + 1) * LANE]
            return acc

        acc = lax.fori_loop(0, nch, body, jnp.zeros((qb, LANE), F32))
        return jnp.sum(acc, axis=1, keepdims=True)

    def count_ge16(ref, cand32):
        cand = cand32.astype(jnp.int16)

        def body(c, acc):
            off = pl.multiple_of(c * kc, kc)
            m = jnp.where(ref[:, pl.ds(off, kc)] >= cand, jnp.int16(1), jnp.int16(0))
            for t in range(kc // LANE):
                acc = acc + m[:, t * LANE:(t + 1) * LANE]
            return acc

        acc = lax.fori_loop(0, nch, body, jnp.zeros((qb, LANE), jnp.int16))
        return jnp.sum(acc.astype(F32), axis=1, keepdims=True)

    def kth_largest16(ref, want):
        tau = jnp.where(count_ge16(ref, jnp.zeros((qb, 1), jnp.int32)) >= want, 0, I16_MIN).astype(jnp.int32)

        def bisect(i, tau):
            cand = tau | jnp.left_shift(jnp.int32(1), 14 - i)
            return jnp.where(count_ge16(ref, cand) >= want, cand, tau)

        return lax.fori_loop(0, 15, bisect, tau)

    want = float(n_sel)
    tau_hi = kth_largest16(hi_ref, want)
    above = jnp.where(tau_hi < I16_MAX, count_ge16(hi_ref, jnp.minimum(tau_hi + 1, I16_MAX)), 0.0)

    def low_body(c, carry):
        off = pl.multiple_of(c * kc, kc)
        key = key_ref[:, pl.ds(off, kc)]
        low = (key & 0xFFFF) + I16_MIN
        lo_ref[:, pl.ds(off, kc)] = jnp.where(jnp.right_shift(key, 16) == tau_hi, low, I16_MIN).astype(jnp.int16)
        return carry

    lax.fori_loop(0, nch, low_body, 0)
    tau_lo = kth_largest16(lo_ref, want - above)
    tau = jnp.left_shift(tau_hi, 16) + (tau_lo - I16_MIN)

    room = want - count_ge(tau + 1)
    r_i = lax.broadcasted_iota(jnp.int32, (LANE, LANE), 0)
    c_i = lax.broadcasted_iota(jnp.int32, (LANE, LANE), 1)
    prefix_ones = jnp.where(r_i <= c_i, 1.0, 0.0).astype(BF16)

    def bias_body(c, seen):
        off = pl.multiple_of(c * kc, kc)
        for t in range(kc // LANE):
            x = key_ref[:, pl.ds(off + t * LANE, LANE)]
            eq = x == tau
            eqf = jnp.where(eq, 1.0, 0.0)
            rank = jnp.dot(eqf.astype(BF16), prefix_ones, preferred_element_type=F32) + seen
            sel = ((x > tau) | (eq & (rank <= room))) & (x != KEY_OF_NEG_INF)
            bias_ref[:, pl.ds(off + t * LANE, LANE)] = jnp.where(sel, 0.0, MASKED)
            seen = seen + jnp.sum(eqf, axis=1, keepdims=True)
        return seen

    lax.fori_loop(0, nch, bias_body, jnp.zeros((qb, 1), F32))

    lane = lax.broadcasted_iota(jnp.int32, (qb, LANE), 1)
    qgs = [jnp.concatenate([q_ref[:, (hpg * g + h) * LANE:(hpg * g + h + 1) * LANE] for h in range(hpg)], axis=0)
           for g in range(A_KV_HEADS)]

    def att_body(c, carry):
        off = pl.multiple_of(c * kc, kc)
        kch = k_ref[pl.ds(off, kc), :]
        vch = v_ref[pl.ds(off, kc), :]
        bias = bias_ref[:, pl.ds(off, kc)][None]
        new = []
        for g in range(A_KV_HEADS):
            m, l, acc = carry[g]
            s = lax.dot_general(qgs[g], kch, nt, preferred_element_type=F32)
            s = (s.reshape(hpg, qb, kc) + bias).reshape(hpg * qb, kc)
            m_new = jnp.maximum(m, jnp.max(s, axis=1, keepdims=True))
            alpha = jnp.exp(m - m_new)
            p = jnp.exp(s - m_new)
            l = alpha * l + jnp.sum(p, axis=1, keepdims=True)
            acc = alpha * acc + jnp.dot(p.astype(BF16), vch, preferred_element_type=F32)
            new.append((m_new, l, acc))
        return tuple(new)

    init = tuple((jnp.full((hpg * qb, 1), MASKED, F32), jnp.zeros((hpg * qb, 1), F32),
                  jnp.zeros((hpg * qb, LANE), F32)) for _ in range(A_KV_HEADS))
    res = lax.fori_loop(0, nch, att_body, init)
    outs = []
    for g in range(A_KV_HEADS):
        _, l, acc = res[g]
        og = acc / l
        for h in range(hpg):
            oh = og[h * qb:(h + 1) * qb]
            outs.append(oh if (h % 2) == g else pltpu.roll(oh, HALF, 1))
    for p in range(A_HEADS // 2):
        o_ref[:, p * LANE:(p + 1) * LANE] = jnp.where(lane < HALF, outs[2 * p], outs[2 * p + 1])


def dsa_attention(qpad, qipad, proj, limit, k16, v16, ki16, *, bsz, tq, tk, causal, n_sel):
    qb = min(Q_BLOCK, tq)
    nqb = tq // qb
    assert tk % KEY_CHUNK == 0 and tk >= n_sel
    kern = functools.partial(_dsa_kernel, causal=causal, n_sel=n_sel)

    def qspec(width, col=0):
        return pl.BlockSpec((qb, width), lambda b, i: (b * nqb + i, col))

    def kspec():
        return pl.BlockSpec((tk, LANE), lambda b, i: (b, 0))

    return pl.pallas_call(
        kern,
        grid=(bsz, nqb),
        in_specs=[qspec(A_HEADS * LANE), qspec(IDX_HEADS * LANE), qspec(LANE, AB_MISC // LANE), qspec(1),
                  kspec(), kspec(), kspec()],
        out_specs=qspec(A_HEADS * HEAD_DIM),
        out_shape=jax.ShapeDtypeStruct((bsz * tq, A_HEADS * HEAD_DIM), F32),
        scratch_shapes=[pltpu.VMEM((qb, tk), jnp.int32), pltpu.VMEM((qb, tk), F32),
                        pltpu.VMEM((qb, tk), jnp.int16), pltpu.VMEM((qb, tk), jnp.int16)],
        compiler_params=pltpu.CompilerParams(
            dimension_semantics=("parallel", "arbitrary"), vmem_limit_bytes=VMEM_LIMIT),
        name="dsa_attention",
    )(qpad, qipad, proj, limit, k16, v16, ki16)


SUB = 16


def _dot(a, b):
    return jnp.dot(a, b, preferred_element_type=F32)


def _dot_nt(a, b):
    return lax.dot_general(a, b, (((1,), (1,)), ((), ())), preferred_element_type=F32)


def _dot_tn(a, b):
    return lax.dot_general(a, b, (((0,), (0,)), ((), ())), preferred_element_type=F32)


def _split2(x):
    hi = x.astype(BF16)
    return hi, (x - hi.astype(F32)).astype(BF16)


def _split3(x):
    hi = x.astype(BF16)
    r = x - hi.astype(F32)
    mid = r.astype(BF16)
    return hi, mid, (r - mid.astype(F32)).astype(BF16)


def _cumsum_rows(x, tril16):
    hi, mid, lo = _split3(x)
    return _dot(tril16, hi) + _dot(tril16, mid) + _dot(tril16, lo)


def _dot_f32(a, b):
    ah, al = _split2(a)
    bh, bl = _split2(b)
    return _dot(ah, bh) + (_dot(ah, bl) + _dot(al, bh))


def _tri_mask(n, strict=False):
    r = lax.broadcasted_iota(jnp.int32, (n, n), 0)
    c = lax.broadcasted_iota(jnp.int32, (n, n), 1)
    return r > c if strict else r >= c


def _rows_to_lanes(x):
    rows = x.shape[0]
    if rows < LANE:
        x = jnp.concatenate([x, jnp.zeros((LANE - rows, LANE), x.dtype)], axis=0)
    return x.T


def _chunk_call(kern, *, bsz, nc, rows, ins, outs, scratch, name):
    def spec(a, kind, width, offset):
        if kind == 'rows':
            return pl.BlockSpec((rows, width), lambda b, c: (b * nc + c, offset // width))
        if kind == 'batch':
            return pl.BlockSpec((1,) + tuple(a.shape[1:]), lambda b, c: (b,) + (0,) * (len(a.shape) - 1))
        return pl.BlockSpec(tuple(a.shape), lambda b, c: (0,) * len(a.shape))

    return pl.pallas_call(
        kern,
        grid=(bsz, nc),
        in_specs=[spec(*i) for i in ins],
        out_specs=[spec(*o) for o in outs],
        out_shape=[o[0] for o in outs],
        scratch_shapes=scratch,
        compiler_params=pltpu.CompilerParams(
            dimension_semantics=("parallel", "arbitrary"), vmem_limit_bytes=VMEM_LIMIT),
        name=name,
    )(*[i[0] for i in ins])


def _mlstm_kernel(q_ref, k_ref, v_ref, og_ref, misc_ref, gb_ref, gain_ref, c0_ref, n0_ref, m0_ref,
                  h_ref, c_out_ref, n_out_ref, m_out_ref, c_scr, n_scr, m_scr):
    ci = pl.program_id(1)

    @pl.when(ci == 0)
    def _():
        c_scr[...] = c0_ref[0]
        n_scr[...] = n0_ref[0]
        m_scr[...] = m0_ref[0]

    rows = q_ref.shape[0]
    hr = B_HEADS * rows
    wq = B_HEADS * B_QK_DIM
    tril16 = jnp.where(_tri_mask(rows), 1.0, 0.0).astype(BF16)
    gates = misc_ref[...] + gb_ref[...]
    bcum = _cumsum_rows(jax.nn.log_sigmoid(gates), tril16)
    m_all = m_scr[...]

    def stack(f):
        return jnp.concatenate([f(h) for h in range(B_HEADS)], axis=0)

    lane_q = lax.broadcasted_iota(jnp.int32, (rows, wq), 1)
    qx, kx = q_ref[...], k_ref[...]
    q_all = stack(lambda h: jnp.where(lane_q // B_QK_DIM == h, qx, 0.0))
    k_all = stack(lambda h: jnp.where(lane_q // B_QK_DIM == h, kx, 0.0)) * B_QK_DIM ** -0.5
    v_all = stack(lambda h: v_ref[:, h * B_V_DIM:(h + 1) * B_V_DIM])
    b_col = stack(lambda h: bcum[:, MISC_FB + h:MISC_FB + h + 1])
    i_col = stack(lambda h: gates[:, MISC_IB + h:MISC_IB + h + 1])
    m_col = stack(lambda h: jnp.broadcast_to(m_all[:, h:h + 1], (rows, 1)))
    b_end = stack(lambda h: jnp.broadcast_to(bcum[rows - 1:rows, MISC_FB + h:MISC_FB + h + 1], (rows, 1)))
    b_row = jnp.broadcast_to(b_col, (hr, LANE)).T[0:1, :]
    i_row = jnp.broadcast_to(i_col, (hr, LANE)).T[0:1, :]

    r = lax.broadcasted_iota(jnp.int32, (hr, hr), 0)
    c = lax.broadcasted_iota(jnp.int32, (hr, hr), 1)
    incl = ((r // rows) == (c // rows)) & (r >= c)
    dmat = jnp.where(incl, b_col - b_row + i_row, -jnp.inf)
    inter = b_col + m_col
    mrow = jnp.maximum(inter, jnp.max(dmat, axis=1, keepdims=True))
    w_state = jnp.exp(inter - mrow)
    q16 = q_all.astype(BF16)
    v16 = v_all.astype(BF16)
    scores = _dot_nt(q16, k_all.astype(BF16)) * jnp.exp(dmat - mrow)
    cs = c_scr[...]
    n_row = n_scr[...]
    num = _dot(scores.astype(BF16), v16) + w_state * _dot(q16, cs.astype(BF16))
    den = jnp.sum(scores, axis=1, keepdims=True) + w_state * jnp.sum(q_all * n_row, axis=1, keepdims=True)
    hh = num / jnp.maximum(jnp.abs(den), jnp.exp(-mrow))
    gain = gain_ref[...]
    for h in range(B_HEADS):
        h_ref[:, h * B_V_DIM:(h + 1) * B_V_DIM] = (_rms_rows(hh[h * rows:(h + 1) * rows], gain)
                                                   * jax.nn.sigmoid(og_ref[:, h * B_V_DIM:(h + 1) * B_V_DIM]))

    g_col = b_end - b_col + i_col
    lane1 = lax.broadcasted_iota(jnp.int32, (1, LANE), 1)
    m_next = m_all
    m_new_rows, keep_rows, keep_lanes = [], [], []
    for h in range(B_HEADS):
        m_h = m_all[:, h:h + 1]
        be = bcum[rows - 1:rows, MISC_FB + h:MISC_FB + h + 1]
        m_new = jnp.maximum(be + m_h, jnp.max(g_col[h * rows:(h + 1) * rows], axis=0, keepdims=True))
        keep = jnp.exp(be + m_h - m_new)
        m_next = jnp.where(lane1 == h, m_new, m_next)
        m_new_rows.append(jnp.broadcast_to(m_new, (rows, 1)))
        keep_rows.append(jnp.broadcast_to(keep, (B_QK_DIM, 1)))
        keep_lanes.append(jnp.broadcast_to(keep, (1, B_QK_DIM)))
    kw = k_all * jnp.exp(g_col - jnp.concatenate(m_new_rows, axis=0))
    c_scr[...] = jnp.concatenate(keep_rows, axis=0) * cs + _dot_tn(kw.astype(BF16), v16)
    n_scr[...] = jnp.concatenate(keep_lanes, axis=1) * n_row + jnp.sum(kw, axis=0, keepdims=True)
    m_scr[...] = m_next

    @pl.when(ci == pl.num_programs(1) - 1)
    def _():
        c_out_ref[0] = c_scr[...]
        n_out_ref[0] = n_scr[...]
        m_out_ref[0] = m_scr[...]


def mlstm_mixer(proj, gate_bias, gain, c0, n0, m0, *, bsz, t, chunk):
    nc = t // chunk
    gb = jnp.zeros((1, LANE), F32)
    gb = gb.at[0, MISC_IB:MISC_IB + B_HEADS].set(gate_bias[0]).at[0, MISC_FB:MISC_FB + B_HEADS].set(gate_bias[1])
    c0 = c0.reshape(bsz, B_HEADS * B_QK_DIM, B_V_DIM)
    n0 = n0.reshape(bsz, 1, B_HEADS * B_QK_DIM)
    m0 = jnp.pad(m0, ((0, 0), (0, LANE - B_HEADS))).reshape(bsz, 1, LANE)
    wq = B_HEADS * B_QK_DIM
    wv = B_HEADS * B_V_DIM
    h, c, n, m = _chunk_call(
        _mlstm_kernel, bsz=bsz, nc=nc, rows=chunk,
        ins=[(proj, 'rows', wq, AB_QB), (proj, 'rows', wq, AB_KB), (proj, 'rows', wv, AB_VB),
             (proj, 'rows', wv, AB_OB), (proj, 'rows', LANE, AB_MISC), (gb, 'const', 0, 0),
             (gain.reshape(1, B_V_DIM), 'const', 0, 0), (c0, 'batch', 0, 0), (n0, 'batch', 0, 0),
             (m0, 'batch', 0, 0)],
        outs=[(jax.ShapeDtypeStruct((bsz * t, wv), F32), 'rows', wv, 0),
              (jax.ShapeDtypeStruct(c0.shape, F32), 'batch', 0, 0),
              (jax.ShapeDtypeStruct(n0.shape, F32), 'batch', 0, 0),
              (jax.ShapeDtypeStruct(m0.shape, F32), 'batch', 0, 0)],
        scratch=[pltpu.VMEM((wq, B_V_DIM), F32), pltpu.VMEM((1, wq), F32), pltpu.VMEM((1, LANE), F32)],
        name="mlstm_mixer")
    return (h, c.reshape(bsz, B_HEADS, B_QK_DIM, B_V_DIM), n.reshape(bsz, B_HEADS, B_QK_DIM),
            m.reshape(bsz, LANE)[:, :B_HEADS])


CD_QKV, CD_ZC, CD_QD, CD_FD, CD_VD, CD_GD, CD_MISC, CD_TOTAL = 0, 1536, 2048, 2560, 3072, 3584, 4096, 4224
MISC_BC, MISC_AC = 0, 4
TAIL = 8


def _permute_w_in_cd(w):
    qkv, bc, ac, zc, qd, fd, vd, gd = _split_cols(w, ODD_SPLITS)
    pad = jnp.zeros((w.shape[0], LANE - 2 * C_HEADS), w.dtype)
    return jnp.concatenate([qkv, zc, qd, fd, vd, gd, bc, ac, pad], axis=1)


def _gdn_kernel(qkv_ref, z_ref, misc_ref, cw_ref, alog_ref, dt_ref, gain_ref, s0_ref, tail0_ref,
                o_ref, s_out_ref, st_scr, tail_scr):
    ci = pl.program_id(1)

    @pl.when(ci == 0)
    def _():
        for h in range(C_HEADS):
            st_scr[:, h * C_DIM:(h + 1) * C_DIM] = s0_ref[0, h].T
        tail_scr[...] = tail0_ref[0]

    rows = qkv_ref.shape[0]
    width = qkv_ref.shape[1]
    x = qkv_ref[...]
    tail = tail_scr[...]
    row8 = lax.broadcasted_iota(jnp.int32, (TAIL, width), 0)
    acc = x * cw_ref[CONV_W - 1:CONV_W, :]
    for back in range(1, CONV_W):
        rolled = pltpu.roll(x, back, 0)
        first = jnp.where(row8 < back, pltpu.roll(tail, back, 0), rolled[0:TAIL])
        shifted = first if rows == TAIL else jnp.concatenate([first, rolled[TAIL:]], axis=0)
        acc = acc + shifted * cw_ref[CONV_W - 1 - back:CONV_W - back, :]
    tail_scr[...] = x[rows - TAIL:rows]
    conv = acc * jax.nn.sigmoid(acc)

    tril16 = jnp.where(_tri_mask(rows), 1.0, 0.0).astype(BF16)
    misc = misc_ref[...]
    beta_t = jax.nn.sigmoid(misc)
    g_t = -jnp.exp(alog_ref[...]) * jax.nn.softplus(misc + dt_ref[...])
    gcum = _cumsum_rows(g_t, tril16)

    hd = C_HEADS * C_DIM
    hr = C_HEADS * rows

    def stack(f):
        return jnp.concatenate([f(h) for h in range(C_HEADS)], axis=0)

    def l2n(v):
        return v * lax.rsqrt(jnp.sum(v * v, axis=-1, keepdims=True) + EPS)

    q_all = stack(lambda h: l2n(conv[:, h * C_DIM:(h + 1) * C_DIM])) * C_DIM ** -0.5
    k_all = stack(lambda h: l2n(conv[:, hd + h * C_DIM:hd + (h + 1) * C_DIM]))
    v_all = stack(lambda h: conv[:, 2 * hd + h * C_DIM:2 * hd + (h + 1) * C_DIM])
    beta = stack(lambda h: beta_t[:, MISC_BC + h:MISC_BC + h + 1])
    gc = stack(lambda h: gcum[:, MISC_AC + h:MISC_AC + h + 1])
    g_end = stack(lambda h: jnp.broadcast_to(gcum[rows - 1:rows, MISC_AC + h:MISC_AC + h + 1], (rows, 1)))
    gc_row = jnp.broadcast_to(gc, (hr, LANE)).T[0:1, :]

    r = lax.broadcasted_iota(jnp.int32, (hr, hr), 0)
    c = lax.broadcasted_iota(jnp.int32, (hr, hr), 1)
    same = (r // rows) == (c // rows)
    incl = same & (r >= c)
    strict = same & (r > c)
    decay = jnp.exp(jnp.where(incl, gc - gc_row, -jnp.inf))
    k16 = k_all.astype(BF16)
    a_mat = jnp.where(strict, beta * _dot_nt(k16, k16) * decay, 0.0)
    power = -a_mat
    inv = jnp.where(r == c, 1.0, 0.0) + power
    for _ in range(int(math.log2(rows)) - 1):
        power = _dot_f32(power, power)
        inv = inv + _dot_f32(inv, power)
    inv_hi, inv_lo = _split2(inv)
    rhs = jnp.concatenate([beta * v_all, beta * jnp.exp(gc) * k_all], axis=1).astype(BF16)
    w = _dot(inv_hi, rhs) + _dot(inv_lo, rhs)
    w_v, w_k = w[:, 0:C_DIM], w[:, C_DIM:2 * C_DIM]
    qk = _dot_nt(q_all.astype(BF16), k16) * decay

    head_of_row = lax.broadcasted_iota(jnp.int32, (hr, C_DIM), 0) // rows

    def per_head_lanes(m):
        return jnp.concatenate([jnp.where(head_of_row == h, m, 0.0) for h in range(C_HEADS)], axis=1).astype(BF16)

    st = st_scr[...]
    st16 = st.astype(BF16)
    delta = w_v - _dot_nt(per_head_lanes(w_k), st16)
    d16 = delta.astype(BF16)
    out = _dot_nt(per_head_lanes(q_all * jnp.exp(gc)), st16) + _dot(qk.astype(BF16), d16)
    keep = jnp.concatenate([jnp.broadcast_to(jnp.exp(gcum[rows - 1:rows, MISC_AC + h:MISC_AC + h + 1]), (1, C_DIM))
                            for h in range(C_HEADS)], axis=1)
    st_scr[...] = keep * st + _dot_tn(d16, per_head_lanes(k_all * jnp.exp(g_end - gc)))
    gain = gain_ref[...]
    for h in range(C_HEADS):
        z = z_ref[:, h * C_DIM:(h + 1) * C_DIM]
        o_ref[:, h * C_DIM:(h + 1) * C_DIM] = _rms_rows(out[h * rows:(h + 1) * rows], gain) * (z * jax.nn.sigmoid(z))

    @pl.when(ci == pl.num_programs(1) - 1)
    def _():
        for h in range(C_HEADS):
            s_out_ref[0, h] = st_scr[:, h * C_DIM:(h + 1) * C_DIM].T


def gdn_mixer(proj, conv_w, a_log, dt_bias, gain, s0, conv_prev, *, bsz, t, chunk):
    nc = t // chunk
    hd = C_HEADS * C_DIM
    lanes = jnp.zeros((1, LANE), F32)
    alog = lanes.at[0, MISC_AC:MISC_AC + C_HEADS].set(a_log)
    dt = lanes.at[0, MISC_AC:MISC_AC + C_HEADS].set(dt_bias)
    tail0 = jnp.pad(conv_prev, ((0, 0), (TAIL - (CONV_W - 1), 0), (0, 0)))
    out, s = _chunk_call(
        _gdn_kernel, bsz=bsz, nc=nc, rows=chunk,
        ins=[(proj, 'rows', 3 * hd, CD_QKV), (proj, 'rows', hd, CD_ZC), (proj, 'rows', LANE, CD_MISC),
             (conv_w, 'const', 0, 0), (alog, 'const', 0, 0), (dt, 'const', 0, 0),
             (gain.reshape(1, C_DIM), 'const', 0, 0), (s0, 'batch', 0, 0), (tail0, 'batch', 0, 0)],
        outs=[(jax.ShapeDtypeStruct((bsz * t, hd), F32), 'rows', hd, 0),
              (jax.ShapeDtypeStruct(s0.shape, F32), 'batch', 0, 0)],
        scratch=[pltpu.VMEM((C_DIM, hd), F32), pltpu.VMEM((TAIL, 3 * hd), F32)],
        name="gdn_mixer")
    return out, s


def _hgrn2_kernel(q_ref, f_ref, v_ref, g_ref, lb_ref, gain_ref, s0_ref, o_ref, s_out_ref, st_scr):
    ci = pl.program_id(1)

    @pl.when(ci == 0)
    def _():
        for h in range(D_HEADS):
            st_scr[h] = s0_ref[0, h].T

    rows = q_ref.shape[0]
    tril16 = jnp.where(_tri_mask(rows), 1.0, 0.0).astype(BF16)
    lb = lb_ref[...]
    zf = f_ref[...]
    logf = jnp.logaddexp(jnp.log(lb), jnp.log1p(-lb) + jax.nn.log_sigmoid(zf))
    kd = (1.0 - lb) * jax.nn.sigmoid(-zf)
    qx = q_ref[...]
    qd = qx * jax.nn.sigmoid(qx)
    bcum = _cumsum_rows(logf, tril16)
    gain = gain_ref[...]
    row_sub = lax.broadcasted_iota(jnp.int32, (SUB, 1), 0)
    for h in range(D_HEADS):
        sl = slice(h * D_EXPAND, (h + 1) * D_EXPAND)
        q, k, b = qd[:, sl], kd[:, sl], bcum[:, sl]
        v = v_ref[:, h * D_V_DIM:(h + 1) * D_V_DIM]
        v16 = v.astype(BF16)
        st = st_scr[h]
        inter = _dot_nt((q * jnp.exp(b)).astype(BF16), st.astype(BF16))
        blocks = []
        for i in range(rows // SUB):
            r0 = i * SUB
            qi, bi = q[r0:r0 + SUB], b[r0:r0 + SUB]
            oi = inter[r0:r0 + SUB]
            if i > 0:
                ref = b[r0 - 1:r0]
                att = _dot_nt((qi * jnp.exp(bi - ref)).astype(BF16),
                              (k[0:r0] * jnp.exp(ref - b[0:r0])).astype(BF16))
                oi = oi + _dot(att.astype(BF16), v16[0:r0])
            for s in range(SUB):
                r = r0 + s
                a = jnp.sum(qi * jnp.exp(bi - b[r:r + 1]) * k[r:r + 1], axis=1, keepdims=True)
                oi = oi + jnp.where(row_sub >= s, a, 0.0) * v[r:r + 1]
            blocks.append(oi)
        out = blocks[0] if len(blocks) == 1 else jnp.concatenate(blocks, axis=0)
        b_end = b[rows - 1:rows]
        st_scr[h] = jnp.exp(b_end) * st + _dot_tn(v16, (k * jnp.exp(b_end - b)).astype(BF16))
        g = g_ref[:, h * D_V_DIM:(h + 1) * D_V_DIM]
        o_ref[:, h * D_V_DIM:(h + 1) * D_V_DIM] = _rms_rows(out, gain) * (g * jax.nn.sigmoid(g))

    @pl.when(ci == pl.num_programs(1) - 1)
    def _():
        for h in range(D_HEADS):
            s_out_ref[0, h] = st_scr[h].T


def hgrn2_mixer(proj, lower_bound, gain, s0, *, bsz, t, chunk):
    nc = t // chunk
    wk = D_HEADS * D_EXPAND
    wv = D_HEADS * D_V_DIM
    out, s = _chunk_call(
        _hgrn2_kernel, bsz=bsz, nc=nc, rows=chunk,
        ins=[(proj, 'rows', wk, CD_QD), (proj, 'rows', wk, CD_FD), (proj, 'rows', wv, CD_VD),
             (proj, 'rows', wv, CD_GD), (lower_bound.reshape(1, wk), 'const', 0, 0),
             (gain.reshape(1, D_V_DIM), 'const', 0, 0), (s0, 'batch', 0, 0)],
        outs=[(jax.ShapeDtypeStruct((bsz * t, wv), F32), 'rows', wv, 0),
              (jax.ShapeDtypeStruct(s0.shape, F32), 'batch', 0, 0)],
        scratch=[pltpu.VMEM((D_HEADS, D_V_DIM, D_EXPAND), F32)],
        name="hgrn2_mixer")
    return out, s


def _pad_cols(w, mult=LANE):
    pad = (-w.shape[-1]) % mult
    return jnp.pad(w, [(0, 0)] * (w.ndim - 1) + [(0, pad)])


def _mixer_ab(proj, bsz, t, pos, prm, cache):
    n = bsz * t
    qpad, qipad, k16, v16, ki16, k32, v32, ki32 = dsa_prep(proj, pos, prm['a_q_gain'][0], prm['a_k_gain'][0], t)
    if cache is None:
        limit = jnp.tile((pos // CHUNK + 1) * CHUNK, bsz).reshape(n, 1)
        a_out = dsa_attention(qpad, qipad, proj, limit, k16, v16, ki16, bsz=bsz, tq=t, tk=t,
                              causal=True, n_sel=min(TOPK_MAX, t // 4))
        c0 = jnp.zeros((bsz, B_HEADS, B_QK_DIM, B_V_DIM), F32)
        n0 = jnp.zeros((bsz, B_HEADS, B_QK_DIM), F32)
        m0 = jnp.zeros((bsz, B_HEADS), F32)
        chunk = CHUNK
    else:
        k_c, v_c, ki_c, c0, n0, m0 = cache
        past = k_c.shape[1]
        n_keys = past + t
        tk = -(-n_keys // KEY_CHUNK) * KEY_CHUNK

        def with_cache(c, new):
            c = c.reshape(bsz, past, -1).astype(BF16)
            c = jnp.pad(c, ((0, 0), (0, 0), (0, LANE - c.shape[-1])))
            return jnp.concatenate([c, new.reshape(bsz, t, LANE),
                                    jnp.zeros((bsz, tk - n_keys, LANE), BF16)], axis=1).reshape(bsz * tk, LANE)

        limit = jnp.full((n, 1), n_keys, jnp.int32)
        a_out = dsa_attention(qpad, qipad, proj, limit, with_cache(k_c, k16), with_cache(v_c, v16),
                              with_cache(ki_c, ki16), bsz=bsz, tq=t, tk=tk, causal=False,
                              n_sel=min(TOPK_MAX, n_keys // 4))
        chunk = t
    h, c, n_, m = mlstm_mixer(proj, prm['b_gate_bias'][0], prm['b_norm_gain'][0], c0, n0, m0,
                              bsz=bsz, t=t, chunk=chunk)
    st = (k32.reshape(bsz, t, A_KV_HEADS, HEAD_DIM), v32.reshape(bsz, t, A_KV_HEADS, HEAD_DIM),
          ki32.reshape(bsz, t, IDX_DIM), c, n_, m)
    return a_out, h, st


def _mixer_cd(proj, bsz, t, prm, lower_bound, cache):
    hd = C_HEADS * C_DIM
    if cache is None:
        sc0 = jnp.zeros((bsz, C_HEADS, C_DIM, C_DIM), F32)
        conv_prev = jnp.zeros((bsz, CONV_W - 1, 3 * hd), F32)
        sd0 = jnp.zeros((bsz, D_HEADS, D_EXPAND, D_V_DIM), F32)
        chunk = CHUNK
    else:
        sc0, conv_prev, sd0 = cache
        chunk = t
    oc, sc = gdn_mixer(proj, prm['c_conv_w'][0], prm['c_a_log'][0], prm['c_dt_bias'][0], prm['c_norm_gain'][0],
                       sc0, conv_prev, bsz=bsz, t=t, chunk=chunk)
    od, sd = hgrn2_mixer(proj, lower_bound, prm['d_norm_gain'][0], sd0, bsz=bsz, t=t, chunk=chunk)
    qkv = proj.reshape(bsz, t, -1)[:, :, CD_QKV:CD_QKV + 3 * hd]
    conv_new = jnp.concatenate([conv_prev, qkv[:, t - (CONV_W - 1):]], axis=1)[:, -(CONV_W - 1):]
    return oc, od, (sc, conv_new, sd)


def _trunk(x, pos_offset, cache, prm, wts):
    bsz, t, d = x.shape
    n = bsz * t
    pos = pos_offset + jnp.arange(t, dtype=jnp.int32)
    probs = jax.nn.softmax(prm['d_lb_logits'], axis=0)
    lower_bounds = jnp.cumsum(probs, axis=0) - probs[0]
    xf = x.reshape(n, d)

    lc = None if cache is None else tuple(c[0] for c in cache[:6])
    proj = norm_matmul(xf, prm['norm_mix'][0], wts['w_in_ab'])
    a_out, b_out, st_even = _mixer_ab(proj, bsz, t, pos, prm, lc)
    xf = matmul_residual(a_out, b_out, wts['w_out_ab'], xf)
    xf = ffn_residual(xf, prm['norm_ffn'][0], wts['ffn_w1'], wts['ffn_w3'], wts['ffn_w2'])

    lc = None if cache is None else tuple(c[0] for c in cache[6:])
    proj = norm_matmul(xf, prm['norm_mix'][1], wts['w_in_cd'])
    c_out, d_out, st_odd = _mixer_cd(proj, bsz, t, prm, lower_bounds[1], lc)
    xf = matmul_residual(c_out, d_out, wts['w_out_cd'], xf)
    xf = moe_residual(xf, prm['norm_ffn'][1], wts['moe_router'], wts['moe_w1'], wts['moe_w3'], wts['moe_w2'])

    new_state = tuple(s[None] for s in st_even + st_odd)
    return xf.reshape(bsz, t, d), new_state


def kernel(x_prompt, x_sample, cache_a_k, cache_a_v, cache_a_kidx, state_b_c, state_b_n, state_b_m,
           state_c_s, state_c_conv, state_d_s, norm_mix, norm_ffn, w_in_ab, w_out_ab, a_q_gain, a_k_gain,
           b_gate_bias, b_norm_gain, w_in_cd, w_out_cd, c_conv_w, c_a_log, c_dt_bias, c_norm_gain,
           d_lb_logits, d_norm_gain, ffn_w1, ffn_w3, ffn_w2, moe_router, moe_w1, moe_w3, moe_w2):
    prm = dict(norm_mix=norm_mix, norm_ffn=norm_ffn, a_q_gain=a_q_gain, a_k_gain=a_k_gain,
               b_gate_bias=b_gate_bias, b_norm_gain=b_norm_gain, c_conv_w=c_conv_w, c_a_log=c_a_log,
               c_dt_bias=c_dt_bias, c_norm_gain=c_norm_gain, d_lb_logits=d_lb_logits, d_norm_gain=d_norm_gain)
    wts = dict(w_in_ab=_permute_w_in_ab(w_in_ab[0]).astype(BF16), w_out_ab=w_out_ab[0].astype(BF16),
               w_in_cd=_permute_w_in_cd(w_in_cd[0]).astype(BF16), w_out_cd=w_out_cd[0].astype(BF16),
               ffn_w1=ffn_w1[0].astype(BF16), ffn_w3=ffn_w3[0].astype(BF16), ffn_w2=ffn_w2[0].astype(BF16),
               moe_router=_pad_cols(moe_router[0]),
               moe_w1=moe_w1[0].astype(BF16), moe_w3=moe_w3[0].astype(BF16), moe_w2=moe_w2[0].astype(BF16))
    cache = (cache_a_k, cache_a_v, cache_a_kidx, state_b_c, state_b_n, state_b_m, state_c_s, state_c_conv, state_d_s)
    y_prompt, st_p = _trunk(x_prompt, 0, None, prm, wts)
    y_sample, st_s = _trunk(x_sample, cache_a_k.shape[2], cache, prm, wts)
    return (y_prompt, y_sample) + st_p + st_s
```

```python
import functools
import math

import jax
import jax.numpy as jnp
import numpy as np
from jax import lax
from jax.experimental import pallas as pl
from jax.experimental.pallas import tpu as pltpu

F32 = jnp.float32
BF16 = jnp.bfloat16

EPS = 1e-6
ROPE_THETA = 500000.0
ROT_FRACTION = 4
CHUNK = 64
A_HEADS, A_KV_HEADS, HEAD_DIM = 8, 2, 64
IDX_HEADS, IDX_DIM = 4, 64
TOPK_MAX, Q_BLOCK = 256, 128
B_HEADS, B_QK_DIM, B_V_DIM = 4, 64, 128
C_HEADS, C_DIM, CONV_W = 4, 128, 4
D_HEADS, D_EXPAND, D_V_DIM = 4, 128, 128
N_EXPERTS, TOP_K_EXPERTS = 8, 2

LANE = 128
VMEM_LIMIT = 48 * 1024 * 1024

EVEN_SPLITS = (A_HEADS * HEAD_DIM, A_KV_HEADS * HEAD_DIM, A_KV_HEADS * HEAD_DIM,
               IDX_HEADS * IDX_DIM, IDX_DIM, IDX_HEADS,
               B_HEADS * B_QK_DIM, B_HEADS * B_QK_DIM, B_HEADS * B_V_DIM,
               B_HEADS, B_HEADS, B_HEADS * B_V_DIM)
ODD_SPLITS = (3 * C_HEADS * C_DIM, C_HEADS, C_HEADS, C_HEADS * C_DIM,
              D_HEADS * D_EXPAND, D_HEADS * D_EXPAND, D_HEADS * D_V_DIM, D_HEADS * D_V_DIM)


def _split_cols(p, widths):
    cuts = [int(c) for c in np.cumsum(widths)[:-1]]
    return jnp.split(p, cuts, axis=-1)


def _row_tile(n, target):
    t = min(n, target)
    while n % t:
        t //= 2
    return t


def _col_tile(n, target):
    best = LANE
    for k in range(1, n // LANE + 1):
        c = k * LANE
        if n % c == 0 and c <= target:
            best = c
    return best


def _rms_rows(x, gain):
    return x * lax.rsqrt(jnp.mean(x * x, axis=-1, keepdims=True) + EPS) * gain


def _norm_matmul_kernel(x_ref, g_ref, w_ref, o_ref, xn_ref):
    @pl.when(pl.program_id(1) == 0)
    def _():
        xn_ref[...] = _rms_rows(x_ref[...], g_ref[...]).astype(BF16)

    o_ref[...] = jnp.dot(xn_ref[...], w_ref[...], preferred_element_type=F32)


def norm_matmul(x, gain, w):
    n, d = x.shape
    m = w.shape[1]
    tm = _row_tile(n, 1024)
    tn = _col_tile(m, 1536)
    return pl.pallas_call(
        _norm_matmul_kernel,
        grid=(n // tm, m // tn),
        in_specs=[pl.BlockSpec((tm, d), lambda i, j: (i, 0)),
                  pl.BlockSpec((1, d), lambda i, j: (0, 0)),
                  pl.BlockSpec((d, tn), lambda i, j: (0, j))],
        out_specs=pl.BlockSpec((tm, tn), lambda i, j: (i, j)),
        out_shape=jax.ShapeDtypeStruct((n, m), F32),
        scratch_shapes=[pltpu.VMEM((tm, d), BF16)],
        compiler_params=pltpu.CompilerParams(
            dimension_semantics=("parallel", "arbitrary"), vmem_limit_bytes=VMEM_LIMIT),
        name="norm_matmul",
    )(x, gain.reshape(1, d), w)


def _matmul_res_kernel(a_ref, b_ref, w_ref, r_ref, o_ref):
    ka = a_ref.shape[1]
    o_ref[...] = (r_ref[...] + jnp.dot(a_ref[...].astype(BF16), w_ref[0:ka, :], preferred_element_type=F32)
                  + jnp.dot(b_ref[...].astype(BF16), w_ref[ka:, :], preferred_element_type=F32))


def matmul_residual(a, b, w, res):
    n, ka = a.shape
    kb = b.shape[1]
    m = w.shape[1]
    tm = _row_tile(n, 1024)
    return pl.pallas_call(
        _matmul_res_kernel,
        grid=(n // tm,),
        in_specs=[pl.BlockSpec((tm, ka), lambda i: (i, 0)),
                  pl.BlockSpec((tm, kb), lambda i: (i, 0)),
                  pl.BlockSpec((ka + kb, m), lambda i: (0, 0)),
                  pl.BlockSpec((tm, m), lambda i: (i, 0))],
        out_specs=pl.BlockSpec((tm, m), lambda i: (i, 0)),
        out_shape=jax.ShapeDtypeStruct((n, m), F32),
        compiler_params=pltpu.CompilerParams(
            dimension_semantics=("parallel",), vmem_limit_bytes=VMEM_LIMIT),
        name="matmul_residual",
    )(a, b, w, res)


def _swiglu_tile(xn, w1, w3):
    h1 = jnp.dot(xn, w1, preferred_element_type=F32)
    h3 = jnp.dot(xn, w3, preferred_element_type=F32)
    return h1 * jax.nn.sigmoid(h1) * h3


def _ffn_kernel(x_ref, g_ref, w1_ref, w3_ref, w2_ref, o_ref, xn_ref):
    @pl.when(pl.program_id(1) == 0)
    def _():
        x = x_ref[...]
        xn_ref[...] = _rms_rows(x, g_ref[...]).astype(BF16)
        o_ref[...] = x

    act = _swiglu_tile(xn_ref[...], w1_ref[...], w3_ref[...])
    o_ref[...] += jnp.dot(act.astype(BF16), w2_ref[...], preferred_element_type=F32)


def ffn_residual(x, gain, w1, w3, w2):
    n, d = x.shape
    f = w1.shape[1]
    tm = _row_tile(n, 1024)
    tf = _col_tile(f, 512)
    return pl.pallas_call(
        _ffn_kernel,
        grid=(n // tm, f // tf),
        in_specs=[pl.BlockSpec((tm, d), lambda i, j: (i, 0)),
                  pl.BlockSpec((1, d), lambda i, j: (0, 0)),
                  pl.BlockSpec((d, tf), lambda i, j: (0, j)),
                  pl.BlockSpec((d, tf), lambda i, j: (0, j)),
                  pl.BlockSpec((tf, d), lambda i, j: (j, 0))],
        out_specs=pl.BlockSpec((tm, d), lambda i, j: (i, 0)),
        out_shape=jax.ShapeDtypeStruct((n, d), F32),
        scratch_shapes=[pltpu.VMEM((tm, d), BF16)],
        compiler_params=pltpu.CompilerParams(
            dimension_semantics=("parallel", "arbitrary"), vmem_limit_bytes=VMEM_LIMIT),
        name="ffn_residual",
    )(x, gain.reshape(1, d), w1, w3, w2)


MOE_VMEM_LIMIT = 58 * 1024 * 1024
MOE_CAPS = (256, 384, 512)


def _moe_route_kernel(x_ref, g_ref, r_ref, xn_ref, comb_ref, post_ref, cnt_ref):
    x = x_ref[...]
    tm = x.shape[0]
    xn = _rms_rows(x, g_ref[...])
    xn_ref[...] = xn.astype(BF16)
    logits = jnp.dot(xn, r_ref[...], preferred_element_type=F32, precision=lax.Precision.HIGHEST)
    lane = lax.broadcasted_iota(jnp.int32, logits.shape, 1)
    logits = jnp.where(lane < N_EXPERTS, logits, -jnp.inf)
    m1 = jnp.max(logits, axis=-1, keepdims=True)
    i1 = jnp.min(jnp.where(logits == m1, lane, LANE), axis=-1, keepdims=True)
    rest = jnp.where(lane == i1, -jnp.inf, logits)
    m2 = jnp.max(rest, axis=-1, keepdims=True)
    i2 = jnp.min(jnp.where(rest == m2, lane, LANE), axis=-1, keepdims=True)
    e2 = jnp.exp(m2 - m1)
    den = 1.0 + e2
    comb_ref[...] = jnp.where(lane == i1, 1.0 / den, 0.0) + jnp.where(lane == i2, e2 / den, 0.0)
    chosen = (lane == i1) | (lane == i2)
    sel = jnp.where(chosen, 1.0, 0.0)
    tril16 = jnp.where(_tri_mask(LANE), 1.0, 0.0).astype(BF16)
    seen = jnp.zeros((1, LANE), F32)
    ranks = []
    for blk in range(tm // LANE):
        sb = sel[blk * LANE:(blk + 1) * LANE]
        ranks.append(_dot(tril16, sb.astype(BF16)) + seen - 1.0)
        seen = seen + jnp.sum(sb, axis=0, keepdims=True)
    rank = jnp.where(chosen, jnp.concatenate(ranks, axis=0), -1.0)
    post_ref[0] = rank.T[0:N_EXPERTS, :]
    cnt_ref[0] = seen


def _moe_expert_kernel(cnt_ref, xn_ref, comb_ref, post_ref, x_ref, w1_ref, w3_ref, w2_ref, o_ref,
                       xe_scr, y_scr, *, caps):
    i = pl.program_id(0)
    e = pl.program_id(1)
    j = pl.program_id(2)
    last = pl.num_programs(2) - 1
    tm = x_ref.shape[0]

    @pl.when(jnp.logical_and(e == 0, j == 0))
    def _():
        o_ref[...] = x_ref[...]

    cnt = cnt_ref[i * N_EXPERTS + e]

    def expert_step(cap):
        def pick():
            rank_row = post_ref[0, pl.ds(e, 1), :].astype(jnp.int32)
            slot = lax.broadcasted_iota(jnp.int32, (cap, tm), 0)
            return jnp.where(rank_row == slot, 1.0, 0.0).astype(BF16)

        @pl.when(j == 0)
        def _():
            xe_scr[0:cap, :] = _dot(pick(), xn_ref[...]).astype(BF16)

        act = _swiglu_tile(xe_scr[0:cap, :], w1_ref[0], w3_ref[0])
        yj = _dot(act.astype(BF16), w2_ref[0])

        @pl.when(j == 0)
        def _():
            y_scr[0:cap, :] = yj

        @pl.when(j > 0)
        def _():
            y_scr[0:cap, :] += yj

        @pl.when(j == last)
        def _():
            comb = comb_ref[...]
            lane = lax.broadcasted_iota(jnp.int32, comb.shape, 1)
            gate = jnp.sum(jnp.where(lane == e, comb, 0.0), axis=-1, keepdims=True)
            hi, lo = _split2(y_scr[0:cap, :])
            p = pick()
            o_ref[...] += gate * (_dot_tn(p, hi) + _dot_tn(p, lo))

    lo = 0
    for cap in caps:
        @pl.when(jnp.logical_and(cnt > lo, cnt <= cap))
        def _(cap=cap):
            expert_step(cap)
        lo = cap


def moe_residual(x, gain, router, w1, w3, w2):
    n, d = x.shape
    ne, _, f = w1.shape
    tm = _row_tile(n, 1024)
    tf = _col_tile(f, 896)
    nt = n // tm
    xn, comb, post, cnt = pl.pallas_call(
        _moe_route_kernel,
        grid=(nt,),
        in_specs=[pl.BlockSpec((tm, d), lambda i: (i, 0)),
                  pl.BlockSpec((1, d), lambda i: (0, 0)),
                  pl.BlockSpec((d, LANE), lambda i: (0, 0))],
        out_specs=[pl.BlockSpec((tm, d), lambda i: (i, 0)),
                   pl.BlockSpec((tm, LANE), lambda i: (i, 0)),
                   pl.BlockSpec((1, ne, tm), lambda i: (i, 0, 0)),
                   pl.BlockSpec((1, 1, LANE), lambda i: (i, 0, 0))],
        out_shape=[jax.ShapeDtypeStruct((n, d), BF16), jax.ShapeDtypeStruct((n, LANE), F32),
                   jax.ShapeDtypeStruct((nt, ne, tm), F32), jax.ShapeDtypeStruct((nt, 1, LANE), F32)],
        compiler_params=pltpu.CompilerParams(dimension_semantics=("parallel",), vmem_limit_bytes=VMEM_LIMIT),
        name="moe_route",
    )(x, gain.reshape(1, d), router)
    counts = cnt[:, 0, :ne].astype(jnp.int32).reshape(nt * ne)
    caps = tuple(c for c in MOE_CAPS if c < tm) + (tm,)
    grid_spec = pltpu.PrefetchScalarGridSpec(
        num_scalar_prefetch=1,
        grid=(nt, ne, f // tf),
        in_specs=[pl.BlockSpec((tm, d), lambda i, e, j, c: (i, 0)),
                  pl.BlockSpec((tm, LANE), lambda i, e, j, c: (i, 0)),
                  pl.BlockSpec((1, ne, tm), lambda i, e, j, c: (i, 0, 0)),
                  pl.BlockSpec((tm, d), lambda i, e, j, c: (i, 0)),
                  pl.BlockSpec((1, d, tf), lambda i, e, j, c: (e, 0, j)),
                  pl.BlockSpec((1, d, tf), lambda i, e, j, c: (e, 0, j)),
                  pl.BlockSpec((1, tf, d), lambda i, e, j, c: (e, j, 0))],
        out_specs=pl.BlockSpec((tm, d), lambda i, e, j, c: (i, 0)),
        scratch_shapes=[pltpu.VMEM((tm, d), BF16), pltpu.VMEM((tm, d), F32)])
    return pl.pallas_call(
        functools.partial(_moe_expert_kernel, caps=caps),
        grid_spec=grid_spec,
        out_shape=jax.ShapeDtypeStruct((n, d), F32),
        compiler_params=pltpu.CompilerParams(
            dimension_semantics=("parallel", "arbitrary", "arbitrary"), vmem_limit_bytes=MOE_VMEM_LIMIT),
        name="moe_experts",
    )(counts, xn, comb, post, x, w1, w3, w2)


AB_QA, AB_VB, AB_OB, AB_QI, AB_QB, AB_KB, AB_KA, AB_VA, AB_MISC, AB_TOTAL = (
    0, 512, 1024, 1536, 1792, 2048, 2304, 2432, 2560, 2688)
MISC_WI, MISC_IB, MISC_FB = 64, 68, 72
HALF = LANE // 2
KEY_CHUNK = 512
MASKED = -1e30
KEY_OF_NEG_INF = -2139095041
I16_MIN, I16_MAX = -32768, 32767


def _permute_w_in_ab(w):
    qa, ka, va, qi, ki, wi, qb, kb, vb, ib, fb, ob = _split_cols(w, EVEN_SPLITS)
    pad = jnp.zeros((w.shape[0], LANE - IDX_DIM - 3 * IDX_HEADS), w.dtype)
    return jnp.concatenate([qa, vb, ob, qi, qb, kb, ka, va, ki, wi, ib, fb, pad], axis=1)


def _rope_tables(pos):
    rot = HEAD_DIM // ROT_FRACTION
    half = rot // 2
    inv_freq = ROPE_THETA ** (-jnp.arange(half, dtype=F32) * 2.0 / rot)
    ang = pos.astype(F32)[:, None] * inv_freq[None, :]
    cos, sin = jnp.cos(ang), jnp.sin(ang)
    t = pos.shape[0]
    one = jnp.ones((t, HEAD_DIM - rot), F32)
    zero_r = jnp.zeros((t, HEAD_DIM - rot), F32)
    zero_h = jnp.zeros((t, half), F32)
    c = jnp.concatenate([cos, cos, one], axis=1)
    s_up = jnp.concatenate([-sin, zero_h, zero_r], axis=1)
    s_dn = jnp.concatenate([zero_h, sin, zero_r], axis=1)
    return tuple(jnp.concatenate([a, a], axis=1) for a in (c, s_up, s_dn))


def _rope_tile(x, c, s_up, s_dn):
    half = HEAD_DIM // ROT_FRACTION // 2
    return x * c + pltpu.roll(x, LANE - half, 1) * s_up + pltpu.roll(x, half, 1) * s_dn


def _head_norm_tile(x, gain, same_head):
    sq = x * x
    hi = sq.astype(BF16)
    lo = (sq - hi.astype(F32)).astype(BF16)
    ss = (jnp.dot(hi, same_head, preferred_element_type=F32)
          + jnp.dot(lo, same_head, preferred_element_type=F32))
    return x * lax.rsqrt(ss * (1.0 / HEAD_DIM) + EPS) * gain


def _aprep_kernel(qa_ref, ka_ref, va_ref, qi_ref, misc_ref, c_ref, su_ref, sd_ref, qg_ref, kg_ref,
                  qpad_ref, qipad_ref, k16_ref, v16_ref, ki16_ref, k32_ref, v32_ref, ki32_ref):
    c, su, sd = c_ref[...], su_ref[...], sd_ref[...]
    tm = c.shape[0]
    row = lax.broadcasted_iota(jnp.int32, (LANE, LANE), 0)
    col = lax.broadcasted_iota(jnp.int32, (LANE, LANE), 1)
    same_head = jnp.where(row // HALF == col // HALF, 1.0, 0.0).astype(BF16)
    lane = lax.broadcasted_iota(jnp.int32, (tm, LANE), 1)
    low = lane < HALF

    heads_per_group = A_HEADS // A_KV_HEADS
    for p in range(A_HEADS // 2):
        y = _rope_tile(_head_norm_tile(qa_ref[:, p * LANE:(p + 1) * LANE], qg_ref[...], same_head), c, su, sd)
        y = y * HEAD_DIM ** -0.5
        y_sw = pltpu.roll(y, HALF, 1)
        for o in range(2):
            h = 2 * p + o
            g = h // heads_per_group
            src = y if o == g else y_sw
            qpad_ref[:, h * LANE:(h + 1) * LANE] = jnp.where(low if g == 0 else ~low, src, 0.0).astype(BF16)
    k = _rope_tile(_head_norm_tile(ka_ref[...], kg_ref[...], same_head), c, su, sd)
    k32_ref[...] = k
    k16_ref[...] = k.astype(BF16)
    v = va_ref[...]
    v32_ref[...] = v
    v16_ref[...] = v.astype(BF16)
    for p in range(IDX_HEADS // 2):
        y = _rope_tile(qi_ref[:, p * LANE:(p + 1) * LANE], c, su, sd)
        y_sw = pltpu.roll(y, HALF, 1)
        qipad_ref[:, (2 * p) * LANE:(2 * p + 1) * LANE] = jnp.where(low, y, 0.0).astype(BF16)
        qipad_ref[:, (2 * p + 1) * LANE:(2 * p + 2) * LANE] = jnp.where(low, y_sw, 0.0).astype(BF16)
    ki = _rope_tile(misc_ref[...], c, su, sd)
    ki32_ref[...] = ki[:, :IDX_DIM]
    ki16_ref[...] = jnp.where(low, ki, 0.0).astype(BF16)


def dsa_prep(proj, pos, q_gain, k_gain, t):
    n = proj.shape[0]
    tm = _row_tile(n, 512)
    tabs = _rope_tables(pos)
    if t < tm:
        tabs = tuple(jnp.tile(a, (tm // t, 1)) for a in tabs)
    nt = tabs[0].shape[0] // tm
    tab_spec = pl.BlockSpec((tm, LANE), lambda i: (i % nt, 0))
    gain_spec = pl.BlockSpec((1, LANE), lambda i: (0, 0))

    def col(width, offset):
        return pl.BlockSpec((tm, width), lambda i: (i, offset // width))

    def out(width, dtype):
        return (jax.ShapeDtypeStruct((n, width), dtype), pl.BlockSpec((tm, width), lambda i: (i, 0)))

    outs = [out(A_HEADS * LANE, BF16), out(IDX_HEADS * LANE, BF16), out(LANE, BF16), out(LANE, BF16),
            out(LANE, BF16), out(LANE, F32), out(LANE, F32), out(IDX_DIM, F32)]
    return pl.pallas_call(
        _aprep_kernel,
        grid=(n // tm,),
        in_specs=[col(A_HEADS * HEAD_DIM, AB_QA), col(LANE, AB_KA), col(LANE, AB_VA),
                  col(IDX_HEADS * IDX_DIM, AB_QI), col(LANE, AB_MISC), tab_spec, tab_spec, tab_spec,
                  gain_spec, gain_spec],
        out_specs=[o[1] for o in outs],
        out_shape=[o[0] for o in outs],
        compiler_params=pltpu.CompilerParams(dimension_semantics=("parallel",), vmem_limit_bytes=VMEM_LIMIT),
        name="dsa_prep",
    )(proj, proj, proj, proj, proj, *tabs, jnp.tile(q_gain, 2).reshape(1, LANE), jnp.tile(k_gain, 2).reshape(1, LANE))


N_PARTIAL = 4


def _add_tiles(accs, m, sub):
    accs = list(accs)
    for t in range(m.shape[0] // sub):
        accs[t % len(accs)] = accs[t % len(accs)] + m[t * sub:(t + 1) * sub]
    return tuple(accs)


def _dsa_kernel(q_ref, qi_ref, misc_ref, lim_ref, k_ref, v_ref, ki_ref, o_ref, key_ref, bias_ref, hi_ref, lo_ref,
                *, causal, n_sel):
    qb = q_ref.shape[0]
    kc = KEY_CHUNK
    n_idx = IDX_HEADS
    hpg = A_HEADS // A_KV_HEADS
    if causal:
        nch = ((pl.program_id(1) + 1) * qb + kc - 1) // kc
    else:
        nch = k_ref.shape[0] // kc
    nt = (((1,), (1,)), ((), ()))

    limit = lim_ref[0]
    misc_t = misc_ref[...].T
    wscale = IDX_HEADS ** -0.5 * IDX_DIM ** -0.5
    w = [misc_t[MISC_WI + j:MISC_WI + j + 1, :] * wscale for j in range(n_idx)]
    qis = [qi_ref[:, j * LANE:(j + 1) * LANE] for j in range(n_idx)]

    def score_body(c, carry):
        off = pl.multiple_of(c * kc, kc)
        kic = ki_ref[pl.ds(off, kc), :]
        s = None
        for j in range(n_idx):
            lg = jnp.maximum(lax.dot_general(kic, qis[j], nt, preferred_element_type=F32), 0.0)
            s = w[j] * lg if s is None else s + w[j] * lg
        kidx = off + lax.broadcasted_iota(jnp.int32, (kc, qb), 0)
        s = jnp.where(kidx < limit, s, -jnp.inf)
        bits = lax.bitcast_convert_type(s, jnp.int32)
        key = jnp.where(bits < 0, bits ^ 0x7FFFFFFF, bits)
        key_ref[pl.ds(off, kc), :] = key
        hi_ref[pl.ds(off, kc), :] = jnp.right_shift(key, 16).astype(jnp.int16)
        return carry

    lax.fori_loop(0, nch, score_body, 0)

    def count_ge(cand):
        sub = 8

        def body(c, accs):
            off = pl.multiple_of(c * kc, kc)
            m = jnp.where(key_ref[pl.ds(off, kc), :] >= cand, 1.0, 0.0)
            return _add_tiles(accs, m, sub)

        accs = lax.fori_loop(0, nch, body, (jnp.zeros((sub, qb), F32),) * N_PARTIAL)
        return jnp.sum(sum(accs), axis=0, keepdims=True)

    def count_ge16(ref, cand32):
        cand = cand32.astype(jnp.int16)
        sub = 16

        def body(c, accs):
            off = pl.multiple_of(c * kc, kc)
            m = jnp.where(ref[pl.ds(off, kc), :] >= cand, jnp.int16(1), jnp.int16(0))
            return _add_tiles(accs, m, sub)

        accs = lax.fori_loop(0, nch, body, (jnp.zeros((sub, qb), jnp.int16),) * N_PARTIAL)
        return jnp.sum(sum(accs).astype(F32), axis=0, keepdims=True)

    def kth_largest16(ref, want):
        tau = jnp.where(count_ge16(ref, jnp.zeros((1, qb), jnp.int32)) >= want, 0, I16_MIN).astype(jnp.int32)

        def bisect(i, tau):
            cand = tau | jnp.left_shift(jnp.int32(1), 14 - i)
            return jnp.where(count_ge16(ref, cand) >= want, cand, tau)

        return lax.fori_loop(0, 15, bisect, tau)

    want = float(n_sel)
    tau_hi = kth_largest16(hi_ref, want)
    above = jnp.where(tau_hi < I16_MAX, count_ge16(hi_ref, jnp.minimum(tau_hi + 1, I16_MAX)), 0.0)

    def low_body(c, carry):
        off = pl.multiple_of(c * kc, kc)
        key = key_ref[pl.ds(off, kc), :]
        low = (key & 0xFFFF) + I16_MIN
        lo_ref[pl.ds(off, kc), :] = jnp.where(jnp.right_shift(key, 16) == tau_hi, low, I16_MIN).astype(jnp.int16)
        return carry

    lax.fori_loop(0, nch, low_body, 0)
    tau_lo = kth_largest16(lo_ref, want - above)
    tau = jnp.left_shift(tau_hi, 16) + (tau_lo - I16_MIN)

    room = want - count_ge(tau + 1)
    r_i = lax.broadcasted_iota(jnp.int32, (LANE, LANE), 0)
    c_i = lax.broadcasted_iota(jnp.int32, (LANE, LANE), 1)
    prefix_ones = jnp.where(r_i >= c_i, 1.0, 0.0).astype(BF16)
    identity = jnp.where(r_i == c_i, 1.0, 0.0).astype(BF16)

    def bias_body(c, seen):
        off = pl.multiple_of(c * kc, kc)
        tiles = range(kc // LANE)
        xs = [key_ref[pl.ds(off + t * LANE, LANE), :] for t in tiles]
        eqs = [x == tau for x in xs]
        eqfs = [jnp.where(eq, 1.0, 0.0) for eq in eqs]
        ranks = [jnp.dot(prefix_ones, eqf.astype(BF16), preferred_element_type=F32) for eqf in eqfs]
        sels = []
        for t in tiles:
            sel = ((xs[t] > tau) | (eqs[t] & (ranks[t] + seen <= room))) & (xs[t] != KEY_OF_NEG_INF)
            sels.append(jnp.where(sel, 1.0, 0.0).astype(BF16))
            seen = seen + jnp.sum(eqfs[t], axis=0, keepdims=True)
        sel_ts = [lax.dot_general(sel, identity, (((0,), (0,)), ((), ())), preferred_element_type=F32)
                  for sel in sels]
        for t in tiles:
            bias_ref[:, pl.ds(off + t * LANE, LANE)] = jnp.where(sel_ts[t] > 0.5, 0.0, MASKED)
        return seen

    lax.fori_loop(0, nch, bias_body, jnp.zeros((1, qb), F32))

    lane = lax.broadcasted_iota(jnp.int32, (qb, LANE), 1)
    qgs = [jnp.concatenate([q_ref[:, (hpg * g + h) * LANE:(hpg * g + h + 1) * LANE] for h in range(hpg)], axis=0)
           for g in range(A_KV_HEADS)]

    def att_body(c, carry):
        off = pl.multiple_of(c * kc, kc)
        kch = k_ref[pl.ds(off, kc), :]
        vch = v_ref[pl.ds(off, kc), :]
        bias = bias_ref[:, pl.ds(off, kc)][None]
        new = []
        for g in range(A_KV_HEADS):
            m, l, acc = carry[g]
            s = lax.dot_general(qgs[g], kch, nt, preferred_element_type=F32)
            s = (s.reshape(hpg, qb, kc) + bias).reshape(hpg * qb, kc)
            m_new = jnp.maximum(m, jnp.max(s, axis=1, keepdims=True))
            alpha = jnp.exp(m - m_new)
            p = jnp.exp(s - m_new)
            l = alpha * l + jnp.sum(p, axis=1, keepdims=True)
            acc = alpha * acc + jnp.dot(p.astype(BF16), vch, preferred_element_type=F32)
            new.append((m_new, l, acc))
        return tuple(new)

    init = tuple((jnp.full((hpg * qb, 1), MASKED, F32), jnp.zeros((hpg * qb, 1), F32),
                  jnp.zeros((hpg * qb, LANE), F32)) for _ in range(A_KV_HEADS))
    res = lax.fori_loop(0, nch, att_body, init)
    outs = []
    for g in range(A_KV_HEADS):
        _, l, acc = res[g]
        og = acc / l
        for h in range(hpg):
            oh = og[h * qb:(h + 1) * qb]
            outs.append(oh if (h % 2) == g else pltpu.roll(oh, HALF, 1))
    for p in range(A_HEADS // 2):
        o_ref[:, p * LANE:(p + 1) * LANE] = jnp.where(lane < HALF, outs[2 * p], outs[2 * p + 1])


def dsa_attention(qpad, qipad, proj, limit, k16, v16, ki16, *, bsz, tq, tk, causal, n_sel):
    qb = min(Q_BLOCK, tq)
    nqb = tq // qb
    assert tk % KEY_CHUNK == 0 and tk >= n_sel
    kern = functools.partial(_dsa_kernel, causal=causal, n_sel=n_sel)

    def qspec(width, col=0):
        return pl.BlockSpec((qb, width), lambda b, i: (b * nqb + i, col))

    def kspec():
        return pl.BlockSpec((tk, LANE), lambda b, i: (b, 0))

    return pl.pallas_call(
        kern,
        grid=(bsz, nqb),
        in_specs=[qspec(A_HEADS * LANE), qspec(IDX_HEADS * LANE), qspec(LANE, AB_MISC // LANE),
                  pl.BlockSpec((1, 1, qb), lambda b, i: (b * nqb + i, 0, 0)),
                  kspec(), kspec(), kspec()],
        out_specs=qspec(A_HEADS * HEAD_DIM),
        out_shape=jax.ShapeDtypeStruct((bsz * tq, A_HEADS * HEAD_DIM), F32),
        scratch_shapes=[pltpu.VMEM((tk, qb), jnp.int32), pltpu.VMEM((qb, tk), F32),
                        pltpu.VMEM((tk, qb), jnp.int16), pltpu.VMEM((tk, qb), jnp.int16)],
        compiler_params=pltpu.CompilerParams(
            dimension_semantics=("parallel", "arbitrary"), vmem_limit_bytes=VMEM_LIMIT),
        name="dsa_attention",
    )(qpad, qipad, proj, limit.reshape(bsz * nqb, 1, qb), k16, v16, ki16)


SUB = 16


def _dot(a, b):
    return jnp.dot(a, b, preferred_element_type=F32)


def _dot_nt(a, b):
    return lax.dot_general(a, b, (((1,), (1,)), ((), ())), preferred_element_type=F32)


def _dot_tn(a, b):
    return lax.dot_general(a, b, (((0,), (0,)), ((), ())), preferred_element_type=F32)


def _split2(x):
    hi = x.astype(BF16)
    return hi, (x - hi.astype(F32)).astype(BF16)


def _split3(x):
    hi = x.astype(BF16)
    r = x - hi.astype(F32)
    mid = r.astype(BF16)
    return hi, mid, (r - mid.astype(F32)).astype(BF16)


def _cumsum_rows(x, tril16):
    hi, mid, lo = _split3(x)
    return _dot(tril16, hi) + _dot(tril16, mid) + _dot(tril16, lo)


def _dot_f32(a, b):
    ah, al = _split2(a)
    bh, bl = _split2(b)
    return _dot(ah, bh) + (_dot(ah, bl) + _dot(al, bh))


def _tri_mask(n, strict=False):
    r = lax.broadcasted_iota(jnp.int32, (n, n), 0)
    c = lax.broadcasted_iota(jnp.int32, (n, n), 1)
    return r > c if strict else r >= c


def _rows_to_lanes(x):
    rows = x.shape[0]
    if rows < LANE:
        x = jnp.concatenate([x, jnp.zeros((LANE - rows, LANE), x.dtype)], axis=0)
    return x.T


def _chunk_call(kern, *, bsz, nc, rows, ins, outs, scratch, name):
    def spec(a, kind, width, offset):
        if kind == 'rows':
            return pl.BlockSpec((rows, width), lambda b, c: (b * nc + c, offset // width))
        if kind == 'batch':
            return pl.BlockSpec((1,) + tuple(a.shape[1:]), lambda b, c: (b,) + (0,) * (len(a.shape) - 1))
        return pl.BlockSpec(tuple(a.shape), lambda b, c: (0,) * len(a.shape))

    return pl.pallas_call(
        kern,
        grid=(bsz, nc),
        in_specs=[spec(*i) for i in ins],
        out_specs=[spec(*o) for o in outs],
        out_shape=[o[0] for o in outs],
        scratch_shapes=scratch,
        compiler_params=pltpu.CompilerParams(
            dimension_semantics=("parallel", "arbitrary"), vmem_limit_bytes=VMEM_LIMIT),
        name=name,
    )(*[i[0] for i in ins])


def _mlstm_kernel(q_ref, k_ref, v_ref, og_ref, misc_ref, gb_ref, gain_ref, c0_ref, n0_ref, m0_ref,
                  h_ref, c_out_ref, n_out_ref, m_out_ref, c_scr, n_scr, m_scr):
    ci = pl.program_id(1)

    @pl.when(ci == 0)
    def _():
        c_scr[...] = c0_ref[0]
        n_scr[...] = n0_ref[0]
        m_scr[...] = m0_ref[0]

    rows = q_ref.shape[0]
    hr = B_HEADS * rows
    wq = B_HEADS * B_QK_DIM
    tril16 = jnp.where(_tri_mask(rows), 1.0, 0.0).astype(BF16)
    gates = misc_ref[...] + gb_ref[...]
    bcum = _cumsum_rows(jax.nn.log_sigmoid(gates), tril16)
    m_all = m_scr[...]

    def stack(f):
        return jnp.concatenate([f(h) for h in range(B_HEADS)], axis=0)

    lane_q = lax.broadcasted_iota(jnp.int32, (rows, wq), 1)
    qx, kx = q_ref[...], k_ref[...]
    q_all = stack(lambda h: jnp.where(lane_q // B_QK_DIM == h, qx, 0.0))
    k_all = stack(lambda h: jnp.where(lane_q // B_QK_DIM == h, kx, 0.0)) * B_QK_DIM ** -0.5
    v_all = stack(lambda h: v_ref[:, h * B_V_DIM:(h + 1) * B_V_DIM])
    b_col = stack(lambda h: bcum[:, MISC_FB + h:MISC_FB + h + 1])
    i_col = stack(lambda h: gates[:, MISC_IB + h:MISC_IB + h + 1])
    m_col = stack(lambda h: jnp.broadcast_to(m_all[:, h:h + 1], (rows, 1)))
    b_end = stack(lambda h: jnp.broadcast_to(bcum[rows - 1:rows, MISC_FB + h:MISC_FB + h + 1], (rows, 1)))
    b_row = jnp.broadcast_to(b_col, (hr, LANE)).T[0:1, :]
    i_row = jnp.broadcast_to(i_col, (hr, LANE)).T[0:1, :]

    r = lax.broadcasted_iota(jnp.int32, (hr, hr), 0)
    c = lax.broadcasted_iota(jnp.int32, (hr, hr), 1)
    incl = ((r // rows) == (c // rows)) & (r >= c)
    dmat = jnp.where(incl, b_col - b_row + i_row, -jnp.inf)
    inter = b_col + m_col
    mrow = jnp.maximum(inter, jnp.max(dmat, axis=1, keepdims=True))
    w_state = jnp.exp(inter - mrow)
    q16 = q_all.astype(BF16)
    v16 = v_all.astype(BF16)
    scores = _dot_nt(q16, k_all.astype(BF16)) * jnp.exp(dmat - mrow)
    cs = c_scr[...]
    n_row = n_scr[...]
    num = _dot(scores.astype(BF16), v16) + w_state * _dot(q16, cs.astype(BF16))
    den = jnp.sum(scores, axis=1, keepdims=True) + w_state * jnp.sum(q_all * n_row, axis=1, keepdims=True)
    hh = num / jnp.maximum(jnp.abs(den), jnp.exp(-mrow))
    gain = gain_ref[...]
    for h in range(B_HEADS):
        h_ref[:, h * B_V_DIM:(h + 1) * B_V_DIM] = (_rms_rows(hh[h * rows:(h + 1) * rows], gain)
                                                   * jax.nn.sigmoid(og_ref[:, h * B_V_DIM:(h + 1) * B_V_DIM]))

    g_col = b_end - b_col + i_col
    lane1 = lax.broadcasted_iota(jnp.int32, (1, LANE), 1)
    m_next = m_all
    m_new_rows, keep_rows, keep_lanes = [], [], []
    for h in range(B_HEADS):
        m_h = m_all[:, h:h + 1]
        be = bcum[rows - 1:rows, MISC_FB + h:MISC_FB + h + 1]
        m_new = jnp.maximum(be + m_h, jnp.max(g_col[h * rows:(h + 1) * rows], axis=0, keepdims=True))
        keep = jnp.exp(be + m_h - m_new)
        m_next = jnp.where(lane1 == h, m_new, m_next)
        m_new_rows.append(jnp.broadcast_to(m_new, (rows, 1)))
        keep_rows.append(jnp.broadcast_to(keep, (B_QK_DIM, 1)))
        keep_lanes.append(jnp.broadcast_to(keep, (1, B_QK_DIM)))
    kw = k_all * jnp.exp(g_col - jnp.concatenate(m_new_rows, axis=0))
    c_scr[...] = jnp.concatenate(keep_rows, axis=0) * cs + _dot_tn(kw.astype(BF16), v16)
    n_scr[...] = jnp.concatenate(keep_lanes, axis=1) * n_row + jnp.sum(kw, axis=0, keepdims=True)
    m_scr[...] = m_next

    @pl.when(ci == pl.num_programs(1) - 1)
    def _():
        c_out_ref[0] = c_scr[...]
        n_out_ref[0] = n_scr[...]
        m_out_ref[0] = m_scr[...]


def mlstm_mixer(proj, gate_bias, gain, c0, n0, m0, *, bsz, t, chunk):
    nc = t // chunk
    gb = jnp.zeros((1, LANE), F32)
    gb = gb.at[0, MISC_IB:MISC_IB + B_HEADS].set(gate_bias[0]).at[0, MISC_FB:MISC_FB + B_HEADS].set(gate_bias[1])
    c0 = c0.reshape(bsz, B_HEADS * B_QK_DIM, B_V_DIM)
    n0 = n0.reshape(bsz, 1, B_HEADS * B_QK_DIM)
    m0 = jnp.pad(m0, ((0, 0), (0, LANE - B_HEADS))).reshape(bsz, 1, LANE)
    wq = B_HEADS * B_QK_DIM
    wv = B_HEADS * B_V_DIM
    h, c, n, m = _chunk_call(
        _mlstm_kernel, bsz=bsz, nc=nc, rows=chunk,
        ins=[(proj, 'rows', wq, AB_QB), (proj, 'rows', wq, AB_KB), (proj, 'rows', wv, AB_VB),
             (proj, 'rows', wv, AB_OB), (proj, 'rows', LANE, AB_MISC), (gb, 'const', 0, 0),
             (gain.reshape(1, B_V_DIM), 'const', 0, 0), (c0, 'batch', 0, 0), (n0, 'batch', 0, 0),
             (m0, 'batch', 0, 0)],
        outs=[(jax.ShapeDtypeStruct((bsz * t, wv), F32), 'rows', wv, 0),
              (jax.ShapeDtypeStruct(c0.shape, F32), 'batch', 0, 0),
              (jax.ShapeDtypeStruct(n0.shape, F32), 'batch', 0, 0),
              (jax.ShapeDtypeStruct(m0.shape, F32), 'batch', 0, 0)],
        scratch=[pltpu.VMEM((wq, B_V_DIM), F32), pltpu.VMEM((1, wq), F32), pltpu.VMEM((1, LANE), F32)],
        name="mlstm_mixer")
    return (h, c.reshape(bsz, B_HEADS, B_QK_DIM, B_V_DIM), n.reshape(bsz, B_HEADS, B_QK_DIM),
            m.reshape(bsz, LANE)[:, :B_HEADS])


CD_QKV, CD_ZC, CD_QD, CD_FD, CD_VD, CD_GD, CD_MISC, CD_TOTAL = 0, 1536, 2048, 2560, 3072, 3584, 4096, 4224
MISC_BC, MISC_AC = 0, 4
TAIL = 8


def _permute_w_in_cd(w):
    qkv, bc, ac, zc, qd, fd, vd, gd = _split_cols(w, ODD_SPLITS)
    pad = jnp.zeros((w.shape[0], LANE - 2 * C_HEADS), w.dtype)
    return jnp.concatenate([qkv, zc, qd, fd, vd, gd, bc, ac, pad], axis=1)


def _gdn_kernel(qkv_ref, z_ref, misc_ref, cw_ref, alog_ref, dt_ref, gain_ref, s0_ref, tail0_ref,
                o_ref, s_out_ref, st_scr, tail_scr):
    ci = pl.program_id(1)

    @pl.when(ci == 0)
    def _():
        for h in range(C_HEADS):
            st_scr[:, h * C_DIM:(h + 1) * C_DIM] = s0_ref[0, h].T
        tail_scr[...] = tail0_ref[0]

    rows = qkv_ref.shape[0]
    width = qkv_ref.shape[1]
    x = qkv_ref[...]
    tail = tail_scr[...]
    row8 = lax.broadcasted_iota(jnp.int32, (TAIL, width), 0)
    acc = x * cw_ref[CONV_W - 1:CONV_W, :]
    for back in range(1, CONV_W):
        rolled = pltpu.roll(x, back, 0)
        first = jnp.where(row8 < back, pltpu.roll(tail, back, 0), rolled[0:TAIL])
        shifted = first if rows == TAIL else jnp.concatenate([first, rolled[TAIL:]], axis=0)
        acc = acc + shifted * cw_ref[CONV_W - 1 - back:CONV_W - back, :]
    tail_scr[...] = x[rows - TAIL:rows]
    conv = acc * jax.nn.sigmoid(acc)

    tril16 = jnp.where(_tri_mask(rows), 1.0, 0.0).astype(BF16)
    misc = misc_ref[...]
    beta_t = jax.nn.sigmoid(misc)
    g_t = -jnp.exp(alog_ref[...]) * jax.nn.softplus(misc + dt_ref[...])
    gcum = _cumsum_rows(g_t, tril16)

    hd = C_HEADS * C_DIM
    hr = C_HEADS * rows

    def stack(f):
        return jnp.concatenate([f(h) for h in range(C_HEADS)], axis=0)

    def l2n(v):
        return v * lax.rsqrt(jnp.sum(v * v, axis=-1, keepdims=True) + EPS)

    q_all = stack(lambda h: l2n(conv[:, h * C_DIM:(h + 1) * C_DIM])) * C_DIM ** -0.5
    k_all = stack(lambda h: l2n(conv[:, hd + h * C_DIM:hd + (h + 1) * C_DIM]))
    v_all = stack(lambda h: conv[:, 2 * hd + h * C_DIM:2 * hd + (h + 1) * C_DIM])
    beta = stack(lambda h: beta_t[:, MISC_BC + h:MISC_BC + h + 1])
    gc = stack(lambda h: gcum[:, MISC_AC + h:MISC_AC + h + 1])
    g_end = stack(lambda h: jnp.broadcast_to(gcum[rows - 1:rows, MISC_AC + h:MISC_AC + h + 1], (rows, 1)))
    gc_row = jnp.broadcast_to(gc, (hr, LANE)).T[0:1, :]

    r = lax.broadcasted_iota(jnp.int32, (hr, hr), 0)
    c = lax.broadcasted_iota(jnp.int32, (hr, hr), 1)
    same = (r // rows) == (c // rows)
    incl = same & (r >= c)
    strict = same & (r > c)
    decay = jnp.exp(jnp.where(incl, gc - gc_row, -jnp.inf))
    k16 = k_all.astype(BF16)
    a_mat = jnp.where(strict, beta * _dot_nt(k16, k16) * decay, 0.0)
    power = -a_mat
    inv = jnp.where(r == c, 1.0, 0.0) + power
    for _ in range(int(math.log2(rows)) - 1):
        power = _dot_f32(power, power)
        inv = inv + _dot_f32(inv, power)
    inv_hi, inv_lo = _split2(inv)
    rhs = jnp.concatenate([beta * v_all, beta * jnp.exp(gc) * k_all], axis=1).astype(BF16)
    w = _dot(inv_hi, rhs) + _dot(inv_lo, rhs)
    w_v, w_k = w[:, 0:C_DIM], w[:, C_DIM:2 * C_DIM]
    qk = _dot_nt(q_all.astype(BF16), k16) * decay

    head_of_row = lax.broadcasted_iota(jnp.int32, (hr, C_DIM), 0) // rows

    def per_head_lanes(m):
        return jnp.concatenate([jnp.where(head_of_row == h, m, 0.0) for h in range(C_HEADS)], axis=1).astype(BF16)

    st = st_scr[...]
    st16 = st.astype(BF16)
    delta = w_v - _dot_nt(per_head_lanes(w_k), st16)
    d16 = delta.astype(BF16)
    out = _dot_nt(per_head_lanes(q_all * jnp.exp(gc)), st16) + _dot(qk.astype(BF16), d16)
    keep = jnp.concatenate([jnp.broadcast_to(jnp.exp(gcum[rows - 1:rows, MISC_AC + h:MISC_AC + h + 1]), (1, C_DIM))
                            for h in range(C_HEADS)], axis=1)
    st_scr[...] = keep * st + _dot_tn(d16, per_head_lanes(k_all * jnp.exp(g_end - gc)))
    gain = gain_ref[...]
    for h in range(C_HEADS):
        z = z_ref[:, h * C_DIM:(h + 1) * C_DIM]
        o_ref[:, h * C_DIM:(h + 1) * C_DIM] = _rms_rows(out[h * rows:(h + 1) * rows], gain) * (z * jax.nn.sigmoid(z))

    @pl.when(ci == pl.num_programs(1) - 1)
    def _():
        for h in range(C_HEADS):
            s_out_ref[0, h] = st_scr[:, h * C_DIM:(h + 1) * C_DIM].T


def gdn_mixer(proj, conv_w, a_log, dt_bias, gain, s0, conv_prev, *, bsz, t, chunk):
    nc = t // chunk
    hd = C_HEADS * C_DIM
    lanes = jnp.zeros((1, LANE), F32)
    alog = lanes.at[0, MISC_AC:MISC_AC + C_HEADS].set(a_log)
    dt = lanes.at[0, MISC_AC:MISC_AC + C_HEADS].set(dt_bias)
    tail0 = jnp.pad(conv_prev, ((0, 0), (TAIL - (CONV_W - 1), 0), (0, 0)))
    out, s = _chunk_call(
        _gdn_kernel, bsz=bsz, nc=nc, rows=chunk,
        ins=[(proj, 'rows', 3 * hd, CD_QKV), (proj, 'rows', hd, CD_ZC), (proj, 'rows', LANE, CD_MISC),
             (conv_w, 'const', 0, 0), (alog, 'const', 0, 0), (dt, 'const', 0, 0),
             (gain.reshape(1, C_DIM), 'const', 0, 0), (s0, 'batch', 0, 0), (tail0, 'batch', 0, 0)],
        outs=[(jax.ShapeDtypeStruct((bsz * t, hd), F32), 'rows', hd, 0),
              (jax.ShapeDtypeStruct(s0.shape, F32), 'batch', 0, 0)],
        scratch=[pltpu.VMEM((C_DIM, hd), F32), pltpu.VMEM((TAIL, 3 * hd), F32)],
        name="gdn_mixer")
    return out, s


def _hgrn2_kernel(q_ref, f_ref, v_ref, g_ref, lb_ref, gain_ref, s0_ref, o_ref, s_out_ref, st_scr):
    ci = pl.program_id(1)

    @pl.when(ci == 0)
    def _():
        for h in range(D_HEADS):
            st_scr[h] = s0_ref[0, h].T

    rows = q_ref.shape[0]
    tril16 = jnp.where(_tri_mask(rows), 1.0, 0.0).astype(BF16)
    lb = lb_ref[...]
    zf = f_ref[...]
    logf = jnp.logaddexp(jnp.log(lb), jnp.log1p(-lb) + jax.nn.log_sigmoid(zf))
    kd = (1.0 - lb) * jax.nn.sigmoid(-zf)
    qx = q_ref[...]
    qd = qx * jax.nn.sigmoid(qx)
    bcum = _cumsum_rows(logf, tril16)
    gain = gain_ref[...]
    row_sub = lax.broadcasted_iota(jnp.int32, (SUB, 1), 0)
    for h in range(D_HEADS):
        sl = slice(h * D_EXPAND, (h + 1) * D_EXPAND)
        q, k, b = qd[:, sl], kd[:, sl], bcum[:, sl]
        v = v_ref[:, h * D_V_DIM:(h + 1) * D_V_DIM]
        v16 = v.astype(BF16)
        st = st_scr[h]
        inter = _dot_nt((q * jnp.exp(b)).astype(BF16), st.astype(BF16))
        blocks = []
        for i in range(rows // SUB):
            r0 = i * SUB
            qi, bi = q[r0:r0 + SUB], b[r0:r0 + SUB]
            oi = inter[r0:r0 + SUB]
            if i > 0:
                ref = b[r0 - 1:r0]
                att = _dot_nt((qi * jnp.exp(bi - ref)).astype(BF16),
                              (k[0:r0] * jnp.exp(ref - b[0:r0])).astype(BF16))
                oi = oi + _dot(att.astype(BF16), v16[0:r0])
            for s in range(SUB):
                r = r0 + s
                a = jnp.sum(qi * jnp.exp(bi - b[r:r + 1]) * k[r:r + 1], axis=1, keepdims=True)
                oi = oi + jnp.where(row_sub >= s, a, 0.0) * v[r:r + 1]
            blocks.append(oi)
        out = blocks[0] if len(blocks) == 1 else jnp.concatenate(blocks, axis=0)
        b_end = b[rows - 1:rows]
        st_scr[h] = jnp.exp(b_end) * st + _dot_tn(v16, (k * jnp.exp(b_end - b)).astype(BF16))
        g = g_ref[:, h * D_V_DIM:(h + 1) * D_V_DIM]
        o_ref[:, h * D_V_DIM:(h + 1) * D_V_DIM] = _rms_rows(out, gain) * (g * jax.nn.sigmoid(g))

    @pl.when(ci == pl.num_programs(1) - 1)
    def _():
        for h in range(D_HEADS):
            s_out_ref[0, h] = st_scr[h].T


def hgrn2_mixer(proj, lower_bound, gain, s0, *, bsz, t, chunk):
    nc = t // chunk
    wk = D_HEADS * D_EXPAND
    wv = D_HEADS * D_V_DIM
    out, s = _chunk_call(
        _hgrn2_kernel, bsz=bsz, nc=nc, rows=chunk,
        ins=[(proj, 'rows', wk, CD_QD), (proj, 'rows', wk, CD_FD), (proj, 'rows', wv, CD_VD),
             (proj, 'rows', wv, CD_GD), (lower_bound.reshape(1, wk), 'const', 0, 0),
             (gain.reshape(1, D_V_DIM), 'const', 0, 0), (s0, 'batch', 0, 0)],
        outs=[(jax.ShapeDtypeStruct((bsz * t, wv), F32), 'rows', wv, 0),
              (jax.ShapeDtypeStruct(s0.shape, F32), 'batch', 0, 0)],
        scratch=[pltpu.VMEM((D_HEADS, D_V_DIM, D_EXPAND), F32)],
        name="hgrn2_mixer")
    return out, s


def _pad_cols(w, mult=LANE):
    pad = (-w.shape[-1]) % mult
    return jnp.pad(w, [(0, 0)] * (w.ndim - 1) + [(0, pad)])


def _mixer_ab(proj, bsz, t, pos, prm, cache):
    n = bsz * t
    qpad, qipad, k16, v16, ki16, k32, v32, ki32 = dsa_prep(proj, pos, prm['a_q_gain'][0], prm['a_k_gain'][0], t)
    if cache is None:
        limit = jnp.tile((pos // CHUNK + 1) * CHUNK, bsz).reshape(n, 1)
        a_out = dsa_attention(qpad, qipad, proj, limit, k16, v16, ki16, bsz=bsz, tq=t, tk=t,
                              causal=True, n_sel=min(TOPK_MAX, t // 4))
        c0 = jnp.zeros((bsz, B_HEADS, B_QK_DIM, B_V_DIM), F32)
        n0 = jnp.zeros((bsz, B_HEADS, B_QK_DIM), F32)
        m0 = jnp.zeros((bsz, B_HEADS), F32)
        chunk = CHUNK
    else:
        k_c, v_c, ki_c, c0, n0, m0 = cache
        past = k_c.shape[1]
        n_keys = past + t
        tk = -(-n_keys // KEY_CHUNK) * KEY_CHUNK

        def with_cache(c, new):
            c = c.reshape(bsz, past, -1).astype(BF16)
            c = jnp.pad(c, ((0, 0), (0, 0), (0, LANE - c.shape[-1])))
            return jnp.concatenate([c, new.reshape(bsz, t, LANE),
                                    jnp.zeros((bsz, tk - n_keys, LANE), BF16)], axis=1).reshape(bsz * tk, LANE)

        limit = jnp.full((n, 1), n_keys, jnp.int32)
        a_out = dsa_attention(qpad, qipad, proj, limit, with_cache(k_c, k16), with_cache(v_c, v16),
                              with_cache(ki_c, ki16), bsz=bsz, tq=t, tk=tk, causal=False,
                              n_sel=min(TOPK_MAX, n_keys // 4))
        chunk = t
    h, c, n_, m = mlstm_mixer(proj, prm['b_gate_bias'][0], prm['b_norm_gain'][0], c0, n0, m0,
                              bsz=bsz, t=t, chunk=chunk)
    st = (k32.reshape(bsz, t, A_KV_HEADS, HEAD_DIM), v32.reshape(bsz, t, A_KV_HEADS, HEAD_DIM),
          ki32.reshape(bsz, t, IDX_DIM), c, n_, m)
    return a_out, h, st


def _mixer_cd(proj, bsz, t, prm, lower_bound, cache):
    hd = C_HEADS * C_DIM
    if cache is None:
        sc0 = jnp.zeros((bsz, C_HEADS, C_DIM, C_DIM), F32)
        conv_prev = jnp.zeros((bsz, CONV_W - 1, 3 * hd), F32)
        sd0 = jnp.zeros((bsz, D_HEADS, D_EXPAND, D_V_DIM), F32)
        chunk = CHUNK
    else:
        sc0, conv_prev, sd0 = cache
        chunk = t
    oc, sc = gdn_mixer(proj, prm['c_conv_w'][0], prm['c_a_log'][0], prm['c_dt_bias'][0], prm['c_norm_gain'][0],
                       sc0, conv_prev, bsz=bsz, t=t, chunk=chunk)
    od, sd = hgrn2_mixer(proj, lower_bound, prm['d_norm_gain'][0], sd0, bsz=bsz, t=t, chunk=chunk)
    qkv = proj.reshape(bsz, t, -1)[:, :, CD_QKV:CD_QKV + 3 * hd]
    conv_new = jnp.concatenate([conv_prev, qkv[:, t - (CONV_W - 1):]], axis=1)[:, -(CONV_W - 1):]
    return oc, od, (sc, conv_new, sd)


def _trunk(x, pos_offset, cache, prm, wts):
    bsz, t, d = x.shape
    n = bsz * t
    pos = pos_offset + jnp.arange(t, dtype=jnp.int32)
    probs = jax.nn.softmax(prm['d_lb_logits'], axis=0)
    lower_bounds = jnp.cumsum(probs, axis=0) - probs[0]
    xf = x.reshape(n, d)

    lc = None if cache is None else tuple(c[0] for c in cache[:6])
    proj = norm_matmul(xf, prm['norm_mix'][0], wts['w_in_ab'])
    a_out, b_out, st_even = _mixer_ab(proj, bsz, t, pos, prm, lc)
    xf = matmul_residual(a_out, b_out, wts['w_out_ab'], xf)
    xf = ffn_residual(xf, prm['norm_ffn'][0], wts['ffn_w1'], wts['ffn_w3'], wts['ffn_w2'])

    lc = None if cache is None else tuple(c[0] for c in cache[6:])
    proj = norm_matmul(xf, prm['norm_mix'][1], wts['w_in_cd'])
    c_out, d_out, st_odd = _mixer_cd(proj, bsz, t, prm, lower_bounds[1], lc)
    xf = matmul_residual(c_out, d_out, wts['w_out_cd'], xf)
    xf = moe_residual(xf, prm['norm_ffn'][1], wts['moe_router'], wts['moe_w1'], wts['moe_w3'], wts['moe_w2'])

    new_state = tuple(s[None] for s in st_even + st_odd)
    return xf.reshape(bsz, t, d), new_state


def kernel(x_prompt, x_sample, cache_a_k, cache_a_v, cache_a_kidx, state_b_c, state_b_n, state_b_m,
           state_c_s, state_c_conv, state_d_s, norm_mix, norm_ffn, w_in_ab, w_out_ab, a_q_gain, a_k_gain,
           b_gate_bias, b_norm_gain, w_in_cd, w_out_cd, c_conv_w, c_a_log, c_dt_bias, c_norm_gain,
           d_lb_logits, d_norm_gain, ffn_w1, ffn_w3, ffn_w2, moe_router, moe_w1, moe_w3, moe_w2):
    prm = dict(norm_mix=norm_mix, norm_ffn=norm_ffn, a_q_gain=a_q_gain, a_k_gain=a_k_gain,
               b_gate_bias=b_gate_bias, b_norm_gain=b_norm_gain, c_conv_w=c_conv_w, c_a_log=c_a_log,
               c_dt_bias=c_dt_bias, c_norm_gain=c_norm_gain, d_lb_logits=d_lb_logits, d_norm_gain=d_norm_gain)
    wts = dict(w_in_ab=_permute_w_in_ab(w_in_ab[0]).astype(BF16), w_out_ab=w_out_ab[0].astype(BF16),
               w_in_cd=_permute_w_in_cd(w_in_cd[0]).astype(BF16), w_out_cd=w_out_cd[0].astype(BF16),
               ffn_w1=ffn_w1[0].astype(BF16), ffn_w3=ffn_w3[0].astype(BF16), ffn_w2=ffn_w2[0].astype(BF16),
               moe_router=_pad_cols(moe_router[0]),
               moe_w1=moe_w1[0].astype(BF16), moe_w3=moe_w3[0].astype(BF16), moe_w2=moe_w2[0].astype(BF16))
    cache = (cache_a_k, cache_a_v, cache_a_kidx, state_b_c, state_b_n, state_b_m, state_c_s, state_c_conv, state_d_s)
    y_prompt, st_p = _trunk(x_prompt, 0, None, prm, wts)
    y_sample, st_s = _trunk(x_sample, cache_a_k.shape[2], cache, prm, wts)
    return (y_prompt, y_sample) + st_p + st_s
```

```python
import functools
import math

import jax
import jax.numpy as jnp
import numpy as np
from jax import lax
from jax.experimental import pallas as pl
from jax.experimental.pallas import tpu as pltpu

F32 = jnp.float32
BF16 = jnp.bfloat16

EPS = 1e-6
ROPE_THETA = 500000.0
ROT_FRACTION = 4
CHUNK = 64
A_HEADS, A_KV_HEADS, HEAD_DIM = 8, 2, 64
IDX_HEADS, IDX_DIM = 4, 64
TOPK_MAX, Q_BLOCK = 256, 128
B_HEADS, B_QK_DIM, B_V_DIM = 4, 64, 128
C_HEADS, C_DIM, CONV_W = 4, 128, 4
D_HEADS, D_EXPAND, D_V_DIM = 4, 128, 128
N_EXPERTS, TOP_K_EXPERTS = 8, 2

LANE = 128
VMEM_LIMIT = 48 * 1024 * 1024

EVEN_SPLITS = (A_HEADS * HEAD_DIM, A_KV_HEADS * HEAD_DIM, A_KV_HEADS * HEAD_DIM,
               IDX_HEADS * IDX_DIM, IDX_DIM, IDX_HEADS,
               B_HEADS * B_QK_DIM, B_HEADS * B_QK_DIM, B_HEADS * B_V_DIM,
               B_HEADS, B_HEADS, B_HEADS * B_V_DIM)
ODD_SPLITS = (3 * C_HEADS * C_DIM, C_HEADS, C_HEADS, C_HEADS * C_DIM,
              D_HEADS * D_EXPAND, D_HEADS * D_EXPAND, D_HEADS * D_V_DIM, D_HEADS * D_V_DIM)


def _split_cols(p, widths):
    cuts = [int(c) for c in np.cumsum(widths)[:-1]]
    return jnp.split(p, cuts, axis=-1)


def _row_tile(n, target):
    t = min(n, target)
    while n % t:
        t //= 2
    return t


def _col_tile(n, target):
    best = LANE
    for k in range(1, n // LANE + 1):
        c = k * LANE
        if n % c == 0 and c <= target:
            best = c
    return best


def _rms_rows(x, gain):
    return x * lax.rsqrt(jnp.mean(x * x, axis=-1, keepdims=True) + EPS) * gain


def _norm_matmul_kernel(x_ref, g_ref, w_ref, o_ref, xn_ref):
    @pl.when(pl.program_id(1) == 0)
    def _():
        xn_ref[...] = _rms_rows(x_ref[...], g_ref[...]).astype(BF16)

    o_ref[...] = jnp.dot(xn_ref[...], w_ref[...], preferred_element_type=F32)


def norm_matmul(x, gain, w):
    n, d = x.shape
    m = w.shape[1]
    tm = _row_tile(n, 1024)
    tn = _col_tile(m, 1536)
    return pl.pallas_call(
        _norm_matmul_kernel,
        grid=(n // tm, m // tn),
        in_specs=[pl.BlockSpec((tm, d), lambda i, j: (i, 0)),
                  pl.BlockSpec((1, d), lambda i, j: (0, 0)),
                  pl.BlockSpec((d, tn), lambda i, j: (0, j))],
        out_specs=pl.BlockSpec((tm, tn), lambda i, j: (i, j)),
        out_shape=jax.ShapeDtypeStruct((n, m), F32),
        scratch_shapes=[pltpu.VMEM((tm, d), BF16)],
        compiler_params=pltpu.CompilerParams(
            dimension_semantics=("parallel", "arbitrary"), vmem_limit_bytes=VMEM_LIMIT),
        name="norm_matmul",
    )(x, gain.reshape(1, d), w)


def _matmul_res_kernel(a_ref, b_ref, w_ref, r_ref, o_ref):
    ka = a_ref.shape[1]
    o_ref[...] = (r_ref[...] + jnp.dot(a_ref[...].astype(BF16), w_ref[0:ka, :], preferred_element_type=F32)
                  + jnp.dot(b_ref[...].astype(BF16), w_ref[ka:, :], preferred_element_type=F32))


def matmul_residual(a, b, w, res):
    n, ka = a.shape
    kb = b.shape[1]
    m = w.shape[1]
    tm = _row_tile(n, 1024)
    return pl.pallas_call(
        _matmul_res_kernel,
        grid=(n // tm,),
        in_specs=[pl.BlockSpec((tm, ka), lambda i: (i, 0)),
                  pl.BlockSpec((tm, kb), lambda i: (i, 0)),
                  pl.BlockSpec((ka + kb, m), lambda i: (0, 0)),
                  pl.BlockSpec((tm, m), lambda i: (i, 0))],
        out_specs=pl.BlockSpec((tm, m), lambda i: (i, 0)),
        out_shape=jax.ShapeDtypeStruct((n, m), F32),
        compiler_params=pltpu.CompilerParams(
            dimension_semantics=("parallel",), vmem_limit_bytes=VMEM_LIMIT),
        name="matmul_residual",
    )(a, b, w, res)


def _swiglu_tile(xn, w1, w3):
    h1 = jnp.dot(xn, w1, preferred_element_type=F32)
    h3 = jnp.dot(xn, w3, preferred_element_type=F32)
    return h1 * jax.nn.sigmoid(h1) * h3


def _ffn_kernel(x_ref, g_ref, w1_ref, w3_ref, w2_ref, o_ref, xn_ref):
    @pl.when(pl.program_id(1) == 0)
    def _():
        x = x_ref[...]
        xn_ref[...] = _rms_rows(x, g_ref[...]).astype(BF16)
        o_ref[...] = x

    act = _swiglu_tile(xn_ref[...], w1_ref[...], w3_ref[...])
    o_ref[...] += jnp.dot(act.astype(BF16), w2_ref[...], preferred_element_type=F32)


def ffn_residual(x, gain, w1, w3, w2):
    n, d = x.shape
    f = w1.shape[1]
    tm = _row_tile(n, 1024)
    tf = _col_tile(f, 512)
    return pl.pallas_call(
        _ffn_kernel,
        grid=(n // tm, f // tf),
        in_specs=[pl.BlockSpec((tm, d), lambda i, j: (i, 0)),
                  pl.BlockSpec((1, d), lambda i, j: (0, 0)),
                  pl.BlockSpec((d, tf), lambda i, j: (0, j)),
                  pl.BlockSpec((d, tf), lambda i, j: (0, j)),
                  pl.BlockSpec((tf, d), lambda i, j: (j, 0))],
        out_specs=pl.BlockSpec((tm, d), lambda i, j: (i, 0)),
        out_shape=jax.ShapeDtypeStruct((n, d), F32),
        scratch_shapes=[pltpu.VMEM((tm, d), BF16)],
        compiler_params=pltpu.CompilerParams(
            dimension_semantics=("parallel", "arbitrary"), vmem_limit_bytes=VMEM_LIMIT),
        name="ffn_residual",
    )(x, gain.reshape(1, d), w1, w3, w2)


MOE_VMEM_LIMIT = 58 * 1024 * 1024
MOE_CAPS = (256, 384, 512)


def _moe_route_kernel(x_ref, g_ref, r_ref, xn_ref, comb_ref, post_ref, cnt_ref):
    x = x_ref[...]
    tm = x.shape[0]
    xn = _rms_rows(x, g_ref[...])
    xn_ref[...] = xn.astype(BF16)
    logits = jnp.dot(xn, r_ref[...], preferred_element_type=F32, precision=lax.Precision.HIGHEST)
    lane = lax.broadcasted_iota(jnp.int32, logits.shape, 1)
    logits = jnp.where(lane < N_EXPERTS, logits, -jnp.inf)
    m1 = jnp.max(logits, axis=-1, keepdims=True)
    i1 = jnp.min(jnp.where(logits == m1, lane, LANE), axis=-1, keepdims=True)
    rest = jnp.where(lane == i1, -jnp.inf, logits)
    m2 = jnp.max(rest, axis=-1, keepdims=True)
    i2 = jnp.min(jnp.where(rest == m2, lane, LANE), axis=-1, keepdims=True)
    e2 = jnp.exp(m2 - m1)
    den = 1.0 + e2
    comb_ref[...] = jnp.where(lane == i1, 1.0 / den, 0.0) + jnp.where(lane == i2, e2 / den, 0.0)
    chosen = (lane == i1) | (lane == i2)
    sel = jnp.where(chosen, 1.0, 0.0)
    tril16 = jnp.where(_tri_mask(LANE), 1.0, 0.0).astype(BF16)
    seen = jnp.zeros((1, LANE), F32)
    ranks = []
    for blk in range(tm // LANE):
        sb = sel[blk * LANE:(blk + 1) * LANE]
        ranks.append(_dot(tril16, sb.astype(BF16)) + seen - 1.0)
        seen = seen + jnp.sum(sb, axis=0, keepdims=True)
    rank = jnp.where(chosen, jnp.concatenate(ranks, axis=0), -1.0)
    post_ref[0] = rank.T[0:N_EXPERTS, :]
    cnt_ref[0] = seen


def _moe_expert_kernel(cnt_ref, xn_ref, comb_ref, post_ref, x_ref, w1_ref, w3_ref, w2_ref, o_ref,
                       xe_scr, y_scr, *, caps):
    i = pl.program_id(0)
    e = pl.program_id(1)
    j = pl.program_id(2)
    last = pl.num_programs(2) - 1
    tm = x_ref.shape[0]

    @pl.when(jnp.logical_and(e == 0, j == 0))
    def _():
        o_ref[...] = x_ref[...]

    cnt = cnt_ref[i * N_EXPERTS + e]

    def expert_step(cap):
        def pick():
            rank_row = post_ref[0, pl.ds(e, 1), :].astype(jnp.int32)
            slot = lax.broadcasted_iota(jnp.int32, (cap, tm), 0)
            return jnp.where(rank_row == slot, 1.0, 0.0).astype(BF16)

        @pl.when(j == 0)
        def _():
            xe_scr[0:cap, :] = _dot(pick(), xn_ref[...]).astype(BF16)

        act = _swiglu_tile(xe_scr[0:cap, :], w1_ref[0], w3_ref[0])
        yj = _dot(act.astype(BF16), w2_ref[0])

        @pl.when(j == 0)
        def _():
            y_scr[0:cap, :] = yj

        @pl.when(j > 0)
        def _():
            y_scr[0:cap, :] += yj

        @pl.when(j == last)
        def _():
            comb = comb_ref[...]
            lane = lax.broadcasted_iota(jnp.int32, comb.shape, 1)
            gate = jnp.sum(jnp.where(lane == e, comb, 0.0), axis=-1, keepdims=True)
            hi, lo = _split2(y_scr[0:cap, :])
            p = pick()
            o_ref[...] += gate * (_dot_tn(p, hi) + _dot_tn(p, lo))

    lo = 0
    for cap in caps:
        @pl.when(jnp.logical_and(cnt > lo, cnt <= cap))
        def _(cap=cap):
            expert_step(cap)
        lo = cap


def moe_residual(x, gain, router, w1, w3, w2):
    n, d = x.shape
    ne, _, f = w1.shape
    tm = _row_tile(n, 1024)
    tf = _col_tile(f, 896)
    nt = n // tm
    xn, comb, post, cnt = pl.pallas_call(
        _moe_route_kernel,
        grid=(nt,),
        in_specs=[pl.BlockSpec((tm, d), lambda i: (i, 0)),
                  pl.BlockSpec((1, d), lambda i: (0, 0)),
                  pl.BlockSpec((d, LANE), lambda i: (0, 0))],
        out_specs=[pl.BlockSpec((tm, d), lambda i: (i, 0)),
                   pl.BlockSpec((tm, LANE), lambda i: (i, 0)),
                   pl.BlockSpec((1, ne, tm), lambda i: (i, 0, 0)),
                   pl.BlockSpec((1, 1, LANE), lambda i: (i, 0, 0))],
        out_shape=[jax.ShapeDtypeStruct((n, d), BF16), jax.ShapeDtypeStruct((n, LANE), F32),
                   jax.ShapeDtypeStruct((nt, ne, tm), F32), jax.ShapeDtypeStruct((nt, 1, LANE), F32)],
        compiler_params=pltpu.CompilerParams(dimension_semantics=("parallel",), vmem_limit_bytes=VMEM_LIMIT),
        name="moe_route",
    )(x, gain.reshape(1, d), router)
    counts = cnt[:, 0, :ne].astype(jnp.int32).reshape(nt * ne)
    caps = tuple(c for c in MOE_CAPS if c < tm) + (tm,)
    grid_spec = pltpu.PrefetchScalarGridSpec(
        num_scalar_prefetch=1,
        grid=(nt, ne, f // tf),
        in_specs=[pl.BlockSpec((tm, d), lambda i, e, j, c: (i, 0)),
                  pl.BlockSpec((tm, LANE), lambda i, e, j, c: (i, 0)),
                  pl.BlockSpec((1, ne, tm), lambda i, e, j, c: (i, 0, 0)),
                  pl.BlockSpec((tm, d), lambda i, e, j, c: (i, 0)),
                  pl.BlockSpec((1, d, tf), lambda i, e, j, c: (e, 0, j)),
                  pl.BlockSpec((1, d, tf), lambda i, e, j, c: (e, 0, j)),
                  pl.BlockSpec((1, tf, d), lambda i, e, j, c: (e, j, 0))],
        out_specs=pl.BlockSpec((tm, d), lambda i, e, j, c: (i, 0)),
        scratch_shapes=[pltpu.VMEM((tm, d), BF16), pltpu.VMEM((tm, d), F32)])
    return pl.pallas_call(
        functools.partial(_moe_expert_kernel, caps=caps),
        grid_spec=grid_spec,
        out_shape=jax.ShapeDtypeStruct((n, d), F32),
        compiler_params=pltpu.CompilerParams(
            dimension_semantics=("parallel", "arbitrary", "arbitrary"), vmem_limit_bytes=MOE_VMEM_LIMIT),
        name="moe_experts",
    )(counts, xn, comb, post, x, w1, w3, w2)


AB_QA, AB_VB, AB_OB, AB_QI, AB_QB, AB_KB, AB_KA, AB_VA, AB_MISC, AB_TOTAL = (
    0, 512, 1024, 1536, 1792, 2048, 2304, 2432, 2560, 2688)
MISC_WI, MISC_IB, MISC_FB = 64, 68, 72
HALF = LANE // 2
KEY_CHUNK = 512
MASKED = -1e30
KEY_OF_NEG_INF = -2139095041
I16_MIN, I16_MAX = -32768, 32767


def _permute_w_in_ab(w):
    qa, ka, va, qi, ki, wi, qb, kb, vb, ib, fb, ob = _split_cols(w, EVEN_SPLITS)
    pad = jnp.zeros((w.shape[0], LANE - IDX_DIM - 3 * IDX_HEADS), w.dtype)
    return jnp.concatenate([qa, vb, ob, qi, qb, kb, ka, va, ki, wi, ib, fb, pad], axis=1)


def _rope_tables(pos):
    rot = HEAD_DIM // ROT_FRACTION
    half = rot // 2
    inv_freq = ROPE_THETA ** (-jnp.arange(half, dtype=F32) * 2.0 / rot)
    ang = pos.astype(F32)[:, None] * inv_freq[None, :]
    cos, sin = jnp.cos(ang), jnp.sin(ang)
    t = pos.shape[0]
    one = jnp.ones((t, HEAD_DIM - rot), F32)
    zero_r = jnp.zeros((t, HEAD_DIM - rot), F32)
    zero_h = jnp.zeros((t, half), F32)
    c = jnp.concatenate([cos, cos, one], axis=1)
    s_up = jnp.concatenate([-sin, zero_h, zero_r], axis=1)
    s_dn = jnp.concatenate([zero_h, sin, zero_r], axis=1)
    return tuple(jnp.concatenate([a, a], axis=1) for a in (c, s_up, s_dn))


def _rope_tile(x, c, s_up, s_dn):
    half = HEAD_DIM // ROT_FRACTION // 2
    return x * c + pltpu.roll(x, LANE - half, 1) * s_up + pltpu.roll(x, half, 1) * s_dn


def _head_norm_tile(x, gain, same_head):
    sq = x * x
    hi = sq.astype(BF16)
    lo = (sq - hi.astype(F32)).astype(BF16)
    ss = (jnp.dot(hi, same_head, preferred_element_type=F32)
          + jnp.dot(lo, same_head, preferred_element_type=F32))
    return x * lax.rsqrt(ss * (1.0 / HEAD_DIM) + EPS) * gain


def _aprep_kernel(qa_ref, ka_ref, va_ref, qi_ref, misc_ref, c_ref, su_ref, sd_ref, qg_ref, kg_ref,
                  qpad_ref, qipad_ref, k16_ref, v16_ref, ki16_ref, k32_ref, v32_ref, ki32_ref):
    c, su, sd = c_ref[...], su_ref[...], sd_ref[...]
    tm = c.shape[0]
    row = lax.broadcasted_iota(jnp.int32, (LANE, LANE), 0)
    col = lax.broadcasted_iota(jnp.int32, (LANE, LANE), 1)
    same_head = jnp.where(row // HALF == col // HALF, 1.0, 0.0).astype(BF16)
    lane = lax.broadcasted_iota(jnp.int32, (tm, LANE), 1)
    low = lane < HALF

    heads_per_group = A_HEADS // A_KV_HEADS
    for p in range(A_HEADS // 2):
        y = _rope_tile(_head_norm_tile(qa_ref[:, p * LANE:(p + 1) * LANE], qg_ref[...], same_head), c, su, sd)
        y = y * HEAD_DIM ** -0.5
        y_sw = pltpu.roll(y, HALF, 1)
        for o in range(2):
            h = 2 * p + o
            g = h // heads_per_group
            src = y if o == g else y_sw
            qpad_ref[:, h * LANE:(h + 1) * LANE] = jnp.where(low if g == 0 else ~low, src, 0.0).astype(BF16)
    k = _rope_tile(_head_norm_tile(ka_ref[...], kg_ref[...], same_head), c, su, sd)
    k32_ref[...] = k
    k16_ref[...] = k.astype(BF16)
    v = va_ref[...]
    v32_ref[...] = v
    v16_ref[...] = v.astype(BF16)
    for p in range(IDX_HEADS // 2):
        y = _rope_tile(qi_ref[:, p * LANE:(p + 1) * LANE], c, su, sd)
        y_sw = pltpu.roll(y, HALF, 1)
        qipad_ref[:, (2 * p) * LANE:(2 * p + 1) * LANE] = jnp.where(low, y, 0.0).astype(BF16)
        qipad_ref[:, (2 * p + 1) * LANE:(2 * p + 2) * LANE] = jnp.where(low, y_sw, 0.0).astype(BF16)
    ki = _rope_tile(misc_ref[...], c, su, sd)
    ki32_ref[...] = ki[:, :IDX_DIM]
    ki16_ref[...] = jnp.where(low, ki, 0.0).astype(BF16)


def dsa_prep(proj, pos, q_gain, k_gain, t):
    n = proj.shape[0]
    tm = _row_tile(n, 512)
    tabs = _rope_tables(pos)
    if t < tm:
        tabs = tuple(jnp.tile(a, (tm // t, 1)) for a in tabs)
    nt = tabs[0].shape[0] // tm
    tab_spec = pl.BlockSpec((tm, LANE), lambda i: (i % nt, 0))
    gain_spec = pl.BlockSpec((1, LANE), lambda i: (0, 0))

    def col(width, offset):
        return pl.BlockSpec((tm, width), lambda i: (i, offset // width))

    def out(width, dtype):
        return (jax.ShapeDtypeStruct((n, width), dtype), pl.BlockSpec((tm, width), lambda i: (i, 0)))

    outs = [out(A_HEADS * LANE, BF16), out(IDX_HEADS * LANE, BF16), out(LANE, BF16), out(LANE, BF16),
            out(LANE, BF16), out(LANE, F32), out(LANE, F32), out(IDX_DIM, F32)]
    return pl.pallas_call(
        _aprep_kernel,
        grid=(n // tm,),
        in_specs=[col(A_HEADS * HEAD_DIM, AB_QA), col(LANE, AB_KA), col(LANE, AB_VA),
                  col(IDX_HEADS * IDX_DIM, AB_QI), col(LANE, AB_MISC), tab_spec, tab_spec, tab_spec,
                  gain_spec, gain_spec],
        out_specs=[o[1] for o in outs],
        out_shape=[o[0] for o in outs],
        compiler_params=pltpu.CompilerParams(dimension_semantics=("parallel",), vmem_limit_bytes=VMEM_LIMIT),
        name="dsa_prep",
    )(proj, proj, proj, proj, proj, *tabs, jnp.tile(q_gain, 2).reshape(1, LANE), jnp.tile(k_gain, 2).reshape(1, LANE))


N_PARTIAL = 4


def _add_tiles(accs, m, sub):
    accs = list(accs)
    for t in range(m.shape[0] // sub):
        accs[t % len(accs)] = accs[t % len(accs)] + m[t * sub:(t + 1) * sub]
    return tuple(accs)


def _dsa_kernel(q_ref, qi_ref, misc_ref, lim_ref, k_ref, v_ref, ki_ref, o_ref, key_ref, bias_ref, hi_ref, lo_ref,
                *, nch, n_sel):
    qb = q_ref.shape[0]
    kc = KEY_CHUNK
    n_idx = IDX_HEADS
    hpg = A_HEADS // A_KV_HEADS
    nt = (((1,), (1,)), ((), ()))

    limit = lim_ref[0]
    misc_t = misc_ref[...].T
    wscale = IDX_HEADS ** -0.5 * IDX_DIM ** -0.5
    w = [misc_t[MISC_WI + j:MISC_WI + j + 1, :] * wscale for j in range(n_idx)]
    qis = [qi_ref[:, j * LANE:(j + 1) * LANE] for j in range(n_idx)]

    def score_body(c, carry):
        off = pl.multiple_of(c * kc, kc)
        kic = ki_ref[pl.ds(off, kc), :]
        s = None
        for j in range(n_idx):
            lg = jnp.maximum(lax.dot_general(kic, qis[j], nt, preferred_element_type=F32), 0.0)
            s = w[j] * lg if s is None else s + w[j] * lg
        kidx = off + lax.broadcasted_iota(jnp.int32, (kc, qb), 0)
        s = jnp.where(kidx < limit, s, -jnp.inf)
        bits = lax.bitcast_convert_type(s, jnp.int32)
        key = jnp.where(bits < 0, bits ^ 0x7FFFFFFF, bits)
        key_ref[pl.ds(off, kc), :] = key
        hi_ref[pl.ds(off, kc), :] = jnp.right_shift(key, 16).astype(jnp.int16)
        return carry

    lax.fori_loop(0, nch, score_body, 0)

    def count_ge(cand):
        sub = 8

        def body(c, accs):
            off = pl.multiple_of(c * kc, kc)
            m = jnp.where(key_ref[pl.ds(off, kc), :] >= cand, 1.0, 0.0)
            return _add_tiles(accs, m, sub)

        accs = lax.fori_loop(0, nch, body, (jnp.zeros((sub, qb), F32),) * N_PARTIAL, unroll=True)
        return jnp.sum(sum(accs), axis=0, keepdims=True)

    def count_ge16(ref, cand32):
        cand = cand32.astype(jnp.int16)
        sub = 16

        def body(c, accs):
            off = pl.multiple_of(c * kc, kc)
            m = jnp.where(ref[pl.ds(off, kc), :] >= cand, jnp.int16(1), jnp.int16(0))
            return _add_tiles(accs, m, sub)

        accs = lax.fori_loop(0, nch, body, (jnp.zeros((sub, qb), jnp.int16),) * N_PARTIAL, unroll=True)
        return jnp.sum(sum(accs).astype(F32), axis=0, keepdims=True)

    def kth_largest16(ref, want):
        tau = jnp.where(count_ge16(ref, jnp.zeros((1, qb), jnp.int32)) >= want, 0, I16_MIN).astype(jnp.int32)

        def bisect(i, tau):
            cand = tau | jnp.left_shift(jnp.int32(1), 14 - i)
            return jnp.where(count_ge16(ref, cand) >= want, cand, tau)

        return lax.fori_loop(0, 15, bisect, tau)

    want = float(n_sel)
    tau_hi = kth_largest16(hi_ref, want)
    above = jnp.where(tau_hi < I16_MAX, count_ge16(hi_ref, jnp.minimum(tau_hi + 1, I16_MAX)), 0.0)

    def low_body(c, carry):
        off = pl.multiple_of(c * kc, kc)
        key = key_ref[pl.ds(off, kc), :]
        low = (key & 0xFFFF) + I16_MIN
        lo_ref[pl.ds(off, kc), :] = jnp.where(jnp.right_shift(key, 16) == tau_hi, low, I16_MIN).astype(jnp.int16)
        return carry

    lax.fori_loop(0, nch, low_body, 0)
    tau_lo = kth_largest16(lo_ref, want - above)
    tau = jnp.left_shift(tau_hi, 16) + (tau_lo - I16_MIN)

    room = want - count_ge(tau + 1)
    r_i = lax.broadcasted_iota(jnp.int32, (LANE, LANE), 0)
    c_i = lax.broadcasted_iota(jnp.int32, (LANE, LANE), 1)
    prefix_ones = jnp.where(r_i >= c_i, 1.0, 0.0).astype(BF16)
    identity = jnp.where(r_i == c_i, 1.0, 0.0).astype(BF16)

    def bias_body(c, seen):
        off = pl.multiple_of(c * kc, kc)
        tiles = range(kc // LANE)
        xs = [key_ref[pl.ds(off + t * LANE, LANE), :] for t in tiles]
        eqs = [x == tau for x in xs]
        eqfs = [jnp.where(eq, 1.0, 0.0) for eq in eqs]
        ranks = [jnp.dot(prefix_ones, eqf.astype(BF16), preferred_element_type=F32) for eqf in eqfs]
        sels = []
        for t in tiles:
            sel = ((xs[t] > tau) | (eqs[t] & (ranks[t] + seen <= room))) & (xs[t] != KEY_OF_NEG_INF)
            sels.append(jnp.where(sel, 1.0, 0.0).astype(BF16))
            seen = seen + jnp.sum(eqfs[t], axis=0, keepdims=True)
        sel_ts = [lax.dot_general(sel, identity, (((0,), (0,)), ((), ())), preferred_element_type=F32)
                  for sel in sels]
        for t in tiles:
            bias_ref[:, pl.ds(off + t * LANE, LANE)] = jnp.where(sel_ts[t] > 0.5, 0.0, MASKED)
        return seen

    lax.fori_loop(0, nch, bias_body, jnp.zeros((1, qb), F32))

    lane = lax.broadcasted_iota(jnp.int32, (qb, LANE), 1)
    qgs = [jnp.concatenate([q_ref[:, (hpg * g + h) * LANE:(hpg * g + h + 1) * LANE] for h in range(hpg)], axis=0)
           for g in range(A_KV_HEADS)]

    def att_body(c, carry):
        off = pl.multiple_of(c * kc, kc)
        kch = k_ref[pl.ds(off, kc), :]
        vch = v_ref[pl.ds(off, kc), :]
        bias = bias_ref[:, pl.ds(off, kc)][None]
        new = []
        for g in range(A_KV_HEADS):
            m, l, acc = carry[g]
            s = lax.dot_general(qgs[g], kch, nt, preferred_element_type=F32)
            s = (s.reshape(hpg, qb, kc) + bias).reshape(hpg * qb, kc)
            m_new = jnp.maximum(m, jnp.max(s, axis=1, keepdims=True))
            alpha = jnp.exp(m - m_new)
            p = jnp.exp(s - m_new)
            l = alpha * l + jnp.sum(p, axis=1, keepdims=True)
            acc = alpha * acc + jnp.dot(p.astype(BF16), vch, preferred_element_type=F32)
            new.append((m_new, l, acc))
        return tuple(new)

    init = tuple((jnp.full((hpg * qb, 1), MASKED, F32), jnp.zeros((hpg * qb, 1), F32),
                  jnp.zeros((hpg * qb, LANE), F32)) for _ in range(A_KV_HEADS))
    res = lax.fori_loop(0, nch, att_body, init)
    outs = []
    for g in range(A_KV_HEADS):
        _, l, acc = res[g]
        og = acc / l
        for h in range(hpg):
            oh = og[h * qb:(h + 1) * qb]
            outs.append(oh if (h % 2) == g else pltpu.roll(oh, HALF, 1))
    for p in range(A_HEADS // 2):
        o_ref[:, p * LANE:(p + 1) * LANE] = jnp.where(lane < HALF, outs[2 * p], outs[2 * p + 1])


def dsa_attention(qpad, qipad, proj, limit, k16, v16, ki16, *, bsz, tq, tk, causal, n_sel):
    qb = min(Q_BLOCK, tq)
    nqb = tq // qb
    assert tk % KEY_CHUNK == 0 and tk >= n_sel
    if causal:
        per_group = KEY_CHUNK // qb
        groups = [(g * per_group, per_group, g + 1) for g in range(nqb // per_group)]
    else:
        groups = [(0, nqb, tk // KEY_CHUNK)]
    lim3 = limit.reshape(bsz * nqb, 1, qb)
    out = None
    for first, count, nch in groups:
        def qspec(width, col=0, first=first):
            return pl.BlockSpec((qb, width), lambda b, i: (b * nqb + first + i, col))

        kspec = pl.BlockSpec((tk, LANE), lambda b, i: (b, 0))
        in_specs = [qspec(A_HEADS * LANE), qspec(IDX_HEADS * LANE), qspec(LANE, AB_MISC // LANE),
                    pl.BlockSpec((1, 1, qb), lambda b, i, first=first: (b * nqb + first + i, 0, 0)),
                    kspec, kspec, kspec]
        args = [qpad, qipad, proj, lim3, k16, v16, ki16]
        kern = functools.partial(_dsa_kernel, nch=nch, n_sel=n_sel)
        aliases = {}
        if out is not None:
            in_specs.append(pl.BlockSpec(memory_space=pl.ANY))
            args.append(out)
            aliases = {len(args) - 1: 0}
            kern = functools.partial(_dsa_kernel_with_carry, nch=nch, n_sel=n_sel)
        out = pl.pallas_call(
            kern,
            grid=(bsz, count),
            in_specs=in_specs,
            out_specs=qspec(A_HEADS * HEAD_DIM),
            out_shape=jax.ShapeDtypeStruct((bsz * tq, A_HEADS * HEAD_DIM), F32),
            scratch_shapes=[pltpu.VMEM((tk, qb), jnp.int32), pltpu.VMEM((qb, tk), F32),
                            pltpu.VMEM((tk, qb), jnp.int16), pltpu.VMEM((tk, qb), jnp.int16)],
            input_output_aliases=aliases,
            compiler_params=pltpu.CompilerParams(
                dimension_semantics=("parallel", "arbitrary"), vmem_limit_bytes=VMEM_LIMIT),
            name="dsa_attention",
        )(*args)
    return out


def _dsa_kernel_with_carry(q_ref, qi_ref, misc_ref, lim_ref, k_ref, v_ref, ki_ref, carry_ref, o_ref, *scratch,
                           nch, n_sel):
    del carry_ref
    _dsa_kernel(q_ref, qi_ref, misc_ref, lim_ref, k_ref, v_ref, ki_ref, o_ref, *scratch, nch=nch, n_sel=n_sel)


SUB = 16


def _dot(a, b):
    return jnp.dot(a, b, preferred_element_type=F32)


def _dot_nt(a, b):
    return lax.dot_general(a, b, (((1,), (1,)), ((), ())), preferred_element_type=F32)


def _dot_tn(a, b):
    return lax.dot_general(a, b, (((0,), (0,)), ((), ())), preferred_element_type=F32)


def _split2(x):
    hi = x.astype(BF16)
    return hi, (x - hi.astype(F32)).astype(BF16)


def _split3(x):
    hi = x.astype(BF16)
    r = x - hi.astype(F32)
    mid = r.astype(BF16)
    return hi, mid, (r - mid.astype(F32)).astype(BF16)


def _cumsum_rows(x, tril16):
    hi, mid, lo = _split3(x)
    return _dot(tril16, hi) + _dot(tril16, mid) + _dot(tril16, lo)


def _dot_f32(a, b):
    ah, al = _split2(a)
    bh, bl = _split2(b)
    return _dot(ah, bh) + (_dot(ah, bl) + _dot(al, bh))


def _tri_mask(n, strict=False):
    r = lax.broadcasted_iota(jnp.int32, (n, n), 0)
    c = lax.broadcasted_iota(jnp.int32, (n, n), 1)
    return r > c if strict else r >= c


def _rows_to_lanes(x):
    rows = x.shape[0]
    if rows < LANE:
        x = jnp.concatenate([x, jnp.zeros((LANE - rows, LANE), x.dtype)], axis=0)
    return x.T


def _chunk_call(kern, *, bsz, nc, rows, ins, outs, scratch, name):
    def spec(a, kind, width, offset):
        if kind == 'rows':
            return pl.BlockSpec((rows, width), lambda b, c: (b * nc + c, offset // width))
        if kind == 'batch':
            return pl.BlockSpec((1,) + tuple(a.shape[1:]), lambda b, c: (b,) + (0,) * (len(a.shape) - 1))
        return pl.BlockSpec(tuple(a.shape), lambda b, c: (0,) * len(a.shape))

    return pl.pallas_call(
        kern,
        grid=(bsz, nc),
        in_specs=[spec(*i) for i in ins],
        out_specs=[spec(*o) for o in outs],
        out_shape=[o[0] for o in outs],
        scratch_shapes=scratch,
        compiler_params=pltpu.CompilerParams(
            dimension_semantics=("parallel", "arbitrary"), vmem_limit_bytes=VMEM_LIMIT),
        name=name,
    )(*[i[0] for i in ins])


def _mlstm_kernel(q_ref, k_ref, v_ref, og_ref, misc_ref, gb_ref, gain_ref, c0_ref, n0_ref, m0_ref,
                  h_ref, c_out_ref, n_out_ref, m_out_ref, c_scr, n_scr, m_scr):
    ci = pl.program_id(1)

    @pl.when(ci == 0)
    def _():
        c_scr[...] = c0_ref[0]
        n_scr[...] = n0_ref[0]
        m_scr[...] = m0_ref[0]

    rows = q_ref.shape[0]
    hr = B_HEADS * rows
    wq = B_HEADS * B_QK_DIM
    tril16 = jnp.where(_tri_mask(rows), 1.0, 0.0).astype(BF16)
    gates = misc_ref[...] + gb_ref[...]
    bcum = _cumsum_rows(jax.nn.log_sigmoid(gates), tril16)
    m_all = m_scr[...]

    def stack(f):
        return jnp.concatenate([f(h) for h in range(B_HEADS)], axis=0)

    lane_q = lax.broadcasted_iota(jnp.int32, (rows, wq), 1)
    qx, kx = q_ref[...], k_ref[...]
    q_all = stack(lambda h: jnp.where(lane_q // B_QK_DIM == h, qx, 0.0))
    k_all = stack(lambda h: jnp.where(lane_q // B_QK_DIM == h, kx, 0.0)) * B_QK_DIM ** -0.5
    v_all = stack(lambda h: v_ref[:, h * B_V_DIM:(h + 1) * B_V_DIM])
    b_col = stack(lambda h: bcum[:, MISC_FB + h:MISC_FB + h + 1])
    i_col = stack(lambda h: gates[:, MISC_IB + h:MISC_IB + h + 1])
    m_col = stack(lambda h: jnp.broadcast_to(m_all[:, h:h + 1], (rows, 1)))
    b_end = stack(lambda h: jnp.broadcast_to(bcum[rows - 1:rows, MISC_FB + h:MISC_FB + h + 1], (rows, 1)))
    b_row = jnp.broadcast_to(b_col, (hr, LANE)).T[0:1, :]
    i_row = jnp.broadcast_to(i_col, (hr, LANE)).T[0:1, :]

    r = lax.broadcasted_iota(jnp.int32, (hr, hr), 0)
    c = lax.broadcasted_iota(jnp.int32, (hr, hr), 1)
    incl = ((r // rows) == (c // rows)) & (r >= c)
    dmat = jnp.where(incl, b_col - b_row + i_row, -jnp.inf)
    inter = b_col + m_col
    mrow = jnp.maximum(inter, jnp.max(dmat, axis=1, keepdims=True))
    w_state = jnp.exp(inter - mrow)
    q16 = q_all.astype(BF16)
    v16 = v_all.astype(BF16)
    scores = _dot_nt(q16, k_all.astype(BF16)) * jnp.exp(dmat - mrow)
    cs = c_scr[...]
    n_row = n_scr[...]
    num = _dot(scores.astype(BF16), v16) + w_state * _dot(q16, cs.astype(BF16))
    den = jnp.sum(scores, axis=1, keepdims=True) + w_state * jnp.sum(q_all * n_row, axis=1, keepdims=True)
    hh = num / jnp.maximum(jnp.abs(den), jnp.exp(-mrow))
    gain = gain_ref[...]
    for h in range(B_HEADS):
        h_ref[:, h * B_V_DIM:(h + 1) * B_V_DIM] = (_rms_rows(hh[h * rows:(h + 1) * rows], gain)
                                                   * jax.nn.sigmoid(og_ref[:, h * B_V_DIM:(h + 1) * B_V_DIM]))

    g_col = b_end - b_col + i_col
    lane1 = lax.broadcasted_iota(jnp.int32, (1, LANE), 1)
    m_next = m_all
    m_new_rows, keep_rows, keep_lanes = [], [], []
    for h in range(B_HEADS):
        m_h = m_all[:, h:h + 1]
        be = bcum[rows - 1:rows, MISC_FB + h:MISC_FB + h + 1]
        m_new = jnp.maximum(be + m_h, jnp.max(g_col[h * rows:(h + 1) * rows], axis=0, keepdims=True))
        keep = jnp.exp(be + m_h - m_new)
        m_next = jnp.where(lane1 == h, m_new, m_next)
        m_new_rows.append(jnp.broadcast_to(m_new, (rows, 1)))
        keep_rows.append(jnp.broadcast_to(keep, (B_QK_DIM, 1)))
        keep_lanes.append(jnp.broadcast_to(keep, (1, B_QK_DIM)))
    kw = k_all * jnp.exp(g_col - jnp.concatenate(m_new_rows, axis=0))
    c_scr[...] = jnp.concatenate(keep_rows, axis=0) * cs + _dot_tn(kw.astype(BF16), v16)
    n_scr[...] = jnp.concatenate(keep_lanes, axis=1) * n_row + jnp.sum(kw, axis=0, keepdims=True)
    m_scr[...] = m_next

    @pl.when(ci == pl.num_programs(1) - 1)
    def _():
        c_out_ref[0] = c_scr[...]
        n_out_ref[0] = n_scr[...]
        m_out_ref[0] = m_scr[...]


def mlstm_mixer(proj, gate_bias, gain, c0, n0, m0, *, bsz, t, chunk):
    nc = t // chunk
    gb = jnp.zeros((1, LANE), F32)
    gb = gb.at[0, MISC_IB:MISC_IB + B_HEADS].set(gate_bias[0]).at[0, MISC_FB:MISC_FB + B_HEADS].set(gate_bias[1])
    c0 = c0.reshape(bsz, B_HEADS * B_QK_DIM, B_V_DIM)
    n0 = n0.reshape(bsz, 1, B_HEADS * B_QK_DIM)
    m0 = jnp.pad(m0, ((0, 0), (0, LANE - B_HEADS))).reshape(bsz, 1, LANE)
    wq = B_HEADS * B_QK_DIM
    wv = B_HEADS * B_V_DIM
    h, c, n, m = _chunk_call(
        _mlstm_kernel, bsz=bsz, nc=nc, rows=chunk,
        ins=[(proj, 'rows', wq, AB_QB), (proj, 'rows', wq, AB_KB), (proj, 'rows', wv, AB_VB),
             (proj, 'rows', wv, AB_OB), (proj, 'rows', LANE, AB_MISC), (gb, 'const', 0, 0),
             (gain.reshape(1, B_V_DIM), 'const', 0, 0), (c0, 'batch', 0, 0), (n0, 'batch', 0, 0),
             (m0, 'batch', 0, 0)],
        outs=[(jax.ShapeDtypeStruct((bsz * t, wv), F32), 'rows', wv, 0),
              (jax.ShapeDtypeStruct(c0.shape, F32), 'batch', 0, 0),
              (jax.ShapeDtypeStruct(n0.shape, F32), 'batch', 0, 0),
              (jax.ShapeDtypeStruct(m0.shape, F32), 'batch', 0, 0)],
        scratch=[pltpu.VMEM((wq, B_V_DIM), F32), pltpu.VMEM((1, wq), F32), pltpu.VMEM((1, LANE), F32)],
        name="mlstm_mixer")
    return (h, c.reshape(bsz, B_HEADS, B_QK_DIM, B_V_DIM), n.reshape(bsz, B_HEADS, B_QK_DIM),
            m.reshape(bsz, LANE)[:, :B_HEADS])


CD_QKV, CD_ZC, CD_QD, CD_FD, CD_VD, CD_GD, CD_MISC, CD_TOTAL = 0, 1536, 2048, 2560, 3072, 3584, 4096, 4224
MISC_BC, MISC_AC = 0, 4
TAIL = 8


def _permute_w_in_cd(w):
    qkv, bc, ac, zc, qd, fd, vd, gd = _split_cols(w, ODD_SPLITS)
    pad = jnp.zeros((w.shape[0], LANE - 2 * C_HEADS), w.dtype)
    return jnp.concatenate([qkv, zc, qd, fd, vd, gd, bc, ac, pad], axis=1)


def _gdn_kernel(qkv_ref, z_ref, misc_ref, cw_ref, alog_ref, dt_ref, gain_ref, s0_ref, tail0_ref,
                o_ref, s_out_ref, st_scr, tail_scr):
    ci = pl.program_id(1)

    @pl.when(ci == 0)
    def _():
        for h in range(C_HEADS):
            st_scr[:, h * C_DIM:(h + 1) * C_DIM] = s0_ref[0, h].T
        tail_scr[...] = tail0_ref[0]

    rows = qkv_ref.shape[0]
    width = qkv_ref.shape[1]
    x = qkv_ref[...]
    tail = tail_scr[...]
    row8 = lax.broadcasted_iota(jnp.int32, (TAIL, width), 0)
    acc = x * cw_ref[CONV_W - 1:CONV_W, :]
    for back in range(1, CONV_W):
        rolled = pltpu.roll(x, back, 0)
        first = jnp.where(row8 < back, pltpu.roll(tail, back, 0), rolled[0:TAIL])
        shifted = first if rows == TAIL else jnp.concatenate([first, rolled[TAIL:]], axis=0)
        acc = acc + shifted * cw_ref[CONV_W - 1 - back:CONV_W - back, :]
    tail_scr[...] = x[rows - TAIL:rows]
    conv = acc * jax.nn.sigmoid(acc)

    tril16 = jnp.where(_tri_mask(rows), 1.0, 0.0).astype(BF16)
    misc = misc_ref[...]
    beta_t = jax.nn.sigmoid(misc)
    g_t = -jnp.exp(alog_ref[...]) * jax.nn.softplus(misc + dt_ref[...])
    gcum = _cumsum_rows(g_t, tril16)

    hd = C_HEADS * C_DIM
    hr = C_HEADS * rows

    def stack(f):
        return jnp.concatenate([f(h) for h in range(C_HEADS)], axis=0)

    def l2n(v):
        return v * lax.rsqrt(jnp.sum(v * v, axis=-1, keepdims=True) + EPS)

    q_all = stack(lambda h: l2n(conv[:, h * C_DIM:(h + 1) * C_DIM])) * C_DIM ** -0.5
    k_all = stack(lambda h: l2n(conv[:, hd + h * C_DIM:hd + (h + 1) * C_DIM]))
    v_all = stack(lambda h: conv[:, 2 * hd + h * C_DIM:2 * hd + (h + 1) * C_DIM])
    beta = stack(lambda h: beta_t[:, MISC_BC + h:MISC_BC + h + 1])
    gc = stack(lambda h: gcum[:, MISC_AC + h:MISC_AC + h + 1])
    g_end = stack(lambda h: jnp.broadcast_to(gcum[rows - 1:rows, MISC_AC + h:MISC_AC + h + 1], (rows, 1)))
    gc_row = jnp.broadcast_to(gc, (hr, LANE)).T[0:1, :]

    r = lax.broadcasted_iota(jnp.int32, (hr, hr), 0)
    c = lax.broadcasted_iota(jnp.int32, (hr, hr), 1)
    same = (r // rows) == (c // rows)
    incl = same & (r >= c)
    strict = same & (r > c)
    decay = jnp.exp(jnp.where(incl, gc - gc_row, -jnp.inf))
    k16 = k_all.astype(BF16)
    a_mat = jnp.where(strict, beta * _dot_nt(k16, k16) * decay, 0.0)
    power = -a_mat
    inv = jnp.where(r == c, 1.0, 0.0) + power
    for _ in range(int(math.log2(rows)) - 1):
        power = _dot_f32(power, power)
        inv = inv + _dot_f32(inv, power)
    inv_hi, inv_lo = _split2(inv)
    rhs = jnp.concatenate([beta * v_all, beta * jnp.exp(gc) * k_all], axis=1).astype(BF16)
    w = _dot(inv_hi, rhs) + _dot(inv_lo, rhs)
    w_v, w_k = w[:, 0:C_DIM], w[:, C_DIM:2 * C_DIM]
    qk = _dot_nt(q_all.astype(BF16), k16) * decay

    head_of_row = lax.broadcasted_iota(jnp.int32, (hr, C_DIM), 0) // rows

    def per_head_lanes(m):
        return jnp.concatenate([jnp.where(head_of_row == h, m, 0.0) for h in range(C_HEADS)], axis=1).astype(BF16)

    st = st_scr[...]
    st16 = st.astype(BF16)
    delta = w_v - _dot_nt(per_head_lanes(w_k), st16)
    d16 = delta.astype(BF16)
    out = _dot_nt(per_head_lanes(q_all * jnp.exp(gc)), st16) + _dot(qk.astype(BF16), d16)
    keep = jnp.concatenate([jnp.broadcast_to(jnp.exp(gcum[rows - 1:rows, MISC_AC + h:MISC_AC + h + 1]), (1, C_DIM))
                            for h in range(C_HEADS)], axis=1)
    st_scr[...] = keep * st + _dot_tn(d16, per_head_lanes(k_all * jnp.exp(g_end - gc)))
    gain = gain_ref[...]
    for h in range(C_HEADS):
        z = z_ref[:, h * C_DIM:(h + 1) * C_DIM]
        o_ref[:, h * C_DIM:(h + 1) * C_DIM] = _rms_rows(out[h * rows:(h + 1) * rows], gain) * (z * jax.nn.sigmoid(z))

    @pl.when(ci == pl.num_programs(1) - 1)
    def _():
        for h in range(C_HEADS):
            s_out_ref[0, h] = st_scr[:, h * C_DIM:(h + 1) * C_DIM].T


def gdn_mixer(proj, conv_w, a_log, dt_bias, gain, s0, conv_prev, *, bsz, t, chunk):
    nc = t // chunk
    hd = C_HEADS * C_DIM
    lanes = jnp.zeros((1, LANE), F32)
    alog = lanes.at[0, MISC_AC:MISC_AC + C_HEADS].set(a_log)
    dt = lanes.at[0, MISC_AC:MISC_AC + C_HEADS].set(dt_bias)
    tail0 = jnp.pad(conv_prev, ((0, 0), (TAIL - (CONV_W - 1), 0), (0, 0)))
    out, s = _chunk_call(
        _gdn_kernel, bsz=bsz, nc=nc, rows=chunk,
        ins=[(proj, 'rows', 3 * hd, CD_QKV), (proj, 'rows', hd, CD_ZC), (proj, 'rows', LANE, CD_MISC),
             (conv_w, 'const', 0, 0), (alog, 'const', 0, 0), (dt, 'const', 0, 0),
             (gain.reshape(1, C_DIM), 'const', 0, 0), (s0, 'batch', 0, 0), (tail0, 'batch', 0, 0)],
        outs=[(jax.ShapeDtypeStruct((bsz * t, hd), F32), 'rows', hd, 0),
              (jax.ShapeDtypeStruct(s0.shape, F32), 'batch', 0, 0)],
        scratch=[pltpu.VMEM((C_DIM, hd), F32), pltpu.VMEM((TAIL, 3 * hd), F32)],
        name="gdn_mixer")
    return out, s


def _hgrn2_kernel(q_ref, f_ref, v_ref, g_ref, lb_ref, gain_ref, s0_ref, o_ref, s_out_ref, st_scr):
    ci = pl.program_id(1)

    @pl.when(ci == 0)
    def _():
        for h in range(D_HEADS):
            st_scr[h] = s0_ref[0, h].T

    rows = q_ref.shape[0]
    tril16 = jnp.where(_tri_mask(rows), 1.0, 0.0).astype(BF16)
    lb = lb_ref[...]
    zf = f_ref[...]
    logf = jnp.logaddexp(jnp.log(lb), jnp.log1p(-lb) + jax.nn.log_sigmoid(zf))
    kd = (1.0 - lb) * jax.nn.sigmoid(-zf)
    qx = q_ref[...]
    qd = qx * jax.nn.sigmoid(qx)
    bcum = _cumsum_rows(logf, tril16)
    gain = gain_ref[...]
    row_sub = lax.broadcasted_iota(jnp.int32, (SUB, 1), 0)
    for h in range(D_HEADS):
        sl = slice(h * D_EXPAND, (h + 1) * D_EXPAND)
        q, k, b = qd[:, sl], kd[:, sl], bcum[:, sl]
        v = v_ref[:, h * D_V_DIM:(h + 1) * D_V_DIM]
        v16 = v.astype(BF16)
        st = st_scr[h]
        inter = _dot_nt((q * jnp.exp(b)).astype(BF16), st.astype(BF16))
        blocks = []
        for i in range(rows // SUB):
            r0 = i * SUB
            qi, bi = q[r0:r0 + SUB], b[r0:r0 + SUB]
            oi = inter[r0:r0 + SUB]
            if i > 0:
                ref = b[r0 - 1:r0]
                att = _dot_nt((qi * jnp.exp(bi - ref)).astype(BF16),
                              (k[0:r0] * jnp.exp(ref - b[0:r0])).astype(BF16))
                oi = oi + _dot(att.astype(BF16), v16[0:r0])
            for s in range(SUB):
                r = r0 + s
                a = jnp.sum(qi * jnp.exp(bi - b[r:r + 1]) * k[r:r + 1], axis=1, keepdims=True)
                oi = oi + jnp.where(row_sub >= s, a, 0.0) * v[r:r + 1]
            blocks.append(oi)
        out = blocks[0] if len(blocks) == 1 else jnp.concatenate(blocks, axis=0)
        b_end = b[rows - 1:rows]
        st_scr[h] = jnp.exp(b_end) * st + _dot_tn(v16, (k * jnp.exp(b_end - b)).astype(BF16))
        g = g_ref[:, h * D_V_DIM:(h + 1) * D_V_DIM]
        o_ref[:, h * D_V_DIM:(h + 1) * D_V_DIM] = _rms_rows(out, gain) * (g * jax.nn.sigmoid(g))

    @pl.when(ci == pl.num_programs(1) - 1)
    def _():
        for h in range(D_HEADS):
            s_out_ref[0, h] = st_scr[h].T


def hgrn2_mixer(proj, lower_bound, gain, s0, *, bsz, t, chunk):
    nc = t // chunk
    wk = D_HEADS * D_EXPAND
    wv = D_HEADS * D_V_DIM
    out, s = _chunk_call(
        _hgrn2_kernel, bsz=bsz, nc=nc, rows=chunk,
        ins=[(proj, 'rows', wk, CD_QD), (proj, 'rows', wk, CD_FD), (proj, 'rows', wv, CD_VD),
             (proj, 'rows', wv, CD_GD), (lower_bound.reshape(1, wk), 'const', 0, 0),
             (gain.reshape(1, D_V_DIM), 'const', 0, 0), (s0, 'batch', 0, 0)],
        outs=[(jax.ShapeDtypeStruct((bsz * t, wv), F32), 'rows', wv, 0),
              (jax.ShapeDtypeStruct(s0.shape, F32), 'batch', 0, 0)],
        scratch=[pltpu.VMEM((D_HEADS, D_V_DIM, D_EXPAND), F32)],
        name="hgrn2_mixer")
    return out, s


def _pad_cols(w, mult=LANE):
    pad = (-w.shape[-1]) % mult
    return jnp.pad(w, [(0, 0)] * (w.ndim - 1) + [(0, pad)])


def _mixer_ab(proj, bsz, t, pos, prm, cache):
    n = bsz * t
    qpad, qipad, k16, v16, ki16, k32, v32, ki32 = dsa_prep(proj, pos, prm['a_q_gain'][0], prm['a_k_gain'][0], t)
    if cache is None:
        limit = jnp.tile((pos // CHUNK + 1) * CHUNK, bsz).reshape(n, 1)
        a_out = dsa_attention(qpad, qipad, proj, limit, k16, v16, ki16, bsz=bsz, tq=t, tk=t,
                              causal=True, n_sel=min(TOPK_MAX, t // 4))
        c0 = jnp.zeros((bsz, B_HEADS, B_QK_DIM, B_V_DIM), F32)
        n0 = jnp.zeros((bsz, B_HEADS, B_QK_DIM), F32)
        m0 = jnp.zeros((bsz, B_HEADS), F32)
        chunk = CHUNK
    else:
        k_c, v_c, ki_c, c0, n0, m0 = cache
        past = k_c.shape[1]
        n_keys = past + t
        tk = -(-n_keys // KEY_CHUNK) * KEY_CHUNK

        def with_cache(c, new):
            c = c.reshape(bsz, past, -1).astype(BF16)
            c = jnp.pad(c, ((0, 0), (0, 0), (0, LANE - c.shape[-1])))
            return jnp.concatenate([c, new.reshape(bsz, t, LANE),
                                    jnp.zeros((bsz, tk - n_keys, LANE), BF16)], axis=1).reshape(bsz * tk, LANE)

        limit = jnp.full((n, 1), n_keys, jnp.int32)
        a_out = dsa_attention(qpad, qipad, proj, limit, with_cache(k_c, k16), with_cache(v_c, v16),
                              with_cache(ki_c, ki16), bsz=bsz, tq=t, tk=tk, causal=False,
                              n_sel=min(TOPK_MAX, n_keys // 4))
        chunk = t
    h, c, n_, m = mlstm_mixer(proj, prm['b_gate_bias'][0], prm['b_norm_gain'][0], c0, n0, m0,
                              bsz=bsz, t=t, chunk=chunk)
    st = (k32.reshape(bsz, t, A_KV_HEADS, HEAD_DIM), v32.reshape(bsz, t, A_KV_HEADS, HEAD_DIM),
          ki32.reshape(bsz, t, IDX_DIM), c, n_, m)
    return a_out, h, st


def _mixer_cd(proj, bsz, t, prm, lower_bound, cache):
    hd = C_HEADS * C_DIM
    if cache is None:
        sc0 = jnp.zeros((bsz, C_HEADS, C_DIM, C_DIM), F32)
        conv_prev = jnp.zeros((bsz, CONV_W - 1, 3 * hd), F32)
        sd0 = jnp.zeros((bsz, D_HEADS, D_EXPAND, D_V_DIM), F32)
        chunk = CHUNK
    else:
        sc0, conv_prev, sd0 = cache
        chunk = t
    oc, sc = gdn_mixer(proj, prm['c_conv_w'][0], prm['c_a_log'][0], prm['c_dt_bias'][0], prm['c_norm_gain'][0],
                       sc0, conv_prev, bsz=bsz, t=t, chunk=chunk)
    od, sd = hgrn2_mixer(proj, lower_bound, prm['d_norm_gain'][0], sd0, bsz=bsz, t=t, chunk=chunk)
    qkv = proj.reshape(bsz, t, -1)[:, :, CD_QKV:CD_QKV + 3 * hd]
    conv_new = jnp.concatenate([conv_prev, qkv[:, t - (CONV_W - 1):]], axis=1)[:, -(CONV_W - 1):]
    return oc, od, (sc, conv_new, sd)


def _trunk(x, pos_offset, cache, prm, wts):
    bsz, t, d = x.shape
    n = bsz * t
    pos = pos_offset + jnp.arange(t, dtype=jnp.int32)
    probs = jax.nn.softmax(prm['d_lb_logits'], axis=0)
    lower_bounds = jnp.cumsum(probs, axis=0) - probs[0]
    xf = x.reshape(n, d)

    lc = None if cache is None else tuple(c[0] for c in cache[:6])
    proj = norm_matmul(xf, prm['norm_mix'][0], wts['w_in_ab'])
    a_out, b_out, st_even = _mixer_ab(proj, bsz, t, pos, prm, lc)
    xf = matmul_residual(a_out, b_out, wts['w_out_ab'], xf)
    xf = ffn_residual(xf, prm['norm_ffn'][0], wts['ffn_w1'], wts['ffn_w3'], wts['ffn_w2'])

    lc = None if cache is None else tuple(c[0] for c in cache[6:])
    proj = norm_matmul(xf, prm['norm_mix'][1], wts['w_in_cd'])
    c_out, d_out, st_odd = _mixer_cd(proj, bsz, t, prm, lower_bounds[1], lc)
    xf = matmul_residual(c_out, d_out, wts['w_out_cd'], xf)
    xf = moe_residual(xf, prm['norm_ffn'][1], wts['moe_router'], wts['moe_w1'], wts['moe_w3'], wts['moe_w2'])

    new_state = tuple(s[None] for s in st_even + st_odd)
    return xf.reshape(bsz, t, d), new_state


def kernel(x_prompt, x_sample, cache_a_k, cache_a_v, cache_a_kidx, state_b_c, state_b_n, state_b_m,
           state_c_s, state_c_conv, state_d_s, norm_mix, norm_ffn, w_in_ab, w_out_ab, a_q_gain, a_k_gain,
           b_gate_bias, b_norm_gain, w_in_cd, w_out_cd, c_conv_w, c_a_log, c_dt_bias, c_norm_gain,
           d_lb_logits, d_norm_gain, ffn_w1, ffn_w3, ffn_w2, moe_router, moe_w1, moe_w3, moe_w2):
    prm = dict(norm_mix=norm_mix, norm_ffn=norm_ffn, a_q_gain=a_q_gain, a_k_gain=a_k_gain,
               b_gate_bias=b_gate_bias, b_norm_gain=b_norm_gain, c_conv_w=c_conv_w, c_a_log=c_a_log,
               c_dt_bias=c_dt_bias, c_norm_gain=c_norm_gain, d_lb_logits=d_lb_logits, d_norm_gain=d_norm_gain)
    wts = dict(w_in_ab=_permute_w_in_ab(w_in_ab[0]).astype(BF16), w_out_ab=w_out_ab[0].astype(BF16),
               w_in_cd=_permute_w_in_cd(w_in_cd[0]).astype(BF16), w_out_cd=w_out_cd[0].astype(BF16),
               ffn_w1=ffn_w1[0].astype(BF16), ffn_w3=ffn_w3[0].astype(BF16), ffn_w2=ffn_w2[0].astype(BF16),
               moe_router=_pad_cols(moe_router[0]),
               moe_w1=moe_w1[0].astype(BF16), moe_w3=moe_w3[0].astype(BF16), moe_w2=moe_w2[0].astype(BF16))
    cache = (cache_a_k, cache_a_v, cache_a_kidx, state_b_c, state_b_n, state_b_m, state_c_s, state_c_conv, state_d_s)
    y_prompt, st_p = _trunk(x_prompt, 0, None, prm, wts)
    y_sample, st_s = _trunk(x_sample, cache_a_k.shape[2], cache, prm, wts)
    return (y_prompt, y_sample) + st_p + st_s
```

```python
import functools
import math

import jax
import jax.numpy as jnp
import numpy as np
from jax import lax
from jax.experimental import pallas as pl
from jax.experimental.pallas import tpu as pltpu

F32 = jnp.float32
BF16 = jnp.bfloat16

EPS = 1e-6
ROPE_THETA = 500000.0
ROT_FRACTION = 4
CHUNK = 64
A_HEADS, A_KV_HEADS, HEAD_DIM = 8, 2, 64
IDX_HEADS, IDX_DIM = 4, 64
TOPK_MAX, Q_BLOCK = 256, 128
B_HEADS, B_QK_DIM, B_V_DIM = 4, 64, 128
C_HEADS, C_DIM, CONV_W = 4, 128, 4
D_HEADS, D_EXPAND, D_V_DIM = 4, 128, 128
N_EXPERTS, TOP_K_EXPERTS = 8, 2

LANE = 128
VMEM_LIMIT = 48 * 1024 * 1024

EVEN_SPLITS = (A_HEADS * HEAD_DIM, A_KV_HEADS * HEAD_DIM, A_KV_HEADS * HEAD_DIM,
               IDX_HEADS * IDX_DIM, IDX_DIM, IDX_HEADS,
               B_HEADS * B_QK_DIM, B_HEADS * B_QK_DIM, B_HEADS * B_V_DIM,
               B_HEADS, B_HEADS, B_HEADS * B_V_DIM)
ODD_SPLITS = (3 * C_HEADS * C_DIM, C_HEADS, C_HEADS, C_HEADS * C_DIM,
              D_HEADS * D_EXPAND, D_HEADS * D_EXPAND, D_HEADS * D_V_DIM, D_HEADS * D_V_DIM)


def _split_cols(p, widths):
    cuts = [int(c) for c in np.cumsum(widths)[:-1]]
    return jnp.split(p, cuts, axis=-1)


def _row_tile(n, target):
    t = min(n, target)
    while n % t:
        t //= 2
    return t


def _col_tile(n, target):
    best = LANE
    for k in range(1, n // LANE + 1):
        c = k * LANE
        if n % c == 0 and c <= target:
            best = c
    return best


def _rms_rows(x, gain):
    return x * lax.rsqrt(jnp.mean(x * x, axis=-1, keepdims=True) + EPS) * gain


def _norm_matmul_kernel(x_ref, g_ref, w_ref, o_ref, xn_ref):
    @pl.when(pl.program_id(1) == 0)
    def _():
        xn_ref[...] = _rms_rows(x_ref[...], g_ref[...]).astype(BF16)

    o_ref[...] = jnp.dot(xn_ref[...], w_ref[...], preferred_element_type=F32)


def norm_matmul(x, gain, w):
    n, d = x.shape
    m = w.shape[1]
    tm = _row_tile(n, 1024)
    tn = _col_tile(m, 1536)
    return pl.pallas_call(
        _norm_matmul_kernel,
        grid=(n // tm, m // tn),
        in_specs=[pl.BlockSpec((tm, d), lambda i, j: (i, 0)),
                  pl.BlockSpec((1, d), lambda i, j: (0, 0)),
                  pl.BlockSpec((d, tn), lambda i, j: (0, j))],
        out_specs=pl.BlockSpec((tm, tn), lambda i, j: (i, j)),
        out_shape=jax.ShapeDtypeStruct((n, m), F32),
        scratch_shapes=[pltpu.VMEM((tm, d), BF16)],
        compiler_params=pltpu.CompilerParams(
            dimension_semantics=("parallel", "arbitrary"), vmem_limit_bytes=VMEM_LIMIT),
        name="norm_matmul",
    )(x, gain.reshape(1, d), w)


def _matmul_res_kernel(a_ref, b_ref, w_ref, r_ref, o_ref):
    ka = a_ref.shape[1]
    o_ref[...] = (r_ref[...] + jnp.dot(a_ref[...].astype(BF16), w_ref[0:ka, :], preferred_element_type=F32)
                  + jnp.dot(b_ref[...].astype(BF16), w_ref[ka:, :], preferred_element_type=F32))


def matmul_residual(a, b, w, res):
    n, ka = a.shape
    kb = b.shape[1]
    m = w.shape[1]
    tm = _row_tile(n, 1024)
    return pl.pallas_call(
        _matmul_res_kernel,
        grid=(n // tm,),
        in_specs=[pl.BlockSpec((tm, ka), lambda i: (i, 0)),
                  pl.BlockSpec((tm, kb), lambda i: (i, 0)),
                  pl.BlockSpec((ka + kb, m), lambda i: (0, 0)),
                  pl.BlockSpec((tm, m), lambda i: (i, 0))],
        out_specs=pl.BlockSpec((tm, m), lambda i: (i, 0)),
        out_shape=jax.ShapeDtypeStruct((n, m), F32),
        compiler_params=pltpu.CompilerParams(
            dimension_semantics=("parallel",), vmem_limit_bytes=VMEM_LIMIT),
        name="matmul_residual",
    )(a, b, w, res)


def _swiglu_tile(xn, w1, w3):
    h1 = jnp.dot(xn, w1, preferred_element_type=F32)
    h3 = jnp.dot(xn, w3, preferred_element_type=F32)
    return h1 * jax.nn.sigmoid(h1) * h3


def _ffn_kernel(x_ref, g_ref, w1_ref, w3_ref, w2_ref, o_ref, xn_ref):
    @pl.when(pl.program_id(1) == 0)
    def _():
        x = x_ref[...]
        xn_ref[...] = _rms_rows(x, g_ref[...]).astype(BF16)
        o_ref[...] = x

    act = _swiglu_tile(xn_ref[...], w1_ref[...], w3_ref[...])
    o_ref[...] += jnp.dot(act.astype(BF16), w2_ref[...], preferred_element_type=F32)


def ffn_residual(x, gain, w1, w3, w2):
    n, d = x.shape
    f = w1.shape[1]
    tm = _row_tile(n, 1024)
    tf = _col_tile(f, 512)
    return pl.pallas_call(
        _ffn_kernel,
        grid=(n // tm, f // tf),
        in_specs=[pl.BlockSpec((tm, d), lambda i, j: (i, 0)),
                  pl.BlockSpec((1, d), lambda i, j: (0, 0)),
                  pl.BlockSpec((d, tf), lambda i, j: (0, j)),
                  pl.BlockSpec((d, tf), lambda i, j: (0, j)),
                  pl.BlockSpec((tf, d), lambda i, j: (j, 0))],
        out_specs=pl.BlockSpec((tm, d), lambda i, j: (i, 0)),
        out_shape=jax.ShapeDtypeStruct((n, d), F32),
        scratch_shapes=[pltpu.VMEM((tm, d), BF16)],
        compiler_params=pltpu.CompilerParams(
            dimension_semantics=("parallel", "arbitrary"), vmem_limit_bytes=VMEM_LIMIT),
        name="ffn_residual",
    )(x, gain.reshape(1, d), w1, w3, w2)


MOE_VMEM_LIMIT = 58 * 1024 * 1024
MOE_CAPS = (256, 384, 512)


def _moe_route_kernel(x_ref, g_ref, r_ref, xn_ref, comb_ref, post_ref, cnt_ref):
    x = x_ref[...]
    tm = x.shape[0]
    xn = _rms_rows(x, g_ref[...])
    xn_ref[...] = xn.astype(BF16)
    logits = jnp.dot(xn, r_ref[...], preferred_element_type=F32, precision=lax.Precision.HIGHEST)
    lane = lax.broadcasted_iota(jnp.int32, logits.shape, 1)
    logits = jnp.where(lane < N_EXPERTS, logits, -jnp.inf)
    m1 = jnp.max(logits, axis=-1, keepdims=True)
    i1 = jnp.min(jnp.where(logits == m1, lane, LANE), axis=-1, keepdims=True)
    rest = jnp.where(lane == i1, -jnp.inf, logits)
    m2 = jnp.max(rest, axis=-1, keepdims=True)
    i2 = jnp.min(jnp.where(rest == m2, lane, LANE), axis=-1, keepdims=True)
    e2 = jnp.exp(m2 - m1)
    den = 1.0 + e2
    comb_ref[...] = jnp.where(lane == i1, 1.0 / den, 0.0) + jnp.where(lane == i2, e2 / den, 0.0)
    chosen = (lane == i1) | (lane == i2)
    sel = jnp.where(chosen, 1.0, 0.0)
    tril16 = jnp.where(_tri_mask(LANE), 1.0, 0.0).astype(BF16)
    seen = jnp.zeros((1, LANE), F32)
    ranks = []
    for blk in range(tm // LANE):
        sb = sel[blk * LANE:(blk + 1) * LANE]
        ranks.append(_dot(tril16, sb.astype(BF16)) + seen - 1.0)
        seen = seen + jnp.sum(sb, axis=0, keepdims=True)
    rank = jnp.where(chosen, jnp.concatenate(ranks, axis=0), -1.0)
    post_ref[0] = rank.T[0:N_EXPERTS, :]
    cnt_ref[0] = seen


def _moe_expert_kernel(cnt_ref, xn_ref, comb_ref, post_ref, x_ref, w1_ref, w3_ref, w2_ref, o_ref,
                       xe_scr, y_scr, *, caps):
    i = pl.program_id(0)
    e = pl.program_id(1)
    j = pl.program_id(2)
    last = pl.num_programs(2) - 1
    tm = x_ref.shape[0]

    @pl.when(jnp.logical_and(e == 0, j == 0))
    def _():
        o_ref[...] = x_ref[...]

    cnt = cnt_ref[i * N_EXPERTS + e]

    def expert_step(cap):
        def pick():
            rank_row = post_ref[0, pl.ds(e, 1), :].astype(jnp.int32)
            slot = lax.broadcasted_iota(jnp.int32, (cap, tm), 0)
            return jnp.where(rank_row == slot, 1.0, 0.0).astype(BF16)

        @pl.when(j == 0)
        def _():
            xe_scr[0:cap, :] = _dot(pick(), xn_ref[...]).astype(BF16)

        act = _swiglu_tile(xe_scr[0:cap, :], w1_ref[0], w3_ref[0])
        yj = _dot(act.astype(BF16), w2_ref[0])

        @pl.when(j == 0)
        def _():
            y_scr[0:cap, :] = yj

        @pl.when(j > 0)
        def _():
            y_scr[0:cap, :] += yj

        @pl.when(j == last)
        def _():
            comb = comb_ref[...]
            lane = lax.broadcasted_iota(jnp.int32, comb.shape, 1)
            gate = jnp.sum(jnp.where(lane == e, comb, 0.0), axis=-1, keepdims=True)
            hi, lo = _split2(y_scr[0:cap, :])
            p = pick()
            o_ref[...] += gate * (_dot_tn(p, hi) + _dot_tn(p, lo))

    lo = 0
    for cap in caps:
        @pl.when(jnp.logical_and(cnt > lo, cnt <= cap))
        def _(cap=cap):
            expert_step(cap)
        lo = cap


def moe_residual(x, gain, router, w1, w3, w2):
    n, d = x.shape
    ne, _, f = w1.shape
    tm = _row_tile(n, 1024)
    tf = _col_tile(f, 896)
    nt = n // tm
    xn, comb, post, cnt = pl.pallas_call(
        _moe_route_kernel,
        grid=(nt,),
        in_specs=[pl.BlockSpec((tm, d), lambda i: (i, 0)),
                  pl.BlockSpec((1, d), lambda i: (0, 0)),
                  pl.BlockSpec((d, LANE), lambda i: (0, 0))],
        out_specs=[pl.BlockSpec((tm, d), lambda i: (i, 0)),
                   pl.BlockSpec((tm, LANE), lambda i: (i, 0)),
                   pl.BlockSpec((1, ne, tm), lambda i: (i, 0, 0)),
                   pl.BlockSpec((1, 1, LANE), lambda i: (i, 0, 0))],
        out_shape=[jax.ShapeDtypeStruct((n, d), BF16), jax.ShapeDtypeStruct((n, LANE), F32),
                   jax.ShapeDtypeStruct((nt, ne, tm), F32), jax.ShapeDtypeStruct((nt, 1, LANE), F32)],
        compiler_params=pltpu.CompilerParams(dimension_semantics=("parallel",), vmem_limit_bytes=VMEM_LIMIT),
        name="moe_route",
    )(x, gain.reshape(1, d), router)
    counts = cnt[:, 0, :ne].astype(jnp.int32).reshape(nt * ne)
    caps = tuple(c for c in MOE_CAPS if c < tm) + (tm,)
    grid_spec = pltpu.PrefetchScalarGridSpec(
        num_scalar_prefetch=1,
        grid=(nt, ne, f // tf),
        in_specs=[pl.BlockSpec((tm, d), lambda i, e, j, c: (i, 0)),
                  pl.BlockSpec((tm, LANE), lambda i, e, j, c: (i, 0)),
                  pl.BlockSpec((1, ne, tm), lambda i, e, j, c: (i, 0, 0)),
                  pl.BlockSpec((tm, d), lambda i, e, j, c: (i, 0)),
                  pl.BlockSpec((1, d, tf), lambda i, e, j, c: (e, 0, j)),
                  pl.BlockSpec((1, d, tf), lambda i, e, j, c: (e, 0, j)),
                  pl.BlockSpec((1, tf, d), lambda i, e, j, c: (e, j, 0))],
        out_specs=pl.BlockSpec((tm, d), lambda i, e, j, c: (i, 0)),
        scratch_shapes=[pltpu.VMEM((tm, d), BF16), pltpu.VMEM((tm, d), F32)])
    return pl.pallas_call(
        functools.partial(_moe_expert_kernel, caps=caps),
        grid_spec=grid_spec,
        out_shape=jax.ShapeDtypeStruct((n, d), F32),
        compiler_params=pltpu.CompilerParams(
            dimension_semantics=("parallel", "arbitrary", "arbitrary"), vmem_limit_bytes=MOE_VMEM_LIMIT),
        name="moe_experts",
    )(counts, xn, comb, post, x, w1, w3, w2)


AB_QA, AB_VB, AB_OB, AB_QI, AB_QB, AB_KB, AB_KA, AB_VA, AB_MISC, AB_TOTAL = (
    0, 512, 1024, 1536, 1792, 2048, 2304, 2432, 2560, 2688)
MISC_WI, MISC_IB, MISC_FB = 64, 68, 72
HALF = LANE // 2
KEY_CHUNK = 512
MASKED = -1e30
KEY_OF_NEG_INF = -2139095041
I16_MIN, I16_MAX = -32768, 32767


def _permute_w_in_ab(w):
    qa, ka, va, qi, ki, wi, qb, kb, vb, ib, fb, ob = _split_cols(w, EVEN_SPLITS)
    pad = jnp.zeros((w.shape[0], LANE - IDX_DIM - 3 * IDX_HEADS), w.dtype)
    return jnp.concatenate([qa, vb, ob, qi, qb, kb, ka, va, ki, wi, ib, fb, pad], axis=1)


def _rope_tables(pos):
    rot = HEAD_DIM // ROT_FRACTION
    half = rot // 2
    inv_freq = ROPE_THETA ** (-jnp.arange(half, dtype=F32) * 2.0 / rot)
    ang = pos.astype(F32)[:, None] * inv_freq[None, :]
    cos, sin = jnp.cos(ang), jnp.sin(ang)
    t = pos.shape[0]
    one = jnp.ones((t, HEAD_DIM - rot), F32)
    zero_r = jnp.zeros((t, HEAD_DIM - rot), F32)
    zero_h = jnp.zeros((t, half), F32)
    c = jnp.concatenate([cos, cos, one], axis=1)
    s_up = jnp.concatenate([-sin, zero_h, zero_r], axis=1)
    s_dn = jnp.concatenate([zero_h, sin, zero_r], axis=1)
    return tuple(jnp.concatenate([a, a], axis=1) for a in (c, s_up, s_dn))


def _rope_tile(x, c, s_up, s_dn):
    half = HEAD_DIM // ROT_FRACTION // 2
    return x * c + pltpu.roll(x, LANE - half, 1) * s_up + pltpu.roll(x, half, 1) * s_dn


def _head_norm_tile(x, gain, same_head):
    sq = x * x
    hi = sq.astype(BF16)
    lo = (sq - hi.astype(F32)).astype(BF16)
    ss = (jnp.dot(hi, same_head, preferred_element_type=F32)
          + jnp.dot(lo, same_head, preferred_element_type=F32))
    return x * lax.rsqrt(ss * (1.0 / HEAD_DIM) + EPS) * gain


def _aprep_kernel(qa_ref, ka_ref, va_ref, qi_ref, misc_ref, c_ref, su_ref, sd_ref, qg_ref, kg_ref,
                  qpad_ref, qipad_ref, k16_ref, v16_ref, ki16_ref, k32_ref, v32_ref, ki32_ref):
    c, su, sd = c_ref[...], su_ref[...], sd_ref[...]
    tm = c.shape[0]
    row = lax.broadcasted_iota(jnp.int32, (LANE, LANE), 0)
    col = lax.broadcasted_iota(jnp.int32, (LANE, LANE), 1)
    same_head = jnp.where(row // HALF == col // HALF, 1.0, 0.0).astype(BF16)
    lane = lax.broadcasted_iota(jnp.int32, (tm, LANE), 1)
    low = lane < HALF

    heads_per_group = A_HEADS // A_KV_HEADS
    for p in range(A_HEADS // 2):
        y = _rope_tile(_head_norm_tile(qa_ref[:, p * LANE:(p + 1) * LANE], qg_ref[...], same_head), c, su, sd)
        y = y * HEAD_DIM ** -0.5
        y_sw = pltpu.roll(y, HALF, 1)
        for o in range(2):
            h = 2 * p + o
            g = h // heads_per_group
            src = y if o == g else y_sw
            qpad_ref[:, h * LANE:(h + 1) * LANE] = jnp.where(low if g == 0 else ~low, src, 0.0).astype(BF16)
    k = _rope_tile(_head_norm_tile(ka_ref[...], kg_ref[...], same_head), c, su, sd)
    k32_ref[...] = k
    k16_ref[...] = k.astype(BF16)
    v = va_ref[...]
    v32_ref[...] = v
    v16_ref[...] = v.astype(BF16)
    for p in range(IDX_HEADS // 2):
        y = _rope_tile(qi_ref[:, p * LANE:(p + 1) * LANE], c, su, sd)
        y_sw = pltpu.roll(y, HALF, 1)
        qipad_ref[:, (2 * p) * LANE:(2 * p + 1) * LANE] = jnp.where(low, y, 0.0).astype(BF16)
        qipad_ref[:, (2 * p + 1) * LANE:(2 * p + 2) * LANE] = jnp.where(low, y_sw, 0.0).astype(BF16)
    ki = _rope_tile(misc_ref[...], c, su, sd)
    ki32_ref[...] = ki[:, :IDX_DIM]
    ki16_ref[...] = jnp.where(low, ki, 0.0).astype(BF16)


def dsa_prep(proj, pos, q_gain, k_gain, t):
    n = proj.shape[0]
    tm = _row_tile(n, 512)
    tabs = _rope_tables(pos)
    if t < tm:
        tabs = tuple(jnp.tile(a, (tm // t, 1)) for a in tabs)
    nt = tabs[0].shape[0] // tm
    tab_spec = pl.BlockSpec((tm, LANE), lambda i: (i % nt, 0))
    gain_spec = pl.BlockSpec((1, LANE), lambda i: (0, 0))

    def col(width, offset):
        return pl.BlockSpec((tm, width), lambda i: (i, offset // width))

    def out(width, dtype):
        return (jax.ShapeDtypeStruct((n, width), dtype), pl.BlockSpec((tm, width), lambda i: (i, 0)))

    outs = [out(A_HEADS * LANE, BF16), out(IDX_HEADS * LANE, BF16), out(LANE, BF16), out(LANE, BF16),
            out(LANE, BF16), out(LANE, F32), out(LANE, F32), out(IDX_DIM, F32)]
    return pl.pallas_call(
        _aprep_kernel,
        grid=(n // tm,),
        in_specs=[col(A_HEADS * HEAD_DIM, AB_QA), col(LANE, AB_KA), col(LANE, AB_VA),
                  col(IDX_HEADS * IDX_DIM, AB_QI), col(LANE, AB_MISC), tab_spec, tab_spec, tab_spec,
                  gain_spec, gain_spec],
        out_specs=[o[1] for o in outs],
        out_shape=[o[0] for o in outs],
        compiler_params=pltpu.CompilerParams(dimension_semantics=("parallel",), vmem_limit_bytes=VMEM_LIMIT),
        name="dsa_prep",
    )(proj, proj, proj, proj, proj, *tabs, jnp.tile(q_gain, 2).reshape(1, LANE), jnp.tile(k_gain, 2).reshape(1, LANE))


N_PARTIAL = 4


def _add_tiles(accs, m, sub):
    accs = list(accs)
    for t in range(m.shape[0] // sub):
        accs[t % len(accs)] = accs[t % len(accs)] + m[t * sub:(t + 1) * sub]
    return tuple(accs)


def _dsa_kernel(q_ref, qi_ref, misc_ref, lim_ref, k_ref, v_ref, ki_ref, o_ref, key_ref, bias_ref, hi_ref, lo_ref,
                *, nch, n_sel):
    qb = q_ref.shape[0]
    kc = KEY_CHUNK
    n_idx = IDX_HEADS
    hpg = A_HEADS // A_KV_HEADS
    nt = (((1,), (1,)), ((), ()))

    limit = lim_ref[0]
    misc_t = misc_ref[...].T
    wscale = IDX_HEADS ** -0.5 * IDX_DIM ** -0.5
    w = [misc_t[MISC_WI + j:MISC_WI + j + 1, :] * wscale for j in range(n_idx)]
    qis = [qi_ref[:, j * LANE:(j + 1) * LANE] for j in range(n_idx)]

    def score_body(c, carry):
        off = pl.multiple_of(c * kc, kc)
        kic = ki_ref[pl.ds(off, kc), :]
        s = None
        for j in range(n_idx):
            lg = jnp.maximum(lax.dot_general(kic, qis[j], nt, preferred_element_type=F32), 0.0)
            s = w[j] * lg if s is None else s + w[j] * lg
        kidx = off + lax.broadcasted_iota(jnp.int32, (kc, qb), 0)
        s = jnp.where(kidx < limit, s, -jnp.inf)
        bits = lax.bitcast_convert_type(s, jnp.int32)
        key = jnp.where(bits < 0, bits ^ 0x7FFFFFFF, bits)
        key_ref[pl.ds(off, kc), :] = key
        hi_ref[pl.ds(off, kc), :] = jnp.right_shift(key, 16).astype(jnp.int16)
        return carry

    lax.fori_loop(0, nch, score_body, 0)

    def count_ge(cand):
        sub = 8

        def body(c, accs):
            off = pl.multiple_of(c * kc, kc)
            m = jnp.where(key_ref[pl.ds(off, kc), :] >= cand, 1.0, 0.0)
            return _add_tiles(accs, m, sub)

        accs = lax.fori_loop(0, nch, body, (jnp.zeros((sub, qb), F32),) * N_PARTIAL, unroll=True)
        return jnp.sum(sum(accs), axis=0, keepdims=True)

    def count_ge16(ref, cand32):
        cand = cand32.astype(jnp.int16)
        sub = 16

        def body(c, accs):
            off = pl.multiple_of(c * kc, kc)
            m = jnp.where(ref[pl.ds(off, kc), :] >= cand, jnp.int16(1), jnp.int16(0))
            return _add_tiles(accs, m, sub)

        accs = lax.fori_loop(0, nch, body, (jnp.zeros((sub, qb), jnp.int16),) * N_PARTIAL, unroll=True)
        return jnp.sum(sum(accs).astype(F32), axis=0, keepdims=True)

    def kth_largest16(ref, want):
        tau = jnp.where(count_ge16(ref, jnp.zeros((1, qb), jnp.int32)) >= want, 0, I16_MIN).astype(jnp.int32)

        def bisect(i, tau):
            cand = tau | jnp.left_shift(jnp.int32(1), 14 - i)
            return jnp.where(count_ge16(ref, cand) >= want, cand, tau)

        return lax.fori_loop(0, 15, bisect, tau)

    want = float(n_sel)
    tau_hi = kth_largest16(hi_ref, want)
    above = jnp.where(tau_hi < I16_MAX, count_ge16(hi_ref, jnp.minimum(tau_hi + 1, I16_MAX)), 0.0)

    def low_body(c, carry):
        off = pl.multiple_of(c * kc, kc)
        key = key_ref[pl.ds(off, kc), :]
        low = (key & 0xFFFF) + I16_MIN
        lo_ref[pl.ds(off, kc), :] = jnp.where(jnp.right_shift(key, 16) == tau_hi, low, I16_MIN).astype(jnp.int16)
        return carry

    lax.fori_loop(0, nch, low_body, 0)
    tau_lo = kth_largest16(lo_ref, want - above)
    tau = jnp.left_shift(tau_hi, 16) + (tau_lo - I16_MIN)

    room = want - count_ge(tau + 1)
    r_i = lax.broadcasted_iota(jnp.int32, (LANE, LANE), 0)
    c_i = lax.broadcasted_iota(jnp.int32, (LANE, LANE), 1)
    prefix_ones = jnp.where(r_i >= c_i, 1.0, 0.0).astype(BF16)
    identity = jnp.where(r_i == c_i, 1.0, 0.0).astype(BF16)

    def bias_body(c, seen):
        off = pl.multiple_of(c * kc, kc)
        tiles = range(kc // LANE)
        xs = [key_ref[pl.ds(off + t * LANE, LANE), :] for t in tiles]
        eqs = [x == tau for x in xs]
        eqfs = [jnp.where(eq, 1.0, 0.0) for eq in eqs]
        ranks = [jnp.dot(prefix_ones, eqf.astype(BF16), preferred_element_type=F32) for eqf in eqfs]
        sels = []
        for t in tiles:
            sel = ((xs[t] > tau) | (eqs[t] & (ranks[t] + seen <= room))) & (xs[t] != KEY_OF_NEG_INF)
            sels.append(jnp.where(sel, 1.0, 0.0).astype(BF16))
            seen = seen + jnp.sum(eqfs[t], axis=0, keepdims=True)
        sel_ts = [lax.dot_general(sel, identity, (((0,), (0,)), ((), ())), preferred_element_type=F32)
                  for sel in sels]
        for t in tiles:
            bias_ref[:, pl.ds(off + t * LANE, LANE)] = jnp.where(sel_ts[t] > 0.5, 0.0, MASKED)
        return seen

    lax.fori_loop(0, nch, bias_body, jnp.zeros((1, qb), F32))

    lane = lax.broadcasted_iota(jnp.int32, (qb, LANE), 1)
    qgs = [jnp.concatenate([q_ref[:, (hpg * g + h) * LANE:(hpg * g + h + 1) * LANE] for h in range(hpg)], axis=0)
           for g in range(A_KV_HEADS)]

    def att_body(c, carry):
        off = pl.multiple_of(c * kc, kc)
        kch = k_ref[pl.ds(off, kc), :]
        vch = v_ref[pl.ds(off, kc), :]
        bias = bias_ref[:, pl.ds(off, kc)][None]
        new = []
        for g in range(A_KV_HEADS):
            m, l, acc = carry[g]
            s = lax.dot_general(qgs[g], kch, nt, preferred_element_type=F32)
            s = (s.reshape(hpg, qb, kc) + bias).reshape(hpg * qb, kc)
            m_new = jnp.maximum(m, jnp.max(s, axis=1, keepdims=True))
            alpha = jnp.exp(m - m_new)
            p = jnp.exp(s - m_new)
            l = alpha * l + jnp.sum(p, axis=1, keepdims=True)
            acc = alpha * acc + jnp.dot(p.astype(BF16), vch, preferred_element_type=F32)
            new.append((m_new, l, acc))
        return tuple(new)

    init = tuple((jnp.full((hpg * qb, 1), MASKED, F32), jnp.zeros((hpg * qb, 1), F32),
                  jnp.zeros((hpg * qb, LANE), F32)) for _ in range(A_KV_HEADS))
    res = lax.fori_loop(0, nch, att_body, init)
    outs = []
    for g in range(A_KV_HEADS):
        _, l, acc = res[g]
        og = acc / l
        for h in range(hpg):
            oh = og[h * qb:(h + 1) * qb]
            outs.append(oh if (h % 2) == g else pltpu.roll(oh, HALF, 1))
    for p in range(A_HEADS // 2):
        o_ref[:, p * LANE:(p + 1) * LANE] = jnp.where(lane < HALF, outs[2 * p], outs[2 * p + 1])


def dsa_attention(qpad, qipad, proj, limit, k16, v16, ki16, *, bsz, tq, tk, causal, n_sel):
    qb = min(Q_BLOCK, tq)
    nqb = tq // qb
    assert tk % KEY_CHUNK == 0 and tk >= n_sel
    if causal:
        per_group = KEY_CHUNK // qb
        groups = [(g * per_group, per_group, g + 1) for g in range(nqb // per_group)]
    else:
        groups = [(0, nqb, tk // KEY_CHUNK)]
    lim3 = limit.reshape(bsz * nqb, 1, qb)
    out = None
    for first, count, nch in groups:
        def qspec(width, col=0, first=first):
            return pl.BlockSpec((qb, width), lambda b, i: (b * nqb + first + i, col))

        kspec = pl.BlockSpec((tk, LANE), lambda b, i: (b, 0))
        in_specs = [qspec(A_HEADS * LANE), qspec(IDX_HEADS * LANE), qspec(LANE, AB_MISC // LANE),
                    pl.BlockSpec((1, 1, qb), lambda b, i, first=first: (b * nqb + first + i, 0, 0)),
                    kspec, kspec, kspec]
        args = [qpad, qipad, proj, lim3, k16, v16, ki16]
        kern = functools.partial(_dsa_kernel, nch=nch, n_sel=n_sel)
        aliases = {}
        if out is not None:
            in_specs.append(pl.BlockSpec(memory_space=pl.ANY))
            args.append(out)
            aliases = {len(args) - 1: 0}
            kern = functools.partial(_dsa_kernel_with_carry, nch=nch, n_sel=n_sel)
        out = pl.pallas_call(
            kern,
            grid=(bsz, count),
            in_specs=in_specs,
            out_specs=qspec(A_HEADS * HEAD_DIM),
            out_shape=jax.ShapeDtypeStruct((bsz * tq, A_HEADS * HEAD_DIM), F32),
            scratch_shapes=[pltpu.VMEM((tk, qb), jnp.int32), pltpu.VMEM((qb, tk), F32),
                            pltpu.VMEM((tk, qb), jnp.int16), pltpu.VMEM((tk, qb), jnp.int16)],
            input_output_aliases=aliases,
            compiler_params=pltpu.CompilerParams(
                dimension_semantics=("parallel", "arbitrary"), vmem_limit_bytes=VMEM_LIMIT),
            name="dsa_attention",
        )(*args)
    return out


def _dsa_kernel_with_carry(q_ref, qi_ref, misc_ref, lim_ref, k_ref, v_ref, ki_ref, carry_ref, o_ref, *scratch,
                           nch, n_sel):
    del carry_ref
    _dsa_kernel(q_ref, qi_ref, misc_ref, lim_ref, k_ref, v_ref, ki_ref, o_ref, *scratch, nch=nch, n_sel=n_sel)


SUB = 16


def _dot(a, b):
    return jnp.dot(a, b, preferred_element_type=F32)


def _dot_nt(a, b):
    return lax.dot_general(a, b, (((1,), (1,)), ((), ())), preferred_element_type=F32)


def _dot_tn(a, b):
    return lax.dot_general(a, b, (((0,), (0,)), ((), ())), preferred_element_type=F32)


def _split2(x):
    hi = x.astype(BF16)
    return hi, (x - hi.astype(F32)).astype(BF16)


def _split3(x):
    hi = x.astype(BF16)
    r = x - hi.astype(F32)
    mid = r.astype(BF16)
    return hi, mid, (r - mid.astype(F32)).astype(BF16)


def _cumsum_rows(x, tril16):
    hi, mid, lo = _split3(x)
    return _dot(tril16, hi) + _dot(tril16, mid) + _dot(tril16, lo)


def _dot_f32(a, b):
    ah, al = _split2(a)
    bh, bl = _split2(b)
    return _dot(ah, bh) + (_dot(ah, bl) + _dot(al, bh))


def _tri_mask(n, strict=False):
    r = lax.broadcasted_iota(jnp.int32, (n, n), 0)
    c = lax.broadcasted_iota(jnp.int32, (n, n), 1)
    return r > c if strict else r >= c


def _rows_to_lanes(x):
    rows = x.shape[0]
    if rows < LANE:
        x = jnp.concatenate([x, jnp.zeros((LANE - rows, LANE), x.dtype)], axis=0)
    return x.T


def _chunk_call(kern, *, bsz, nc, rows, ins, outs, scratch, name):
    def spec(a, kind, width, offset):
        if kind == 'rows':
            return pl.BlockSpec((rows, width), lambda b, c: (b * nc + c, offset // width))
        if kind == 'batch':
            return pl.BlockSpec((1,) + tuple(a.shape[1:]), lambda b, c: (b,) + (0,) * (len(a.shape) - 1))
        return pl.BlockSpec(tuple(a.shape), lambda b, c: (0,) * len(a.shape))

    return pl.pallas_call(
        kern,
        grid=(bsz, nc),
        in_specs=[spec(*i) for i in ins],
        out_specs=[spec(*o) for o in outs],
        out_shape=[o[0] for o in outs],
        scratch_shapes=scratch,
        compiler_params=pltpu.CompilerParams(
            dimension_semantics=("parallel", "arbitrary"), vmem_limit_bytes=VMEM_LIMIT),
        name=name,
    )(*[i[0] for i in ins])


def _mlstm_kernel(q_ref, k_ref, v_ref, og_ref, misc_ref, gb_ref, gain_ref, c0_ref, n0_ref, m0_ref,
                  h_ref, c_out_ref, n_out_ref, m_out_ref, c_scr, n_scr, m_scr):
    ci = pl.program_id(1)

    @pl.when(ci == 0)
    def _():
        c_scr[...] = c0_ref[0]
        n_scr[...] = n0_ref[0]
        m_scr[...] = m0_ref[0]

    rows = q_ref.shape[0]
    hr = B_HEADS * rows
    wq = B_HEADS * B_QK_DIM
    tril16 = jnp.where(_tri_mask(rows), 1.0, 0.0).astype(BF16)
    gates = misc_ref[...] + gb_ref[...]
    bcum = _cumsum_rows(jax.nn.log_sigmoid(gates), tril16)
    m_all = m_scr[...]

    def stack(f):
        return jnp.concatenate([f(h) for h in range(B_HEADS)], axis=0)

    lane_q = lax.broadcasted_iota(jnp.int32, (rows, wq), 1)
    qx, kx = q_ref[...], k_ref[...]
    q_all = stack(lambda h: jnp.where(lane_q // B_QK_DIM == h, qx, 0.0))
    k_all = stack(lambda h: jnp.where(lane_q // B_QK_DIM == h, kx, 0.0)) * B_QK_DIM ** -0.5
    v_all = stack(lambda h: v_ref[:, h * B_V_DIM:(h + 1) * B_V_DIM])
    b_col = stack(lambda h: bcum[:, MISC_FB + h:MISC_FB + h + 1])
    i_col = stack(lambda h: gates[:, MISC_IB + h:MISC_IB + h + 1])
    m_col = stack(lambda h: jnp.broadcast_to(m_all[:, h:h + 1], (rows, 1)))
    b_end = stack(lambda h: jnp.broadcast_to(bcum[rows - 1:rows, MISC_FB + h:MISC_FB + h + 1], (rows, 1)))
    b_row = jnp.broadcast_to(b_col, (hr, LANE)).T[0:1, :]
    i_row = jnp.broadcast_to(i_col, (hr, LANE)).T[0:1, :]

    r = lax.broadcasted_iota(jnp.int32, (hr, hr), 0)
    c = lax.broadcasted_iota(jnp.int32, (hr, hr), 1)
    incl = ((r // rows) == (c // rows)) & (r >= c)
    dmat = jnp.where(incl, b_col - b_row + i_row, -jnp.inf)
    inter = b_col + m_col
    mrow = jnp.maximum(inter, jnp.max(dmat, axis=1, keepdims=True))
    w_state = jnp.exp(inter - mrow)
    q16 = q_all.astype(BF16)
    v16 = v_all.astype(BF16)
    scores = _dot_nt(q16, k_all.astype(BF16)) * jnp.exp(dmat - mrow)
    cs = c_scr[...]
    n_row = n_scr[...]
    num = _dot(scores.astype(BF16), v16) + w_state * _dot(q16, cs.astype(BF16))
    den = jnp.sum(scores, axis=1, keepdims=True) + w_state * jnp.sum(q_all * n_row, axis=1, keepdims=True)
    hh = num / jnp.maximum(jnp.abs(den), jnp.exp(-mrow))
    gain = gain_ref[...]
    for h in range(B_HEADS):
        h_ref[:, h * B_V_DIM:(h + 1) * B_V_DIM] = (_rms_rows(hh[h * rows:(h + 1) * rows], gain)
                                                   * jax.nn.sigmoid(og_ref[:, h * B_V_DIM:(h + 1) * B_V_DIM]))

    g_col = b_end - b_col + i_col
    lane1 = lax.broadcasted_iota(jnp.int32, (1, LANE), 1)
    m_next = m_all
    m_new_rows, keep_rows, keep_lanes = [], [], []
    for h in range(B_HEADS):
        m_h = m_all[:, h:h + 1]
        be = bcum[rows - 1:rows, MISC_FB + h:MISC_FB + h + 1]
        m_new = jnp.maximum(be + m_h, jnp.max(g_col[h * rows:(h + 1) * rows], axis=0, keepdims=True))
        keep = jnp.exp(be + m_h - m_new)
        m_next = jnp.where(lane1 == h, m_new, m_next)
        m_new_rows.append(jnp.broadcast_to(m_new, (rows, 1)))
        keep_rows.append(jnp.broadcast_to(keep, (B_QK_DIM, 1)))
        keep_lanes.append(jnp.broadcast_to(keep, (1, B_QK_DIM)))
    kw = k_all * jnp.exp(g_col - jnp.concatenate(m_new_rows, axis=0))
    c_scr[...] = jnp.concatenate(keep_rows, axis=0) * cs + _dot_tn(kw.astype(BF16), v16)
    n_scr[...] = jnp.concatenate(keep_lanes, axis=1) * n_row + jnp.sum(kw, axis=0, keepdims=True)
    m_scr[...] = m_next

    @pl.when(ci == pl.num_programs(1) - 1)
    def _():
        c_out_ref[0] = c_scr[...]
        n_out_ref[0] = n_scr[...]
        m_out_ref[0] = m_scr[...]


def mlstm_mixer(proj, gate_bias, gain, c0, n0, m0, *, bsz, t, chunk):
    nc = t // chunk
    gb = jnp.zeros((1, LANE), F32)
    gb = gb.at[0, MISC_IB:MISC_IB + B_HEADS].set(gate_bias[0]).at[0, MISC_FB:MISC_FB + B_HEADS].set(gate_bias[1])
    c0 = c0.reshape(bsz, B_HEADS * B_QK_DIM, B_V_DIM)
    n0 = n0.reshape(bsz, 1, B_HEADS * B_QK_DIM)
    m0 = jnp.pad(m0, ((0, 0), (0, LANE - B_HEADS))).reshape(bsz, 1, LANE)
    wq = B_HEADS * B_QK_DIM
    wv = B_HEADS * B_V_DIM
    h, c, n, m = _chunk_call(
        _mlstm_kernel, bsz=bsz, nc=nc, rows=chunk,
        ins=[(proj, 'rows', wq, AB_QB), (proj, 'rows', wq, AB_KB), (proj, 'rows', wv, AB_VB),
             (proj, 'rows', wv, AB_OB), (proj, 'rows', LANE, AB_MISC), (gb, 'const', 0, 0),
             (gain.reshape(1, B_V_DIM), 'const', 0, 0), (c0, 'batch', 0, 0), (n0, 'batch', 0, 0),
             (m0, 'batch', 0, 0)],
        outs=[(jax.ShapeDtypeStruct((bsz * t, wv), F32), 'rows', wv, 0),
              (jax.ShapeDtypeStruct(c0.shape, F32), 'batch', 0, 0),
              (jax.ShapeDtypeStruct(n0.shape, F32), 'batch', 0, 0),
              (jax.ShapeDtypeStruct(m0.shape, F32), 'batch', 0, 0)],
        scratch=[pltpu.VMEM((wq, B_V_DIM), F32), pltpu.VMEM((1, wq), F32), pltpu.VMEM((1, LANE), F32)],
        name="mlstm_mixer")
    return (h, c.reshape(bsz, B_HEADS, B_QK_DIM, B_V_DIM), n.reshape(bsz, B_HEADS, B_QK_DIM),
            m.reshape(bsz, LANE)[:, :B_HEADS])


CD_QKV, CD_ZC, CD_QD, CD_FD, CD_VD, CD_GD, CD_MISC, CD_TOTAL = 0, 1536, 2048, 2560, 3072, 3584, 4096, 4224
MISC_BC, MISC_AC = 0, 4
TAIL = 8


def _permute_w_in_cd(w):
    qkv, bc, ac, zc, qd, fd, vd, gd = _split_cols(w, ODD_SPLITS)
    pad = jnp.zeros((w.shape[0], LANE - 2 * C_HEADS), w.dtype)
    return jnp.concatenate([qkv, zc, qd, fd, vd, gd, bc, ac, pad], axis=1)


def _gdn_init(s0_ref, tail0_ref, st_scr, tail_scr):
    for h in range(C_HEADS):
        st_scr[:, h * C_DIM:(h + 1) * C_DIM] = s0_ref[0, h].T
    tail_scr[...] = tail0_ref[0]


def _gdn_final(s_out_ref, st_scr):
    for h in range(C_HEADS):
        s_out_ref[0, h] = st_scr[:, h * C_DIM:(h + 1) * C_DIM].T


def _gdn_chunk(qkv_ref, z_ref, misc_ref, cw_ref, alog_ref, dt_ref, gain_ref, o_ref, st_scr, tail_scr):
    rows = qkv_ref.shape[0]
    width = qkv_ref.shape[1]
    x = qkv_ref[...]
    tail = tail_scr[...]
    row8 = lax.broadcasted_iota(jnp.int32, (TAIL, width), 0)
    acc = x * cw_ref[CONV_W - 1:CONV_W, :]
    for back in range(1, CONV_W):
        rolled = pltpu.roll(x, back, 0)
        first = jnp.where(row8 < back, pltpu.roll(tail, back, 0), rolled[0:TAIL])
        shifted = first if rows == TAIL else jnp.concatenate([first, rolled[TAIL:]], axis=0)
        acc = acc + shifted * cw_ref[CONV_W - 1 - back:CONV_W - back, :]
    tail_scr[...] = x[rows - TAIL:rows]
    conv = acc * jax.nn.sigmoid(acc)

    tril16 = jnp.where(_tri_mask(rows), 1.0, 0.0).astype(BF16)
    misc = misc_ref[...]
    beta_t = jax.nn.sigmoid(misc)
    g_t = -jnp.exp(alog_ref[...]) * jax.nn.softplus(misc + dt_ref[...])
    gcum = _cumsum_rows(g_t, tril16)

    hd = C_HEADS * C_DIM
    hr = C_HEADS * rows

    def stack(f):
        return jnp.concatenate([f(h) for h in range(C_HEADS)], axis=0)

    def l2n(v):
        return v * lax.rsqrt(jnp.sum(v * v, axis=-1, keepdims=True) + EPS)

    q_all = stack(lambda h: l2n(conv[:, h * C_DIM:(h + 1) * C_DIM])) * C_DIM ** -0.5
    k_all = stack(lambda h: l2n(conv[:, hd + h * C_DIM:hd + (h + 1) * C_DIM]))
    v_all = stack(lambda h: conv[:, 2 * hd + h * C_DIM:2 * hd + (h + 1) * C_DIM])
    beta = stack(lambda h: beta_t[:, MISC_BC + h:MISC_BC + h + 1])
    gc = stack(lambda h: gcum[:, MISC_AC + h:MISC_AC + h + 1])
    g_end = stack(lambda h: jnp.broadcast_to(gcum[rows - 1:rows, MISC_AC + h:MISC_AC + h + 1], (rows, 1)))
    gc_row = jnp.broadcast_to(gc, (hr, LANE)).T[0:1, :]

    r = lax.broadcasted_iota(jnp.int32, (hr, hr), 0)
    c = lax.broadcasted_iota(jnp.int32, (hr, hr), 1)
    same = (r // rows) == (c // rows)
    incl = same & (r >= c)
    strict = same & (r > c)
    decay = jnp.exp(jnp.where(incl, gc - gc_row, -jnp.inf))
    k16 = k_all.astype(BF16)
    a_mat = jnp.where(strict, beta * _dot_nt(k16, k16) * decay, 0.0)
    power = -a_mat
    inv = jnp.where(r == c, 1.0, 0.0) + power
    for _ in range(int(math.log2(rows)) - 1):
        power = _dot_f32(power, power)
        inv = inv + _dot_f32(inv, power)
    inv_hi, inv_lo = _split2(inv)
    rhs = jnp.concatenate([beta * v_all, beta * jnp.exp(gc) * k_all], axis=1).astype(BF16)
    w = _dot(inv_hi, rhs) + _dot(inv_lo, rhs)
    w_v, w_k = w[:, 0:C_DIM], w[:, C_DIM:2 * C_DIM]
    qk = _dot_nt(q_all.astype(BF16), k16) * decay

    head_of_row = lax.broadcasted_iota(jnp.int32, (hr, C_DIM), 0) // rows

    def per_head_lanes(m):
        return jnp.concatenate([jnp.where(head_of_row == h, m, 0.0) for h in range(C_HEADS)], axis=1).astype(BF16)

    st = st_scr[...]
    st16 = st.astype(BF16)
    delta = w_v - _dot_nt(per_head_lanes(w_k), st16)
    d16 = delta.astype(BF16)
    out = _dot_nt(per_head_lanes(q_all * jnp.exp(gc)), st16) + _dot(qk.astype(BF16), d16)
    keep = jnp.concatenate([jnp.broadcast_to(jnp.exp(gcum[rows - 1:rows, MISC_AC + h:MISC_AC + h + 1]), (1, C_DIM))
                            for h in range(C_HEADS)], axis=1)
    st_scr[...] = keep * st + _dot_tn(d16, per_head_lanes(k_all * jnp.exp(g_end - gc)))
    gain = gain_ref[...]
    for h in range(C_HEADS):
        z = z_ref[:, h * C_DIM:(h + 1) * C_DIM]
        o_ref[:, h * C_DIM:(h + 1) * C_DIM] = _rms_rows(out[h * rows:(h + 1) * rows], gain) * (z * jax.nn.sigmoid(z))


def _hgrn2_init(s0_ref, st_scr):
    for h in range(D_HEADS):
        st_scr[h] = s0_ref[0, h].T


def _hgrn2_final(s_out_ref, st_scr):
    for h in range(D_HEADS):
        s_out_ref[0, h] = st_scr[h].T


def _hgrn2_chunk(q_ref, f_ref, v_ref, g_ref, lb_ref, gain_ref, o_ref, st_scr):
    rows = q_ref.shape[0]
    tril16 = jnp.where(_tri_mask(rows), 1.0, 0.0).astype(BF16)
    lb = lb_ref[...]
    zf = f_ref[...]
    logf = jnp.logaddexp(jnp.log(lb), jnp.log1p(-lb) + jax.nn.log_sigmoid(zf))
    kd = (1.0 - lb) * jax.nn.sigmoid(-zf)
    qx = q_ref[...]
    qd = qx * jax.nn.sigmoid(qx)
    bcum = _cumsum_rows(logf, tril16)
    gain = gain_ref[...]
    row_sub = lax.broadcasted_iota(jnp.int32, (SUB, 1), 0)
    for h in range(D_HEADS):
        sl = slice(h * D_EXPAND, (h + 1) * D_EXPAND)
        q, k, b = qd[:, sl], kd[:, sl], bcum[:, sl]
        v = v_ref[:, h * D_V_DIM:(h + 1) * D_V_DIM]
        v16 = v.astype(BF16)
        st = st_scr[h]
        inter = _dot_nt((q * jnp.exp(b)).astype(BF16), st.astype(BF16))
        blocks = []
        for i in range(rows // SUB):
            r0 = i * SUB
            qi, bi = q[r0:r0 + SUB], b[r0:r0 + SUB]
            oi = inter[r0:r0 + SUB]
            if i > 0:
                ref = b[r0 - 1:r0]
                att = _dot_nt((qi * jnp.exp(bi - ref)).astype(BF16),
                              (k[0:r0] * jnp.exp(ref - b[0:r0])).astype(BF16))
                oi = oi + _dot(att.astype(BF16), v16[0:r0])
            for s in range(SUB):
                r = r0 + s
                a = jnp.sum(qi * jnp.exp(bi - b[r:r + 1]) * k[r:r + 1], axis=1, keepdims=True)
                oi = oi + jnp.where(row_sub >= s, a, 0.0) * v[r:r + 1]
            blocks.append(oi)
        out = blocks[0] if len(blocks) == 1 else jnp.concatenate(blocks, axis=0)
        b_end = b[rows - 1:rows]
        st_scr[h] = jnp.exp(b_end) * st + _dot_tn(v16, (k * jnp.exp(b_end - b)).astype(BF16))
        g = g_ref[:, h * D_V_DIM:(h + 1) * D_V_DIM]
        o_ref[:, h * D_V_DIM:(h + 1) * D_V_DIM] = _rms_rows(out, gain) * (g * jax.nn.sigmoid(g))


def _cd_kernel(qkv_ref, z_ref, misc_ref, cw_ref, alog_ref, dt_ref, cgain_ref, sc0_ref, tail0_ref,
               qd_ref, fd_ref, vd_ref, gd_ref, lb_ref, dgain_ref, sd0_ref,
               oc_ref, sc_out_ref, od_ref, sd_out_ref, stc_scr, tail_scr, std_scr):
    ci = pl.program_id(1)

    @pl.when(ci == 0)
    def _():
        _gdn_init(sc0_ref, tail0_ref, stc_scr, tail_scr)
        _hgrn2_init(sd0_ref, std_scr)

    _gdn_chunk(qkv_ref, z_ref, misc_ref, cw_ref, alog_ref, dt_ref, cgain_ref, oc_ref, stc_scr, tail_scr)
    _hgrn2_chunk(qd_ref, fd_ref, vd_ref, gd_ref, lb_ref, dgain_ref, od_ref, std_scr)

    @pl.when(ci == pl.num_programs(1) - 1)
    def _():
        _gdn_final(sc_out_ref, stc_scr)
        _hgrn2_final(sd_out_ref, std_scr)


def cd_mixers(proj, conv_w, a_log, dt_bias, c_gain, sc0, conv_prev, lower_bound, d_gain, sd0, *, bsz, t, chunk):
    nc = t // chunk
    hd = C_HEADS * C_DIM
    wk = D_HEADS * D_EXPAND
    wv = D_HEADS * D_V_DIM
    lanes = jnp.zeros((1, LANE), F32)
    alog = lanes.at[0, MISC_AC:MISC_AC + C_HEADS].set(a_log)
    dt = lanes.at[0, MISC_AC:MISC_AC + C_HEADS].set(dt_bias)
    tail0 = jnp.pad(conv_prev, ((0, 0), (TAIL - (CONV_W - 1), 0), (0, 0)))
    return _chunk_call(
        _cd_kernel, bsz=bsz, nc=nc, rows=chunk,
        ins=[(proj, 'rows', 3 * hd, CD_QKV), (proj, 'rows', hd, CD_ZC), (proj, 'rows', LANE, CD_MISC),
             (conv_w, 'const', 0, 0), (alog, 'const', 0, 0), (dt, 'const', 0, 0),
             (c_gain.reshape(1, C_DIM), 'const', 0, 0), (sc0, 'batch', 0, 0), (tail0, 'batch', 0, 0),
             (proj, 'rows', wk, CD_QD), (proj, 'rows', wk, CD_FD), (proj, 'rows', wv, CD_VD),
             (proj, 'rows', wv, CD_GD), (lower_bound.reshape(1, wk), 'const', 0, 0),
             (d_gain.reshape(1, D_V_DIM), 'const', 0, 0), (sd0, 'batch', 0, 0)],
        outs=[(jax.ShapeDtypeStruct((bsz * t, hd), F32), 'rows', hd, 0),
              (jax.ShapeDtypeStruct(sc0.shape, F32), 'batch', 0, 0),
              (jax.ShapeDtypeStruct((bsz * t, wv), F32), 'rows', wv, 0),
              (jax.ShapeDtypeStruct(sd0.shape, F32), 'batch', 0, 0)],
        scratch=[pltpu.VMEM((C_DIM, hd), F32), pltpu.VMEM((TAIL, 3 * hd), F32),
                 pltpu.VMEM((D_HEADS, D_V_DIM, D_EXPAND), F32)],
        name="cd_mixers")


def _pad_cols(w, mult=LANE):
    pad = (-w.shape[-1]) % mult
    return jnp.pad(w, [(0, 0)] * (w.ndim - 1) + [(0, pad)])


def _mixer_ab(proj, bsz, t, pos, prm, cache):
    n = bsz * t
    qpad, qipad, k16, v16, ki16, k32, v32, ki32 = dsa_prep(proj, pos, prm['a_q_gain'][0], prm['a_k_gain'][0], t)
    if cache is None:
        limit = jnp.tile((pos // CHUNK + 1) * CHUNK, bsz).reshape(n, 1)
        a_out = dsa_attention(qpad, qipad, proj, limit, k16, v16, ki16, bsz=bsz, tq=t, tk=t,
                              causal=True, n_sel=min(TOPK_MAX, t // 4))
        c0 = jnp.zeros((bsz, B_HEADS, B_QK_DIM, B_V_DIM), F32)
        n0 = jnp.zeros((bsz, B_HEADS, B_QK_DIM), F32)
        m0 = jnp.zeros((bsz, B_HEADS), F32)
        chunk = CHUNK
    else:
        k_c, v_c, ki_c, c0, n0, m0 = cache
        past = k_c.shape[1]
        n_keys = past + t
        tk = -(-n_keys // KEY_CHUNK) * KEY_CHUNK

        def with_cache(c, new):
            c = c.reshape(bsz, past, -1).astype(BF16)
            c = jnp.pad(c, ((0, 0), (0, 0), (0, LANE - c.shape[-1])))
            return jnp.concatenate([c, new.reshape(bsz, t, LANE),
                                    jnp.zeros((bsz, tk - n_keys, LANE), BF16)], axis=1).reshape(bsz * tk, LANE)

        limit = jnp.full((n, 1), n_keys, jnp.int32)
        a_out = dsa_attention(qpad, qipad, proj, limit, with_cache(k_c, k16), with_cache(v_c, v16),
                              with_cache(ki_c, ki16), bsz=bsz, tq=t, tk=tk, causal=False,
                              n_sel=min(TOPK_MAX, n_keys // 4))
        chunk = t
    h, c, n_, m = mlstm_mixer(proj, prm['b_gate_bias'][0], prm['b_norm_gain'][0], c0, n0, m0,
                              bsz=bsz, t=t, chunk=chunk)
    st = (k32.reshape(bsz, t, A_KV_HEADS, HEAD_DIM), v32.reshape(bsz, t, A_KV_HEADS, HEAD_DIM),
          ki32.reshape(bsz, t, IDX_DIM), c, n_, m)
    return a_out, h, st


def _mixer_cd(proj, bsz, t, prm, lower_bound, cache):
    hd = C_HEADS * C_DIM
    if cache is None:
        sc0 = jnp.zeros((bsz, C_HEADS, C_DIM, C_DIM), F32)
        conv_prev = jnp.zeros((bsz, CONV_W - 1, 3 * hd), F32)
        sd0 = jnp.zeros((bsz, D_HEADS, D_EXPAND, D_V_DIM), F32)
        chunk = CHUNK
    else:
        sc0, conv_prev, sd0 = cache
        chunk = t
    oc, sc, od, sd = cd_mixers(proj, prm['c_conv_w'][0], prm['c_a_log'][0], prm['c_dt_bias'][0],
                               prm['c_norm_gain'][0], sc0, conv_prev, lower_bound, prm['d_norm_gain'][0], sd0,
                               bsz=bsz, t=t, chunk=chunk)
    qkv = proj.reshape(bsz, t, -1)[:, :, CD_QKV:CD_QKV + 3 * hd]
    conv_new = jnp.concatenate([conv_prev, qkv[:, t - (CONV_W - 1):]], axis=1)[:, -(CONV_W - 1):]
    return oc, od, (sc, conv_new, sd)


def _trunk(x, pos_offset, cache, prm, wts):
    bsz, t, d = x.shape
    n = bsz * t
    pos = pos_offset + jnp.arange(t, dtype=jnp.int32)
    probs = jax.nn.softmax(prm['d_lb_logits'], axis=0)
    lower_bounds = jnp.cumsum(probs, axis=0) - probs[0]
    xf = x.reshape(n, d)

    lc = None if cache is None else tuple(c[0] for c in cache[:6])
    proj = norm_matmul(xf, prm['norm_mix'][0], wts['w_in_ab'])
    a_out, b_out, st_even = _mixer_ab(proj, bsz, t, pos, prm, lc)
    xf = matmul_residual(a_out, b_out, wts['w_out_ab'], xf)
    xf = ffn_residual(xf, prm['norm_ffn'][0], wts['ffn_w1'], wts['ffn_w3'], wts['ffn_w2'])

    lc = None if cache is None else tuple(c[0] for c in cache[6:])
    proj = norm_matmul(xf, prm['norm_mix'][1], wts['w_in_cd'])
    c_out, d_out, st_odd = _mixer_cd(proj, bsz, t, prm, lower_bounds[1], lc)
    xf = matmul_residual(c_out, d_out, wts['w_out_cd'], xf)
    xf = moe_residual(xf, prm['norm_ffn'][1], wts['moe_router'], wts['moe_w1'], wts['moe_w3'], wts['moe_w2'])

    new_state = tuple(s[None] for s in st_even + st_odd)
    return xf.reshape(bsz, t, d), new_state


def kernel(x_prompt, x_sample, cache_a_k, cache_a_v, cache_a_kidx, state_b_c, state_b_n, state_b_m,
           state_c_s, state_c_conv, state_d_s, norm_mix, norm_ffn, w_in_ab, w_out_ab, a_q_gain, a_k_gain,
           b_gate_bias, b_norm_gain, w_in_cd, w_out_cd, c_conv_w, c_a_log, c_dt_bias, c_norm_gain,
           d_lb_logits, d_norm_gain, ffn_w1, ffn_w3, ffn_w2, moe_router, moe_w1, moe_w3, moe_w2):
    prm = dict(norm_mix=norm_mix, norm_ffn=norm_ffn, a_q_gain=a_q_gain, a_k_gain=a_k_gain,
               b_gate_bias=b_gate_bias, b_norm_gain=b_norm_gain, c_conv_w=c_conv_w, c_a_log=c_a_log,
               c_dt_bias=c_dt_bias, c_norm_gain=c_norm_gain, d_lb_logits=d_lb_logits, d_norm_gain=d_norm_gain)
    wts = dict(w_in_ab=_permute_w_in_ab(w_in_ab[0]).astype(BF16), w_out_ab=w_out_ab[0].astype(BF16),
               w_in_cd=_permute_w_in_cd(w_in_cd[0]).astype(BF16), w_out_cd=w_out_cd[0].astype(BF16),
               ffn_w1=ffn_w1[0].astype(BF16), ffn_w3=ffn_w3[0].astype(BF16), ffn_w2=ffn_w2[0].astype(BF16),
               moe_router=_pad_cols(moe_router[0]),
               moe_w1=moe_w1[0].astype(BF16), moe_w3=moe_w3[0].astype(BF16), moe_w2=moe_w2[0].astype(BF16))
    cache = (cache_a_k, cache_a_v, cache_a_kidx, state_b_c, state_b_n, state_b_m, state_c_s, state_c_conv, state_d_s)
    y_prompt, st_p = _trunk(x_prompt, 0, None, prm, wts)
    y_sample, st_s = _trunk(x_sample, cache_a_k.shape[2], cache, prm, wts)
    return (y_prompt, y_sample) + st_p + st_s
```

```python
import functools
import math

import jax
import jax.numpy as jnp
import numpy as np
from jax import lax
from jax.experimental import pallas as pl
from jax.experimental.pallas import tpu as pltpu

F32 = jnp.float32
BF16 = jnp.bfloat16

EPS = 1e-6
ROPE_THETA = 500000.0
ROT_FRACTION = 4
CHUNK = 64
A_HEADS, A_KV_HEADS, HEAD_DIM = 8, 2, 64
IDX_HEADS, IDX_DIM = 4, 64
TOPK_MAX, Q_BLOCK = 256, 128
B_HEADS, B_QK_DIM, B_V_DIM = 4, 64, 128
C_HEADS, C_DIM, CONV_W = 4, 128, 4
D_HEADS, D_EXPAND, D_V_DIM = 4, 128, 128
N_EXPERTS, TOP_K_EXPERTS = 8, 2

LANE = 128
VMEM_LIMIT = 48 * 1024 * 1024

EVEN_SPLITS = (A_HEADS * HEAD_DIM, A_KV_HEADS * HEAD_DIM, A_KV_HEADS * HEAD_DIM,
               IDX_HEADS * IDX_DIM, IDX_DIM, IDX_HEADS,
               B_HEADS * B_QK_DIM, B_HEADS * B_QK_DIM, B_HEADS * B_V_DIM,
               B_HEADS, B_HEADS, B_HEADS * B_V_DIM)
ODD_SPLITS = (3 * C_HEADS * C_DIM, C_HEADS, C_HEADS, C_HEADS * C_DIM,
              D_HEADS * D_EXPAND, D_HEADS * D_EXPAND, D_HEADS * D_V_DIM, D_HEADS * D_V_DIM)


def _split_cols(p, widths):
    cuts = [int(c) for c in np.cumsum(widths)[:-1]]
    return jnp.split(p, cuts, axis=-1)


def _row_tile(n, target):
    t = min(n, target)
    while n % t:
        t //= 2
    return t


def _col_tile(n, target):
    best = LANE
    for k in range(1, n // LANE + 1):
        c = k * LANE
        if n % c == 0 and c <= target:
            best = c
    return best


def _rms_rows(x, gain):
    return x * lax.rsqrt(jnp.mean(x * x, axis=-1, keepdims=True) + EPS) * gain


def _norm_matmul_kernel(x_ref, g_ref, w_ref, o_ref, xn_ref):
    @pl.when(pl.program_id(1) == 0)
    def _():
        xn_ref[...] = _rms_rows(x_ref[...], g_ref[...]).astype(BF16)

    o_ref[...] = jnp.dot(xn_ref[...], w_ref[...], preferred_element_type=F32)


def norm_matmul(x, gain, w):
    n, d = x.shape
    m = w.shape[1]
    tm = _row_tile(n, 1024)
    tn = _col_tile(m, 1536)
    return pl.pallas_call(
        _norm_matmul_kernel,
        grid=(n // tm, m // tn),
        in_specs=[pl.BlockSpec((tm, d), lambda i, j: (i, 0)),
                  pl.BlockSpec((1, d), lambda i, j: (0, 0)),
                  pl.BlockSpec((d, tn), lambda i, j: (0, j))],
        out_specs=pl.BlockSpec((tm, tn), lambda i, j: (i, j)),
        out_shape=jax.ShapeDtypeStruct((n, m), F32),
        scratch_shapes=[pltpu.VMEM((tm, d), BF16)],
        compiler_params=pltpu.CompilerParams(
            dimension_semantics=("parallel", "arbitrary"), vmem_limit_bytes=VMEM_LIMIT),
        name="norm_matmul",
    )(x, gain.reshape(1, d), w)


def _matmul_res_kernel(a_ref, b_ref, w_ref, r_ref, o_ref):
    ka = a_ref.shape[1]
    o_ref[...] = (r_ref[...] + jnp.dot(a_ref[...].astype(BF16), w_ref[0:ka, :], preferred_element_type=F32)
                  + jnp.dot(b_ref[...].astype(BF16), w_ref[ka:, :], preferred_element_type=F32))


def matmul_residual(a, b, w, res):
    n, ka = a.shape
    kb = b.shape[1]
    m = w.shape[1]
    tm = _row_tile(n, 1024)
    return pl.pallas_call(
        _matmul_res_kernel,
        grid=(n // tm,),
        in_specs=[pl.BlockSpec((tm, ka), lambda i: (i, 0)),
                  pl.BlockSpec((tm, kb), lambda i: (i, 0)),
                  pl.BlockSpec((ka + kb, m), lambda i: (0, 0)),
                  pl.BlockSpec((tm, m), lambda i: (i, 0))],
        out_specs=pl.BlockSpec((tm, m), lambda i: (i, 0)),
        out_shape=jax.ShapeDtypeStruct((n, m), F32),
        compiler_params=pltpu.CompilerParams(
            dimension_semantics=("parallel",), vmem_limit_bytes=VMEM_LIMIT),
        name="matmul_residual",
    )(a, b, w, res)


def _swiglu_tile(xn, w1, w3):
    h1 = jnp.dot(xn, w1, preferred_element_type=F32)
    h3 = jnp.dot(xn, w3, preferred_element_type=F32)
    return h1 * jax.nn.sigmoid(h1) * h3


def _ffn_kernel(x_ref, g_ref, w1_ref, w3_ref, w2_ref, o_ref, xn_ref):
    @pl.when(pl.program_id(1) == 0)
    def _():
        x = x_ref[...]
        xn_ref[...] = _rms_rows(x, g_ref[...]).astype(BF16)
        o_ref[...] = x

    act = _swiglu_tile(xn_ref[...], w1_ref[...], w3_ref[...])
    o_ref[...] += jnp.dot(act.astype(BF16), w2_ref[...], preferred_element_type=F32)


def ffn_residual(x, gain, w1, w3, w2):
    n, d = x.shape
    f = w1.shape[1]
    tm = _row_tile(n, 1024)
    tf = _col_tile(f, 512)
    return pl.pallas_call(
        _ffn_kernel,
        grid=(n // tm, f // tf),
        in_specs=[pl.BlockSpec((tm, d), lambda i, j: (i, 0)),
                  pl.BlockSpec((1, d), lambda i, j: (0, 0)),
                  pl.BlockSpec((d, tf), lambda i, j: (0, j)),
                  pl.BlockSpec((d, tf), lambda i, j: (0, j)),
                  pl.BlockSpec((tf, d), lambda i, j: (j, 0))],
        out_specs=pl.BlockSpec((tm, d), lambda i, j: (i, 0)),
        out_shape=jax.ShapeDtypeStruct((n, d), F32),
        scratch_shapes=[pltpu.VMEM((tm, d), BF16)],
        compiler_params=pltpu.CompilerParams(
            dimension_semantics=("parallel", "arbitrary"), vmem_limit_bytes=VMEM_LIMIT),
        name="ffn_residual",
    )(x, gain.reshape(1, d), w1, w3, w2)


MOE_VMEM_LIMIT = 58 * 1024 * 1024
MOE_CAPS = (256, 288, 320, 384, 512)


def _moe_route_kernel(x_ref, g_ref, r_ref, xn_ref, comb_ref, post_ref, cnt_ref):
    x = x_ref[...]
    tm = x.shape[0]
    xn = _rms_rows(x, g_ref[...])
    xn_ref[...] = xn.astype(BF16)
    logits = jnp.dot(xn, r_ref[...], preferred_element_type=F32, precision=lax.Precision.HIGHEST)
    lane = lax.broadcasted_iota(jnp.int32, logits.shape, 1)
    logits = jnp.where(lane < N_EXPERTS, logits, -jnp.inf)
    m1 = jnp.max(logits, axis=-1, keepdims=True)
    i1 = jnp.min(jnp.where(logits == m1, lane, LANE), axis=-1, keepdims=True)
    rest = jnp.where(lane == i1, -jnp.inf, logits)
    m2 = jnp.max(rest, axis=-1, keepdims=True)
    i2 = jnp.min(jnp.where(rest == m2, lane, LANE), axis=-1, keepdims=True)
    e2 = jnp.exp(m2 - m1)
    den = 1.0 + e2
    comb_ref[...] = jnp.where(lane == i1, 1.0 / den, 0.0) + jnp.where(lane == i2, e2 / den, 0.0)
    chosen = (lane == i1) | (lane == i2)
    sel = jnp.where(chosen, 1.0, 0.0)
    tril16 = jnp.where(_tri_mask(LANE), 1.0, 0.0).astype(BF16)
    seen = jnp.zeros((1, LANE), F32)
    ranks = []
    for blk in range(tm // LANE):
        sb = sel[blk * LANE:(blk + 1) * LANE]
        ranks.append(_dot(tril16, sb.astype(BF16)) + seen - 1.0)
        seen = seen + jnp.sum(sb, axis=0, keepdims=True)
    rank = jnp.where(chosen, jnp.concatenate(ranks, axis=0), -1.0)
    post_ref[0] = rank.T[0:N_EXPERTS, :]
    cnt_ref[0] = seen


def _moe_expert_kernel(cnt_ref, xn_ref, comb_ref, post_ref, x_ref, w1_ref, w3_ref, w2_ref, o_ref,
                       xe_scr, y_scr, *, caps):
    i = pl.program_id(0)
    e = pl.program_id(1)
    j = pl.program_id(2)
    last = pl.num_programs(2) - 1
    tm = x_ref.shape[0]

    @pl.when(jnp.logical_and(e == 0, j == 0))
    def _():
        o_ref[...] = x_ref[...]

    cnt = cnt_ref[i * N_EXPERTS + e]

    def expert_step(cap):
        def pick():
            rank_row = post_ref[0, pl.ds(e, 1), :].astype(jnp.int32)
            slot = lax.broadcasted_iota(jnp.int32, (cap, tm), 0)
            return jnp.where(rank_row == slot, 1.0, 0.0).astype(BF16)

        @pl.when(j == 0)
        def _():
            xe_scr[0:cap, :] = _dot(pick(), xn_ref[...]).astype(BF16)

        act = _swiglu_tile(xe_scr[0:cap, :], w1_ref[0], w3_ref[0])
        yj = _dot(act.astype(BF16), w2_ref[0])

        @pl.when(j == 0)
        def _():
            y_scr[0:cap, :] = yj

        @pl.when(j > 0)
        def _():
            y_scr[0:cap, :] += yj

        @pl.when(j == last)
        def _():
            comb = comb_ref[...]
            lane = lax.broadcasted_iota(jnp.int32, comb.shape, 1)
            gate = jnp.sum(jnp.where(lane == e, comb, 0.0), axis=-1, keepdims=True)
            hi, lo = _split2(y_scr[0:cap, :])
            p = pick()
            o_ref[...] += gate * (_dot_tn(p, hi) + _dot_tn(p, lo))

    lo = 0
    for cap in caps:
        @pl.when(jnp.logical_and(cnt > lo, cnt <= cap))
        def _(cap=cap):
            expert_step(cap)
        lo = cap


def moe_residual(x, gain, router, w1, w3, w2):
    n, d = x.shape
    ne, _, f = w1.shape
    tm = _row_tile(n, 1024)
    tf = _col_tile(f, 896)
    nt = n // tm
    xn, comb, post, cnt = pl.pallas_call(
        _moe_route_kernel,
        grid=(nt,),
        in_specs=[pl.BlockSpec((tm, d), lambda i: (i, 0)),
                  pl.BlockSpec((1, d), lambda i: (0, 0)),
                  pl.BlockSpec((d, LANE), lambda i: (0, 0))],
        out_specs=[pl.BlockSpec((tm, d), lambda i: (i, 0)),
                   pl.BlockSpec((tm, LANE), lambda i: (i, 0)),
                   pl.BlockSpec((1, ne, tm), lambda i: (i, 0, 0)),
                   pl.BlockSpec((1, 1, LANE), lambda i: (i, 0, 0))],
        out_shape=[jax.ShapeDtypeStruct((n, d), BF16), jax.ShapeDtypeStruct((n, LANE), F32),
                   jax.ShapeDtypeStruct((nt, ne, tm), F32), jax.ShapeDtypeStruct((nt, 1, LANE), F32)],
        compiler_params=pltpu.CompilerParams(dimension_semantics=("parallel",), vmem_limit_bytes=VMEM_LIMIT),
        name="moe_route",
    )(x, gain.reshape(1, d), router)
    counts = cnt[:, 0, :ne].astype(jnp.int32).reshape(nt * ne)
    caps = tuple(c for c in MOE_CAPS if c < tm) + (tm,)
    grid_spec = pltpu.PrefetchScalarGridSpec(
        num_scalar_prefetch=1,
        grid=(nt, ne, f // tf),
        in_specs=[pl.BlockSpec((tm, d), lambda i, e, j, c: (i, 0)),
                  pl.BlockSpec((tm, LANE), lambda i, e, j, c: (i, 0)),
                  pl.BlockSpec((1, ne, tm), lambda i, e, j, c: (i, 0, 0)),
                  pl.BlockSpec((tm, d), lambda i, e, j, c: (i, 0)),
                  pl.BlockSpec((1, d, tf), lambda i, e, j, c: (e, 0, j)),
                  pl.BlockSpec((1, d, tf), lambda i, e, j, c: (e, 0, j)),
                  pl.BlockSpec((1, tf, d), lambda i, e, j, c: (e, j, 0))],
        out_specs=pl.BlockSpec((tm, d), lambda i, e, j, c: (i, 0)),
        scratch_shapes=[pltpu.VMEM((tm, d), BF16), pltpu.VMEM((tm, d), F32)])
    return pl.pallas_call(
        functools.partial(_moe_expert_kernel, caps=caps),
        grid_spec=grid_spec,
        out_shape=jax.ShapeDtypeStruct((n, d), F32),
        compiler_params=pltpu.CompilerParams(
            dimension_semantics=("parallel", "arbitrary", "arbitrary"), vmem_limit_bytes=MOE_VMEM_LIMIT),
        name="moe_experts",
    )(counts, xn, comb, post, x, w1, w3, w2)


AB_QA, AB_VB, AB_OB, AB_QI, AB_QB, AB_KB, AB_KA, AB_VA, AB_MISC, AB_TOTAL = (
    0, 512, 1024, 1536, 1792, 2048, 2304, 2432, 2560, 2688)
MISC_WI, MISC_IB, MISC_FB = 64, 68, 72
HALF = LANE // 2
KEY_CHUNK = 512
MASKED = -1e30
KEY_OF_NEG_INF = -2139095041
I16_MIN, I16_MAX = -32768, 32767


def _permute_w_in_ab(w):
    qa, ka, va, qi, ki, wi, qb, kb, vb, ib, fb, ob = _split_cols(w, EVEN_SPLITS)
    pad = jnp.zeros((w.shape[0], LANE - IDX_DIM - 3 * IDX_HEADS), w.dtype)
    return jnp.concatenate([qa, vb, ob, qi, qb, kb, ka, va, ki, wi, ib, fb, pad], axis=1)


def _rope_tables(pos):
    rot = HEAD_DIM // ROT_FRACTION
    half = rot // 2
    inv_freq = ROPE_THETA ** (-jnp.arange(half, dtype=F32) * 2.0 / rot)
    ang = pos.astype(F32)[:, None] * inv_freq[None, :]
    cos, sin = jnp.cos(ang), jnp.sin(ang)
    t = pos.shape[0]
    one = jnp.ones((t, HEAD_DIM - rot), F32)
    zero_r = jnp.zeros((t, HEAD_DIM - rot), F32)
    zero_h = jnp.zeros((t, half), F32)
    c = jnp.concatenate([cos, cos, one], axis=1)
    s_up = jnp.concatenate([-sin, zero_h, zero_r], axis=1)
    s_dn = jnp.concatenate([zero_h, sin, zero_r], axis=1)
    return tuple(jnp.concatenate([a, a], axis=1) for a in (c, s_up, s_dn))


def _rope_tile(x, c, s_up, s_dn):
    half = HEAD_DIM // ROT_FRACTION // 2
    return x * c + pltpu.roll(x, LANE - half, 1) * s_up + pltpu.roll(x, half, 1) * s_dn


def _head_norm_tile(x, gain, same_head):
    sq = x * x
    hi = sq.astype(BF16)
    lo = (sq - hi.astype(F32)).astype(BF16)
    ss = (jnp.dot(hi, same_head, preferred_element_type=F32)
          + jnp.dot(lo, same_head, preferred_element_type=F32))
    return x * lax.rsqrt(ss * (1.0 / HEAD_DIM) + EPS) * gain


def _aprep_kernel(qa_ref, ka_ref, va_ref, qi_ref, misc_ref, c_ref, su_ref, sd_ref, qg_ref, kg_ref,
                  qpad_ref, qipad_ref, k16_ref, v16_ref, ki16_ref, k32_ref, v32_ref, ki32_ref):
    c, su, sd = c_ref[...], su_ref[...], sd_ref[...]
    tm = c.shape[0]
    row = lax.broadcasted_iota(jnp.int32, (LANE, LANE), 0)
    col = lax.broadcasted_iota(jnp.int32, (LANE, LANE), 1)
    same_head = jnp.where(row // HALF == col // HALF, 1.0, 0.0).astype(BF16)
    lane = lax.broadcasted_iota(jnp.int32, (tm, LANE), 1)
    low = lane < HALF

    heads_per_group = A_HEADS // A_KV_HEADS
    for p in range(A_HEADS // 2):
        y = _rope_tile(_head_norm_tile(qa_ref[:, p * LANE:(p + 1) * LANE], qg_ref[...], same_head), c, su, sd)
        y = y * HEAD_DIM ** -0.5
        y_sw = pltpu.roll(y, HALF, 1)
        for o in range(2):
            h = 2 * p + o
            g = h // heads_per_group
            src = y if o == g else y_sw
            qpad_ref[:, h * LANE:(h + 1) * LANE] = jnp.where(low if g == 0 else ~low, src, 0.0).astype(BF16)
    k = _rope_tile(_head_norm_tile(ka_ref[...], kg_ref[...], same_head), c, su, sd)
    k32_ref[...] = k
    k16_ref[...] = k.astype(BF16)
    v = va_ref[...]
    v32_ref[...] = v
    v16_ref[...] = v.astype(BF16)
    for p in range(IDX_HEADS // 2):
        y = _rope_tile(qi_ref[:, p * LANE:(p + 1) * LANE], c, su, sd)
        y_sw = pltpu.roll(y, HALF, 1)
        qipad_ref[:, (2 * p) * LANE:(2 * p + 1) * LANE] = jnp.where(low, y, 0.0).astype(BF16)
        qipad_ref[:, (2 * p + 1) * LANE:(2 * p + 2) * LANE] = jnp.where(low, y_sw, 0.0).astype(BF16)
    ki = _rope_tile(misc_ref[...], c, su, sd)
    ki32_ref[...] = ki[:, :IDX_DIM]
    ki16_ref[...] = jnp.where(low, ki, 0.0).astype(BF16)


def dsa_prep(proj, pos, q_gain, k_gain, t):
    n = proj.shape[0]
    tm = _row_tile(n, 512)
    tabs = _rope_tables(pos)
    if t < tm:
        tabs = tuple(jnp.tile(a, (tm // t, 1)) for a in tabs)
    nt = tabs[0].shape[0] // tm
    tab_spec = pl.BlockSpec((tm, LANE), lambda i: (i % nt, 0))
    gain_spec = pl.BlockSpec((1, LANE), lambda i: (0, 0))

    def col(width, offset):
        return pl.BlockSpec((tm, width), lambda i: (i, offset // width))

    def out(width, dtype):
        return (jax.ShapeDtypeStruct((n, width), dtype), pl.BlockSpec((tm, width), lambda i: (i, 0)))

    outs = [out(A_HEADS * LANE, BF16), out(IDX_HEADS * LANE, BF16), out(LANE, BF16), out(LANE, BF16),
            out(LANE, BF16), out(LANE, F32), out(LANE, F32), out(IDX_DIM, F32)]
    return pl.pallas_call(
        _aprep_kernel,
        grid=(n // tm,),
        in_specs=[col(A_HEADS * HEAD_DIM, AB_QA), col(LANE, AB_KA), col(LANE, AB_VA),
                  col(IDX_HEADS * IDX_DIM, AB_QI), col(LANE, AB_MISC), tab_spec, tab_spec, tab_spec,
                  gain_spec, gain_spec],
        out_specs=[o[1] for o in outs],
        out_shape=[o[0] for o in outs],
        compiler_params=pltpu.CompilerParams(dimension_semantics=("parallel",), vmem_limit_bytes=VMEM_LIMIT),
        name="dsa_prep",
    )(proj, proj, proj, proj, proj, *tabs, jnp.tile(q_gain, 2).reshape(1, LANE), jnp.tile(k_gain, 2).reshape(1, LANE))


N_PARTIAL = 4


def _add_tiles(accs, m, sub):
    accs = list(accs)
    for t in range(m.shape[0] // sub):
        accs[t % len(accs)] = accs[t % len(accs)] + m[t * sub:(t + 1) * sub]
    return tuple(accs)


def _dsa_kernel(q_ref, qi_ref, misc_ref, lim_ref, k_ref, v_ref, ki_ref, o_ref, key_ref, bias_ref, hi_ref, lo_ref,
                *, nch, n_sel):
    qb = q_ref.shape[0]
    kc = KEY_CHUNK
    n_idx = IDX_HEADS
    hpg = A_HEADS // A_KV_HEADS
    nt = (((1,), (1,)), ((), ()))

    limit = lim_ref[0]
    misc_t = misc_ref[...].T
    wscale = IDX_HEADS ** -0.5 * IDX_DIM ** -0.5
    w = [misc_t[MISC_WI + j:MISC_WI + j + 1, :] * wscale for j in range(n_idx)]
    qis = [qi_ref[:, j * LANE:(j + 1) * LANE] for j in range(n_idx)]

    def score_body(c, carry):
        off = pl.multiple_of(c * kc, kc)
        kic = ki_ref[pl.ds(off, kc), :]
        s = None
        for j in range(n_idx):
            lg = jnp.maximum(lax.dot_general(kic, qis[j], nt, preferred_element_type=F32), 0.0)
            s = w[j] * lg if s is None else s + w[j] * lg
        kidx = off + lax.broadcasted_iota(jnp.int32, (kc, qb), 0)
        s = jnp.where(kidx < limit, s, -jnp.inf)
        bits = lax.bitcast_convert_type(s, jnp.int32)
        key = jnp.where(bits < 0, bits ^ 0x7FFFFFFF, bits)
        key_ref[pl.ds(off, kc), :] = key
        hi_ref[pl.ds(off, kc), :] = jnp.right_shift(key, 16).astype(jnp.int16)
        return carry

    lax.fori_loop(0, nch, score_body, 0)

    def count_ge(cand):
        sub = 8

        def body(c, accs):
            off = pl.multiple_of(c * kc, kc)
            m = jnp.where(key_ref[pl.ds(off, kc), :] >= cand, 1.0, 0.0)
            return _add_tiles(accs, m, sub)

        accs = lax.fori_loop(0, nch, body, (jnp.zeros((sub, qb), F32),) * N_PARTIAL, unroll=True)
        return jnp.sum(sum(accs), axis=0, keepdims=True)

    def count_ge16(ref, cand32):
        cand = cand32.astype(jnp.int16)
        sub = 16

        def body(c, accs):
            off = pl.multiple_of(c * kc, kc)
            m = jnp.where(ref[pl.ds(off, kc), :] >= cand, jnp.int16(1), jnp.int16(0))
            return _add_tiles(accs, m, sub)

        accs = lax.fori_loop(0, nch, body, (jnp.zeros((sub, qb), jnp.int16),) * N_PARTIAL, unroll=True)
        return jnp.sum(sum(accs).astype(F32), axis=0, keepdims=True)

    def kth_largest16(ref, want):
        tau = jnp.where(count_ge16(ref, jnp.zeros((1, qb), jnp.int32)) >= want, 0, I16_MIN).astype(jnp.int32)

        def bisect(i, tau):
            cand = tau | jnp.left_shift(jnp.int32(1), 14 - i)
            return jnp.where(count_ge16(ref, cand) >= want, cand, tau)

        return lax.fori_loop(0, 15, bisect, tau)

    want = float(n_sel)
    tau_hi = kth_largest16(hi_ref, want)
    above = jnp.where(tau_hi < I16_MAX, count_ge16(hi_ref, jnp.minimum(tau_hi + 1, I16_MAX)), 0.0)

    def low_body(c, carry):
        off = pl.multiple_of(c * kc, kc)
        key = key_ref[pl.ds(off, kc), :]
        low = (key & 0xFFFF) + I16_MIN
        lo_ref[pl.ds(off, kc), :] = jnp.where(jnp.right_shift(key, 16) == tau_hi, low, I16_MIN).astype(jnp.int16)
        return carry

    lax.fori_loop(0, nch, low_body, 0)
    tau_lo = kth_largest16(lo_ref, want - above)
    tau = jnp.left_shift(tau_hi, 16) + (tau_lo - I16_MIN)

    room = want - count_ge(tau + 1)
    r_i = lax.broadcasted_iota(jnp.int32, (LANE, LANE), 0)
    c_i = lax.broadcasted_iota(jnp.int32, (LANE, LANE), 1)
    prefix_ones = jnp.where(r_i >= c_i, 1.0, 0.0).astype(BF16)
    identity = jnp.where(r_i == c_i, 1.0, 0.0).astype(BF16)

    def bias_body(c, seen):
        off = pl.multiple_of(c * kc, kc)
        tiles = range(kc // LANE)
        xs = [key_ref[pl.ds(off + t * LANE, LANE), :] for t in tiles]
        eqs = [x == tau for x in xs]
        eqfs = [jnp.where(eq, 1.0, 0.0) for eq in eqs]
        ranks = [jnp.dot(prefix_ones, eqf.astype(BF16), preferred_element_type=F32) for eqf in eqfs]
        sels = []
        for t in tiles:
            sel = ((xs[t] > tau) | (eqs[t] & (ranks[t] + seen <= room))) & (xs[t] != KEY_OF_NEG_INF)
            sels.append(jnp.where(sel, 1.0, 0.0).astype(BF16))
            seen = seen + jnp.sum(eqfs[t], axis=0, keepdims=True)
        sel_ts = [lax.dot_general(sel, identity, (((0,), (0,)), ((), ())), preferred_element_type=F32)
                  for sel in sels]
        for t in tiles:
            bias_ref[:, pl.ds(off + t * LANE, LANE)] = jnp.where(sel_ts[t] > 0.5, 0.0, MASKED)
        return seen

    lax.fori_loop(0, nch, bias_body, jnp.zeros((1, qb), F32))

    lane = lax.broadcasted_iota(jnp.int32, (qb, LANE), 1)
    qgs = [jnp.concatenate([q_ref[:, (hpg * g + h) * LANE:(hpg * g + h + 1) * LANE] for h in range(hpg)], axis=0)
           for g in range(A_KV_HEADS)]

    def att_body(c, carry):
        off = pl.multiple_of(c * kc, kc)
        kch = k_ref[pl.ds(off, kc), :]
        vch = v_ref[pl.ds(off, kc), :]
        bias = bias_ref[:, pl.ds(off, kc)][None]
        new = []
        for g in range(A_KV_HEADS):
            m, l, acc = carry[g]
            s = lax.dot_general(qgs[g], kch, nt, preferred_element_type=F32)
            s = (s.reshape(hpg, qb, kc) + bias).reshape(hpg * qb, kc)
            m_new = jnp.maximum(m, jnp.max(s, axis=1, keepdims=True))
            alpha = jnp.exp(m - m_new)
            p = jnp.exp(s - m_new)
            l = alpha * l + jnp.sum(p, axis=1, keepdims=True)
            acc = alpha * acc + jnp.dot(p.astype(BF16), vch, preferred_element_type=F32)
            new.append((m_new, l, acc))
        return tuple(new)

    init = tuple((jnp.full((hpg * qb, 1), MASKED, F32), jnp.zeros((hpg * qb, 1), F32),
                  jnp.zeros((hpg * qb, LANE), F32)) for _ in range(A_KV_HEADS))
    res = lax.fori_loop(0, nch, att_body, init)
    outs = []
    for g in range(A_KV_HEADS):
        _, l, acc = res[g]
        og = acc / l
        for h in range(hpg):
            oh = og[h * qb:(h + 1) * qb]
            outs.append(oh if (h % 2) == g else pltpu.roll(oh, HALF, 1))
    for p in range(A_HEADS // 2):
        o_ref[:, p * LANE:(p + 1) * LANE] = jnp.where(lane < HALF, outs[2 * p], outs[2 * p + 1])


def dsa_attention(qpad, qipad, proj, limit, k16, v16, ki16, *, bsz, tq, tk, causal, n_sel):
    qb = min(Q_BLOCK, tq)
    nqb = tq // qb
    assert tk % KEY_CHUNK == 0 and tk >= n_sel
    if causal:
        per_group = KEY_CHUNK // qb
        groups = [(g * per_group, per_group, g + 1) for g in range(nqb // per_group)]
    else:
        groups = [(0, nqb, tk // KEY_CHUNK)]
    lim3 = limit.reshape(bsz * nqb, 1, qb)
    out = None
    for first, count, nch in groups:
        def qspec(width, col=0, first=first):
            return pl.BlockSpec((qb, width), lambda b, i: (b * nqb + first + i, col))

        kspec = pl.BlockSpec((tk, LANE), lambda b, i: (b, 0))
        in_specs = [qspec(A_HEADS * LANE), qspec(IDX_HEADS * LANE), qspec(LANE, AB_MISC // LANE),
                    pl.BlockSpec((1, 1, qb), lambda b, i, first=first: (b * nqb + first + i, 0, 0)),
                    kspec, kspec, kspec]
        args = [qpad, qipad, proj, lim3, k16, v16, ki16]
        kern = functools.partial(_dsa_kernel, nch=nch, n_sel=n_sel)
        aliases = {}
        if out is not None:
            in_specs.append(pl.BlockSpec(memory_space=pl.ANY))
            args.append(out)
            aliases = {len(args) - 1: 0}
            kern = functools.partial(_dsa_kernel_with_carry, nch=nch, n_sel=n_sel)
        out = pl.pallas_call(
            kern,
            grid=(bsz, count),
            in_specs=in_specs,
            out_specs=qspec(A_HEADS * HEAD_DIM),
            out_shape=jax.ShapeDtypeStruct((bsz * tq, A_HEADS * HEAD_DIM), F32),
            scratch_shapes=[pltpu.VMEM((tk, qb), jnp.int32), pltpu.VMEM((qb, tk), F32),
                            pltpu.VMEM((tk, qb), jnp.int16), pltpu.VMEM((tk, qb), jnp.int16)],
            input_output_aliases=aliases,
            compiler_params=pltpu.CompilerParams(
                dimension_semantics=("parallel", "arbitrary"), vmem_limit_bytes=VMEM_LIMIT),
            name="dsa_attention",
        )(*args)
    return out


def _dsa_kernel_with_carry(q_ref, qi_ref, misc_ref, lim_ref, k_ref, v_ref, ki_ref, carry_ref, o_ref, *scratch,
                           nch, n_sel):
    del carry_ref
    _dsa_kernel(q_ref, qi_ref, misc_ref, lim_ref, k_ref, v_ref, ki_ref, o_ref, *scratch, nch=nch, n_sel=n_sel)


SUB = 16


def _dot(a, b):
    return jnp.dot(a, b, preferred_element_type=F32)


def _dot_nt(a, b):
    return lax.dot_general(a, b, (((1,), (1,)), ((), ())), preferred_element_type=F32)


def _dot_tn(a, b):
    return lax.dot_general(a, b, (((0,), (0,)), ((), ())), preferred_element_type=F32)


def _split2(x):
    hi = x.astype(BF16)
    return hi, (x - hi.astype(F32)).astype(BF16)


def _split3(x):
    hi = x.astype(BF16)
    r = x - hi.astype(F32)
    mid = r.astype(BF16)
    return hi, mid, (r - mid.astype(F32)).astype(BF16)


def _cumsum_rows(x, tril16):
    hi, mid, lo = _split3(x)
    return _dot(tril16, hi) + _dot(tril16, mid) + _dot(tril16, lo)


def _dot_f32(a, b):
    ah, al = _split2(a)
    bh, bl = _split2(b)
    return _dot(ah, bh) + (_dot(ah, bl) + _dot(al, bh))


def _tri_mask(n, strict=False):
    r = lax.broadcasted_iota(jnp.int32, (n, n), 0)
    c = lax.broadcasted_iota(jnp.int32, (n, n), 1)
    return r > c if strict else r >= c


def _rows_to_lanes(x):
    rows = x.shape[0]
    if rows < LANE:
        x = jnp.concatenate([x, jnp.zeros((LANE - rows, LANE), x.dtype)], axis=0)
    return x.T


SEQS_PER_STEP = 2


def _lockstep(stages):
    stages = list(stages)
    while stages:
        for g in list(stages):
            if next(g, StopIteration) is StopIteration:
                stages.remove(g)


def _chunk_call(kern, *, bsz, nc, rows, ins, outs, scratch, name):
    nb = SEQS_PER_STEP

    def spec(a, kind, width, offset):
        if kind == 'rows':
            return pl.BlockSpec((nb, rows, width), lambda b, c: (b, c, offset // width))
        if kind == 'batch':
            return pl.BlockSpec((nb,) + tuple(a.shape[1:]), lambda b, c: (b,) + (0,) * (len(a.shape) - 1))
        return pl.BlockSpec(tuple(a.shape), lambda b, c: (0,) * len(a.shape))

    return pl.pallas_call(
        kern,
        grid=(bsz // nb, nc),
        in_specs=[spec(*i) for i in ins],
        out_specs=[spec(*o) for o in outs],
        out_shape=[o[0] for o in outs],
        scratch_shapes=scratch,
        compiler_params=pltpu.CompilerParams(
            dimension_semantics=("parallel", "arbitrary"), vmem_limit_bytes=VMEM_LIMIT),
        name=name,
    )(*[i[0] for i in ins])


def _mlstm_kernel(q_ref, k_ref, v_ref, og_ref, misc_ref, gb_ref, gain_ref, c0_ref, n0_ref, m0_ref,
                  h_ref, c_out_ref, n_out_ref, m_out_ref, c_scr, n_scr, m_scr):
    ci = pl.program_id(1)
    seqs = range(q_ref.shape[0])

    @pl.when(ci == 0)
    def _():
        c_scr[...] = c0_ref[...]
        n_scr[...] = n0_ref[...]
        m_scr[...] = m0_ref[...]

    _lockstep(_mlstm_chunk(q_ref.at[i], k_ref.at[i], v_ref.at[i], og_ref.at[i], misc_ref.at[i], gb_ref, gain_ref,
                           h_ref.at[i], c_scr.at[i], n_scr.at[i], m_scr.at[i]) for i in seqs)

    @pl.when(ci == pl.num_programs(1) - 1)
    def _():
        c_out_ref[...] = c_scr[...]
        n_out_ref[...] = n_scr[...]
        m_out_ref[...] = m_scr[...]


def _mlstm_chunk(q_ref, k_ref, v_ref, og_ref, misc_ref, gb_ref, gain_ref, h_ref, c_scr, n_scr, m_scr):
    rows = q_ref.shape[0]
    hr = B_HEADS * rows
    wq = B_HEADS * B_QK_DIM
    tril16 = jnp.where(_tri_mask(rows), 1.0, 0.0).astype(BF16)
    gates = misc_ref[...] + gb_ref[...]
    bcum = _cumsum_rows(jax.nn.log_sigmoid(gates), tril16)
    yield
    m_all = m_scr[...]

    def stack(f):
        return jnp.concatenate([f(h) for h in range(B_HEADS)], axis=0)

    lane_q = lax.broadcasted_iota(jnp.int32, (rows, wq), 1)
    qx, kx = q_ref[...], k_ref[...]
    q_all = stack(lambda h: jnp.where(lane_q // B_QK_DIM == h, qx, 0.0))
    k_all = stack(lambda h: jnp.where(lane_q // B_QK_DIM == h, kx, 0.0)) * B_QK_DIM ** -0.5
    v_all = stack(lambda h: v_ref[:, h * B_V_DIM:(h + 1) * B_V_DIM])
    b_col = stack(lambda h: bcum[:, MISC_FB + h:MISC_FB + h + 1])
    i_col = stack(lambda h: gates[:, MISC_IB + h:MISC_IB + h + 1])
    m_col = stack(lambda h: jnp.broadcast_to(m_all[:, h:h + 1], (rows, 1)))
    b_end = stack(lambda h: jnp.broadcast_to(bcum[rows - 1:rows, MISC_FB + h:MISC_FB + h + 1], (rows, 1)))
    bi_row = jnp.broadcast_to(b_col - i_col, (hr, LANE)).T[0:1, :]

    r = lax.broadcasted_iota(jnp.int32, (hr, hr), 0)
    c = lax.broadcasted_iota(jnp.int32, (hr, hr), 1)
    incl = ((r // rows) == (c // rows)) & (r >= c)
    dmat = jnp.where(incl, b_col - bi_row, -jnp.inf)
    inter = b_col + m_col
    mrow = jnp.maximum(inter, jnp.max(dmat, axis=1, keepdims=True))
    w_state = jnp.exp(inter - mrow)
    q16 = q_all.astype(BF16)
    v16 = v_all.astype(BF16)
    scores = _dot_nt(q16, k_all.astype(BF16)) * jnp.exp(dmat - mrow)
    yield
    cs = c_scr[...]
    n_row = n_scr[...]
    num = _dot(scores.astype(BF16), v16) + w_state * _dot(q16, cs.astype(BF16))
    den = jnp.sum(scores, axis=1, keepdims=True) + w_state * jnp.sum(q_all * n_row, axis=1, keepdims=True)
    hh = num / jnp.maximum(jnp.abs(den), jnp.exp(-mrow))
    yield
    gain = gain_ref[...]
    for h in range(B_HEADS):
        h_ref[:, h * B_V_DIM:(h + 1) * B_V_DIM] = (_rms_rows(hh[h * rows:(h + 1) * rows], gain)
                                                   * jax.nn.sigmoid(og_ref[:, h * B_V_DIM:(h + 1) * B_V_DIM]))

    g_col = b_end - b_col + i_col
    lane1 = lax.broadcasted_iota(jnp.int32, (1, LANE), 1)
    m_next = m_all
    m_new_rows, keep_rows, keep_lanes = [], [], []
    for h in range(B_HEADS):
        m_h = m_all[:, h:h + 1]
        be = bcum[rows - 1:rows, MISC_FB + h:MISC_FB + h + 1]
        m_new = jnp.maximum(be + m_h, jnp.max(g_col[h * rows:(h + 1) * rows], axis=0, keepdims=True))
        keep = jnp.exp(be + m_h - m_new)
        m_next = jnp.where(lane1 == h, m_new, m_next)
        m_new_rows.append(jnp.broadcast_to(m_new, (rows, 1)))
        keep_rows.append(jnp.broadcast_to(keep, (B_QK_DIM, 1)))
        keep_lanes.append(jnp.broadcast_to(keep, (1, B_QK_DIM)))
    kw = k_all * jnp.exp(g_col - jnp.concatenate(m_new_rows, axis=0))
    c_scr[...] = jnp.concatenate(keep_rows, axis=0) * cs + _dot_tn(kw.astype(BF16), v16)
    n_scr[...] = jnp.concatenate(keep_lanes, axis=1) * n_row + jnp.sum(kw, axis=0, keepdims=True)
    m_scr[...] = m_next
    yield


def mlstm_mixer(proj, gate_bias, gain, c0, n0, m0, *, bsz, t, chunk):
    nc = t // chunk
    nb = SEQS_PER_STEP
    gb = jnp.zeros((1, LANE), F32)
    gb = gb.at[0, MISC_IB:MISC_IB + B_HEADS].set(gate_bias[0]).at[0, MISC_FB:MISC_FB + B_HEADS].set(gate_bias[1])
    c0 = c0.reshape(bsz, B_HEADS * B_QK_DIM, B_V_DIM)
    n0 = n0.reshape(bsz, 1, B_HEADS * B_QK_DIM)
    m0 = jnp.pad(m0, ((0, 0), (0, LANE - B_HEADS))).reshape(bsz, 1, LANE)
    wq = B_HEADS * B_QK_DIM
    wv = B_HEADS * B_V_DIM
    h, c, n, m = _chunk_call(
        _mlstm_kernel, bsz=bsz, nc=nc, rows=chunk,
        ins=[(proj, 'rows', wq, AB_QB), (proj, 'rows', wq, AB_KB), (proj, 'rows', wv, AB_VB),
             (proj, 'rows', wv, AB_OB), (proj, 'rows', LANE, AB_MISC), (gb, 'const', 0, 0),
             (gain.reshape(1, B_V_DIM), 'const', 0, 0), (c0, 'batch', 0, 0), (n0, 'batch', 0, 0),
             (m0, 'batch', 0, 0)],
        outs=[(jax.ShapeDtypeStruct((bsz, t, wv), F32), 'rows', wv, 0),
              (jax.ShapeDtypeStruct(c0.shape, F32), 'batch', 0, 0),
              (jax.ShapeDtypeStruct(n0.shape, F32), 'batch', 0, 0),
              (jax.ShapeDtypeStruct(m0.shape, F32), 'batch', 0, 0)],
        scratch=[pltpu.VMEM((nb, wq, B_V_DIM), F32), pltpu.VMEM((nb, 1, wq), F32), pltpu.VMEM((nb, 1, LANE), F32)],
        name="mlstm_mixer")
    return (h.reshape(bsz * t, wv), c.reshape(bsz, B_HEADS, B_QK_DIM, B_V_DIM), n.reshape(bsz, B_HEADS, B_QK_DIM),
            m.reshape(bsz, LANE)[:, :B_HEADS])


CD_QKV, CD_ZC, CD_QD, CD_FD, CD_VD, CD_GD, CD_MISC, CD_TOTAL = 0, 1536, 2048, 2560, 3072, 3584, 4096, 4224
MISC_BC, MISC_AC = 0, 4
TAIL = 8


def _permute_w_in_cd(w):
    qkv, bc, ac, zc, qd, fd, vd, gd = _split_cols(w, ODD_SPLITS)
    pad = jnp.zeros((w.shape[0], LANE - 2 * C_HEADS), w.dtype)
    return jnp.concatenate([qkv, zc, qd, fd, vd, gd, bc, ac, pad], axis=1)


def _gdn_init(s0_ref, tail0_ref, st_scr, tail_scr):
    for h in range(C_HEADS):
        st_scr[:, h * C_DIM:(h + 1) * C_DIM] = s0_ref[h].T
    tail_scr[...] = tail0_ref[...]


def _gdn_final(s_out_ref, st_scr):
    for h in range(C_HEADS):
        s_out_ref[h] = st_scr[:, h * C_DIM:(h + 1) * C_DIM].T


def _gdn_chunk(qkv_ref, z_ref, misc_ref, cw_ref, alog_ref, dt_ref, gain_ref, o_ref, st_scr, tail_scr):
    rows = qkv_ref.shape[0]
    width = qkv_ref.shape[1]
    x = qkv_ref[...]
    tail = tail_scr[...]
    row8 = lax.broadcasted_iota(jnp.int32, (TAIL, width), 0)
    acc = x * cw_ref[CONV_W - 1:CONV_W, :]
    for back in range(1, CONV_W):
        rolled = pltpu.roll(x, back, 0)
        first = jnp.where(row8 < back, pltpu.roll(tail, back, 0), rolled[0:TAIL])
        shifted = first if rows == TAIL else jnp.concatenate([first, rolled[TAIL:]], axis=0)
        acc = acc + shifted * cw_ref[CONV_W - 1 - back:CONV_W - back, :]
    tail_scr[...] = x[rows - TAIL:rows]
    conv = acc * jax.nn.sigmoid(acc)

    tril16 = jnp.where(_tri_mask(rows), 1.0, 0.0).astype(BF16)
    misc = misc_ref[...]
    beta_t = jax.nn.sigmoid(misc)
    g_t = -jnp.exp(alog_ref[...]) * jax.nn.softplus(misc + dt_ref[...])
    gcum = _cumsum_rows(g_t, tril16)
    yield

    hd = C_HEADS * C_DIM
    hr = C_HEADS * rows

    def stack(f):
        return jnp.concatenate([f(h) for h in range(C_HEADS)], axis=0)

    def l2n(v):
        return v * lax.rsqrt(jnp.sum(v * v, axis=-1, keepdims=True) + EPS)

    q_all = stack(lambda h: l2n(conv[:, h * C_DIM:(h + 1) * C_DIM])) * C_DIM ** -0.5
    k_all = stack(lambda h: l2n(conv[:, hd + h * C_DIM:hd + (h + 1) * C_DIM]))
    v_all = stack(lambda h: conv[:, 2 * hd + h * C_DIM:2 * hd + (h + 1) * C_DIM])
    beta = stack(lambda h: beta_t[:, MISC_BC + h:MISC_BC + h + 1])
    gc = stack(lambda h: gcum[:, MISC_AC + h:MISC_AC + h + 1])
    g_end = stack(lambda h: jnp.broadcast_to(gcum[rows - 1:rows, MISC_AC + h:MISC_AC + h + 1], (rows, 1)))
    gc_row = jnp.broadcast_to(gc, (hr, LANE)).T[0:1, :]

    r = lax.broadcasted_iota(jnp.int32, (hr, hr), 0)
    c = lax.broadcasted_iota(jnp.int32, (hr, hr), 1)
    same = (r // rows) == (c // rows)
    incl = same & (r >= c)
    strict = same & (r > c)
    decay = jnp.exp(jnp.where(incl, gc - gc_row, -jnp.inf))
    k16 = k_all.astype(BF16)
    a_mat = jnp.where(strict, beta * _dot_nt(k16, k16) * decay, 0.0)
    yield
    power = -a_mat
    inv = jnp.where(r == c, 1.0, 0.0) + power
    for _ in range(int(math.log2(rows)) - 1):
        power = _dot_f32(power, power)
        yield
        inv = inv + _dot_f32(inv, power)
        yield
    inv_hi, inv_lo = _split2(inv)
    rhs = jnp.concatenate([beta * v_all, beta * jnp.exp(gc) * k_all], axis=1).astype(BF16)
    w = _dot(inv_hi, rhs) + _dot(inv_lo, rhs)
    yield
    w_v, w_k = w[:, 0:C_DIM], w[:, C_DIM:2 * C_DIM]
    qk = _dot_nt(q_all.astype(BF16), k16) * decay
    yield

    head_of_row = lax.broadcasted_iota(jnp.int32, (hr, C_DIM), 0) // rows

    def per_head_lanes(m):
        return jnp.concatenate([jnp.where(head_of_row == h, m, 0.0) for h in range(C_HEADS)], axis=1).astype(BF16)

    st = st_scr[...]
    st16 = st.astype(BF16)
    delta = w_v - _dot_nt(per_head_lanes(w_k), st16)
    yield
    d16 = delta.astype(BF16)
    out = _dot_nt(per_head_lanes(q_all * jnp.exp(gc)), st16) + _dot(qk.astype(BF16), d16)
    keep = jnp.concatenate([jnp.broadcast_to(jnp.exp(gcum[rows - 1:rows, MISC_AC + h:MISC_AC + h + 1]), (1, C_DIM))
                            for h in range(C_HEADS)], axis=1)
    st_scr[...] = keep * st + _dot_tn(d16, per_head_lanes(k_all * jnp.exp(g_end - gc)))
    yield
    gain = gain_ref[...]
    for h in range(C_HEADS):
        z = z_ref[:, h * C_DIM:(h + 1) * C_DIM]
        o_ref[:, h * C_DIM:(h + 1) * C_DIM] = _rms_rows(out[h * rows:(h + 1) * rows], gain) * (z * jax.nn.sigmoid(z))


def _hgrn2_init(s0_ref, st_scr):
    for h in range(D_HEADS):
        st_scr[h] = s0_ref[h].T


def _hgrn2_final(s_out_ref, st_scr):
    for h in range(D_HEADS):
        s_out_ref[h] = st_scr[h].T


def _hgrn2_chunk(q_ref, f_ref, v_ref, g_ref, lb_ref, gain_ref, o_ref, st_scr):
    rows = q_ref.shape[0]
    tril16 = jnp.where(_tri_mask(rows), 1.0, 0.0).astype(BF16)
    lb = lb_ref[...]
    zf = f_ref[...]
    logf = jnp.logaddexp(jnp.log(lb), jnp.log1p(-lb) + jax.nn.log_sigmoid(zf))
    kd = (1.0 - lb) * jax.nn.sigmoid(-zf)
    qx = q_ref[...]
    qd = qx * jax.nn.sigmoid(qx)
    bcum = _cumsum_rows(logf, tril16)
    yield
    gain = gain_ref[...]
    row_sub = lax.broadcasted_iota(jnp.int32, (SUB, 1), 0)
    for h in range(D_HEADS):
        sl = slice(h * D_EXPAND, (h + 1) * D_EXPAND)
        q, k, b = qd[:, sl], kd[:, sl], bcum[:, sl]
        v = v_ref[:, h * D_V_DIM:(h + 1) * D_V_DIM]
        v16 = v.astype(BF16)
        st = st_scr[h]
        inter = _dot_nt((q * jnp.exp(b)).astype(BF16), st.astype(BF16))
        blocks = []
        for i in range(rows // SUB):
            r0 = i * SUB
            qi, bi = q[r0:r0 + SUB], b[r0:r0 + SUB]
            oi = inter[r0:r0 + SUB]
            if i > 0:
                ref = b[r0 - 1:r0]
                att = _dot_nt((qi * jnp.exp(bi - ref)).astype(BF16),
                              (k[0:r0] * jnp.exp(ref - b[0:r0])).astype(BF16))
                oi = oi + _dot(att.astype(BF16), v16[0:r0])
            for s in range(SUB):
                r = r0 + s
                a = jnp.sum(qi * jnp.exp(bi - b[r:r + 1]) * k[r:r + 1], axis=1, keepdims=True)
                oi = oi + jnp.where(row_sub >= s, a, 0.0) * v[r:r + 1]
            blocks.append(oi)
        out = blocks[0] if len(blocks) == 1 else jnp.concatenate(blocks, axis=0)
        b_end = b[rows - 1:rows]
        st_scr[h] = jnp.exp(b_end) * st + _dot_tn(v16, (k * jnp.exp(b_end - b)).astype(BF16))
        g = g_ref[:, h * D_V_DIM:(h + 1) * D_V_DIM]
        o_ref[:, h * D_V_DIM:(h + 1) * D_V_DIM] = _rms_rows(out, gain) * (g * jax.nn.sigmoid(g))
        yield


def _cd_kernel(qkv_ref, z_ref, misc_ref, cw_ref, alog_ref, dt_ref, cgain_ref, sc0_ref, tail0_ref,
               qd_ref, fd_ref, vd_ref, gd_ref, lb_ref, dgain_ref, sd0_ref,
               oc_ref, sc_out_ref, od_ref, sd_out_ref, stc_scr, tail_scr, std_scr):
    ci = pl.program_id(1)
    seqs = range(qkv_ref.shape[0])

    @pl.when(ci == 0)
    def _():
        for i in seqs:
            _gdn_init(sc0_ref.at[i], tail0_ref.at[i], stc_scr.at[i], tail_scr.at[i])
            _hgrn2_init(sd0_ref.at[i], std_scr.at[i])

    gdn = [_gdn_chunk(qkv_ref.at[i], z_ref.at[i], misc_ref.at[i], cw_ref, alog_ref, dt_ref, cgain_ref,
                      oc_ref.at[i], stc_scr.at[i], tail_scr.at[i]) for i in seqs]
    hgrn2 = [_hgrn2_chunk(qd_ref.at[i], fd_ref.at[i], vd_ref.at[i], gd_ref.at[i], lb_ref, dgain_ref, od_ref.at[i],
                          std_scr.at[i]) for i in seqs]
    _lockstep(gdn + hgrn2)

    @pl.when(ci == pl.num_programs(1) - 1)
    def _():
        for i in seqs:
            _gdn_final(sc_out_ref.at[i], stc_scr.at[i])
            _hgrn2_final(sd_out_ref.at[i], std_scr.at[i])


def cd_mixers(proj, conv_w, a_log, dt_bias, c_gain, sc0, conv_prev, lower_bound, d_gain, sd0, *, bsz, t, chunk):
    nc = t // chunk
    nb = SEQS_PER_STEP
    hd = C_HEADS * C_DIM
    wk = D_HEADS * D_EXPAND
    wv = D_HEADS * D_V_DIM
    lanes = jnp.zeros((1, LANE), F32)
    alog = lanes.at[0, MISC_AC:MISC_AC + C_HEADS].set(a_log)
    dt = lanes.at[0, MISC_AC:MISC_AC + C_HEADS].set(dt_bias)
    tail0 = jnp.pad(conv_prev, ((0, 0), (TAIL - (CONV_W - 1), 0), (0, 0)))
    oc, sc, od, sd = _chunk_call(
        _cd_kernel, bsz=bsz, nc=nc, rows=chunk,
        ins=[(proj, 'rows', 3 * hd, CD_QKV), (proj, 'rows', hd, CD_ZC), (proj, 'rows', LANE, CD_MISC),
             (conv_w, 'const', 0, 0), (alog, 'const', 0, 0), (dt, 'const', 0, 0),
             (c_gain.reshape(1, C_DIM), 'const', 0, 0), (sc0, 'batch', 0, 0), (tail0, 'batch', 0, 0),
             (proj, 'rows', wk, CD_QD), (proj, 'rows', wk, CD_FD), (proj, 'rows', wv, CD_VD),
             (proj, 'rows', wv, CD_GD), (lower_bound.reshape(1, wk), 'const', 0, 0),
             (d_gain.reshape(1, D_V_DIM), 'const', 0, 0), (sd0, 'batch', 0, 0)],
        outs=[(jax.ShapeDtypeStruct((bsz, t, hd), F32), 'rows', hd, 0),
              (jax.ShapeDtypeStruct(sc0.shape, F32), 'batch', 0, 0),
              (jax.ShapeDtypeStruct((bsz, t, wv), F32), 'rows', wv, 0),
              (jax.ShapeDtypeStruct(sd0.shape, F32), 'batch', 0, 0)],
        scratch=[pltpu.VMEM((nb, C_DIM, hd), F32), pltpu.VMEM((nb, TAIL, 3 * hd), F32),
                 pltpu.VMEM((nb, D_HEADS, D_V_DIM, D_EXPAND), F32)],
        name="cd_mixers")
    return oc.reshape(bsz * t, hd), sc, od.reshape(bsz * t, wv), sd


def _pad_cols(w, mult=LANE):
    pad = (-w.shape[-1]) % mult
    return jnp.pad(w, [(0, 0)] * (w.ndim - 1) + [(0, pad)])


def _mixer_ab(proj, bsz, t, pos, prm, cache):
    n = bsz * t
    qpad, qipad, k16, v16, ki16, k32, v32, ki32 = dsa_prep(proj, pos, prm['a_q_gain'][0], prm['a_k_gain'][0], t)
    if cache is None:
        limit = jnp.tile((pos // CHUNK + 1) * CHUNK, bsz).reshape(n, 1)
        a_out = dsa_attention(qpad, qipad, proj, limit, k16, v16, ki16, bsz=bsz, tq=t, tk=t,
                              causal=True, n_sel=min(TOPK_MAX, t // 4))
        c0 = jnp.zeros((bsz, B_HEADS, B_QK_DIM, B_V_DIM), F32)
        n0 = jnp.zeros((bsz, B_HEADS, B_QK_DIM), F32)
        m0 = jnp.zeros((bsz, B_HEADS), F32)
        chunk = CHUNK
    else:
        k_c, v_c, ki_c, c0, n0, m0 = cache
        past = k_c.shape[1]
        n_keys = past + t
        tk = -(-n_keys // KEY_CHUNK) * KEY_CHUNK

        def with_cache(c, new):
            c = c.reshape(bsz, past, -1).astype(BF16)
            c = jnp.pad(c, ((0, 0), (0, 0), (0, LANE - c.shape[-1])))
            return jnp.concatenate([c, new.reshape(bsz, t, LANE),
                                    jnp.zeros((bsz, tk - n_keys, LANE), BF16)], axis=1).reshape(bsz * tk, LANE)

        limit = jnp.full((n, 1), n_keys, jnp.int32)
        a_out = dsa_attention(qpad, qipad, proj, limit, with_cache(k_c, k16), with_cache(v_c, v16),
                              with_cache(ki_c, ki16), bsz=bsz, tq=t, tk=tk, causal=False,
                              n_sel=min(TOPK_MAX, n_keys // 4))
        chunk = t
    h, c, n_, m = mlstm_mixer(proj.reshape(bsz, t, -1), prm['b_gate_bias'][0], prm['b_norm_gain'][0], c0, n0, m0,
                              bsz=bsz, t=t, chunk=chunk)
    st = (k32.reshape(bsz, t, A_KV_HEADS, HEAD_DIM), v32.reshape(bsz, t, A_KV_HEADS, HEAD_DIM),
          ki32.reshape(bsz, t, IDX_DIM), c, n_, m)
    return a_out, h, st


def _mixer_cd(proj, bsz, t, prm, lower_bound, cache):
    hd = C_HEADS * C_DIM
    if cache is None:
        sc0 = jnp.zeros((bsz, C_HEADS, C_DIM, C_DIM), F32)
        conv_prev = jnp.zeros((bsz, CONV_W - 1, 3 * hd), F32)
        sd0 = jnp.zeros((bsz, D_HEADS, D_EXPAND, D_V_DIM), F32)
        chunk = CHUNK
    else:
        sc0, conv_prev, sd0 = cache
        chunk = t
    oc, sc, od, sd = cd_mixers(proj.reshape(bsz, t, -1), prm['c_conv_w'][0], prm['c_a_log'][0], prm['c_dt_bias'][0],
                               prm['c_norm_gain'][0], sc0, conv_prev, lower_bound, prm['d_norm_gain'][0], sd0,
                               bsz=bsz, t=t, chunk=chunk)
    qkv = proj.reshape(bsz, t, -1)[:, :, CD_QKV:CD_QKV + 3 * hd]
    conv_new = jnp.concatenate([conv_prev, qkv[:, t - (CONV_W - 1):]], axis=1)[:, -(CONV_W - 1):]
    return oc, od, (sc, conv_new, sd)


def _trunk(x, pos_offset, cache, prm, wts):
    bsz, t, d = x.shape
    n = bsz * t
    pos = pos_offset + jnp.arange(t, dtype=jnp.int32)
    probs = jax.nn.softmax(prm['d_lb_logits'], axis=0)
    lower_bounds = jnp.cumsum(probs, axis=0) - probs[0]
    xf = x.reshape(n, d)

    lc = None if cache is None else tuple(c[0] for c in cache[:6])
    proj = norm_matmul(xf, prm['norm_mix'][0], wts['w_in_ab'])
    a_out, b_out, st_even = _mixer_ab(proj, bsz, t, pos, prm, lc)
    xf = matmul_residual(a_out, b_out, wts['w_out_ab'], xf)
    xf = ffn_residual(xf, prm['norm_ffn'][0], wts['ffn_w1'], wts['ffn_w3'], wts['ffn_w2'])

    lc = None if cache is None else tuple(c[0] for c in cache[6:])
    proj = norm_matmul(xf, prm['norm_mix'][1], wts['w_in_cd'])
    c_out, d_out, st_odd = _mixer_cd(proj, bsz, t, prm, lower_bounds[1], lc)
    xf = matmul_residual(c_out, d_out, wts['w_out_cd'], xf)
    xf = moe_residual(xf, prm['norm_ffn'][1], wts['moe_router'], wts['moe_w1'], wts['moe_w3'], wts['moe_w2'])

    new_state = tuple(s[None] for s in st_even + st_odd)
    return xf.reshape(bsz, t, d), new_state


def kernel(x_prompt, x_sample, cache_a_k, cache_a_v, cache_a_kidx, state_b_c, state_b_n, state_b_m,
           state_c_s, state_c_conv, state_d_s, norm_mix, norm_ffn, w_in_ab, w_out_ab, a_q_gain, a_k_gain,
           b_gate_bias, b_norm_gain, w_in_cd, w_out_cd, c_conv_w, c_a_log, c_dt_bias, c_norm_gain,
           d_lb_logits, d_norm_gain, ffn_w1, ffn_w3, ffn_w2, moe_router, moe_w1, moe_w3, moe_w2):
    prm = dict(norm_mix=norm_mix, norm_ffn=norm_ffn, a_q_gain=a_q_gain, a_k_gain=a_k_gain,
               b_gate_bias=b_gate_bias, b_norm_gain=b_norm_gain, c_conv_w=c_conv_w, c_a_log=c_a_log,
               c_dt_bias=c_dt_bias, c_norm_gain=c_norm_gain, d_lb_logits=d_lb_logits, d_norm_gain=d_norm_gain)
    wts = dict(w_in_ab=_permute_w_in_ab(w_in_ab[0]).astype(BF16), w_out_ab=w_out_ab[0].astype(BF16),
               w_in_cd=_permute_w_in_cd(w_in_cd[0]).astype(BF16), w_out_cd=w_out_cd[0].astype(BF16),
               ffn_w1=ffn_w1[0].astype(BF16), ffn_w3=ffn_w3[0].astype(BF16), ffn_w2=ffn_w2[0].astype(BF16),
               moe_router=_pad_cols(moe_router[0]),
               moe_w1=moe_w1[0].astype(BF16), moe_w3=moe_w3[0].astype(BF16), moe_w2=moe_w2[0].astype(BF16))
    cache = (cache_a_k, cache_a_v, cache_a_kidx, state_b_c, state_b_n, state_b_m, state_c_s, state_c_conv, state_d_s)
    y_prompt, st_p = _trunk(x_prompt, 0, None, prm, wts)
    y_sample, st_s = _trunk(x_sample, cache_a_k.shape[2], cache, prm, wts)
    return (y_prompt, y_sample) + st_p + st_s
```

```python
import functools
import math

import jax
import jax.numpy as jnp
import numpy as np
from jax import lax
from jax.experimental import pallas as pl
from jax.experimental.pallas import tpu as pltpu

F32 = jnp.float32
BF16 = jnp.bfloat16

EPS = 1e-6
ROPE_THETA = 500000.0
ROT_FRACTION = 4
CHUNK = 64
A_HEADS, A_KV_HEADS, HEAD_DIM = 8, 2, 64
IDX_HEADS, IDX_DIM = 4, 64
TOPK_MAX, Q_BLOCK = 256, 128
B_HEADS, B_QK_DIM, B_V_DIM = 4, 64, 128
C_HEADS, C_DIM, CONV_W = 4, 128, 4
D_HEADS, D_EXPAND, D_V_DIM = 4, 128, 128
N_EXPERTS, TOP_K_EXPERTS = 8, 2

LANE = 128
VMEM_LIMIT = 48 * 1024 * 1024

EVEN_SPLITS = (A_HEADS * HEAD_DIM, A_KV_HEADS * HEAD_DIM, A_KV_HEADS * HEAD_DIM,
               IDX_HEADS * IDX_DIM, IDX_DIM, IDX_HEADS,
               B_HEADS * B_QK_DIM, B_HEADS * B_QK_DIM, B_HEADS * B_V_DIM,
               B_HEADS, B_HEADS, B_HEADS * B_V_DIM)
ODD_SPLITS = (3 * C_HEADS * C_DIM, C_HEADS, C_HEADS, C_HEADS * C_DIM,
              D_HEADS * D_EXPAND, D_HEADS * D_EXPAND, D_HEADS * D_V_DIM, D_HEADS * D_V_DIM)


def _split_cols(p, widths):
    cuts = [int(c) for c in np.cumsum(widths)[:-1]]
    return jnp.split(p, cuts, axis=-1)


def _row_tile(n, target):
    t = min(n, target)
    while n % t:
        t //= 2
    return t


def _col_tile(n, target):
    best = LANE
    for k in range(1, n // LANE + 1):
        c = k * LANE
        if n % c == 0 and c <= target:
            best = c
    return best


def _rms_rows(x, gain):
    return x * lax.rsqrt(jnp.mean(x * x, axis=-1, keepdims=True) + EPS) * gain


def _norm_matmul_kernel(x_ref, g_ref, w_ref, o_ref, xn_ref):
    @pl.when(pl.program_id(1) == 0)
    def _():
        xn_ref[...] = _rms_rows(x_ref[...], g_ref[...]).astype(BF16)

    o_ref[...] = jnp.dot(xn_ref[...], w_ref[...], preferred_element_type=F32)


def norm_matmul(x, gain, w):
    n, d = x.shape
    m = w.shape[1]
    tm = _row_tile(n, 1024)
    tn = _col_tile(m, 1536)
    return pl.pallas_call(
        _norm_matmul_kernel,
        grid=(n // tm, m // tn),
        in_specs=[pl.BlockSpec((tm, d), lambda i, j: (i, 0)),
                  pl.BlockSpec((1, d), lambda i, j: (0, 0)),
                  pl.BlockSpec((d, tn), lambda i, j: (0, j))],
        out_specs=pl.BlockSpec((tm, tn), lambda i, j: (i, j)),
        out_shape=jax.ShapeDtypeStruct((n, m), F32),
        scratch_shapes=[pltpu.VMEM((tm, d), BF16)],
        compiler_params=pltpu.CompilerParams(
            dimension_semantics=("parallel", "arbitrary"), vmem_limit_bytes=VMEM_LIMIT),
        name="norm_matmul",
    )(x, gain.reshape(1, d), w)


def _matmul_res_kernel(a_ref, b_ref, w_ref, r_ref, o_ref):
    ka = a_ref.shape[1]
    o_ref[...] = (r_ref[...] + jnp.dot(a_ref[...].astype(BF16), w_ref[0:ka, :], preferred_element_type=F32)
                  + jnp.dot(b_ref[...].astype(BF16), w_ref[ka:, :], preferred_element_type=F32))


def matmul_residual(a, b, w, res):
    n, ka = a.shape
    kb = b.shape[1]
    m = w.shape[1]
    tm = _row_tile(n, 1024)
    return pl.pallas_call(
        _matmul_res_kernel,
        grid=(n // tm,),
        in_specs=[pl.BlockSpec((tm, ka), lambda i: (i, 0)),
                  pl.BlockSpec((tm, kb), lambda i: (i, 0)),
                  pl.BlockSpec((ka + kb, m), lambda i: (0, 0)),
                  pl.BlockSpec((tm, m), lambda i: (i, 0))],
        out_specs=pl.BlockSpec((tm, m), lambda i: (i, 0)),
        out_shape=jax.ShapeDtypeStruct((n, m), F32),
        compiler_params=pltpu.CompilerParams(
            dimension_semantics=("parallel",), vmem_limit_bytes=VMEM_LIMIT),
        name="matmul_residual",
    )(a, b, w, res)


def _swiglu_tile(xn, w1, w3):
    h1 = jnp.dot(xn, w1, preferred_element_type=F32)
    h3 = jnp.dot(xn, w3, preferred_element_type=F32)
    return h1 * jax.nn.sigmoid(h1) * h3


def _ffn_kernel(x_ref, g_ref, w1_ref, w3_ref, w2_ref, o_ref, xn_ref):
    @pl.when(pl.program_id(1) == 0)
    def _():
        x = x_ref[...]
        xn_ref[...] = _rms_rows(x, g_ref[...]).astype(BF16)
        o_ref[...] = x

    act = _swiglu_tile(xn_ref[...], w1_ref[...], w3_ref[...])
    o_ref[...] += jnp.dot(act.astype(BF16), w2_ref[...], preferred_element_type=F32)


def ffn_residual(x, gain, w1, w3, w2):
    n, d = x.shape
    f = w1.shape[1]
    tm = _row_tile(n, 1024)
    tf = _col_tile(f, 512)
    return pl.pallas_call(
        _ffn_kernel,
        grid=(n // tm, f // tf),
        in_specs=[pl.BlockSpec((tm, d), lambda i, j: (i, 0)),
                  pl.BlockSpec((1, d), lambda i, j: (0, 0)),
                  pl.BlockSpec((d, tf), lambda i, j: (0, j)),
                  pl.BlockSpec((d, tf), lambda i, j: (0, j)),
                  pl.BlockSpec((tf, d), lambda i, j: (j, 0))],
        out_specs=pl.BlockSpec((tm, d), lambda i, j: (i, 0)),
        out_shape=jax.ShapeDtypeStruct((n, d), F32),
        scratch_shapes=[pltpu.VMEM((tm, d), BF16)],
        compiler_params=pltpu.CompilerParams(
            dimension_semantics=("parallel", "arbitrary"), vmem_limit_bytes=VMEM_LIMIT),
        name="ffn_residual",
    )(x, gain.reshape(1, d), w1, w3, w2)


MOE_VMEM_LIMIT = 58 * 1024 * 1024
MOE_CAPS = (256, 288, 320, 384, 512)


def _moe_route_kernel(x_ref, g_ref, r_ref, xn_ref, comb_ref, post_ref, cnt_ref):
    x = x_ref[...]
    tm = x.shape[0]
    xn = _rms_rows(x, g_ref[...])
    xn_ref[...] = xn.astype(BF16)
    logits = jnp.dot(xn, r_ref[...], preferred_element_type=F32, precision=lax.Precision.HIGHEST)
    lane = lax.broadcasted_iota(jnp.int32, logits.shape, 1)
    logits = jnp.where(lane < N_EXPERTS, logits, -jnp.inf)
    m1 = jnp.max(logits, axis=-1, keepdims=True)
    i1 = jnp.min(jnp.where(logits == m1, lane, LANE), axis=-1, keepdims=True)
    rest = jnp.where(lane == i1, -jnp.inf, logits)
    m2 = jnp.max(rest, axis=-1, keepdims=True)
    i2 = jnp.min(jnp.where(rest == m2, lane, LANE), axis=-1, keepdims=True)
    e2 = jnp.exp(m2 - m1)
    den = 1.0 + e2
    comb_ref[...] = jnp.where(lane == i1, 1.0 / den, 0.0) + jnp.where(lane == i2, e2 / den, 0.0)
    chosen = (lane == i1) | (lane == i2)
    sel = jnp.where(chosen, 1.0, 0.0)
    tril16 = jnp.where(_tri_mask(LANE), 1.0, 0.0).astype(BF16)
    seen = jnp.zeros((1, LANE), F32)
    ranks = []
    for blk in range(tm // LANE):
        sb = sel[blk * LANE:(blk + 1) * LANE]
        ranks.append(_dot(tril16, sb.astype(BF16)) + seen - 1.0)
        seen = seen + jnp.sum(sb, axis=0, keepdims=True)
    rank = jnp.where(chosen, jnp.concatenate(ranks, axis=0), -1.0)
    post_ref[0] = rank.T[0:N_EXPERTS, :]
    cnt_ref[0] = seen


def _moe_expert_kernel(cnt_ref, xn_ref, comb_ref, post_ref, x_ref, w1_ref, w3_ref, w2_ref, o_ref,
                       xe_scr, y_scr, *, caps):
    i = pl.program_id(0)
    e = pl.program_id(1)
    j = pl.program_id(2)
    last = pl.num_programs(2) - 1
    tm = x_ref.shape[0]

    @pl.when(jnp.logical_and(e == 0, j == 0))
    def _():
        o_ref[...] = x_ref[...]

    cnt = cnt_ref[i * N_EXPERTS + e]

    def expert_step(cap):
        def pick():
            rank_row = post_ref[0, pl.ds(e, 1), :].astype(jnp.int32)
            slot = lax.broadcasted_iota(jnp.int32, (cap, tm), 0)
            return jnp.where(rank_row == slot, 1.0, 0.0).astype(BF16)

        @pl.when(j == 0)
        def _():
            xe_scr[0:cap, :] = _dot(pick(), xn_ref[...]).astype(BF16)

        act = _swiglu_tile(xe_scr[0:cap, :], w1_ref[0], w3_ref[0])
        yj = _dot(act.astype(BF16), w2_ref[0])

        @pl.when(j == 0)
        def _():
            y_scr[0:cap, :] = yj

        @pl.when(j > 0)
        def _():
            y_scr[0:cap, :] += yj

        @pl.when(j == last)
        def _():
            comb = comb_ref[...]
            lane = lax.broadcasted_iota(jnp.int32, comb.shape, 1)
            gate = jnp.sum(jnp.where(lane == e, comb, 0.0), axis=-1, keepdims=True)
            hi, lo = _split2(y_scr[0:cap, :])
            p = pick()
            o_ref[...] += gate * (_dot_tn(p, hi) + _dot_tn(p, lo))

    lo = 0
    for cap in caps:
        @pl.when(jnp.logical_and(cnt > lo, cnt <= cap))
        def _(cap=cap):
            expert_step(cap)
        lo = cap


def moe_residual(x, gain, router, w1, w3, w2):
    n, d = x.shape
    ne, _, f = w1.shape
    tm = _row_tile(n, 1024)
    tf = _col_tile(f, 896)
    nt = n // tm
    xn, comb, post, cnt = pl.pallas_call(
        _moe_route_kernel,
        grid=(nt,),
        in_specs=[pl.BlockSpec((tm, d), lambda i: (i, 0)),
                  pl.BlockSpec((1, d), lambda i: (0, 0)),
                  pl.BlockSpec((d, LANE), lambda i: (0, 0))],
        out_specs=[pl.BlockSpec((tm, d), lambda i: (i, 0)),
                   pl.BlockSpec((tm, LANE), lambda i: (i, 0)),
                   pl.BlockSpec((1, ne, tm), lambda i: (i, 0, 0)),
                   pl.BlockSpec((1, 1, LANE), lambda i: (i, 0, 0))],
        out_shape=[jax.ShapeDtypeStruct((n, d), BF16), jax.ShapeDtypeStruct((n, LANE), F32),
                   jax.ShapeDtypeStruct((nt, ne, tm), F32), jax.ShapeDtypeStruct((nt, 1, LANE), F32)],
        compiler_params=pltpu.CompilerParams(dimension_semantics=("parallel",), vmem_limit_bytes=VMEM_LIMIT),
        name="moe_route",
    )(x, gain.reshape(1, d), router)
    counts = cnt[:, 0, :ne].astype(jnp.int32).reshape(nt * ne)
    caps = tuple(c for c in MOE_CAPS if c < tm) + (tm,)
    grid_spec = pltpu.PrefetchScalarGridSpec(
        num_scalar_prefetch=1,
        grid=(nt, ne, f // tf),
        in_specs=[pl.BlockSpec((tm, d), lambda i, e, j, c: (i, 0)),
                  pl.BlockSpec((tm, LANE), lambda i, e, j, c: (i, 0)),
                  pl.BlockSpec((1, ne, tm), lambda i, e, j, c: (i, 0, 0)),
                  pl.BlockSpec((tm, d), lambda i, e, j, c: (i, 0)),
                  pl.BlockSpec((1, d, tf), lambda i, e, j, c: (e, 0, j)),
                  pl.BlockSpec((1, d, tf), lambda i, e, j, c: (e, 0, j)),
                  pl.BlockSpec((1, tf, d), lambda i, e, j, c: (e, j, 0))],
        out_specs=pl.BlockSpec((tm, d), lambda i, e, j, c: (i, 0)),
        scratch_shapes=[pltpu.VMEM((tm, d), BF16), pltpu.VMEM((tm, d), F32)])
    return pl.pallas_call(
        functools.partial(_moe_expert_kernel, caps=caps),
        grid_spec=grid_spec,
        out_shape=jax.ShapeDtypeStruct((n, d), F32),
        compiler_params=pltpu.CompilerParams(
            dimension_semantics=("parallel", "arbitrary", "arbitrary"), vmem_limit_bytes=MOE_VMEM_LIMIT),
        name="moe_experts",
    )(counts, xn, comb, post, x, w1, w3, w2)


AB_QA, AB_VB, AB_OB, AB_QI, AB_QB, AB_KB, AB_KA, AB_VA, AB_MISC, AB_TOTAL = (
    0, 512, 1024, 1536, 1792, 2048, 2304, 2432, 2560, 2688)
MISC_WI, MISC_IB, MISC_FB = 64, 68, 72
HALF = LANE // 2
KEY_CHUNK = 512
MASKED = -1e30
KEY_OF_NEG_INF = -2139095041
I16_MIN, I16_MAX = -32768, 32767


def _permute_w_in_ab(w):
    qa, ka, va, qi, ki, wi, qb, kb, vb, ib, fb, ob = _split_cols(w, EVEN_SPLITS)
    pad = jnp.zeros((w.shape[0], LANE - IDX_DIM - 3 * IDX_HEADS), w.dtype)
    return jnp.concatenate([qa, vb, ob, qi, qb, kb, ka, va, ki, wi, ib, fb, pad], axis=1)


def _rope_tables(pos):
    rot = HEAD_DIM // ROT_FRACTION
    half = rot // 2
    inv_freq = ROPE_THETA ** (-jnp.arange(half, dtype=F32) * 2.0 / rot)
    ang = pos.astype(F32)[:, None] * inv_freq[None, :]
    cos, sin = jnp.cos(ang), jnp.sin(ang)
    t = pos.shape[0]
    one = jnp.ones((t, HEAD_DIM - rot), F32)
    zero_r = jnp.zeros((t, HEAD_DIM - rot), F32)
    zero_h = jnp.zeros((t, half), F32)
    c = jnp.concatenate([cos, cos, one], axis=1)
    s_up = jnp.concatenate([-sin, zero_h, zero_r], axis=1)
    s_dn = jnp.concatenate([zero_h, sin, zero_r], axis=1)
    return tuple(jnp.concatenate([a, a], axis=1) for a in (c, s_up, s_dn))


def _rope_tile(x, c, s_up, s_dn):
    half = HEAD_DIM // ROT_FRACTION // 2
    return x * c + pltpu.roll(x, LANE - half, 1) * s_up + pltpu.roll(x, half, 1) * s_dn


def _head_norm_tile(x, gain, same_head):
    sq = x * x
    hi = sq.astype(BF16)
    lo = (sq - hi.astype(F32)).astype(BF16)
    ss = (jnp.dot(hi, same_head, preferred_element_type=F32)
          + jnp.dot(lo, same_head, preferred_element_type=F32))
    return x * lax.rsqrt(ss * (1.0 / HEAD_DIM) + EPS) * gain


def _aprep_kernel(qa_ref, ka_ref, va_ref, qi_ref, misc_ref, c_ref, su_ref, sd_ref, qg_ref, kg_ref,
                  qpad_ref, qipad_ref, k16_ref, v16_ref, ki16_ref, k32_ref, v32_ref, ki32_ref):
    c, su, sd = c_ref[...], su_ref[...], sd_ref[...]
    tm = c.shape[0]
    row = lax.broadcasted_iota(jnp.int32, (LANE, LANE), 0)
    col = lax.broadcasted_iota(jnp.int32, (LANE, LANE), 1)
    same_head = jnp.where(row // HALF == col // HALF, 1.0, 0.0).astype(BF16)
    lane = lax.broadcasted_iota(jnp.int32, (tm, LANE), 1)
    low = lane < HALF

    heads_per_group = A_HEADS // A_KV_HEADS
    for p in range(A_HEADS // 2):
        y = _rope_tile(_head_norm_tile(qa_ref[:, p * LANE:(p + 1) * LANE], qg_ref[...], same_head), c, su, sd)
        y = y * HEAD_DIM ** -0.5
        y_sw = pltpu.roll(y, HALF, 1)
        for o in range(2):
            h = 2 * p + o
            g = h // heads_per_group
            src = y if o == g else y_sw
            qpad_ref[:, h * LANE:(h + 1) * LANE] = jnp.where(low if g == 0 else ~low, src, 0.0).astype(BF16)
    k = _rope_tile(_head_norm_tile(ka_ref[...], kg_ref[...], same_head), c, su, sd)
    k32_ref[...] = k
    k16_ref[...] = k.astype(BF16)
    v = va_ref[...]
    v32_ref[...] = v
    v16_ref[...] = v.astype(BF16)
    for p in range(IDX_HEADS // 2):
        y = _rope_tile(qi_ref[:, p * LANE:(p + 1) * LANE], c, su, sd)
        y_sw = pltpu.roll(y, HALF, 1)
        qipad_ref[:, (2 * p) * LANE:(2 * p + 1) * LANE] = jnp.where(low, y, 0.0).astype(BF16)
        qipad_ref[:, (2 * p + 1) * LANE:(2 * p + 2) * LANE] = jnp.where(low, y_sw, 0.0).astype(BF16)
    ki = _rope_tile(misc_ref[...], c, su, sd)
    ki32_ref[...] = ki[:, :IDX_DIM]
    ki16_ref[...] = jnp.where(low, ki, 0.0).astype(BF16)


def dsa_prep(proj, pos, q_gain, k_gain, t):
    n = proj.shape[0]
    tm = _row_tile(n, 512)
    tabs = _rope_tables(pos)
    if t < tm:
        tabs = tuple(jnp.tile(a, (tm // t, 1)) for a in tabs)
    nt = tabs[0].shape[0] // tm
    tab_spec = pl.BlockSpec((tm, LANE), lambda i: (i % nt, 0))
    gain_spec = pl.BlockSpec((1, LANE), lambda i: (0, 0))

    def col(width, offset):
        return pl.BlockSpec((tm, width), lambda i: (i, offset // width))

    def out(width, dtype):
        return (jax.ShapeDtypeStruct((n, width), dtype), pl.BlockSpec((tm, width), lambda i: (i, 0)))

    outs = [out(A_HEADS * LANE, BF16), out(IDX_HEADS * LANE, BF16), out(LANE, BF16), out(LANE, BF16),
            out(LANE, BF16), out(LANE, F32), out(LANE, F32), out(IDX_DIM, F32)]
    return pl.pallas_call(
        _aprep_kernel,
        grid=(n // tm,),
        in_specs=[col(A_HEADS * HEAD_DIM, AB_QA), col(LANE, AB_KA), col(LANE, AB_VA),
                  col(IDX_HEADS * IDX_DIM, AB_QI), col(LANE, AB_MISC), tab_spec, tab_spec, tab_spec,
                  gain_spec, gain_spec],
        out_specs=[o[1] for o in outs],
        out_shape=[o[0] for o in outs],
        compiler_params=pltpu.CompilerParams(dimension_semantics=("parallel",), vmem_limit_bytes=VMEM_LIMIT),
        name="dsa_prep",
    )(proj, proj, proj, proj, proj, *tabs, jnp.tile(q_gain, 2).reshape(1, LANE), jnp.tile(k_gain, 2).reshape(1, LANE))


N_PARTIAL = 4


def _add_tiles(accs, m, sub):
    accs = list(accs)
    for t in range(m.shape[0] // sub):
        accs[t % len(accs)] = accs[t % len(accs)] + m[t * sub:(t + 1) * sub]
    return tuple(accs)


def _dsa_kernel(q_ref, qi_ref, misc_ref, lim_ref, k_ref, v_ref, ki_ref, o_ref, key_ref, bias_ref, hi_ref, lo_ref,
                *, nch, n_sel):
    qb = q_ref.shape[0]
    kc = KEY_CHUNK
    n_idx = IDX_HEADS
    hpg = A_HEADS // A_KV_HEADS
    nt = (((1,), (1,)), ((), ()))

    limit = lim_ref[0]
    misc_t = misc_ref[...].T
    wscale = IDX_HEADS ** -0.5 * IDX_DIM ** -0.5
    w = [misc_t[MISC_WI + j:MISC_WI + j + 1, :] * wscale for j in range(n_idx)]
    qis = [qi_ref[:, j * LANE:(j + 1) * LANE] for j in range(n_idx)]

    def score_body(c, carry):
        off = pl.multiple_of(c * kc, kc)
        kic = ki_ref[pl.ds(off, kc), :]
        lgs = [lax.dot_general(kic, qis[j], nt, preferred_element_type=F32) for j in range(n_idx)]
        s = w[0] * jnp.maximum(lgs[0], 0.0)
        for j in range(1, n_idx):
            s = s + w[j] * jnp.maximum(lgs[j], 0.0)
        kidx = off + lax.broadcasted_iota(jnp.int32, (kc, qb), 0)
        s = jnp.where(kidx < limit, s, -jnp.inf)
        bits = lax.bitcast_convert_type(s, jnp.int32)
        key = jnp.where(bits < 0, bits ^ 0x7FFFFFFF, bits)
        key_ref[pl.ds(off, kc), :] = key
        hi_ref[pl.ds(off, kc), :] = jnp.right_shift(key, 16).astype(jnp.int16)
        return carry

    lax.fori_loop(0, nch, score_body, 0)

    def count_ge(cand):
        sub = 8

        def body(c, accs):
            off = pl.multiple_of(c * kc, kc)
            m = jnp.where(key_ref[pl.ds(off, kc), :] >= cand, 1.0, 0.0)
            return _add_tiles(accs, m, sub)

        accs = lax.fori_loop(0, nch, body, (jnp.zeros((sub, qb), F32),) * N_PARTIAL, unroll=True)
        return jnp.sum(sum(accs), axis=0, keepdims=True)

    def count_ge16(ref, cand32):
        cand = cand32.astype(jnp.int16)
        sub = 16

        def body(c, accs):
            off = pl.multiple_of(c * kc, kc)
            m = jnp.where(ref[pl.ds(off, kc), :] >= cand, jnp.int16(1), jnp.int16(0))
            return _add_tiles(accs, m, sub)

        accs = lax.fori_loop(0, nch, body, (jnp.zeros((sub, qb), jnp.int16),) * N_PARTIAL, unroll=True)
        return jnp.sum(sum(accs).astype(F32), axis=0, keepdims=True)

    def kth_largest16(ref, want):
        tau = jnp.where(count_ge16(ref, jnp.zeros((1, qb), jnp.int32)) >= want, 0, I16_MIN).astype(jnp.int32)

        def bisect(i, tau):
            cand = tau | jnp.left_shift(jnp.int32(1), 14 - i)
            return jnp.where(count_ge16(ref, cand) >= want, cand, tau)

        return lax.fori_loop(0, 15, bisect, tau)

    want = float(n_sel)
    tau_hi = kth_largest16(hi_ref, want)
    above = jnp.where(tau_hi < I16_MAX, count_ge16(hi_ref, jnp.minimum(tau_hi + 1, I16_MAX)), 0.0)

    def low_body(c, carry):
        off = pl.multiple_of(c * kc, kc)
        key = key_ref[pl.ds(off, kc), :]
        low = (key & 0xFFFF) + I16_MIN
        lo_ref[pl.ds(off, kc), :] = jnp.where(jnp.right_shift(key, 16) == tau_hi, low, I16_MIN).astype(jnp.int16)
        return carry

    lax.fori_loop(0, nch, low_body, 0)
    tau_lo = kth_largest16(lo_ref, want - above)
    tau = jnp.left_shift(tau_hi, 16) + (tau_lo - I16_MIN)

    room = want - count_ge(tau + 1)
    r_i = lax.broadcasted_iota(jnp.int32, (LANE, LANE), 0)
    c_i = lax.broadcasted_iota(jnp.int32, (LANE, LANE), 1)
    prefix_ones = jnp.where(r_i >= c_i, 1.0, 0.0).astype(BF16)
    identity = jnp.where(r_i == c_i, 1.0, 0.0).astype(BF16)

    def bias_body(c, seen):
        off = pl.multiple_of(c * kc, kc)
        tiles = range(kc // LANE)
        xs = [key_ref[pl.ds(off + t * LANE, LANE), :] for t in tiles]
        eqs = [x == tau for x in xs]
        eqfs = [jnp.where(eq, 1.0, 0.0) for eq in eqs]
        ranks = [jnp.dot(prefix_ones, eqf.astype(BF16), preferred_element_type=F32) for eqf in eqfs]
        sels = []
        for t in tiles:
            sel = ((xs[t] > tau) | (eqs[t] & (ranks[t] + seen <= room))) & (xs[t] != KEY_OF_NEG_INF)
            sels.append(sel)
            seen = seen + jnp.sum(eqfs[t], axis=0, keepdims=True)
        if qb == LANE:
            for t in tiles:
                bias_ref[:, pl.ds(off + t * LANE, LANE)] = jnp.where(sels[t], 0.0, MASKED).T
        else:
            sel_ts = [lax.dot_general(jnp.where(sel, 1.0, 0.0).astype(BF16), identity, (((0,), (0,)), ((), ())),
                                      preferred_element_type=F32) for sel in sels]
            for t in tiles:
                bias_ref[:, pl.ds(off + t * LANE, LANE)] = jnp.where(sel_ts[t] > 0.5, 0.0, MASKED)
        return seen

    lax.fori_loop(0, nch, bias_body, jnp.zeros((1, qb), F32))

    lane = lax.broadcasted_iota(jnp.int32, (qb, LANE), 1)
    qgs = [jnp.concatenate([q_ref[:, (hpg * g + h) * LANE:(hpg * g + h + 1) * LANE] for h in range(hpg)], axis=0)
           for g in range(A_KV_HEADS)]

    def att_body(c, carry):
        off = pl.multiple_of(c * kc, kc)
        kch = k_ref[pl.ds(off, kc), :]
        vch = v_ref[pl.ds(off, kc), :]
        bias = bias_ref[:, pl.ds(off, kc)][None]
        new = [None] * A_KV_HEADS

        def group(g):
            m, l, acc = carry[g]
            s = lax.dot_general(qgs[g], kch, nt, preferred_element_type=F32)
            yield
            s = (s.reshape(hpg, qb, kc) + bias).reshape(hpg * qb, kc)
            m_new = jnp.maximum(m, jnp.max(s, axis=1, keepdims=True))
            alpha = jnp.exp(m - m_new)
            p = jnp.exp(s - m_new)
            l = alpha * l + jnp.sum(p, axis=1, keepdims=True)
            p16 = p.astype(BF16)
            yield
            new[g] = (m_new, l, alpha * acc + jnp.dot(p16, vch, preferred_element_type=F32))

        _lockstep(group(g) for g in range(A_KV_HEADS))
        return tuple(new)

    init = tuple((jnp.full((hpg * qb, 1), MASKED, F32), jnp.zeros((hpg * qb, 1), F32),
                  jnp.zeros((hpg * qb, LANE), F32)) for _ in range(A_KV_HEADS))
    res = lax.fori_loop(0, nch, att_body, init)
    outs = []
    for g in range(A_KV_HEADS):
        _, l, acc = res[g]
        og = acc / l
        for h in range(hpg):
            oh = og[h * qb:(h + 1) * qb]
            outs.append(oh if (h % 2) == g else pltpu.roll(oh, HALF, 1))
    for p in range(A_HEADS // 2):
        o_ref[:, p * LANE:(p + 1) * LANE] = jnp.where(lane < HALF, outs[2 * p], outs[2 * p + 1])


def dsa_attention(qpad, qipad, proj, limit, k16, v16, ki16, *, bsz, tq, tk, causal, n_sel):
    qb = min(Q_BLOCK, tq)
    nqb = tq // qb
    assert tk % KEY_CHUNK == 0 and tk >= n_sel
    if causal:
        per_group = KEY_CHUNK // qb
        groups = [(g * per_group, per_group, g + 1) for g in range(nqb // per_group)]
    else:
        groups = [(0, nqb, tk // KEY_CHUNK)]
    lim3 = limit.reshape(bsz * nqb, 1, qb)
    out = None
    for first, count, nch in groups:
        def qspec(width, col=0, first=first):
            return pl.BlockSpec((qb, width), lambda b, i: (b * nqb + first + i, col))

        kspec = pl.BlockSpec((tk, LANE), lambda b, i: (b, 0))
        in_specs = [qspec(A_HEADS * LANE), qspec(IDX_HEADS * LANE), qspec(LANE, AB_MISC // LANE),
                    pl.BlockSpec((1, 1, qb), lambda b, i, first=first: (b * nqb + first + i, 0, 0)),
                    kspec, kspec, kspec]
        args = [qpad, qipad, proj, lim3, k16, v16, ki16]
        kern = functools.partial(_dsa_kernel, nch=nch, n_sel=n_sel)
        aliases = {}
        if out is not None:
            in_specs.append(pl.BlockSpec(memory_space=pl.ANY))
            args.append(out)
            aliases = {len(args) - 1: 0}
            kern = functools.partial(_dsa_kernel_with_carry, nch=nch, n_sel=n_sel)
        out = pl.pallas_call(
            kern,
            grid=(bsz, count),
            in_specs=in_specs,
            out_specs=qspec(A_HEADS * HEAD_DIM),
            out_shape=jax.ShapeDtypeStruct((bsz * tq, A_HEADS * HEAD_DIM), F32),
            scratch_shapes=[pltpu.VMEM((tk, qb), jnp.int32), pltpu.VMEM((qb, tk), F32),
                            pltpu.VMEM((tk, qb), jnp.int16), pltpu.VMEM((tk, qb), jnp.int16)],
            input_output_aliases=aliases,
            compiler_params=pltpu.CompilerParams(
                dimension_semantics=("parallel", "arbitrary"), vmem_limit_bytes=VMEM_LIMIT),
            name="dsa_attention",
        )(*args)
    return out


def _dsa_kernel_with_carry(q_ref, qi_ref, misc_ref, lim_ref, k_ref, v_ref, ki_ref, carry_ref, o_ref, *scratch,
                           nch, n_sel):
    del carry_ref
    _dsa_kernel(q_ref, qi_ref, misc_ref, lim_ref, k_ref, v_ref, ki_ref, o_ref, *scratch, nch=nch, n_sel=n_sel)


SUB = 16


def _dot(a, b):
    return jnp.dot(a, b, preferred_element_type=F32)


def _dot_nt(a, b):
    return lax.dot_general(a, b, (((1,), (1,)), ((), ())), preferred_element_type=F32)


def _dot_tn(a, b):
    return lax.dot_general(a, b, (((0,), (0,)), ((), ())), preferred_element_type=F32)


def _split2(x):
    hi = x.astype(BF16)
    return hi, (x - hi.astype(F32)).astype(BF16)


def _split3(x):
    hi = x.astype(BF16)
    r = x - hi.astype(F32)
    mid = r.astype(BF16)
    return hi, mid, (r - mid.astype(F32)).astype(BF16)


def _cumsum_rows(x, tril16):
    hi, mid, lo = _split3(x)
    return _dot(tril16, hi) + _dot(tril16, mid) + _dot(tril16, lo)


def _dot_f32(a, b):
    ah, al = _split2(a)
    bh, bl = _split2(b)
    return _dot(ah, bh) + (_dot(ah, bl) + _dot(al, bh))


def _tri_mask(n, strict=False):
    r = lax.broadcasted_iota(jnp.int32, (n, n), 0)
    c = lax.broadcasted_iota(jnp.int32, (n, n), 1)
    return r > c if strict else r >= c


def _rows_to_lanes(x):
    rows = x.shape[0]
    if rows < LANE:
        x = jnp.concatenate([x, jnp.zeros((LANE - rows, LANE), x.dtype)], axis=0)
    return x.T


SEQS_PER_STEP = 2


def _lockstep(stages):
    stages = list(stages)
    while stages:
        for g in list(stages):
            if next(g, StopIteration) is StopIteration:
                stages.remove(g)


def _chunk_call(kern, *, bsz, nc, rows, ins, outs, scratch, name):
    nb = SEQS_PER_STEP

    def spec(a, kind, width, offset):
        if kind == 'rows':
            return pl.BlockSpec((nb, rows, width), lambda b, c: (b, c, offset // width))
        if kind == 'batch':
            return pl.BlockSpec((nb,) + tuple(a.shape[1:]), lambda b, c: (b,) + (0,) * (len(a.shape) - 1))
        return pl.BlockSpec(tuple(a.shape), lambda b, c: (0,) * len(a.shape))

    return pl.pallas_call(
        kern,
        grid=(bsz // nb, nc),
        in_specs=[spec(*i) for i in ins],
        out_specs=[spec(*o) for o in outs],
        out_shape=[o[0] for o in outs],
        scratch_shapes=scratch,
        compiler_params=pltpu.CompilerParams(
            dimension_semantics=("parallel", "arbitrary"), vmem_limit_bytes=VMEM_LIMIT),
        name=name,
    )(*[i[0] for i in ins])


def _mlstm_kernel(q_ref, k_ref, v_ref, og_ref, misc_ref, gb_ref, gain_ref, c0_ref, n0_ref, m0_ref,
                  h_ref, c_out_ref, n_out_ref, m_out_ref, c_scr, n_scr, m_scr):
    ci = pl.program_id(1)
    seqs = range(q_ref.shape[0])

    @pl.when(ci == 0)
    def _():
        c_scr[...] = c0_ref[...]
        n_scr[...] = n0_ref[...]
        m_scr[...] = m0_ref[...]

    _lockstep(_mlstm_chunk(q_ref.at[i], k_ref.at[i], v_ref.at[i], og_ref.at[i], misc_ref.at[i], gb_ref, gain_ref,
                           h_ref.at[i], c_scr.at[i], n_scr.at[i], m_scr.at[i]) for i in seqs)

    @pl.when(ci == pl.num_programs(1) - 1)
    def _():
        c_out_ref[...] = c_scr[...]
        n_out_ref[...] = n_scr[...]
        m_out_ref[...] = m_scr[...]


def _mlstm_chunk(q_ref, k_ref, v_ref, og_ref, misc_ref, gb_ref, gain_ref, h_ref, c_scr, n_scr, m_scr):
    rows = q_ref.shape[0]
    hr = B_HEADS * rows
    wq = B_HEADS * B_QK_DIM
    tril16 = jnp.where(_tri_mask(rows), 1.0, 0.0).astype(BF16)
    gates = misc_ref[...] + gb_ref[...]
    bcum = _cumsum_rows(jax.nn.log_sigmoid(gates), tril16)
    yield
    m_all = m_scr[...]

    def stack(f):
        return jnp.concatenate([f(h) for h in range(B_HEADS)], axis=0)

    lane_q = lax.broadcasted_iota(jnp.int32, (rows, wq), 1)
    qx, kx = q_ref[...], k_ref[...]
    q_all = stack(lambda h: jnp.where(lane_q // B_QK_DIM == h, qx, 0.0))
    k_all = stack(lambda h: jnp.where(lane_q // B_QK_DIM == h, kx, 0.0)) * B_QK_DIM ** -0.5
    v_all = stack(lambda h: v_ref[:, h * B_V_DIM:(h + 1) * B_V_DIM])
    b_col = stack(lambda h: bcum[:, MISC_FB + h:MISC_FB + h + 1])
    i_col = stack(lambda h: gates[:, MISC_IB + h:MISC_IB + h + 1])
    m_col = stack(lambda h: jnp.broadcast_to(m_all[:, h:h + 1], (rows, 1)))
    b_end = stack(lambda h: jnp.broadcast_to(bcum[rows - 1:rows, MISC_FB + h:MISC_FB + h + 1], (rows, 1)))
    bi_row = jnp.broadcast_to(b_col - i_col, (hr, LANE)).T[0:1, :]

    r = lax.broadcasted_iota(jnp.int32, (hr, hr), 0)
    c = lax.broadcasted_iota(jnp.int32, (hr, hr), 1)
    incl = ((r // rows) == (c // rows)) & (r >= c)
    dmat = jnp.where(incl, b_col - bi_row, -jnp.inf)
    inter = b_col + m_col
    mrow = jnp.maximum(inter, jnp.max(dmat, axis=1, keepdims=True))
    w_state = jnp.exp(inter - mrow)
    q16 = q_all.astype(BF16)
    v16 = v_all.astype(BF16)
    scores = _dot_nt(q16, k_all.astype(BF16)) * jnp.exp(dmat - mrow)
    yield
    cs = c_scr[...]
    n_row = n_scr[...]
    num = _dot(scores.astype(BF16), v16) + w_state * _dot(q16, cs.astype(BF16))
    den = jnp.sum(scores, axis=1, keepdims=True) + w_state * jnp.sum(q_all * n_row, axis=1, keepdims=True)
    hh = num / jnp.maximum(jnp.abs(den), jnp.exp(-mrow))
    yield
    gain = gain_ref[...]
    for h in range(B_HEADS):
        h_ref[:, h * B_V_DIM:(h + 1) * B_V_DIM] = (_rms_rows(hh[h * rows:(h + 1) * rows], gain)
                                                   * jax.nn.sigmoid(og_ref[:, h * B_V_DIM:(h + 1) * B_V_DIM]))

    g_col = b_end - b_col + i_col
    lane1 = lax.broadcasted_iota(jnp.int32, (1, LANE), 1)
    m_next = m_all
    m_new_rows, keep_rows, keep_lanes = [], [], []
    for h in range(B_HEADS):
        m_h = m_all[:, h:h + 1]
        be = bcum[rows - 1:rows, MISC_FB + h:MISC_FB + h + 1]
        m_new = jnp.maximum(be + m_h, jnp.max(g_col[h * rows:(h + 1) * rows], axis=0, keepdims=True))
        keep = jnp.exp(be + m_h - m_new)
        m_next = jnp.where(lane1 == h, m_new, m_next)
        m_new_rows.append(jnp.broadcast_to(m_new, (rows, 1)))
        keep_rows.append(jnp.broadcast_to(keep, (B_QK_DIM, 1)))
        keep_lanes.append(jnp.broadcast_to(keep, (1, B_QK_DIM)))
    kw = k_all * jnp.exp(g_col - jnp.concatenate(m_new_rows, axis=0))
    c_scr[...] = jnp.concatenate(keep_rows, axis=0) * cs + _dot_tn(kw.astype(BF16), v16)
    n_scr[...] = jnp.concatenate(keep_lanes, axis=1) * n_row + jnp.sum(kw, axis=0, keepdims=True)
    m_scr[...] = m_next
    yield


def mlstm_mixer(proj, gate_bias, gain, c0, n0, m0, *, bsz, t, chunk):
    nc = t // chunk
    nb = SEQS_PER_STEP
    gb = jnp.zeros((1, LANE), F32)
    gb = gb.at[0, MISC_IB:MISC_IB + B_HEADS].set(gate_bias[0]).at[0, MISC_FB:MISC_FB + B_HEADS].set(gate_bias[1])
    c0 = c0.reshape(bsz, B_HEADS * B_QK_DIM, B_V_DIM)
    n0 = n0.reshape(bsz, 1, B_HEADS * B_QK_DIM)
    m0 = jnp.pad(m0, ((0, 0), (0, LANE - B_HEADS))).reshape(bsz, 1, LANE)
    wq = B_HEADS * B_QK_DIM
    wv = B_HEADS * B_V_DIM
    h, c, n, m = _chunk_call(
        _mlstm_kernel, bsz=bsz, nc=nc, rows=chunk,
        ins=[(proj, 'rows', wq, AB_QB), (proj, 'rows', wq, AB_KB), (proj, 'rows', wv, AB_VB),
             (proj, 'rows', wv, AB_OB), (proj, 'rows', LANE, AB_MISC), (gb, 'const', 0, 0),
             (gain.reshape(1, B_V_DIM), 'const', 0, 0), (c0, 'batch', 0, 0), (n0, 'batch', 0, 0),
             (m0, 'batch', 0, 0)],
        outs=[(jax.ShapeDtypeStruct((bsz, t, wv), F32), 'rows', wv, 0),
              (jax.ShapeDtypeStruct(c0.shape, F32), 'batch', 0, 0),
              (jax.ShapeDtypeStruct(n0.shape, F32), 'batch', 0, 0),
              (jax.ShapeDtypeStruct(m0.shape, F32), 'batch', 0, 0)],
        scratch=[pltpu.VMEM((nb, wq, B_V_DIM), F32), pltpu.VMEM((nb, 1, wq), F32), pltpu.VMEM((nb, 1, LANE), F32)],
        name="mlstm_mixer")
    return (h.reshape(bsz * t, wv), c.reshape(bsz, B_HEADS, B_QK_DIM, B_V_DIM), n.reshape(bsz, B_HEADS, B_QK_DIM),
            m.reshape(bsz, LANE)[:, :B_HEADS])


CD_QKV, CD_ZC, CD_QD, CD_FD, CD_VD, CD_GD, CD_MISC, CD_TOTAL = 0, 1536, 2048, 2560, 3072, 3584, 4096, 4224
MISC_BC, MISC_AC = 0, 4
TAIL = 8


def _permute_w_in_cd(w):
    qkv, bc, ac, zc, qd, fd, vd, gd = _split_cols(w, ODD_SPLITS)
    pad = jnp.zeros((w.shape[0], LANE - 2 * C_HEADS), w.dtype)
    return jnp.concatenate([qkv, zc, qd, fd, vd, gd, bc, ac, pad], axis=1)


def _gdn_init(s0_ref, tail0_ref, st_scr, tail_scr):
    for h in range(C_HEADS):
        st_scr[:, h * C_DIM:(h + 1) * C_DIM] = s0_ref[h].T
    tail_scr[...] = tail0_ref[...]


def _gdn_final(s_out_ref, st_scr):
    for h in range(C_HEADS):
        s_out_ref[h] = st_scr[:, h * C_DIM:(h + 1) * C_DIM].T


def _gdn_chunk(qkv_ref, z_ref, misc_ref, cw_ref, alog_ref, dt_ref, gain_ref, o_ref, st_scr, tail_scr):
    rows = qkv_ref.shape[0]
    width = qkv_ref.shape[1]
    x = qkv_ref[...]
    tail = tail_scr[...]
    row8 = lax.broadcasted_iota(jnp.int32, (TAIL, width), 0)
    acc = x * cw_ref[CONV_W - 1:CONV_W, :]
    for back in range(1, CONV_W):
        rolled = pltpu.roll(x, back, 0)
        first = jnp.where(row8 < back, pltpu.roll(tail, back, 0), rolled[0:TAIL])
        shifted = first if rows == TAIL else jnp.concatenate([first, rolled[TAIL:]], axis=0)
        acc = acc + shifted * cw_ref[CONV_W - 1 - back:CONV_W - back, :]
    tail_scr[...] = x[rows - TAIL:rows]
    conv = acc * jax.nn.sigmoid(acc)

    tril16 = jnp.where(_tri_mask(rows), 1.0, 0.0).astype(BF16)
    misc = misc_ref[...]
    beta_t = jax.nn.sigmoid(misc)
    g_t = -jnp.exp(alog_ref[...]) * jax.nn.softplus(misc + dt_ref[...])
    gcum = _cumsum_rows(g_t, tril16)
    yield

    hd = C_HEADS * C_DIM
    hr = C_HEADS * rows

    def stack(f):
        return jnp.concatenate([f(h) for h in range(C_HEADS)], axis=0)

    def l2n(v):
        return v * lax.rsqrt(jnp.sum(v * v, axis=-1, keepdims=True) + EPS)

    q_all = stack(lambda h: l2n(conv[:, h * C_DIM:(h + 1) * C_DIM])) * C_DIM ** -0.5
    k_all = stack(lambda h: l2n(conv[:, hd + h * C_DIM:hd + (h + 1) * C_DIM]))
    v_all = stack(lambda h: conv[:, 2 * hd + h * C_DIM:2 * hd + (h + 1) * C_DIM])
    beta = stack(lambda h: beta_t[:, MISC_BC + h:MISC_BC + h + 1])
    gc = stack(lambda h: gcum[:, MISC_AC + h:MISC_AC + h + 1])
    g_end = stack(lambda h: jnp.broadcast_to(gcum[rows - 1:rows, MISC_AC + h:MISC_AC + h + 1], (rows, 1)))
    gc_row = jnp.broadcast_to(gc, (hr, LANE)).T[0:1, :]

    r = lax.broadcasted_iota(jnp.int32, (hr, hr), 0)
    c = lax.broadcasted_iota(jnp.int32, (hr, hr), 1)
    same = (r // rows) == (c // rows)
    incl = same & (r >= c)
    strict = same & (r > c)
    decay = jnp.exp(jnp.where(incl, gc - gc_row, -jnp.inf))
    k16 = k_all.astype(BF16)
    a_mat = jnp.where(strict, beta * _dot_nt(k16, k16) * decay, 0.0)
    yield
    power = -a_mat
    inv = jnp.where(r == c, 1.0, 0.0) + power
    for _ in range(int(math.log2(rows)) - 1):
        power = _dot_f32(power, power)
        yield
        inv = inv + _dot_f32(inv, power)
        yield
    inv_hi, inv_lo = _split2(inv)
    rhs = jnp.concatenate([beta * v_all, beta * jnp.exp(gc) * k_all], axis=1).astype(BF16)
    w = _dot(inv_hi, rhs) + _dot(inv_lo, rhs)
    yield
    w_v, w_k = w[:, 0:C_DIM], w[:, C_DIM:2 * C_DIM]
    qk = _dot_nt(q_all.astype(BF16), k16) * decay
    yield

    head_of_row = lax.broadcasted_iota(jnp.int32, (hr, C_DIM), 0) // rows

    def per_head_lanes(m):
        return jnp.concatenate([jnp.where(head_of_row == h, m, 0.0) for h in range(C_HEADS)], axis=1).astype(BF16)

    st = st_scr[...]
    st16 = st.astype(BF16)
    delta = w_v - _dot_nt(per_head_lanes(w_k), st16)
    yield
    d16 = delta.astype(BF16)
    out = _dot_nt(per_head_lanes(q_all * jnp.exp(gc)), st16) + _dot(qk.astype(BF16), d16)
    keep = jnp.concatenate([jnp.broadcast_to(jnp.exp(gcum[rows - 1:rows, MISC_AC + h:MISC_AC + h + 1]), (1, C_DIM))
                            for h in range(C_HEADS)], axis=1)
    st_scr[...] = keep * st + _dot_tn(d16, per_head_lanes(k_all * jnp.exp(g_end - gc)))
    yield
    gain = gain_ref[...]
    for h in range(C_HEADS):
        z = z_ref[:, h * C_DIM:(h + 1) * C_DIM]
        o_ref[:, h * C_DIM:(h + 1) * C_DIM] = _rms_rows(out[h * rows:(h + 1) * rows], gain) * (z * jax.nn.sigmoid(z))


def _hgrn2_init(s0_ref, st_scr):
    for h in range(D_HEADS):
        st_scr[h] = s0_ref[h].T


def _hgrn2_final(s_out_ref, st_scr):
    for h in range(D_HEADS):
        s_out_ref[h] = st_scr[h].T


def _hgrn2_chunk(q_ref, f_ref, v_ref, g_ref, lb_ref, gain_ref, o_ref, st_scr):
    rows = q_ref.shape[0]
    tril16 = jnp.where(_tri_mask(rows), 1.0, 0.0).astype(BF16)
    lb = lb_ref[...]
    zf = f_ref[...]
    logf = jnp.logaddexp(jnp.log(lb), jnp.log1p(-lb) + jax.nn.log_sigmoid(zf))
    kd = (1.0 - lb) * jax.nn.sigmoid(-zf)
    qx = q_ref[...]
    qd = qx * jax.nn.sigmoid(qx)
    bcum = _cumsum_rows(logf, tril16)
    yield
    gain = gain_ref[...]
    row_sub = lax.broadcasted_iota(jnp.int32, (SUB, 1), 0)
    for h in range(D_HEADS):
        sl = slice(h * D_EXPAND, (h + 1) * D_EXPAND)
        q, k, b = qd[:, sl], kd[:, sl], bcum[:, sl]
        v = v_ref[:, h * D_V_DIM:(h + 1) * D_V_DIM]
        v16 = v.astype(BF16)
        st = st_scr[h]
        inter = _dot_nt((q * jnp.exp(b)).astype(BF16), st.astype(BF16))
        blocks = []
        for i in range(rows // SUB):
            r0 = i * SUB
            qi, bi = q[r0:r0 + SUB], b[r0:r0 + SUB]
            oi = inter[r0:r0 + SUB]
            if i > 0:
                ref = b[r0 - 1:r0]
                att = _dot_nt((qi * jnp.exp(bi - ref)).astype(BF16),
                              (k[0:r0] * jnp.exp(ref - b[0:r0])).astype(BF16))
                oi = oi + _dot(att.astype(BF16), v16[0:r0])
            for s in range(SUB):
                r = r0 + s
                a = jnp.sum(qi * jnp.exp(bi - b[r:r + 1]) * k[r:r + 1], axis=1, keepdims=True)
                oi = oi + jnp.where(row_sub >= s, a, 0.0) * v[r:r + 1]
            blocks.append(oi)
        out = blocks[0] if len(blocks) == 1 else jnp.concatenate(blocks, axis=0)
        b_end = b[rows - 1:rows]
        st_scr[h] = jnp.exp(b_end) * st + _dot_tn(v16, (k * jnp.exp(b_end - b)).astype(BF16))
        g = g_ref[:, h * D_V_DIM:(h + 1) * D_V_DIM]
        o_ref[:, h * D_V_DIM:(h + 1) * D_V_DIM] = _rms_rows(out, gain) * (g * jax.nn.sigmoid(g))
        yield


def _cd_kernel(qkv_ref, z_ref, misc_ref, cw_ref, alog_ref, dt_ref, cgain_ref, sc0_ref, tail0_ref,
               qd_ref, fd_ref, vd_ref, gd_ref, lb_ref, dgain_ref, sd0_ref,
               oc_ref, sc_out_ref, od_ref, sd_out_ref, stc_scr, tail_scr, std_scr):
    ci = pl.program_id(1)
    seqs = range(qkv_ref.shape[0])

    @pl.when(ci == 0)
    def _():
        for i in seqs:
            _gdn_init(sc0_ref.at[i], tail0_ref.at[i], stc_scr.at[i], tail_scr.at[i])
            _hgrn2_init(sd0_ref.at[i], std_scr.at[i])

    gdn = [_gdn_chunk(qkv_ref.at[i], z_ref.at[i], misc_ref.at[i], cw_ref, alog_ref, dt_ref, cgain_ref,
                      oc_ref.at[i], stc_scr.at[i], tail_scr.at[i]) for i in seqs]
    hgrn2 = [_hgrn2_chunk(qd_ref.at[i], fd_ref.at[i], vd_ref.at[i], gd_ref.at[i], lb_ref, dgain_ref, od_ref.at[i],
                          std_scr.at[i]) for i in seqs]
    _lockstep(gdn + hgrn2)

    @pl.when(ci == pl.num_programs(1) - 1)
    def _():
        for i in seqs:
            _gdn_final(sc_out_ref.at[i], stc_scr.at[i])
            _hgrn2_final(sd_out_ref.at[i], std_scr.at[i])


def cd_mixers(proj, conv_w, a_log, dt_bias, c_gain, sc0, conv_prev, lower_bound, d_gain, sd0, *, bsz, t, chunk):
    nc = t // chunk
    nb = SEQS_PER_STEP
    hd = C_HEADS * C_DIM
    wk = D_HEADS * D_EXPAND
    wv = D_HEADS * D_V_DIM
    lanes = jnp.zeros((1, LANE), F32)
    alog = lanes.at[0, MISC_AC:MISC_AC + C_HEADS].set(a_log)
    dt = lanes.at[0, MISC_AC:MISC_AC + C_HEADS].set(dt_bias)
    tail0 = jnp.pad(conv_prev, ((0, 0), (TAIL - (CONV_W - 1), 0), (0, 0)))
    oc, sc, od, sd = _chunk_call(
        _cd_kernel, bsz=bsz, nc=nc, rows=chunk,
        ins=[(proj, 'rows', 3 * hd, CD_QKV), (proj, 'rows', hd, CD_ZC), (proj, 'rows', LANE, CD_MISC),
             (conv_w, 'const', 0, 0), (alog, 'const', 0, 0), (dt, 'const', 0, 0),
             (c_gain.reshape(1, C_DIM), 'const', 0, 0), (sc0, 'batch', 0, 0), (tail0, 'batch', 0, 0),
             (proj, 'rows', wk, CD_QD), (proj, 'rows', wk, CD_FD), (proj, 'rows', wv, CD_VD),
             (proj, 'rows', wv, CD_GD), (lower_bound.reshape(1, wk), 'const', 0, 0),
             (d_gain.reshape(1, D_V_DIM), 'const', 0, 0), (sd0, 'batch', 0, 0)],
        outs=[(jax.ShapeDtypeStruct((bsz, t, hd), F32), 'rows', hd, 0),
              (jax.ShapeDtypeStruct(sc0.shape, F32), 'batch', 0, 0),
              (jax.ShapeDtypeStruct((bsz, t, wv), F32), 'rows', wv, 0),
              (jax.ShapeDtypeStruct(sd0.shape, F32), 'batch', 0, 0)],
        scratch=[pltpu.VMEM((nb, C_DIM, hd), F32), pltpu.VMEM((nb, TAIL, 3 * hd), F32),
                 pltpu.VMEM((nb, D_HEADS, D_V_DIM, D_EXPAND), F32)],
        name="cd_mixers")
    return oc.reshape(bsz * t, hd), sc, od.reshape(bsz * t, wv), sd


def _pad_cols(w, mult=LANE):
    pad = (-w.shape[-1]) % mult
    return jnp.pad(w, [(0, 0)] * (w.ndim - 1) + [(0, pad)])


def _mixer_ab(proj, bsz, t, pos, prm, cache):
    n = bsz * t
    qpad, qipad, k16, v16, ki16, k32, v32, ki32 = dsa_prep(proj, pos, prm['a_q_gain'][0], prm['a_k_gain'][0], t)
    if cache is None:
        limit = jnp.tile((pos // CHUNK + 1) * CHUNK, bsz).reshape(n, 1)
        a_out = dsa_attention(qpad, qipad, proj, limit, k16, v16, ki16, bsz=bsz, tq=t, tk=t,
                              causal=True, n_sel=min(TOPK_MAX, t // 4))
        c0 = jnp.zeros((bsz, B_HEADS, B_QK_DIM, B_V_DIM), F32)
        n0 = jnp.zeros((bsz, B_HEADS, B_QK_DIM), F32)
        m0 = jnp.zeros((bsz, B_HEADS), F32)
        chunk = CHUNK
    else:
        k_c, v_c, ki_c, c0, n0, m0 = cache
        past = k_c.shape[1]
        n_keys = past + t
        tk = -(-n_keys // KEY_CHUNK) * KEY_CHUNK

        def with_cache(c, new):
            c = c.reshape(bsz, past, -1).astype(BF16)
            c = jnp.pad(c, ((0, 0), (0, 0), (0, LANE - c.shape[-1])))
            return jnp.concatenate([c, new.reshape(bsz, t, LANE),
                                    jnp.zeros((bsz, tk - n_keys, LANE), BF16)], axis=1).reshape(bsz * tk, LANE)

        limit = jnp.full((n, 1), n_keys, jnp.int32)
        a_out = dsa_attention(qpad, qipad, proj, limit, with_cache(k_c, k16), with_cache(v_c, v16),
                              with_cache(ki_c, ki16), bsz=bsz, tq=t, tk=tk, causal=False,
                              n_sel=min(TOPK_MAX, n_keys // 4))
        chunk = t
    h, c, n_, m = mlstm_mixer(proj.reshape(bsz, t, -1), prm['b_gate_bias'][0], prm['b_norm_gain'][0], c0, n0, m0,
                              bsz=bsz, t=t, chunk=chunk)
    st = (k32.reshape(bsz, t, A_KV_HEADS, HEAD_DIM), v32.reshape(bsz, t, A_KV_HEADS, HEAD_DIM),
          ki32.reshape(bsz, t, IDX_DIM), c, n_, m)
    return a_out, h, st


def _mixer_cd(proj, bsz, t, prm, lower_bound, cache):
    hd = C_HEADS * C_DIM
    if cache is None:
        sc0 = jnp.zeros((bsz, C_HEADS, C_DIM, C_DIM), F32)
        conv_prev = jnp.zeros((bsz, CONV_W - 1, 3 * hd), F32)
        sd0 = jnp.zeros((bsz, D_HEADS, D_EXPAND, D_V_DIM), F32)
        chunk = CHUNK
    else:
        sc0, conv_prev, sd0 = cache
        chunk = t
    oc, sc, od, sd = cd_mixers(proj.reshape(bsz, t, -1), prm['c_conv_w'][0], prm['c_a_log'][0], prm['c_dt_bias'][0],
                               prm['c_norm_gain'][0], sc0, conv_prev, lower_bound, prm['d_norm_gain'][0], sd0,
                               bsz=bsz, t=t, chunk=chunk)
    qkv = proj.reshape(bsz, t, -1)[:, :, CD_QKV:CD_QKV + 3 * hd]
    conv_new = jnp.concatenate([conv_prev, qkv[:, t - (CONV_W - 1):]], axis=1)[:, -(CONV_W - 1):]
    return oc, od, (sc, conv_new, sd)


def _trunk(x, pos_offset, cache, prm, wts):
    bsz, t, d = x.shape
    n = bsz * t
    pos = pos_offset + jnp.arange(t, dtype=jnp.int32)
    probs = jax.nn.softmax(prm['d_lb_logits'], axis=0)
    lower_bounds = jnp.cumsum(probs, axis=0) - probs[0]
    xf = x.reshape(n, d)

    lc = None if cache is None else tuple(c[0] for c in cache[:6])
    proj = norm_matmul(xf, prm['norm_mix'][0], wts['w_in_ab'])
    a_out, b_out, st_even = _mixer_ab(proj, bsz, t, pos, prm, lc)
    xf = matmul_residual(a_out, b_out, wts['w_out_ab'], xf)
    xf = ffn_residual(xf, prm['norm_ffn'][0], wts['ffn_w1'], wts['ffn_w3'], wts['ffn_w2'])

    lc = None if cache is None else tuple(c[0] for c in cache[6:])
    proj = norm_matmul(xf, prm['norm_mix'][1], wts['w_in_cd'])
    c_out, d_out, st_odd = _mixer_cd(proj, bsz, t, prm, lower_bounds[1], lc)
    xf = matmul_residual(c_out, d_out, wts['w_out_cd'], xf)
    xf = moe_residual(xf, prm['norm_ffn'][1], wts['moe_router'], wts['moe_w1'], wts['moe_w3'], wts['moe_w2'])

    new_state = tuple(s[None] for s in st_even + st_odd)
    return xf.reshape(bsz, t, d), new_state


def kernel(x_prompt, x_sample, cache_a_k, cache_a_v, cache_a_kidx, state_b_c, state_b_n, state_b_m,
           state_c_s, state_c_conv, state_d_s, norm_mix, norm_ffn, w_in_ab, w_out_ab, a_q_gain, a_k_gain,
           b_gate_bias, b_norm_gain, w_in_cd, w_out_cd, c_conv_w, c_a_log, c_dt_bias, c_norm_gain,
           d_lb_logits, d_norm_gain, ffn_w1, ffn_w3, ffn_w2, moe_router, moe_w1, moe_w3, moe_w2):
    prm = dict(norm_mix=norm_mix, norm_ffn=norm_ffn, a_q_gain=a_q_gain, a_k_gain=a_k_gain,
               b_gate_bias=b_gate_bias, b_norm_gain=b_norm_gain, c_conv_w=c_conv_w, c_a_log=c_a_log,
               c_dt_bias=c_dt_bias, c_norm_gain=c_norm_gain, d_lb_logits=d_lb_logits, d_norm_gain=d_norm_gain)
    wts = dict(w_in_ab=_permute_w_in_ab(w_in_ab[0]).astype(BF16), w_out_ab=w_out_ab[0].astype(BF16),
               w_in_cd=_permute_w_in_cd(w_in_cd[0]).astype(BF16), w_out_cd=w_out_cd[0].astype(BF16),
               ffn_w1=ffn_w1[0].astype(BF16), ffn_w3=ffn_w3[0].astype(BF16), ffn_w2=ffn_w2[0].astype(BF16),
               moe_router=_pad_cols(moe_router[0]),
               moe_w1=moe_w1[0].astype(BF16), moe_w3=moe_w3[0].astype(BF16), moe_w2=moe_w2[0].astype(BF16))
    cache = (cache_a_k, cache_a_v, cache_a_kidx, state_b_c, state_b_n, state_b_m, state_c_s, state_c_conv, state_d_s)
    y_prompt, st_p = _trunk(x_prompt, 0, None, prm, wts)
    y_sample, st_s = _trunk(x_sample, cache_a_k.shape[2], cache, prm, wts)
    return (y_prompt, y_sample) + st_p + st_s
```

```python
import functools
import math

import jax
import jax.numpy as jnp
import numpy as np
from jax import lax
from jax.experimental import pallas as pl
from jax.experimental.pallas import tpu as pltpu

F32 = jnp.float32
BF16 = jnp.bfloat16

EPS = 1e-6
ROPE_THETA = 500000.0
ROT_FRACTION = 4
CHUNK = 64
A_HEADS, A_KV_HEADS, HEAD_DIM = 8, 2, 64
IDX_HEADS, IDX_DIM = 4, 64
TOPK_MAX, Q_BLOCK = 256, 128
B_HEADS, B_QK_DIM, B_V_DIM = 4, 64, 128
C_HEADS, C_DIM, CONV_W = 4, 128, 4
D_HEADS, D_EXPAND, D_V_DIM = 4, 128, 128
N_EXPERTS, TOP_K_EXPERTS = 8, 2

LANE = 128
VMEM_LIMIT = 48 * 1024 * 1024

EVEN_SPLITS = (A_HEADS * HEAD_DIM, A_KV_HEADS * HEAD_DIM, A_KV_HEADS * HEAD_DIM,
               IDX_HEADS * IDX_DIM, IDX_DIM, IDX_HEADS,
               B_HEADS * B_QK_DIM, B_HEADS * B_QK_DIM, B_HEADS * B_V_DIM,
               B_HEADS, B_HEADS, B_HEADS * B_V_DIM)
ODD_SPLITS = (3 * C_HEADS * C_DIM, C_HEADS, C_HEADS, C_HEADS * C_DIM,
              D_HEADS * D_EXPAND, D_HEADS * D_EXPAND, D_HEADS * D_V_DIM, D_HEADS * D_V_DIM)


def _split_cols(p, widths):
    cuts = [int(c) for c in np.cumsum(widths)[:-1]]
    return jnp.split(p, cuts, axis=-1)


def _row_tile(n, target):
    t = min(n, target)
    while n % t:
        t //= 2
    return t


def _col_tile(n, target):
    best = LANE
    for k in range(1, n // LANE + 1):
        c = k * LANE
        if n % c == 0 and c <= target:
            best = c
    return best


def _rms_rows(x, gain):
    return x * lax.rsqrt(jnp.mean(x * x, axis=-1, keepdims=True) + EPS) * gain


def _norm_matmul_kernel(x_ref, g_ref, w_ref, o_ref, xn_ref):
    @pl.when(pl.program_id(1) == 0)
    def _():
        xn_ref[...] = _rms_rows(x_ref[...], g_ref[...]).astype(BF16)

    o_ref[...] = jnp.dot(xn_ref[...], w_ref[...], preferred_element_type=F32)


def norm_matmul(x, gain, w):
    n, d = x.shape
    m = w.shape[1]
    tm = _row_tile(n, 1024)
    tn = _col_tile(m, 1536)
    return pl.pallas_call(
        _norm_matmul_kernel,
        grid=(n // tm, m // tn),
        in_specs=[pl.BlockSpec((tm, d), lambda i, j: (i, 0)),
                  pl.BlockSpec((1, d), lambda i, j: (0, 0)),
                  pl.BlockSpec((d, tn), lambda i, j: (0, j))],
        out_specs=pl.BlockSpec((tm, tn), lambda i, j: (i, j)),
        out_shape=jax.ShapeDtypeStruct((n, m), F32),
        scratch_shapes=[pltpu.VMEM((tm, d), BF16)],
        compiler_params=pltpu.CompilerParams(
            dimension_semantics=("parallel", "arbitrary"), vmem_limit_bytes=VMEM_LIMIT),
        name="norm_matmul",
    )(x, gain.reshape(1, d), w)


def _matmul_res_kernel(a_ref, b_ref, w_ref, r_ref, o_ref):
    ka = a_ref.shape[1]
    o_ref[...] = (r_ref[...] + jnp.dot(a_ref[...].astype(BF16), w_ref[0:ka, :], preferred_element_type=F32)
                  + jnp.dot(b_ref[...].astype(BF16), w_ref[ka:, :], preferred_element_type=F32))


def matmul_residual(a, b, w, res):
    n, ka = a.shape
    kb = b.shape[1]
    m = w.shape[1]
    tm = _row_tile(n, 1024)
    return pl.pallas_call(
        _matmul_res_kernel,
        grid=(n // tm,),
        in_specs=[pl.BlockSpec((tm, ka), lambda i: (i, 0)),
                  pl.BlockSpec((tm, kb), lambda i: (i, 0)),
                  pl.BlockSpec((ka + kb, m), lambda i: (0, 0)),
                  pl.BlockSpec((tm, m), lambda i: (i, 0))],
        out_specs=pl.BlockSpec((tm, m), lambda i: (i, 0)),
        out_shape=jax.ShapeDtypeStruct((n, m), F32),
        compiler_params=pltpu.CompilerParams(
            dimension_semantics=("parallel",), vmem_limit_bytes=VMEM_LIMIT),
        name="matmul_residual",
    )(a, b, w, res)


def _swiglu_tile(xn, w1, w3):
    h1 = jnp.dot(xn, w1, preferred_element_type=F32)
    h3 = jnp.dot(xn, w3, preferred_element_type=F32)
    return h1 * jax.nn.sigmoid(h1) * h3


def _ffn_kernel(x_ref, g_ref, w1_ref, w3_ref, w2_ref, o_ref, xn_ref):
    @pl.when(pl.program_id(1) == 0)
    def _():
        x = x_ref[...]
        xn_ref[...] = _rms_rows(x, g_ref[...]).astype(BF16)
        o_ref[...] = x

    act = _swiglu_tile(xn_ref[...], w1_ref[...], w3_ref[...])
    o_ref[...] += jnp.dot(act.astype(BF16), w2_ref[...], preferred_element_type=F32)


def ffn_residual(x, gain, w1, w3, w2):
    n, d = x.shape
    f = w1.shape[1]
    tm = _row_tile(n, 1024)
    tf = _col_tile(f, 896)
    return pl.pallas_call(
        _ffn_kernel,
        grid=(n // tm, f // tf),
        in_specs=[pl.BlockSpec((tm, d), lambda i, j: (i, 0)),
                  pl.BlockSpec((1, d), lambda i, j: (0, 0)),
                  pl.BlockSpec((d, tf), lambda i, j: (0, j)),
                  pl.BlockSpec((d, tf), lambda i, j: (0, j)),
                  pl.BlockSpec((tf, d), lambda i, j: (j, 0))],
        out_specs=pl.BlockSpec((tm, d), lambda i, j: (i, 0)),
        out_shape=jax.ShapeDtypeStruct((n, d), F32),
        scratch_shapes=[pltpu.VMEM((tm, d), BF16)],
        compiler_params=pltpu.CompilerParams(
            dimension_semantics=("parallel", "arbitrary"), vmem_limit_bytes=VMEM_LIMIT),
        name="ffn_residual",
    )(x, gain.reshape(1, d), w1, w3, w2)


MOE_VMEM_LIMIT = 58 * 1024 * 1024
MOE_CAPS = (256, 288, 320, 384, 512)


def _moe_route_kernel(x_ref, g_ref, r_ref, xn_ref, comb_ref, post_ref, cnt_ref):
    x = x_ref[...]
    tm = x.shape[0]
    xn = _rms_rows(x, g_ref[...])
    xn_ref[...] = xn.astype(BF16)
    logits = jnp.dot(xn, r_ref[...], preferred_element_type=F32, precision=lax.Precision.HIGHEST)
    lane = lax.broadcasted_iota(jnp.int32, logits.shape, 1)
    logits = jnp.where(lane < N_EXPERTS, logits, -jnp.inf)
    m1 = jnp.max(logits, axis=-1, keepdims=True)
    i1 = jnp.min(jnp.where(logits == m1, lane, LANE), axis=-1, keepdims=True)
    rest = jnp.where(lane == i1, -jnp.inf, logits)
    m2 = jnp.max(rest, axis=-1, keepdims=True)
    i2 = jnp.min(jnp.where(rest == m2, lane, LANE), axis=-1, keepdims=True)
    e2 = jnp.exp(m2 - m1)
    den = 1.0 + e2
    comb_ref[...] = jnp.where(lane == i1, 1.0 / den, 0.0) + jnp.where(lane == i2, e2 / den, 0.0)
    chosen = (lane == i1) | (lane == i2)
    sel = jnp.where(chosen, 1.0, 0.0)
    tril16 = jnp.where(_tri_mask(LANE), 1.0, 0.0).astype(BF16)
    seen = jnp.zeros((1, LANE), F32)
    ranks = []
    for blk in range(tm // LANE):
        sb = sel[blk * LANE:(blk + 1) * LANE]
        ranks.append(_dot(tril16, sb.astype(BF16)) + seen - 1.0)
        seen = seen + jnp.sum(sb, axis=0, keepdims=True)
    rank = jnp.where(chosen, jnp.concatenate(ranks, axis=0), -1.0)
    post_ref[0] = rank.T[0:N_EXPERTS, :]
    cnt_ref[0] = seen


def _moe_expert_kernel(cnt_ref, xn_ref, comb_ref, post_ref, x_ref, w1_ref, w3_ref, w2_ref, o_ref,
                       xe_scr, y_scr, *, caps):
    i = pl.program_id(0)
    e = pl.program_id(1)
    j = pl.program_id(2)
    last = pl.num_programs(2) - 1
    tm = x_ref.shape[0]

    @pl.when(jnp.logical_and(e == 0, j == 0))
    def _():
        o_ref[...] = x_ref[...]

    cnt = cnt_ref[i * N_EXPERTS + e]

    def expert_step(cap):
        def pick():
            rank_row = post_ref[0, pl.ds(e, 1), :].astype(jnp.int32)
            slot = lax.broadcasted_iota(jnp.int32, (cap, tm), 0)
            return jnp.where(rank_row == slot, 1.0, 0.0).astype(BF16)

        @pl.when(j == 0)
        def _():
            xe_scr[0:cap, :] = _dot(pick(), xn_ref[...]).astype(BF16)

        act = _swiglu_tile(xe_scr[0:cap, :], w1_ref[0], w3_ref[0])
        yj = _dot(act.astype(BF16), w2_ref[0])

        @pl.when(j == 0)
        def _():
            y_scr[0:cap, :] = yj

        @pl.when(j > 0)
        def _():
            y_scr[0:cap, :] += yj

        @pl.when(j == last)
        def _():
            comb = comb_ref[...]
            lane = lax.broadcasted_iota(jnp.int32, comb.shape, 1)
            gate = jnp.sum(jnp.where(lane == e, comb, 0.0), axis=-1, keepdims=True)
            o_ref[...] += gate * _dot_tn(pick(), y_scr[0:cap, :].astype(BF16))

    lo = 0
    for cap in caps:
        @pl.when(jnp.logical_and(cnt > lo, cnt <= cap))
        def _(cap=cap):
            expert_step(cap)
        lo = cap


def moe_residual(x, gain, router, w1, w3, w2):
    n, d = x.shape
    ne, _, f = w1.shape
    tm = _row_tile(n, 1024)
    tf = _col_tile(f, 896)
    nt = n // tm
    xn, comb, post, cnt = pl.pallas_call(
        _moe_route_kernel,
        grid=(nt,),
        in_specs=[pl.BlockSpec((tm, d), lambda i: (i, 0)),
                  pl.BlockSpec((1, d), lambda i: (0, 0)),
                  pl.BlockSpec((d, LANE), lambda i: (0, 0))],
        out_specs=[pl.BlockSpec((tm, d), lambda i: (i, 0)),
                   pl.BlockSpec((tm, LANE), lambda i: (i, 0)),
                   pl.BlockSpec((1, ne, tm), lambda i: (i, 0, 0)),
                   pl.BlockSpec((1, 1, LANE), lambda i: (i, 0, 0))],
        out_shape=[jax.ShapeDtypeStruct((n, d), BF16), jax.ShapeDtypeStruct((n, LANE), F32),
                   jax.ShapeDtypeStruct((nt, ne, tm), F32), jax.ShapeDtypeStruct((nt, 1, LANE), F32)],
        compiler_params=pltpu.CompilerParams(dimension_semantics=("parallel",), vmem_limit_bytes=VMEM_LIMIT),
        name="moe_route",
    )(x, gain.reshape(1, d), router)
    counts = cnt[:, 0, :ne].astype(jnp.int32).reshape(nt * ne)
    caps = tuple(c for c in MOE_CAPS if c < tm) + (tm,)
    grid_spec = pltpu.PrefetchScalarGridSpec(
        num_scalar_prefetch=1,
        grid=(nt, ne, f // tf),
        in_specs=[pl.BlockSpec((tm, d), lambda i, e, j, c: (i, 0)),
                  pl.BlockSpec((tm, LANE), lambda i, e, j, c: (i, 0)),
                  pl.BlockSpec((1, ne, tm), lambda i, e, j, c: (i, 0, 0)),
                  pl.BlockSpec((tm, d), lambda i, e, j, c: (i, 0)),
                  pl.BlockSpec((1, d, tf), lambda i, e, j, c: (e, 0, j)),
                  pl.BlockSpec((1, d, tf), lambda i, e, j, c: (e, 0, j)),
                  pl.BlockSpec((1, tf, d), lambda i, e, j, c: (e, j, 0))],
        out_specs=pl.BlockSpec((tm, d), lambda i, e, j, c: (i, 0)),
        scratch_shapes=[pltpu.VMEM((tm, d), BF16), pltpu.VMEM((tm, d), F32)])
    return pl.pallas_call(
        functools.partial(_moe_expert_kernel, caps=caps),
        grid_spec=grid_spec,
        out_shape=jax.ShapeDtypeStruct((n, d), F32),
        compiler_params=pltpu.CompilerParams(
            dimension_semantics=("parallel", "arbitrary", "arbitrary"), vmem_limit_bytes=MOE_VMEM_LIMIT),
        name="moe_experts",
    )(counts, xn, comb, post, x, w1, w3, w2)


AB_QA, AB_VB, AB_OB, AB_QI, AB_QB, AB_KB, AB_KA, AB_VA, AB_MISC, AB_TOTAL = (
    0, 512, 1024, 1536, 1792, 2048, 2304, 2432, 2560, 2688)
MISC_WI, MISC_IB, MISC_FB = 64, 68, 72
HALF = LANE // 2
KEY_CHUNK = 512
MASKED = -1e30
KEY_OF_NEG_INF = -2139095041
I16_MIN, I16_MAX = -32768, 32767


def _permute_w_in_ab(w):
    qa, ka, va, qi, ki, wi, qb, kb, vb, ib, fb, ob = _split_cols(w, EVEN_SPLITS)
    pad = jnp.zeros((w.shape[0], LANE - IDX_DIM - 3 * IDX_HEADS), w.dtype)
    return jnp.concatenate([qa, vb, ob, qi, qb, kb, ka, va, ki, wi, ib, fb, pad], axis=1)


def _rope_tables(pos):
    rot = HEAD_DIM // ROT_FRACTION
    half = rot // 2
    inv_freq = ROPE_THETA ** (-jnp.arange(half, dtype=F32) * 2.0 / rot)
    ang = pos.astype(F32)[:, None] * inv_freq[None, :]
    cos, sin = jnp.cos(ang), jnp.sin(ang)
    t = pos.shape[0]
    one = jnp.ones((t, HEAD_DIM - rot), F32)
    zero_r = jnp.zeros((t, HEAD_DIM - rot), F32)
    zero_h = jnp.zeros((t, half), F32)
    c = jnp.concatenate([cos, cos, one], axis=1)
    s_up = jnp.concatenate([-sin, zero_h, zero_r], axis=1)
    s_dn = jnp.concatenate([zero_h, sin, zero_r], axis=1)
    return tuple(jnp.concatenate([a, a], axis=1) for a in (c, s_up, s_dn))


def _rope_tile(x, c, s_up, s_dn):
    half = HEAD_DIM // ROT_FRACTION // 2
    return x * c + pltpu.roll(x, LANE - half, 1) * s_up + pltpu.roll(x, half, 1) * s_dn


def _head_norm_tile(x, gain, same_head):
    sq = x * x
    hi = sq.astype(BF16)
    lo = (sq - hi.astype(F32)).astype(BF16)
    ss = (jnp.dot(hi, same_head, preferred_element_type=F32)
          + jnp.dot(lo, same_head, preferred_element_type=F32))
    return x * lax.rsqrt(ss * (1.0 / HEAD_DIM) + EPS) * gain


def _aprep_kernel(qa_ref, ka_ref, va_ref, qi_ref, misc_ref, c_ref, su_ref, sd_ref, qg_ref, kg_ref,
                  qpad_ref, qipad_ref, k16_ref, v16_ref, ki16_ref, k32_ref, v32_ref, ki32_ref):
    c, su, sd = c_ref[...], su_ref[...], sd_ref[...]
    tm = c.shape[0]
    row = lax.broadcasted_iota(jnp.int32, (LANE, LANE), 0)
    col = lax.broadcasted_iota(jnp.int32, (LANE, LANE), 1)
    same_head = jnp.where(row // HALF == col // HALF, 1.0, 0.0).astype(BF16)
    lane = lax.broadcasted_iota(jnp.int32, (tm, LANE), 1)
    low = lane < HALF

    heads_per_group = A_HEADS // A_KV_HEADS
    for p in range(A_HEADS // 2):
        y = _rope_tile(_head_norm_tile(qa_ref[:, p * LANE:(p + 1) * LANE], qg_ref[...], same_head), c, su, sd)
        y = y * HEAD_DIM ** -0.5
        y_sw = pltpu.roll(y, HALF, 1)
        for o in range(2):
            h = 2 * p + o
            g = h // heads_per_group
            src = y if o == g else y_sw
            qpad_ref[:, h * LANE:(h + 1) * LANE] = jnp.where(low if g == 0 else ~low, src, 0.0).astype(BF16)
    k = _rope_tile(_head_norm_tile(ka_ref[...], kg_ref[...], same_head), c, su, sd)
    k32_ref[...] = k
    k16_ref[...] = k.astype(BF16)
    v = va_ref[...]
    v32_ref[...] = v
    v16_ref[...] = v.astype(BF16)
    for p in range(IDX_HEADS // 2):
        y = _rope_tile(qi_ref[:, p * LANE:(p + 1) * LANE], c, su, sd)
        y_sw = pltpu.roll(y, HALF, 1)
        qipad_ref[:, (2 * p) * LANE:(2 * p + 1) * LANE] = jnp.where(low, y, 0.0).astype(BF16)
        qipad_ref[:, (2 * p + 1) * LANE:(2 * p + 2) * LANE] = jnp.where(low, y_sw, 0.0).astype(BF16)
    ki = _rope_tile(misc_ref[...], c, su, sd)
    ki32_ref[...] = ki[:, :IDX_DIM]
    ki16_ref[...] = jnp.where(low, ki, 0.0).astype(BF16)


def dsa_prep(proj, pos, q_gain, k_gain, t):
    n = proj.shape[0]
    tm = _row_tile(n, 512)
    tabs = _rope_tables(pos)
    if t < tm:
        tabs = tuple(jnp.tile(a, (tm // t, 1)) for a in tabs)
    nt = tabs[0].shape[0] // tm
    tab_spec = pl.BlockSpec((tm, LANE), lambda i: (i % nt, 0))
    gain_spec = pl.BlockSpec((1, LANE), lambda i: (0, 0))

    def col(width, offset):
        return pl.BlockSpec((tm, width), lambda i: (i, offset // width))

    def out(width, dtype):
        return (jax.ShapeDtypeStruct((n, width), dtype), pl.BlockSpec((tm, width), lambda i: (i, 0)))

    outs = [out(A_HEADS * LANE, BF16), out(IDX_HEADS * LANE, BF16), out(LANE, BF16), out(LANE, BF16),
            out(LANE, BF16), out(LANE, F32), out(LANE, F32), out(IDX_DIM, F32)]
    return pl.pallas_call(
        _aprep_kernel,
        grid=(n // tm,),
        in_specs=[col(A_HEADS * HEAD_DIM, AB_QA), col(LANE, AB_KA), col(LANE, AB_VA),
                  col(IDX_HEADS * IDX_DIM, AB_QI), col(LANE, AB_MISC), tab_spec, tab_spec, tab_spec,
                  gain_spec, gain_spec],
        out_specs=[o[1] for o in outs],
        out_shape=[o[0] for o in outs],
        compiler_params=pltpu.CompilerParams(dimension_semantics=("parallel",), vmem_limit_bytes=VMEM_LIMIT),
        name="dsa_prep",
    )(proj, proj, proj, proj, proj, *tabs, jnp.tile(q_gain, 2).reshape(1, LANE), jnp.tile(k_gain, 2).reshape(1, LANE))


N_PARTIAL = 4


def _add_tiles(accs, m, sub):
    accs = list(accs)
    for t in range(m.shape[0] // sub):
        accs[t % len(accs)] = accs[t % len(accs)] + m[t * sub:(t + 1) * sub]
    return tuple(accs)


def _dsa_kernel(q_ref, qi_ref, misc_ref, lim_ref, k_ref, v_ref, ki_ref, o_ref, key_ref, bias_ref, hi_ref, lo_ref,
                *, nch, n_sel):
    qb = q_ref.shape[0]
    kc = KEY_CHUNK
    n_idx = IDX_HEADS
    hpg = A_HEADS // A_KV_HEADS
    nt = (((1,), (1,)), ((), ()))

    limit = lim_ref[0]
    misc_t = misc_ref[...].T
    wscale = IDX_HEADS ** -0.5 * IDX_DIM ** -0.5
    w = [misc_t[MISC_WI + j:MISC_WI + j + 1, :] * wscale for j in range(n_idx)]
    qis = [qi_ref[:, j * LANE:(j + 1) * LANE] for j in range(n_idx)]

    def score_body(c, carry):
        off = pl.multiple_of(c * kc, kc)
        kic = ki_ref[pl.ds(off, kc), :]
        lgs = [lax.dot_general(kic, qis[j], nt, preferred_element_type=F32) for j in range(n_idx)]
        s = w[0] * jnp.maximum(lgs[0], 0.0)
        for j in range(1, n_idx):
            s = s + w[j] * jnp.maximum(lgs[j], 0.0)
        kidx = off + lax.broadcasted_iota(jnp.int32, (kc, qb), 0)
        s = jnp.where(kidx < limit, s, -jnp.inf)
        bits = lax.bitcast_convert_type(s, jnp.int32)
        key = jnp.where(bits < 0, bits ^ 0x7FFFFFFF, bits)
        key_ref[pl.ds(off, kc), :] = key
        hi_ref[pl.ds(off, kc), :] = jnp.right_shift(key, 16).astype(jnp.int16)
        return carry

    lax.fori_loop(0, nch, score_body, 0)

    def count_ge(cand):
        sub = 8

        def body(c, accs):
            off = pl.multiple_of(c * kc, kc)
            m = jnp.where(key_ref[pl.ds(off, kc), :] >= cand, 1.0, 0.0)
            return _add_tiles(accs, m, sub)

        accs = lax.fori_loop(0, nch, body, (jnp.zeros((sub, qb), F32),) * N_PARTIAL, unroll=True)
        return jnp.sum(sum(accs), axis=0, keepdims=True)

    def count_ge16(ref, cand32):
        cand = cand32.astype(jnp.int16)
        sub = 16

        def body(c, accs):
            off = pl.multiple_of(c * kc, kc)
            m = jnp.where(ref[pl.ds(off, kc), :] >= cand, jnp.int16(1), jnp.int16(0))
            return _add_tiles(accs, m, sub)

        accs = lax.fori_loop(0, nch, body, (jnp.zeros((sub, qb), jnp.int16),) * N_PARTIAL, unroll=True)
        return jnp.sum(sum(accs).astype(F32), axis=0, keepdims=True)

    def kth_largest16(ref, want):
        tau = jnp.where(count_ge16(ref, jnp.zeros((1, qb), jnp.int32)) >= want, 0, I16_MIN).astype(jnp.int32)

        def bisect(i, tau):
            cand = tau | jnp.left_shift(jnp.int32(1), 14 - i)
            return jnp.where(count_ge16(ref, cand) >= want, cand, tau)

        return lax.fori_loop(0, 15, bisect, tau)

    want = float(n_sel)
    tau_hi = kth_largest16(hi_ref, want)
    above = jnp.where(tau_hi < I16_MAX, count_ge16(hi_ref, jnp.minimum(tau_hi + 1, I16_MAX)), 0.0)

    def low_body(c, carry):
        off = pl.multiple_of(c * kc, kc)
        key = key_ref[pl.ds(off, kc), :]
        low = (key & 0xFFFF) + I16_MIN
        lo_ref[pl.ds(off, kc), :] = jnp.where(jnp.right_shift(key, 16) == tau_hi, low, I16_MIN).astype(jnp.int16)
        return carry

    lax.fori_loop(0, nch, low_body, 0)
    tau_lo = kth_largest16(lo_ref, want - above)
    tau = jnp.left_shift(tau_hi, 16) + (tau_lo - I16_MIN)

    room = want - count_ge(tau + 1)
    r_i = lax.broadcasted_iota(jnp.int32, (LANE, LANE), 0)
    c_i = lax.broadcasted_iota(jnp.int32, (LANE, LANE), 1)
    prefix_ones = jnp.where(r_i >= c_i, 1.0, 0.0).astype(BF16)
    identity = jnp.where(r_i == c_i, 1.0, 0.0).astype(BF16)

    def bias_body(c, seen):
        off = pl.multiple_of(c * kc, kc)
        tiles = range(kc // LANE)
        xs = [key_ref[pl.ds(off + t * LANE, LANE), :] for t in tiles]
        eqs = [x == tau for x in xs]
        eqfs = [jnp.where(eq, 1.0, 0.0) for eq in eqs]
        ranks = [jnp.dot(prefix_ones, eqf.astype(BF16), preferred_element_type=F32) for eqf in eqfs]
        sels = []
        for t in tiles:
            sel = ((xs[t] > tau) | (eqs[t] & (ranks[t] + seen <= room))) & (xs[t] != KEY_OF_NEG_INF)
            sels.append(sel)
            seen = seen + jnp.sum(eqfs[t], axis=0, keepdims=True)
        if qb == LANE:
            for t in tiles:
                bias_ref[:, pl.ds(off + t * LANE, LANE)] = jnp.where(sels[t], 0.0, MASKED).T
        else:
            sel_ts = [lax.dot_general(jnp.where(sel, 1.0, 0.0).astype(BF16), identity, (((0,), (0,)), ((), ())),
                                      preferred_element_type=F32) for sel in sels]
            for t in tiles:
                bias_ref[:, pl.ds(off + t * LANE, LANE)] = jnp.where(sel_ts[t] > 0.5, 0.0, MASKED)
        return seen

    lax.fori_loop(0, nch, bias_body, jnp.zeros((1, qb), F32))

    lane = lax.broadcasted_iota(jnp.int32, (qb, LANE), 1)
    qgs = [jnp.concatenate([q_ref[:, (hpg * g + h) * LANE:(hpg * g + h + 1) * LANE] for h in range(hpg)], axis=0)
           for g in range(A_KV_HEADS)]

    def att_body(c, carry):
        off = pl.multiple_of(c * kc, kc)
        kch = k_ref[pl.ds(off, kc), :]
        vch = v_ref[pl.ds(off, kc), :]
        bias = bias_ref[:, pl.ds(off, kc)][None]
        new = [None] * A_KV_HEADS

        def group(g):
            m, l, acc = carry[g]
            s = lax.dot_general(qgs[g], kch, nt, preferred_element_type=F32)
            yield
            s = (s.reshape(hpg, qb, kc) + bias).reshape(hpg * qb, kc)
            m_new = jnp.maximum(m, jnp.max(s, axis=1, keepdims=True))
            alpha = jnp.exp(m - m_new)
            p = jnp.exp(s - m_new)
            l = alpha * l + jnp.sum(p, axis=1, keepdims=True)
            p16 = p.astype(BF16)
            yield
            new[g] = (m_new, l, alpha * acc + jnp.dot(p16, vch, preferred_element_type=F32))

        _lockstep(group(g) for g in range(A_KV_HEADS))
        return tuple(new)

    init = tuple((jnp.full((hpg * qb, 1), MASKED, F32), jnp.zeros((hpg * qb, 1), F32),
                  jnp.zeros((hpg * qb, LANE), F32)) for _ in range(A_KV_HEADS))
    res = lax.fori_loop(0, nch, att_body, init)
    outs = []
    for g in range(A_KV_HEADS):
        _, l, acc = res[g]
        og = acc / l
        for h in range(hpg):
            oh = og[h * qb:(h + 1) * qb]
            outs.append(oh if (h % 2) == g else pltpu.roll(oh, HALF, 1))
    for p in range(A_HEADS // 2):
        o_ref[:, p * LANE:(p + 1) * LANE] = jnp.where(lane < HALF, outs[2 * p], outs[2 * p + 1])


def dsa_attention(qpad, qipad, proj, limit, k16, v16, ki16, *, bsz, tq, tk, causal, n_sel):
    qb = min(Q_BLOCK, tq)
    nqb = tq // qb
    assert tk % KEY_CHUNK == 0 and tk >= n_sel
    if causal:
        per_group = KEY_CHUNK // qb
        groups = [(g * per_group, per_group, g + 1) for g in range(nqb // per_group)]
    else:
        groups = [(0, nqb, tk // KEY_CHUNK)]
    lim3 = limit.reshape(bsz * nqb, 1, qb)
    out = None
    for first, count, nch in groups:
        def qspec(width, col=0, first=first):
            return pl.BlockSpec((qb, width), lambda b, i: (b * nqb + first + i, col))

        kspec = pl.BlockSpec((tk, LANE), lambda b, i: (b, 0))
        in_specs = [qspec(A_HEADS * LANE), qspec(IDX_HEADS * LANE), qspec(LANE, AB_MISC // LANE),
                    pl.BlockSpec((1, 1, qb), lambda b, i, first=first: (b * nqb + first + i, 0, 0)),
                    kspec, kspec, kspec]
        args = [qpad, qipad, proj, lim3, k16, v16, ki16]
        kern = functools.partial(_dsa_kernel, nch=nch, n_sel=n_sel)
        aliases = {}
        if out is not None:
            in_specs.append(pl.BlockSpec(memory_space=pl.ANY))
            args.append(out)
            aliases = {len(args) - 1: 0}
            kern = functools.partial(_dsa_kernel_with_carry, nch=nch, n_sel=n_sel)
        out = pl.pallas_call(
            kern,
            grid=(bsz, count),
            in_specs=in_specs,
            out_specs=qspec(A_HEADS * HEAD_DIM),
            out_shape=jax.ShapeDtypeStruct((bsz * tq, A_HEADS * HEAD_DIM), F32),
            scratch_shapes=[pltpu.VMEM((tk, qb), jnp.int32), pltpu.VMEM((qb, tk), F32),
                            pltpu.VMEM((tk, qb), jnp.int16), pltpu.VMEM((tk, qb), jnp.int16)],
            input_output_aliases=aliases,
            compiler_params=pltpu.CompilerParams(
                dimension_semantics=("parallel", "arbitrary"), vmem_limit_bytes=VMEM_LIMIT),
            name="dsa_attention",
        )(*args)
    return out


def _dsa_kernel_with_carry(q_ref, qi_ref, misc_ref, lim_ref, k_ref, v_ref, ki_ref, carry_ref, o_ref, *scratch,
                           nch, n_sel):
    del carry_ref
    _dsa_kernel(q_ref, qi_ref, misc_ref, lim_ref, k_ref, v_ref, ki_ref, o_ref, *scratch, nch=nch, n_sel=n_sel)


SUB = 16


def _dot(a, b):
    return jnp.dot(a, b, preferred_element_type=F32)


def _dot_nt(a, b):
    return lax.dot_general(a, b, (((1,), (1,)), ((), ())), preferred_element_type=F32)


def _dot_tn(a, b):
    return lax.dot_general(a, b, (((0,), (0,)), ((), ())), preferred_element_type=F32)


def _split2(x):
    hi = x.astype(BF16)
    return hi, (x - hi.astype(F32)).astype(BF16)


def _split3(x):
    hi = x.astype(BF16)
    r = x - hi.astype(F32)
    mid = r.astype(BF16)
    return hi, mid, (r - mid.astype(F32)).astype(BF16)


def _cumsum_rows(x, tril16):
    hi, mid, lo = _split3(x)
    return _dot(tril16, hi) + _dot(tril16, mid) + _dot(tril16, lo)


def _dot_f32(a, b):
    ah, al = _split2(a)
    bh, bl = _split2(b)
    return _dot(ah, bh) + (_dot(ah, bl) + _dot(al, bh))


def _tri_mask(n, strict=False):
    r = lax.broadcasted_iota(jnp.int32, (n, n), 0)
    c = lax.broadcasted_iota(jnp.int32, (n, n), 1)
    return r > c if strict else r >= c


def _rows_to_lanes(x):
    rows = x.shape[0]
    if rows < LANE:
        x = jnp.concatenate([x, jnp.zeros((LANE - rows, LANE), x.dtype)], axis=0)
    return x.T


MLSTM_SEQS_PER_STEP = 2
CD_SEQS_PER_STEP = 4


def _lockstep(stages):
    stages = list(stages)
    while stages:
        for g in list(stages):
            if next(g, StopIteration) is StopIteration:
                stages.remove(g)


def _chunk_call(kern, *, bsz, nb, nc, rows, ins, outs, scratch, name):
    def spec(a, kind, width, offset):
        if kind == 'rows':
            return pl.BlockSpec((nb, rows, width), lambda b, c: (b, c, offset // width))
        if kind == 'batch':
            return pl.BlockSpec((nb,) + tuple(a.shape[1:]), lambda b, c: (b,) + (0,) * (len(a.shape) - 1))
        return pl.BlockSpec(tuple(a.shape), lambda b, c: (0,) * len(a.shape))

    return pl.pallas_call(
        kern,
        grid=(bsz // nb, nc),
        in_specs=[spec(*i) for i in ins],
        out_specs=[spec(*o) for o in outs],
        out_shape=[o[0] for o in outs],
        scratch_shapes=scratch,
        compiler_params=pltpu.CompilerParams(
            dimension_semantics=("parallel", "arbitrary"), vmem_limit_bytes=VMEM_LIMIT),
        name=name,
    )(*[i[0] for i in ins])


def _mlstm_kernel(q_ref, k_ref, v_ref, og_ref, misc_ref, gb_ref, gain_ref, c0_ref, n0_ref, m0_ref,
                  h_ref, c_out_ref, n_out_ref, m_out_ref, c_scr, n_scr, m_scr):
    ci = pl.program_id(1)
    seqs = range(q_ref.shape[0])

    @pl.when(ci == 0)
    def _():
        c_scr[...] = c0_ref[...]
        n_scr[...] = n0_ref[...]
        m_scr[...] = m0_ref[...]

    _lockstep(_mlstm_chunk(q_ref.at[i], k_ref.at[i], v_ref.at[i], og_ref.at[i], misc_ref.at[i], gb_ref, gain_ref,
                           h_ref.at[i], c_scr.at[i], n_scr.at[i], m_scr.at[i]) for i in seqs)

    @pl.when(ci == pl.num_programs(1) - 1)
    def _():
        c_out_ref[...] = c_scr[...]
        n_out_ref[...] = n_scr[...]
        m_out_ref[...] = m_scr[...]


def _mlstm_chunk(q_ref, k_ref, v_ref, og_ref, misc_ref, gb_ref, gain_ref, h_ref, c_scr, n_scr, m_scr):
    rows = q_ref.shape[0]
    hr = B_HEADS * rows
    wq = B_HEADS * B_QK_DIM
    tril16 = jnp.where(_tri_mask(rows), 1.0, 0.0).astype(BF16)
    gates = misc_ref[...] + gb_ref[...]
    bcum = _cumsum_rows(jax.nn.log_sigmoid(gates), tril16)
    yield
    m_all = m_scr[...]

    def stack(f):
        return jnp.concatenate([f(h) for h in range(B_HEADS)], axis=0)

    lane_q = lax.broadcasted_iota(jnp.int32, (rows, wq), 1)
    qx, kx = q_ref[...], k_ref[...]
    q_all = stack(lambda h: jnp.where(lane_q // B_QK_DIM == h, qx, 0.0))
    k_all = stack(lambda h: jnp.where(lane_q // B_QK_DIM == h, kx, 0.0)) * B_QK_DIM ** -0.5
    v_all = stack(lambda h: v_ref[:, h * B_V_DIM:(h + 1) * B_V_DIM])
    b_col = stack(lambda h: bcum[:, MISC_FB + h:MISC_FB + h + 1])
    i_col = stack(lambda h: gates[:, MISC_IB + h:MISC_IB + h + 1])
    m_col = stack(lambda h: jnp.broadcast_to(m_all[:, h:h + 1], (rows, 1)))
    b_end = stack(lambda h: jnp.broadcast_to(bcum[rows - 1:rows, MISC_FB + h:MISC_FB + h + 1], (rows, 1)))
    bi_row = jnp.broadcast_to(b_col - i_col, (hr, LANE)).T[0:1, :]

    r = lax.broadcasted_iota(jnp.int32, (hr, hr), 0)
    c = lax.broadcasted_iota(jnp.int32, (hr, hr), 1)
    incl = ((r // rows) == (c // rows)) & (r >= c)
    dmat = jnp.where(incl, b_col - bi_row, -jnp.inf)
    inter = b_col + m_col
    mrow = jnp.maximum(inter, jnp.max(dmat, axis=1, keepdims=True))
    w_state = jnp.exp(inter - mrow)
    q16 = q_all.astype(BF16)
    v16 = v_all.astype(BF16)
    scores = _dot_nt(q16, k_all.astype(BF16)) * jnp.exp(dmat - mrow)
    yield
    cs = c_scr[...]
    n_row = n_scr[...]
    num = _dot(scores.astype(BF16), v16) + w_state * _dot(q16, cs.astype(BF16))
    den = jnp.sum(scores, axis=1, keepdims=True) + w_state * jnp.sum(q_all * n_row, axis=1, keepdims=True)
    hh = num / jnp.maximum(jnp.abs(den), jnp.exp(-mrow))
    yield
    gain = gain_ref[...]
    for h in range(B_HEADS):
        h_ref[:, h * B_V_DIM:(h + 1) * B_V_DIM] = (_rms_rows(hh[h * rows:(h + 1) * rows], gain)
                                                   * jax.nn.sigmoid(og_ref[:, h * B_V_DIM:(h + 1) * B_V_DIM]))

    g_col = b_end - b_col + i_col
    lane1 = lax.broadcasted_iota(jnp.int32, (1, LANE), 1)
    m_next = m_all
    m_new_rows, keep_rows, keep_lanes = [], [], []
    for h in range(B_HEADS):
        m_h = m_all[:, h:h + 1]
        be = bcum[rows - 1:rows, MISC_FB + h:MISC_FB + h + 1]
        m_new = jnp.maximum(be + m_h, jnp.max(g_col[h * rows:(h + 1) * rows], axis=0, keepdims=True))
        keep = jnp.exp(be + m_h - m_new)
        m_next = jnp.where(lane1 == h, m_new, m_next)
        m_new_rows.append(jnp.broadcast_to(m_new, (rows, 1)))
        keep_rows.append(jnp.broadcast_to(keep, (B_QK_DIM, 1)))
        keep_lanes.append(jnp.broadcast_to(keep, (1, B_QK_DIM)))
    kw = k_all * jnp.exp(g_col - jnp.concatenate(m_new_rows, axis=0))
    c_scr[...] = jnp.concatenate(keep_rows, axis=0) * cs + _dot_tn(kw.astype(BF16), v16)
    n_scr[...] = jnp.concatenate(keep_lanes, axis=1) * n_row + jnp.sum(kw, axis=0, keepdims=True)
    m_scr[...] = m_next
    yield


def mlstm_mixer(proj, gate_bias, gain, c0, n0, m0, *, bsz, t, chunk):
    nc = t // chunk
    nb = MLSTM_SEQS_PER_STEP
    gb = jnp.zeros((1, LANE), F32)
    gb = gb.at[0, MISC_IB:MISC_IB + B_HEADS].set(gate_bias[0]).at[0, MISC_FB:MISC_FB + B_HEADS].set(gate_bias[1])
    c0 = c0.reshape(bsz, B_HEADS * B_QK_DIM, B_V_DIM)
    n0 = n0.reshape(bsz, 1, B_HEADS * B_QK_DIM)
    m0 = jnp.pad(m0, ((0, 0), (0, LANE - B_HEADS))).reshape(bsz, 1, LANE)
    wq = B_HEADS * B_QK_DIM
    wv = B_HEADS * B_V_DIM
    h, c, n, m = _chunk_call(
        _mlstm_kernel, bsz=bsz, nb=nb, nc=nc, rows=chunk,
        ins=[(proj, 'rows', wq, AB_QB), (proj, 'rows', wq, AB_KB), (proj, 'rows', wv, AB_VB),
             (proj, 'rows', wv, AB_OB), (proj, 'rows', LANE, AB_MISC), (gb, 'const', 0, 0),
             (gain.reshape(1, B_V_DIM), 'const', 0, 0), (c0, 'batch', 0, 0), (n0, 'batch', 0, 0),
             (m0, 'batch', 0, 0)],
        outs=[(jax.ShapeDtypeStruct((bsz, t, wv), F32), 'rows', wv, 0),
              (jax.ShapeDtypeStruct(c0.shape, F32), 'batch', 0, 0),
              (jax.ShapeDtypeStruct(n0.shape, F32), 'batch', 0, 0),
              (jax.ShapeDtypeStruct(m0.shape, F32), 'batch', 0, 0)],
        scratch=[pltpu.VMEM((nb, wq, B_V_DIM), F32), pltpu.VMEM((nb, 1, wq), F32), pltpu.VMEM((nb, 1, LANE), F32)],
        name="mlstm_mixer")
    return (h.reshape(bsz * t, wv), c.reshape(bsz, B_HEADS, B_QK_DIM, B_V_DIM), n.reshape(bsz, B_HEADS, B_QK_DIM),
            m.reshape(bsz, LANE)[:, :B_HEADS])


CD_QKV, CD_ZC, CD_QD, CD_FD, CD_VD, CD_GD, CD_MISC, CD_TOTAL = 0, 1536, 2048, 2560, 3072, 3584, 4096, 4224
MISC_BC, MISC_AC = 0, 4
TAIL = 8


def _permute_w_in_cd(w):
    qkv, bc, ac, zc, qd, fd, vd, gd = _split_cols(w, ODD_SPLITS)
    pad = jnp.zeros((w.shape[0], LANE - 2 * C_HEADS), w.dtype)
    return jnp.concatenate([qkv, zc, qd, fd, vd, gd, bc, ac, pad], axis=1)


def _gdn_init(s0_ref, tail0_ref, st_scr, tail_scr):
    for h in range(C_HEADS):
        st_scr[:, h * C_DIM:(h + 1) * C_DIM] = s0_ref[h].T
    tail_scr[...] = tail0_ref[...]


def _gdn_final(s_out_ref, st_scr):
    for h in range(C_HEADS):
        s_out_ref[h] = st_scr[:, h * C_DIM:(h + 1) * C_DIM].T


def _gdn_chunk(qkv_ref, z_ref, misc_ref, cw_ref, alog_ref, dt_ref, gain_ref, o_ref, st_scr, tail_scr):
    rows = qkv_ref.shape[0]
    width = qkv_ref.shape[1]
    x = qkv_ref[...]
    tail = tail_scr[...]
    row8 = lax.broadcasted_iota(jnp.int32, (TAIL, width), 0)
    acc = x * cw_ref[CONV_W - 1:CONV_W, :]
    for back in range(1, CONV_W):
        rolled = pltpu.roll(x, back, 0)
        first = jnp.where(row8 < back, pltpu.roll(tail, back, 0), rolled[0:TAIL])
        shifted = first if rows == TAIL else jnp.concatenate([first, rolled[TAIL:]], axis=0)
        acc = acc + shifted * cw_ref[CONV_W - 1 - back:CONV_W - back, :]
    tail_scr[...] = x[rows - TAIL:rows]
    conv = acc * jax.nn.sigmoid(acc)

    tril16 = jnp.where(_tri_mask(rows), 1.0, 0.0).astype(BF16)
    misc = misc_ref[...]
    beta_t = jax.nn.sigmoid(misc)
    g_t = -jnp.exp(alog_ref[...]) * jax.nn.softplus(misc + dt_ref[...])
    gcum = _cumsum_rows(g_t, tril16)
    yield

    hd = C_HEADS * C_DIM
    hr = C_HEADS * rows

    def stack(f):
        return jnp.concatenate([f(h) for h in range(C_HEADS)], axis=0)

    def l2n(v):
        return v * lax.rsqrt(jnp.sum(v * v, axis=-1, keepdims=True) + EPS)

    q_all = stack(lambda h: l2n(conv[:, h * C_DIM:(h + 1) * C_DIM])) * C_DIM ** -0.5
    k_all = stack(lambda h: l2n(conv[:, hd + h * C_DIM:hd + (h + 1) * C_DIM]))
    v_all = stack(lambda h: conv[:, 2 * hd + h * C_DIM:2 * hd + (h + 1) * C_DIM])
    beta = stack(lambda h: beta_t[:, MISC_BC + h:MISC_BC + h + 1])
    gc = stack(lambda h: gcum[:, MISC_AC + h:MISC_AC + h + 1])
    g_end = stack(lambda h: jnp.broadcast_to(gcum[rows - 1:rows, MISC_AC + h:MISC_AC + h + 1], (rows, 1)))
    gc_row = jnp.broadcast_to(gc, (hr, LANE)).T[0:1, :]

    r = lax.broadcasted_iota(jnp.int32, (hr, hr), 0)
    c = lax.broadcasted_iota(jnp.int32, (hr, hr), 1)
    same = (r // rows) == (c // rows)
    incl = same & (r >= c)
    strict = same & (r > c)
    decay = jnp.exp(jnp.where(incl, gc - gc_row, -jnp.inf))
    k16 = k_all.astype(BF16)
    a_mat = jnp.where(strict, beta * _dot_nt(k16, k16) * decay, 0.0)
    yield
    power = -a_mat
    inv = jnp.where(r == c, 1.0, 0.0) + power
    for _ in range(int(math.log2(rows)) - 1):
        power = _dot_f32(power, power)
        yield
        inv = inv + _dot_f32(inv, power)
        yield
    inv_hi, inv_lo = _split2(inv)
    rhs = jnp.concatenate([beta * v_all, beta * jnp.exp(gc) * k_all], axis=1).astype(BF16)
    w = _dot(inv_hi, rhs) + _dot(inv_lo, rhs)
    yield
    w_v, w_k = w[:, 0:C_DIM], w[:, C_DIM:2 * C_DIM]
    qk = _dot_nt(q_all.astype(BF16), k16) * decay
    yield

    head_of_row = lax.broadcasted_iota(jnp.int32, (hr, C_DIM), 0) // rows

    def per_head_lanes(m):
        return jnp.concatenate([jnp.where(head_of_row == h, m, 0.0) for h in range(C_HEADS)], axis=1).astype(BF16)

    st = st_scr[...]
    st16 = st.astype(BF16)
    delta = w_v - _dot_nt(per_head_lanes(w_k), st16)
    yield
    d16 = delta.astype(BF16)
    out = _dot_nt(per_head_lanes(q_all * jnp.exp(gc)), st16) + _dot(qk.astype(BF16), d16)
    keep = jnp.concatenate([jnp.broadcast_to(jnp.exp(gcum[rows - 1:rows, MISC_AC + h:MISC_AC + h + 1]), (1, C_DIM))
                            for h in range(C_HEADS)], axis=1)
    st_scr[...] = keep * st + _dot_tn(d16, per_head_lanes(k_all * jnp.exp(g_end - gc)))
    yield
    gain = gain_ref[...]
    for h in range(C_HEADS):
        z = z_ref[:, h * C_DIM:(h + 1) * C_DIM]
        o_ref[:, h * C_DIM:(h + 1) * C_DIM] = _rms_rows(out[h * rows:(h + 1) * rows], gain) * (z * jax.nn.sigmoid(z))


def _hgrn2_init(s0_ref, st_scr):
    for h in range(D_HEADS):
        st_scr[h] = s0_ref[h].T


def _hgrn2_final(s_out_ref, st_scr):
    for h in range(D_HEADS):
        s_out_ref[h] = st_scr[h].T


def _hgrn2_chunk(q_ref, f_ref, v_ref, g_ref, lb_ref, gain_ref, o_ref, st_scr):
    rows = q_ref.shape[0]
    tril16 = jnp.where(_tri_mask(rows), 1.0, 0.0).astype(BF16)
    lb = lb_ref[...]
    zf = f_ref[...]
    logf = jnp.logaddexp(jnp.log(lb), jnp.log1p(-lb) + jax.nn.log_sigmoid(zf))
    kd = (1.0 - lb) * jax.nn.sigmoid(-zf)
    qx = q_ref[...]
    qd = qx * jax.nn.sigmoid(qx)
    bcum = _cumsum_rows(logf, tril16)
    yield
    gain = gain_ref[...]
    row_sub = lax.broadcasted_iota(jnp.int32, (SUB, 1), 0)
    for h in range(D_HEADS):
        sl = slice(h * D_EXPAND, (h + 1) * D_EXPAND)
        q, k, b = qd[:, sl], kd[:, sl], bcum[:, sl]
        v = v_ref[:, h * D_V_DIM:(h + 1) * D_V_DIM]
        v16 = v.astype(BF16)
        st = st_scr[h]
        inter = _dot_nt((q * jnp.exp(b)).astype(BF16), st.astype(BF16))
        blocks = []
        for i in range(rows // SUB):
            r0 = i * SUB
            qi, bi = q[r0:r0 + SUB], b[r0:r0 + SUB]
            oi = inter[r0:r0 + SUB]
            if i > 0:
                ref = b[r0 - 1:r0]
                att = _dot_nt((qi * jnp.exp(bi - ref)).astype(BF16),
                              (k[0:r0] * jnp.exp(ref - b[0:r0])).astype(BF16))
                oi = oi + _dot(att.astype(BF16), v16[0:r0])
            for s in range(SUB):
                r = r0 + s
                a = jnp.sum(qi * jnp.exp(bi - b[r:r + 1]) * k[r:r + 1], axis=1, keepdims=True)
                oi = oi + jnp.where(row_sub >= s, a, 0.0) * v[r:r + 1]
            blocks.append(oi)
        out = blocks[0] if len(blocks) == 1 else jnp.concatenate(blocks, axis=0)
        b_end = b[rows - 1:rows]
        st_scr[h] = jnp.exp(b_end) * st + _dot_tn(v16, (k * jnp.exp(b_end - b)).astype(BF16))
        g = g_ref[:, h * D_V_DIM:(h + 1) * D_V_DIM]
        o_ref[:, h * D_V_DIM:(h + 1) * D_V_DIM] = _rms_rows(out, gain) * (g * jax.nn.sigmoid(g))
        yield


def _cd_kernel(qkv_ref, z_ref, misc_ref, cw_ref, alog_ref, dt_ref, cgain_ref, sc0_ref, tail0_ref,
               qd_ref, fd_ref, vd_ref, gd_ref, lb_ref, dgain_ref, sd0_ref,
               oc_ref, sc_out_ref, od_ref, sd_out_ref, stc_scr, tail_scr, std_scr):
    ci = pl.program_id(1)
    seqs = range(qkv_ref.shape[0])

    @pl.when(ci == 0)
    def _():
        for i in seqs:
            _gdn_init(sc0_ref.at[i], tail0_ref.at[i], stc_scr.at[i], tail_scr.at[i])
            _hgrn2_init(sd0_ref.at[i], std_scr.at[i])

    gdn = [_gdn_chunk(qkv_ref.at[i], z_ref.at[i], misc_ref.at[i], cw_ref, alog_ref, dt_ref, cgain_ref,
                      oc_ref.at[i], stc_scr.at[i], tail_scr.at[i]) for i in seqs]
    hgrn2 = [_hgrn2_chunk(qd_ref.at[i], fd_ref.at[i], vd_ref.at[i], gd_ref.at[i], lb_ref, dgain_ref, od_ref.at[i],
                          std_scr.at[i]) for i in seqs]
    _lockstep(gdn + hgrn2)

    @pl.when(ci == pl.num_programs(1) - 1)
    def _():
        for i in seqs:
            _gdn_final(sc_out_ref.at[i], stc_scr.at[i])
            _hgrn2_final(sd_out_ref.at[i], std_scr.at[i])


def cd_mixers(proj, conv_w, a_log, dt_bias, c_gain, sc0, conv_prev, lower_bound, d_gain, sd0, *, bsz, t, chunk):
    nc = t // chunk
    nb = CD_SEQS_PER_STEP
    hd = C_HEADS * C_DIM
    wk = D_HEADS * D_EXPAND
    wv = D_HEADS * D_V_DIM
    lanes = jnp.zeros((1, LANE), F32)
    alog = lanes.at[0, MISC_AC:MISC_AC + C_HEADS].set(a_log)
    dt = lanes.at[0, MISC_AC:MISC_AC + C_HEADS].set(dt_bias)
    tail0 = jnp.pad(conv_prev, ((0, 0), (TAIL - (CONV_W - 1), 0), (0, 0)))
    oc, sc, od, sd = _chunk_call(
        _cd_kernel, bsz=bsz, nb=nb, nc=nc, rows=chunk,
        ins=[(proj, 'rows', 3 * hd, CD_QKV), (proj, 'rows', hd, CD_ZC), (proj, 'rows', LANE, CD_MISC),
             (conv_w, 'const', 0, 0), (alog, 'const', 0, 0), (dt, 'const', 0, 0),
             (c_gain.reshape(1, C_DIM), 'const', 0, 0), (sc0, 'batch', 0, 0), (tail0, 'batch', 0, 0),
             (proj, 'rows', wk, CD_QD), (proj, 'rows', wk, CD_FD), (proj, 'rows', wv, CD_VD),
             (proj, 'rows', wv, CD_GD), (lower_bound.reshape(1, wk), 'const', 0, 0),
             (d_gain.reshape(1, D_V_DIM), 'const', 0, 0), (sd0, 'batch', 0, 0)],
        outs=[(jax.ShapeDtypeStruct((bsz, t, hd), F32), 'rows', hd, 0),
              (jax.ShapeDtypeStruct(sc0.shape, F32), 'batch', 0, 0),
              (jax.ShapeDtypeStruct((bsz, t, wv), F32), 'rows', wv, 0),
              (jax.ShapeDtypeStruct(sd0.shape, F32), 'batch', 0, 0)],
        scratch=[pltpu.VMEM((nb, C_DIM, hd), F32), pltpu.VMEM((nb, TAIL, 3 * hd), F32),
                 pltpu.VMEM((nb, D_HEADS, D_V_DIM, D_EXPAND), F32)],
        name="cd_mixers")
    return oc.reshape(bsz * t, hd), sc, od.reshape(bsz * t, wv), sd


def _pad_cols(w, mult=LANE):
    pad = (-w.shape[-1]) % mult
    return jnp.pad(w, [(0, 0)] * (w.ndim - 1) + [(0, pad)])


def _mixer_ab(proj, bsz, t, pos, prm, cache):
    n = bsz * t
    qpad, qipad, k16, v16, ki16, k32, v32, ki32 = dsa_prep(proj, pos, prm['a_q_gain'][0], prm['a_k_gain'][0], t)
    if cache is None:
        limit = jnp.tile((pos // CHUNK + 1) * CHUNK, bsz).reshape(n, 1)
        a_out = dsa_attention(qpad, qipad, proj, limit, k16, v16, ki16, bsz=bsz, tq=t, tk=t,
                              causal=True, n_sel=min(TOPK_MAX, t // 4))
        c0 = jnp.zeros((bsz, B_HEADS, B_QK_DIM, B_V_DIM), F32)
        n0 = jnp.zeros((bsz, B_HEADS, B_QK_DIM), F32)
        m0 = jnp.zeros((bsz, B_HEADS), F32)
        chunk = CHUNK
    else:
        k_c, v_c, ki_c, c0, n0, m0 = cache
        past = k_c.shape[1]
        n_keys = past + t
        tk = -(-n_keys // KEY_CHUNK) * KEY_CHUNK

        def with_cache(c, new):
            c = c.reshape(bsz, past, -1).astype(BF16)
            c = jnp.pad(c, ((0, 0), (0, 0), (0, LANE - c.shape[-1])))
            return jnp.concatenate([c, new.reshape(bsz, t, LANE),
                                    jnp.zeros((bsz, tk - n_keys, LANE), BF16)], axis=1).reshape(bsz * tk, LANE)

        limit = jnp.full((n, 1), n_keys, jnp.int32)
        a_out = dsa_attention(qpad, qipad, proj, limit, with_cache(k_c, k16), with_cache(v_c, v16),
                              with_cache(ki_c, ki16), bsz=bsz, tq=t, tk=tk, causal=False,
                              n_sel=min(TOPK_MAX, n_keys // 4))
        chunk = t
    h, c, n_, m = mlstm_mixer(proj.reshape(bsz, t, -1), prm['b_gate_bias'][0], prm['b_norm_gain'][0], c0, n0, m0,
                              bsz=bsz, t=t, chunk=chunk)
    st = (k32.reshape(bsz, t, A_KV_HEADS, HEAD_DIM), v32.reshape(bsz, t, A_KV_HEADS, HEAD_DIM),
          ki32.reshape(bsz, t, IDX_DIM), c, n_, m)
    return a_out, h, st


def _mixer_cd(proj, bsz, t, prm, lower_bound, cache):
    hd = C_HEADS * C_DIM
    if cache is None:
        sc0 = jnp.zeros((bsz, C_HEADS, C_DIM, C_DIM), F32)
        conv_prev = jnp.zeros((bsz, CONV_W - 1, 3 * hd), F32)
        sd0 = jnp.zeros((bsz, D_HEADS, D_EXPAND, D_V_DIM), F32)
        chunk = CHUNK
    else:
        sc0, conv_prev, sd0 = cache
        chunk = t
    oc, sc, od, sd = cd_mixers(proj.reshape(bsz, t, -1), prm['c_conv_w'][0], prm['c_a_log'][0], prm['c_dt_bias'][0],
                               prm['c_norm_gain'][0], sc0, conv_prev, lower_bound, prm['d_norm_gain'][0], sd0,
                               bsz=bsz, t=t, chunk=chunk)
    qkv = proj.reshape(bsz, t, -1)[:, :, CD_QKV:CD_QKV + 3 * hd]
    conv_new = jnp.concatenate([conv_prev, qkv[:, t - (CONV_W - 1):]], axis=1)[:, -(CONV_W - 1):]
    return oc, od, (sc, conv_new, sd)


def _trunk(x, pos_offset, cache, prm, wts):
    bsz, t, d = x.shape
    n = bsz * t
    pos = pos_offset + jnp.arange(t, dtype=jnp.int32)
    probs = jax.nn.softmax(prm['d_lb_logits'], axis=0)
    lower_bounds = jnp.cumsum(probs, axis=0) - probs[0]
    xf = x.reshape(n, d)

    lc = None if cache is None else tuple(c[0] for c in cache[:6])
    proj = norm_matmul(xf, prm['norm_mix'][0], wts['w_in_ab'])
    a_out, b_out, st_even = _mixer_ab(proj, bsz, t, pos, prm, lc)
    xf = matmul_residual(a_out, b_out, wts['w_out_ab'], xf)
    xf = ffn_residual(xf, prm['norm_ffn'][0], wts['ffn_w1'], wts['ffn_w3'], wts['ffn_w2'])

    lc = None if cache is None else tuple(c[0] for c in cache[6:])
    proj = norm_matmul(xf, prm['norm_mix'][1], wts['w_in_cd'])
    c_out, d_out, st_odd = _mixer_cd(proj, bsz, t, prm, lower_bounds[1], lc)
    xf = matmul_residual(c_out, d_out, wts['w_out_cd'], xf)
    xf = moe_residual(xf, prm['norm_ffn'][1], wts['moe_router'], wts['moe_w1'], wts['moe_w3'], wts['moe_w2'])

    new_state = tuple(s[None] for s in st_even + st_odd)
    return xf.reshape(bsz, t, d), new_state


def kernel(x_prompt, x_sample, cache_a_k, cache_a_v, cache_a_kidx, state_b_c, state_b_n, state_b_m,
           state_c_s, state_c_conv, state_d_s, norm_mix, norm_ffn, w_in_ab, w_out_ab, a_q_gain, a_k_gain,
           b_gate_bias, b_norm_gain, w_in_cd, w_out_cd, c_conv_w, c_a_log, c_dt_bias, c_norm_gain,
           d_lb_logits, d_norm_gain, ffn_w1, ffn_w3, ffn_w2, moe_router, moe_w1, moe_w3, moe_w2):
    prm = dict(norm_mix=norm_mix, norm_ffn=norm_ffn, a_q_gain=a_q_gain, a_k_gain=a_k_gain,
               b_gate_bias=b_gate_bias, b_norm_gain=b_norm_gain, c_conv_w=c_conv_w, c_a_log=c_a_log,
               c_dt_bias=c_dt_bias, c_norm_gain=c_norm_gain, d_lb_logits=d_lb_logits, d_norm_gain=d_norm_gain)
    wts = dict(w_in_ab=_permute_w_in_ab(w_in_ab[0]).astype(BF16), w_out_ab=w_out_ab[0].astype(BF16),
               w_in_cd=_permute_w_in_cd(w_in_cd[0]).astype(BF16), w_out_cd=w_out_cd[0].astype(BF16),
               ffn_w1=ffn_w1[0].astype(BF16), ffn_w3=ffn_w3[0].astype(BF16), ffn_w2=ffn_w2[0].astype(BF16),
               moe_router=_pad_cols(moe_router[0]),
               moe_w1=moe_w1[0].astype(BF16), moe_w3=moe_w3[0].astype(BF16), moe_w2=moe_w2[0].astype(BF16))
    cache = (cache_a_k, cache_a_v, cache_a_kidx, state_b_c, state_b_n, state_b_m, state_c_s, state_c_conv, state_d_s)
    y_prompt, st_p = _trunk(x_prompt, 0, None, prm, wts)
    y_sample, st_s = _trunk(x_sample, cache_a_k.shape[2], cache, prm, wts)
    return (y_prompt, y_sample) + st_p + st_s
```

```python
import functools
import math

import jax
import jax.numpy as jnp
import numpy as np
from jax import lax
from jax.experimental import pallas as pl
from jax.experimental.pallas import tpu as pltpu

F32 = jnp.float32
BF16 = jnp.bfloat16

EPS = 1e-6
ROPE_THETA = 500000.0
ROT_FRACTION = 4
CHUNK = 64
A_HEADS, A_KV_HEADS, HEAD_DIM = 8, 2, 64
IDX_HEADS, IDX_DIM = 4, 64
TOPK_MAX, Q_BLOCK = 256, 128
B_HEADS, B_QK_DIM, B_V_DIM = 4, 64, 128
C_HEADS, C_DIM, CONV_W = 4, 128, 4
D_HEADS, D_EXPAND, D_V_DIM = 4, 128, 128
N_EXPERTS, TOP_K_EXPERTS = 8, 2

LANE = 128
VMEM_LIMIT = 48 * 1024 * 1024

EVEN_SPLITS = (A_HEADS * HEAD_DIM, A_KV_HEADS * HEAD_DIM, A_KV_HEADS * HEAD_DIM,
               IDX_HEADS * IDX_DIM, IDX_DIM, IDX_HEADS,
               B_HEADS * B_QK_DIM, B_HEADS * B_QK_DIM, B_HEADS * B_V_DIM,
               B_HEADS, B_HEADS, B_HEADS * B_V_DIM)
ODD_SPLITS = (3 * C_HEADS * C_DIM, C_HEADS, C_HEADS, C_HEADS * C_DIM,
              D_HEADS * D_EXPAND, D_HEADS * D_EXPAND, D_HEADS * D_V_DIM, D_HEADS * D_V_DIM)


def _split_cols(p, widths):
    cuts = [int(c) for c in np.cumsum(widths)[:-1]]
    return jnp.split(p, cuts, axis=-1)


def _row_tile(n, target):
    t = min(n, target)
    while n % t:
        t //= 2
    return t


def _col_tile(n, target):
    best = LANE
    for k in range(1, n // LANE + 1):
        c = k * LANE
        if n % c == 0 and c <= target:
            best = c
    return best


def _rms_rows(x, gain):
    return x * lax.rsqrt(jnp.mean(x * x, axis=-1, keepdims=True) + EPS) * gain


def _norm_matmul_kernel(x_ref, g_ref, w_ref, o_ref, xn_ref):
    @pl.when(pl.program_id(1) == 0)
    def _():
        xn_ref[...] = _rms_rows(x_ref[...], g_ref[...]).astype(BF16)

    o_ref[...] = jnp.dot(xn_ref[...], w_ref[...], preferred_element_type=F32)


def norm_matmul(x, gain, w):
    n, d = x.shape
    m = w.shape[1]
    tm = _row_tile(n, 1024)
    tn = _col_tile(m, 1536)
    return pl.pallas_call(
        _norm_matmul_kernel,
        grid=(n // tm, m // tn),
        in_specs=[pl.BlockSpec((tm, d), lambda i, j: (i, 0)),
                  pl.BlockSpec((1, d), lambda i, j: (0, 0)),
                  pl.BlockSpec((d, tn), lambda i, j: (0, j))],
        out_specs=pl.BlockSpec((tm, tn), lambda i, j: (i, j)),
        out_shape=jax.ShapeDtypeStruct((n, m), F32),
        scratch_shapes=[pltpu.VMEM((tm, d), BF16)],
        compiler_params=pltpu.CompilerParams(
            dimension_semantics=("parallel", "arbitrary"), vmem_limit_bytes=VMEM_LIMIT),
        name="norm_matmul",
    )(x, gain.reshape(1, d), w)


def _matmul_res_kernel(a_ref, b_ref, w_ref, r_ref, o_ref):
    ka = a_ref.shape[1]
    o_ref[...] = (r_ref[...] + jnp.dot(a_ref[...].astype(BF16), w_ref[0:ka, :], preferred_element_type=F32)
                  + jnp.dot(b_ref[...].astype(BF16), w_ref[ka:, :], preferred_element_type=F32))


def matmul_residual(a, b, w, res):
    n, ka = a.shape
    kb = b.shape[1]
    m = w.shape[1]
    tm = _row_tile(n, 1024)
    return pl.pallas_call(
        _matmul_res_kernel,
        grid=(n // tm,),
        in_specs=[pl.BlockSpec((tm, ka), lambda i: (i, 0)),
                  pl.BlockSpec((tm, kb), lambda i: (i, 0)),
                  pl.BlockSpec((ka + kb, m), lambda i: (0, 0)),
                  pl.BlockSpec((tm, m), lambda i: (i, 0))],
        out_specs=pl.BlockSpec((tm, m), lambda i: (i, 0)),
        out_shape=jax.ShapeDtypeStruct((n, m), F32),
        compiler_params=pltpu.CompilerParams(
            dimension_semantics=("parallel",), vmem_limit_bytes=VMEM_LIMIT),
        name="matmul_residual",
    )(a, b, w, res)


def _swiglu_tile(xn, w1, w3):
    h1 = jnp.dot(xn, w1, preferred_element_type=F32)
    h3 = jnp.dot(xn, w3, preferred_element_type=F32)
    return h1 * jax.nn.sigmoid(h1) * h3


def _ffn_kernel(x_ref, g_ref, w1_ref, w3_ref, w2_ref, o_ref, xn_ref):
    @pl.when(pl.program_id(1) == 0)
    def _():
        x = x_ref[...]
        xn_ref[...] = _rms_rows(x, g_ref[...]).astype(BF16)
        o_ref[...] = x

    act = _swiglu_tile(xn_ref[...], w1_ref[...], w3_ref[...])
    o_ref[...] += jnp.dot(act.astype(BF16), w2_ref[...], preferred_element_type=F32)


def ffn_residual(x, gain, w1, w3, w2):
    n, d = x.shape
    f = w1.shape[1]
    tm = _row_tile(n, 1024)
    tf = _col_tile(f, 512)
    return pl.pallas_call(
        _ffn_kernel,
        grid=(n // tm, f // tf),
        in_specs=[pl.BlockSpec((tm, d), lambda i, j: (i, 0)),
                  pl.BlockSpec((1, d), lambda i, j: (0, 0)),
                  pl.BlockSpec((d, tf), lambda i, j: (0, j)),
                  pl.BlockSpec((d, tf), lambda i, j: (0, j)),
                  pl.BlockSpec((tf, d), lambda i, j: (j, 0))],
        out_specs=pl.BlockSpec((tm, d), lambda i, j: (i, 0)),
        out_shape=jax.ShapeDtypeStruct((n, d), F32),
        scratch_shapes=[pltpu.VMEM((tm, d), BF16)],
        compiler_params=pltpu.CompilerParams(
            dimension_semantics=("parallel", "arbitrary"), vmem_limit_bytes=VMEM_LIMIT),
        name="ffn_residual",
    )(x, gain.reshape(1, d), w1, w3, w2)


MOE_VMEM_LIMIT = 58 * 1024 * 1024
MOE_CAPS = (256, 288, 320, 384, 512)


def _moe_route_kernel(x_ref, g_ref, r_ref, xn_ref, comb_ref, post_ref, cnt_ref):
    x = x_ref[...]
    tm = x.shape[0]
    xn = _rms_rows(x, g_ref[...])
    xn_ref[...] = xn.astype(BF16)
    logits = jnp.dot(xn, r_ref[...], preferred_element_type=F32, precision=lax.Precision.HIGHEST)
    lane = lax.broadcasted_iota(jnp.int32, logits.shape, 1)
    logits = jnp.where(lane < N_EXPERTS, logits, -jnp.inf)
    m1 = jnp.max(logits, axis=-1, keepdims=True)
    i1 = jnp.min(jnp.where(logits == m1, lane, LANE), axis=-1, keepdims=True)
    rest = jnp.where(lane == i1, -jnp.inf, logits)
    m2 = jnp.max(rest, axis=-1, keepdims=True)
    i2 = jnp.min(jnp.where(rest == m2, lane, LANE), axis=-1, keepdims=True)
    e2 = jnp.exp(m2 - m1)
    den = 1.0 + e2
    comb_ref[...] = jnp.where(lane == i1, 1.0 / den, 0.0) + jnp.where(lane == i2, e2 / den, 0.0)
    chosen = (lane == i1) | (lane == i2)
    sel = jnp.where(chosen, 1.0, 0.0)
    tril16 = jnp.where(_tri_mask(LANE), 1.0, 0.0).astype(BF16)
    seen = jnp.zeros((1, LANE), F32)
    ranks = []
    for blk in range(tm // LANE):
        sb = sel[blk * LANE:(blk + 1) * LANE]
        ranks.append(_dot(tril16, sb.astype(BF16)) + seen - 1.0)
        seen = seen + jnp.sum(sb, axis=0, keepdims=True)
    rank = jnp.where(chosen, jnp.concatenate(ranks, axis=0), -1.0)
    post_ref[0] = rank.T[0:N_EXPERTS, :]
    cnt_ref[0] = seen


def _moe_expert_kernel(cnt_ref, xn_ref, comb_ref, post_ref, x_ref, w1_ref, w3_ref, w2_ref, o_ref,
                       xe_scr, y_scr, *, caps):
    i = pl.program_id(0)
    e = pl.program_id(1)
    j = pl.program_id(2)
    last = pl.num_programs(2) - 1
    tm = x_ref.shape[0]

    @pl.when(jnp.logical_and(e == 0, j == 0))
    def _():
        o_ref[...] = x_ref[...]

    cnt = cnt_ref[i * N_EXPERTS + e]

    def expert_step(cap):
        def pick():
            rank_row = post_ref[0, pl.ds(e, 1), :].astype(jnp.int32)
            slot = lax.broadcasted_iota(jnp.int32, (cap, tm), 0)
            return jnp.where(rank_row == slot, 1.0, 0.0).astype(BF16)

        @pl.when(j == 0)
        def _():
            xe_scr[0:cap, :] = _dot(pick(), xn_ref[...]).astype(BF16)

        act = _swiglu_tile(xe_scr[0:cap, :], w1_ref[0], w3_ref[0])
        yj = _dot(act.astype(BF16), w2_ref[0])

        @pl.when(j == 0)
        def _():
            y_scr[0:cap, :] = yj

        @pl.when(j > 0)
        def _():
            y_scr[0:cap, :] += yj

        @pl.when(j == last)
        def _():
            comb = comb_ref[...]
            lane = lax.broadcasted_iota(jnp.int32, comb.shape, 1)
            gate = jnp.sum(jnp.where(lane == e, comb, 0.0), axis=-1, keepdims=True)
            o_ref[...] += gate * _dot_tn(pick(), y_scr[0:cap, :].astype(BF16))

    lo = 0
    for cap in caps:
        @pl.when(jnp.logical_and(cnt > lo, cnt <= cap))
        def _(cap=cap):
            expert_step(cap)
        lo = cap


def moe_residual(x, gain, router, w1, w3, w2):
    n, d = x.shape
    ne, _, f = w1.shape
    tm = _row_tile(n, 1024)
    tf = _col_tile(f, 896)
    nt = n // tm
    xn, comb, post, cnt = pl.pallas_call(
        _moe_route_kernel,
        grid=(nt,),
        in_specs=[pl.BlockSpec((tm, d), lambda i: (i, 0)),
                  pl.BlockSpec((1, d), lambda i: (0, 0)),
                  pl.BlockSpec((d, LANE), lambda i: (0, 0))],
        out_specs=[pl.BlockSpec((tm, d), lambda i: (i, 0)),
                   pl.BlockSpec((tm, LANE), lambda i: (i, 0)),
                   pl.BlockSpec((1, ne, tm), lambda i: (i, 0, 0)),
                   pl.BlockSpec((1, 1, LANE), lambda i: (i, 0, 0))],
        out_shape=[jax.ShapeDtypeStruct((n, d), BF16), jax.ShapeDtypeStruct((n, LANE), F32),
                   jax.ShapeDtypeStruct((nt, ne, tm), F32), jax.ShapeDtypeStruct((nt, 1, LANE), F32)],
        compiler_params=pltpu.CompilerParams(dimension_semantics=("parallel",), vmem_limit_bytes=VMEM_LIMIT),
        name="moe_route",
    )(x, gain.reshape(1, d), router)
    counts = cnt[:, 0, :ne].astype(jnp.int32).reshape(nt * ne)
    caps = tuple(c for c in MOE_CAPS if c < tm) + (tm,)
    grid_spec = pltpu.PrefetchScalarGridSpec(
        num_scalar_prefetch=1,
        grid=(nt, ne, f // tf),
        in_specs=[pl.BlockSpec((tm, d), lambda i, e, j, c: (i, 0)),
                  pl.BlockSpec((tm, LANE), lambda i, e, j, c: (i, 0)),
                  pl.BlockSpec((1, ne, tm), lambda i, e, j, c: (i, 0, 0)),
                  pl.BlockSpec((tm, d), lambda i, e, j, c: (i, 0)),
                  pl.BlockSpec((1, d, tf), lambda i, e, j, c: (e, 0, j)),
                  pl.BlockSpec((1, d, tf), lambda i, e, j, c: (e, 0, j)),
                  pl.BlockSpec((1, tf, d), lambda i, e, j, c: (e, j, 0))],
        out_specs=pl.BlockSpec((tm, d), lambda i, e, j, c: (i, 0)),
        scratch_shapes=[pltpu.VMEM((tm, d), BF16), pltpu.VMEM((tm, d), F32)])
    return pl.pallas_call(
        functools.partial(_moe_expert_kernel, caps=caps),
        grid_spec=grid_spec,
        out_shape=jax.ShapeDtypeStruct((n, d), F32),
        compiler_params=pltpu.CompilerParams(
            dimension_semantics=("parallel", "arbitrary", "arbitrary"), vmem_limit_bytes=MOE_VMEM_LIMIT),
        name="moe_experts",
    )(counts, xn, comb, post, x, w1, w3, w2)


AB_QA, AB_VB, AB_OB, AB_QI, AB_QB, AB_KB, AB_KA, AB_VA, AB_MISC, AB_TOTAL = (
    0, 512, 1024, 1536, 1792, 2048, 2304, 2432, 2560, 2688)
MISC_WI, MISC_IB, MISC_FB = 64, 68, 72
HALF = LANE // 2
KEY_CHUNK = 512
MASKED = -1e30
KEY_OF_NEG_INF = -2139095041
I16_MIN, I16_MAX = -32768, 32767


def _permute_w_in_ab(w):
    qa, ka, va, qi, ki, wi, qb, kb, vb, ib, fb, ob = _split_cols(w, EVEN_SPLITS)
    pad = jnp.zeros((w.shape[0], LANE - IDX_DIM - 3 * IDX_HEADS), w.dtype)
    return jnp.concatenate([qa, vb, ob, qi, qb, kb, ka, va, ki, wi, ib, fb, pad], axis=1)


def _rope_tables(pos):
    rot = HEAD_DIM // ROT_FRACTION
    half = rot // 2
    inv_freq = ROPE_THETA ** (-jnp.arange(half, dtype=F32) * 2.0 / rot)
    ang = pos.astype(F32)[:, None] * inv_freq[None, :]
    cos, sin = jnp.cos(ang), jnp.sin(ang)
    t = pos.shape[0]
    one = jnp.ones((t, HEAD_DIM - rot), F32)
    zero_r = jnp.zeros((t, HEAD_DIM - rot), F32)
    zero_h = jnp.zeros((t, half), F32)
    c = jnp.concatenate([cos, cos, one], axis=1)
    s_up = jnp.concatenate([-sin, zero_h, zero_r], axis=1)
    s_dn = jnp.concatenate([zero_h, sin, zero_r], axis=1)
    return tuple(jnp.concatenate([a, a], axis=1) for a in (c, s_up, s_dn))


def _rope_tile(x, c, s_up, s_dn):
    half = HEAD_DIM // ROT_FRACTION // 2
    return x * c + pltpu.roll(x, LANE - half, 1) * s_up + pltpu.roll(x, half, 1) * s_dn


def _head_norm_tile(x, gain, same_head):
    sq = x * x
    hi = sq.astype(BF16)
    lo = (sq - hi.astype(F32)).astype(BF16)
    ss = (jnp.dot(hi, same_head, preferred_element_type=F32)
          + jnp.dot(lo, same_head, preferred_element_type=F32))
    return x * lax.rsqrt(ss * (1.0 / HEAD_DIM) + EPS) * gain


def _aprep_kernel(qa_ref, ka_ref, va_ref, qi_ref, misc_ref, c_ref, su_ref, sd_ref, qg_ref, kg_ref,
                  qpad_ref, qipad_ref, k16_ref, v16_ref, ki16_ref, k32_ref, v32_ref, ki32_ref):
    c, su, sd = c_ref[...], su_ref[...], sd_ref[...]
    tm = c.shape[0]
    row = lax.broadcasted_iota(jnp.int32, (LANE, LANE), 0)
    col = lax.broadcasted_iota(jnp.int32, (LANE, LANE), 1)
    same_head = jnp.where(row // HALF == col // HALF, 1.0, 0.0).astype(BF16)
    lane = lax.broadcasted_iota(jnp.int32, (tm, LANE), 1)
    low = lane < HALF

    heads_per_group = A_HEADS // A_KV_HEADS
    for p in range(A_HEADS // 2):
        y = _rope_tile(_head_norm_tile(qa_ref[:, p * LANE:(p + 1) * LANE], qg_ref[...], same_head), c, su, sd)
        y = y * HEAD_DIM ** -0.5
        y_sw = pltpu.roll(y, HALF, 1)
        for o in range(2):
            h = 2 * p + o
            g = h // heads_per_group
            src = y if o == g else y_sw
            qpad_ref[:, h * LANE:(h + 1) * LANE] = jnp.where(low if g == 0 else ~low, src, 0.0).astype(BF16)
    k = _rope_tile(_head_norm_tile(ka_ref[...], kg_ref[...], same_head), c, su, sd)
    k32_ref[...] = k
    k16_ref[...] = k.astype(BF16)
    v = va_ref[...]
    v32_ref[...] = v
    v16_ref[...] = v.astype(BF16)
    for p in range(IDX_HEADS // 2):
        y = _rope_tile(qi_ref[:, p * LANE:(p + 1) * LANE], c, su, sd)
        y_sw = pltpu.roll(y, HALF, 1)
        qipad_ref[:, (2 * p) * LANE:(2 * p + 1) * LANE] = jnp.where(low, y, 0.0).astype(BF16)
        qipad_ref[:, (2 * p + 1) * LANE:(2 * p + 2) * LANE] = jnp.where(low, y_sw, 0.0).astype(BF16)
    ki = _rope_tile(misc_ref[...], c, su, sd)
    ki32_ref[...] = ki[:, :IDX_DIM]
    ki16_ref[...] = jnp.where(low, ki, 0.0).astype(BF16)


def dsa_prep(proj, pos, q_gain, k_gain, t):
    n = proj.shape[0]
    tm = _row_tile(n, 512)
    tabs = _rope_tables(pos)
    if t < tm:
        tabs = tuple(jnp.tile(a, (tm // t, 1)) for a in tabs)
    nt = tabs[0].shape[0] // tm
    tab_spec = pl.BlockSpec((tm, LANE), lambda i: (i % nt, 0))
    gain_spec = pl.BlockSpec((1, LANE), lambda i: (0, 0))

    def col(width, offset):
        return pl.BlockSpec((tm, width), lambda i: (i, offset // width))

    def out(width, dtype):
        return (jax.ShapeDtypeStruct((n, width), dtype), pl.BlockSpec((tm, width), lambda i: (i, 0)))

    outs = [out(A_HEADS * LANE, BF16), out(IDX_HEADS * LANE, BF16), out(LANE, BF16), out(LANE, BF16),
            out(LANE, BF16), out(LANE, F32), out(LANE, F32), out(IDX_DIM, F32)]
    return pl.pallas_call(
        _aprep_kernel,
        grid=(n // tm,),
        in_specs=[col(A_HEADS * HEAD_DIM, AB_QA), col(LANE, AB_KA), col(LANE, AB_VA),
                  col(IDX_HEADS * IDX_DIM, AB_QI), col(LANE, AB_MISC), tab_spec, tab_spec, tab_spec,
                  gain_spec, gain_spec],
        out_specs=[o[1] for o in outs],
        out_shape=[o[0] for o in outs],
        compiler_params=pltpu.CompilerParams(dimension_semantics=("parallel",), vmem_limit_bytes=VMEM_LIMIT),
        name="dsa_prep",
    )(proj, proj, proj, proj, proj, *tabs, jnp.tile(q_gain, 2).reshape(1, LANE), jnp.tile(k_gain, 2).reshape(1, LANE))


N_PARTIAL = 4


def _add_tiles(accs, m, sub):
    accs = list(accs)
    for t in range(m.shape[0] // sub):
        accs[t % len(accs)] = accs[t % len(accs)] + m[t * sub:(t + 1) * sub]
    return tuple(accs)


def _dsa_kernel(q_ref, qi_ref, misc_ref, lim_ref, k_ref, v_ref, ki_ref, o_ref, key_ref, bias_ref, hi_ref, lo_ref,
                *, nch, n_sel):
    qb = q_ref.shape[0]
    kc = KEY_CHUNK
    n_idx = IDX_HEADS
    hpg = A_HEADS // A_KV_HEADS
    nt = (((1,), (1,)), ((), ()))

    limit = lim_ref[0]
    misc_t = misc_ref[...].T
    wscale = IDX_HEADS ** -0.5 * IDX_DIM ** -0.5
    w = [misc_t[MISC_WI + j:MISC_WI + j + 1, :] * wscale for j in range(n_idx)]
    qis = [qi_ref[:, j * LANE:(j + 1) * LANE] for j in range(n_idx)]

    def score_body(c, carry):
        off = pl.multiple_of(c * kc, kc)
        kic = ki_ref[pl.ds(off, kc), :]
        lgs = [lax.dot_general(kic, qis[j], nt, preferred_element_type=F32) for j in range(n_idx)]
        s = w[0] * jnp.maximum(lgs[0], 0.0)
        for j in range(1, n_idx):
            s = s + w[j] * jnp.maximum(lgs[j], 0.0)
        kidx = off + lax.broadcasted_iota(jnp.int32, (kc, qb), 0)
        s = jnp.where(kidx < limit, s, -jnp.inf)
        bits = lax.bitcast_convert_type(s, jnp.int32)
        key = jnp.where(bits < 0, bits ^ 0x7FFFFFFF, bits)
        key_ref[pl.ds(off, kc), :] = key
        hi_ref[pl.ds(off, kc), :] = jnp.right_shift(key, 16).astype(jnp.int16)
        return carry

    lax.fori_loop(0, nch, score_body, 0)

    def count_ge(cand):
        sub = 8

        def body(c, accs):
            off = pl.multiple_of(c * kc, kc)
            m = jnp.where(key_ref[pl.ds(off, kc), :] >= cand, 1.0, 0.0)
            return _add_tiles(accs, m, sub)

        accs = lax.fori_loop(0, nch, body, (jnp.zeros((sub, qb), F32),) * N_PARTIAL, unroll=True)
        return jnp.sum(sum(accs), axis=0, keepdims=True)

    def count_ge16(ref, cand32):
        cand = cand32.astype(jnp.int16)
        sub = 16

        def body(c, accs):
            off = pl.multiple_of(c * kc, kc)
            m = jnp.where(ref[pl.ds(off, kc), :] >= cand, jnp.int16(1), jnp.int16(0))
            return _add_tiles(accs, m, sub)

        accs = lax.fori_loop(0, nch, body, (jnp.zeros((sub, qb), jnp.int16),) * N_PARTIAL, unroll=True)
        return jnp.sum(sum(accs).astype(F32), axis=0, keepdims=True)

    def kth_largest16(ref, want):
        tau = jnp.where(count_ge16(ref, jnp.zeros((1, qb), jnp.int32)) >= want, 0, I16_MIN).astype(jnp.int32)

        def bisect(i, tau):
            cand = tau | jnp.left_shift(jnp.int32(1), 14 - i)
            return jnp.where(count_ge16(ref, cand) >= want, cand, tau)

        return lax.fori_loop(0, 15, bisect, tau)

    want = float(n_sel)
    tau_hi = kth_largest16(hi_ref, want)
    above = jnp.where(tau_hi < I16_MAX, count_ge16(hi_ref, jnp.minimum(tau_hi + 1, I16_MAX)), 0.0)

    def low_body(c, carry):
        off = pl.multiple_of(c * kc, kc)
        key = key_ref[pl.ds(off, kc), :]
        low = (key & 0xFFFF) + I16_MIN
        lo_ref[pl.ds(off, kc), :] = jnp.where(jnp.right_shift(key, 16) == tau_hi, low, I16_MIN).astype(jnp.int16)
        return carry

    lax.fori_loop(0, nch, low_body, 0)
    tau_lo = kth_largest16(lo_ref, want - above)
    tau = jnp.left_shift(tau_hi, 16) + (tau_lo - I16_MIN)

    room = want - count_ge(tau + 1)
    r_i = lax.broadcasted_iota(jnp.int32, (LANE, LANE), 0)
    c_i = lax.broadcasted_iota(jnp.int32, (LANE, LANE), 1)
    prefix_ones = jnp.where(r_i >= c_i, 1.0, 0.0).astype(BF16)
    identity = jnp.where(r_i == c_i, 1.0, 0.0).astype(BF16)

    def bias_body(c, seen):
        off = pl.multiple_of(c * kc, kc)
        tiles = range(kc // LANE)
        xs = [key_ref[pl.ds(off + t * LANE, LANE), :] for t in tiles]
        eqs = [x == tau for x in xs]
        eqfs = [jnp.where(eq, 1.0, 0.0) for eq in eqs]
        ranks = [jnp.dot(prefix_ones, eqf.astype(BF16), preferred_element_type=F32) for eqf in eqfs]
        sels = []
        for t in tiles:
            sel = ((xs[t] > tau) | (eqs[t] & (ranks[t] + seen <= room))) & (xs[t] != KEY_OF_NEG_INF)
            sels.append(sel)
            seen = seen + jnp.sum(eqfs[t], axis=0, keepdims=True)
        if qb == LANE:
            for t in tiles:
                bias_ref[:, pl.ds(off + t * LANE, LANE)] = jnp.where(sels[t], 0.0, MASKED).T
        else:
            sel_ts = [lax.dot_general(jnp.where(sel, 1.0, 0.0).astype(BF16), identity, (((0,), (0,)), ((), ())),
                                      preferred_element_type=F32) for sel in sels]
            for t in tiles:
                bias_ref[:, pl.ds(off + t * LANE, LANE)] = jnp.where(sel_ts[t] > 0.5, 0.0, MASKED)
        return seen

    lax.fori_loop(0, nch, bias_body, jnp.zeros((1, qb), F32))

    lane = lax.broadcasted_iota(jnp.int32, (qb, LANE), 1)
    qgs = [jnp.concatenate([q_ref[:, (hpg * g + h) * LANE:(hpg * g + h + 1) * LANE] for h in range(hpg)], axis=0)
           for g in range(A_KV_HEADS)]

    def att_body(c, carry):
        off = pl.multiple_of(c * kc, kc)
        kch = k_ref[pl.ds(off, kc), :]
        vch = v_ref[pl.ds(off, kc), :]
        bias = bias_ref[:, pl.ds(off, kc)][None]
        new = [None] * A_KV_HEADS

        def group(g):
            m, l, acc = carry[g]
            s = lax.dot_general(qgs[g], kch, nt, preferred_element_type=F32)
            yield
            s = (s.reshape(hpg, qb, kc) + bias).reshape(hpg * qb, kc)
            m_new = jnp.maximum(m, jnp.max(s, axis=1, keepdims=True))
            alpha = jnp.exp(m - m_new)
            p = jnp.exp(s - m_new)
            l = alpha * l + jnp.sum(p, axis=1, keepdims=True)
            p16 = p.astype(BF16)
            yield
            new[g] = (m_new, l, alpha * acc + jnp.dot(p16, vch, preferred_element_type=F32))

        _lockstep(group(g) for g in range(A_KV_HEADS))
        return tuple(new)

    init = tuple((jnp.full((hpg * qb, 1), MASKED, F32), jnp.zeros((hpg * qb, 1), F32),
                  jnp.zeros((hpg * qb, LANE), F32)) for _ in range(A_KV_HEADS))
    res = lax.fori_loop(0, nch, att_body, init)
    outs = []
    for g in range(A_KV_HEADS):
        _, l, acc = res[g]
        og = acc / l
        for h in range(hpg):
            oh = og[h * qb:(h + 1) * qb]
            outs.append(oh if (h % 2) == g else pltpu.roll(oh, HALF, 1))
    for p in range(A_HEADS // 2):
        o_ref[:, p * LANE:(p + 1) * LANE] = jnp.where(lane < HALF, outs[2 * p], outs[2 * p + 1])


def dsa_attention(qpad, qipad, proj, limit, k16, v16, ki16, *, bsz, tq, tk, causal, n_sel):
    qb = min(Q_BLOCK, tq)
    nqb = tq // qb
    assert tk % KEY_CHUNK == 0 and tk >= n_sel
    if causal:
        per_group = KEY_CHUNK // qb
        groups = [(g * per_group, per_group, g + 1) for g in range(nqb // per_group)]
    else:
        groups = [(0, nqb, tk // KEY_CHUNK)]
    lim3 = limit.reshape(bsz * nqb, 1, qb)
    out = None
    for first, count, nch in groups:
        def qspec(width, col=0, first=first):
            return pl.BlockSpec((qb, width), lambda b, i: (b * nqb + first + i, col))

        kspec = pl.BlockSpec((tk, LANE), lambda b, i: (b, 0))
        in_specs = [qspec(A_HEADS * LANE), qspec(IDX_HEADS * LANE), qspec(LANE, AB_MISC // LANE),
                    pl.BlockSpec((1, 1, qb), lambda b, i, first=first: (b * nqb + first + i, 0, 0)),
                    kspec, kspec, kspec]
        args = [qpad, qipad, proj, lim3, k16, v16, ki16]
        kern = functools.partial(_dsa_kernel, nch=nch, n_sel=n_sel)
        aliases = {}
        if out is not None:
            in_specs.append(pl.BlockSpec(memory_space=pl.ANY))
            args.append(out)
            aliases = {len(args) - 1: 0}
            kern = functools.partial(_dsa_kernel_with_carry, nch=nch, n_sel=n_sel)
        out = pl.pallas_call(
            kern,
            grid=(bsz, count),
            in_specs=in_specs,
            out_specs=qspec(A_HEADS * HEAD_DIM),
            out_shape=jax.ShapeDtypeStruct((bsz * tq, A_HEADS * HEAD_DIM), F32),
            scratch_shapes=[pltpu.VMEM((tk, qb), jnp.int32), pltpu.VMEM((qb, tk), F32),
                            pltpu.VMEM((tk, qb), jnp.int16), pltpu.VMEM((tk, qb), jnp.int16)],
            input_output_aliases=aliases,
            compiler_params=pltpu.CompilerParams(
                dimension_semantics=("parallel", "arbitrary"), vmem_limit_bytes=VMEM_LIMIT),
            name="dsa_attention",
        )(*args)
    return out


def _dsa_kernel_with_carry(q_ref, qi_ref, misc_ref, lim_ref, k_ref, v_ref, ki_ref, carry_ref, o_ref, *scratch,
                           nch, n_sel):
    del carry_ref
    _dsa_kernel(q_ref, qi_ref, misc_ref, lim_ref, k_ref, v_ref, ki_ref, o_ref, *scratch, nch=nch, n_sel=n_sel)


SUB = 16


def _dot(a, b):
    return jnp.dot(a, b, preferred_element_type=F32)


def _dot_nt(a, b):
    return lax.dot_general(a, b, (((1,), (1,)), ((), ())), preferred_element_type=F32)


def _dot_tn(a, b):
    return lax.dot_general(a, b, (((0,), (0,)), ((), ())), preferred_element_type=F32)


def _split2(x):
    hi = x.astype(BF16)
    return hi, (x - hi.astype(F32)).astype(BF16)


def _split3(x):
    hi = x.astype(BF16)
    r = x - hi.astype(F32)
    mid = r.astype(BF16)
    return hi, mid, (r - mid.astype(F32)).astype(BF16)


def _cumsum_rows(x, tril16):
    hi, mid, lo = _split3(x)
    return _dot(tril16, hi) + _dot(tril16, mid) + _dot(tril16, lo)


def _dot_f32(a, b):
    ah, al = _split2(a)
    bh, bl = _split2(b)
    return _dot(ah, bh) + (_dot(ah, bl) + _dot(al, bh))


def _tri_mask(n, strict=False):
    r = lax.broadcasted_iota(jnp.int32, (n, n), 0)
    c = lax.broadcasted_iota(jnp.int32, (n, n), 1)
    return r > c if strict else r >= c


def _rows_to_lanes(x):
    rows = x.shape[0]
    if rows < LANE:
        x = jnp.concatenate([x, jnp.zeros((LANE - rows, LANE), x.dtype)], axis=0)
    return x.T


MLSTM_SEQS_PER_STEP = 2
CD_SEQS_PER_STEP = 4


def _lockstep(stages):
    stages = list(stages)
    while stages:
        for g in list(stages):
            if next(g, StopIteration) is StopIteration:
                stages.remove(g)


def _chunk_call(kern, *, bsz, nb, nc, rows, ins, outs, scratch, name):
    def spec(a, kind, width, offset):
        if kind == 'rows':
            return pl.BlockSpec((nb, rows, width), lambda b, c: (b, c, offset // width))
        if kind == 'batch':
            return pl.BlockSpec((nb,) + tuple(a.shape[1:]), lambda b, c: (b,) + (0,) * (len(a.shape) - 1))
        return pl.BlockSpec(tuple(a.shape), lambda b, c: (0,) * len(a.shape))

    return pl.pallas_call(
        kern,
        grid=(bsz // nb, nc),
        in_specs=[spec(*i) for i in ins],
        out_specs=[spec(*o) for o in outs],
        out_shape=[o[0] for o in outs],
        scratch_shapes=scratch,
        compiler_params=pltpu.CompilerParams(
            dimension_semantics=("parallel", "arbitrary"), vmem_limit_bytes=VMEM_LIMIT),
        name=name,
    )(*[i[0] for i in ins])


def _mlstm_kernel(q_ref, k_ref, v_ref, og_ref, misc_ref, gb_ref, gain_ref, c0_ref, n0_ref, m0_ref,
                  h_ref, c_out_ref, n_out_ref, m_out_ref, c_scr, n_scr, m_scr):
    ci = pl.program_id(1)
    seqs = range(q_ref.shape[0])

    @pl.when(ci == 0)
    def _():
        c_scr[...] = c0_ref[...]
        n_scr[...] = n0_ref[...]
        m_scr[...] = m0_ref[...]

    _lockstep(_mlstm_chunk(q_ref.at[i], k_ref.at[i], v_ref.at[i], og_ref.at[i], misc_ref.at[i], gb_ref, gain_ref,
                           h_ref.at[i], c_scr.at[i], n_scr.at[i], m_scr.at[i]) for i in seqs)

    @pl.when(ci == pl.num_programs(1) - 1)
    def _():
        c_out_ref[...] = c_scr[...]
        n_out_ref[...] = n_scr[...]
        m_out_ref[...] = m_scr[...]


def _mlstm_chunk(q_ref, k_ref, v_ref, og_ref, misc_ref, gb_ref, gain_ref, h_ref, c_scr, n_scr, m_scr):
    rows = q_ref.shape[0]
    hr = B_HEADS * rows
    wq = B_HEADS * B_QK_DIM
    tril16 = jnp.where(_tri_mask(rows), 1.0, 0.0).astype(BF16)
    gates = misc_ref[...] + gb_ref[...]
    bcum = _cumsum_rows(jax.nn.log_sigmoid(gates), tril16)
    yield
    m_all = m_scr[...]

    def stack(f):
        return jnp.concatenate([f(h) for h in range(B_HEADS)], axis=0)

    lane_q = lax.broadcasted_iota(jnp.int32, (rows, wq), 1)
    qx, kx = q_ref[...], k_ref[...]
    q_all = stack(lambda h: jnp.where(lane_q // B_QK_DIM == h, qx, 0.0))
    k_all = stack(lambda h: jnp.where(lane_q // B_QK_DIM == h, kx, 0.0)) * B_QK_DIM ** -0.5
    v_all = stack(lambda h: v_ref[:, h * B_V_DIM:(h + 1) * B_V_DIM])
    b_col = stack(lambda h: bcum[:, MISC_FB + h:MISC_FB + h + 1])
    i_col = stack(lambda h: gates[:, MISC_IB + h:MISC_IB + h + 1])
    m_col = stack(lambda h: jnp.broadcast_to(m_all[:, h:h + 1], (rows, 1)))
    b_end = stack(lambda h: jnp.broadcast_to(bcum[rows - 1:rows, MISC_FB + h:MISC_FB + h + 1], (rows, 1)))
    bi_row = jnp.broadcast_to(b_col - i_col, (hr, LANE)).T[0:1, :]

    r = lax.broadcasted_iota(jnp.int32, (hr, hr), 0)
    c = lax.broadcasted_iota(jnp.int32, (hr, hr), 1)
    incl = ((r // rows) == (c // rows)) & (r >= c)
    dmat = jnp.where(incl, b_col - bi_row, -jnp.inf)
    inter = b_col + m_col
    mrow = jnp.maximum(inter, jnp.max(dmat, axis=1, keepdims=True))
    w_state = jnp.exp(inter - mrow)
    q16 = q_all.astype(BF16)
    v16 = v_all.astype(BF16)
    scores = _dot_nt(q16, k_all.astype(BF16)) * jnp.exp(dmat - mrow)
    yield
    cs = c_scr[...]
    n_row = n_scr[...]
    num = _dot(scores.astype(BF16), v16) + w_state * _dot(q16, cs.astype(BF16))
    den = jnp.sum(scores, axis=1, keepdims=True) + w_state * jnp.sum(q_all * n_row, axis=1, keepdims=True)
    hh = num / jnp.maximum(jnp.abs(den), jnp.exp(-mrow))
    yield
    gain = gain_ref[...]
    for h in range(B_HEADS):
        h_ref[:, h * B_V_DIM:(h + 1) * B_V_DIM] = (_rms_rows(hh[h * rows:(h + 1) * rows], gain)
                                                   * jax.nn.sigmoid(og_ref[:, h * B_V_DIM:(h + 1) * B_V_DIM]))

    g_col = b_end - b_col + i_col
    lane1 = lax.broadcasted_iota(jnp.int32, (1, LANE), 1)
    m_next = m_all
    m_new_rows, keep_rows, keep_lanes = [], [], []
    for h in range(B_HEADS):
        m_h = m_all[:, h:h + 1]
        be = bcum[rows - 1:rows, MISC_FB + h:MISC_FB + h + 1]
        m_new = jnp.maximum(be + m_h, jnp.max(g_col[h * rows:(h + 1) * rows], axis=0, keepdims=True))
        keep = jnp.exp(be + m_h - m_new)
        m_next = jnp.where(lane1 == h, m_new, m_next)
        m_new_rows.append(jnp.broadcast_to(m_new, (rows, 1)))
        keep_rows.append(jnp.broadcast_to(keep, (B_QK_DIM, 1)))
        keep_lanes.append(jnp.broadcast_to(keep, (1, B_QK_DIM)))
    kw = k_all * jnp.exp(g_col - jnp.concatenate(m_new_rows, axis=0))
    c_scr[...] = jnp.concatenate(keep_rows, axis=0) * cs + _dot_tn(kw.astype(BF16), v16)
    n_scr[...] = jnp.concatenate(keep_lanes, axis=1) * n_row + jnp.sum(kw, axis=0, keepdims=True)
    m_scr[...] = m_next
    yield


def mlstm_mixer(proj, gate_bias, gain, c0, n0, m0, *, bsz, t, chunk):
    nc = t // chunk
    nb = MLSTM_SEQS_PER_STEP
    gb = jnp.zeros((1, LANE), F32)
    gb = gb.at[0, MISC_IB:MISC_IB + B_HEADS].set(gate_bias[0]).at[0, MISC_FB:MISC_FB + B_HEADS].set(gate_bias[1])
    c0 = c0.reshape(bsz, B_HEADS * B_QK_DIM, B_V_DIM)
    n0 = n0.reshape(bsz, 1, B_HEADS * B_QK_DIM)
    m0 = jnp.pad(m0, ((0, 0), (0, LANE - B_HEADS))).reshape(bsz, 1, LANE)
    wq = B_HEADS * B_QK_DIM
    wv = B_HEADS * B_V_DIM
    h, c, n, m = _chunk_call(
        _mlstm_kernel, bsz=bsz, nb=nb, nc=nc, rows=chunk,
        ins=[(proj, 'rows', wq, AB_QB), (proj, 'rows', wq, AB_KB), (proj, 'rows', wv, AB_VB),
             (proj, 'rows', wv, AB_OB), (proj, 'rows', LANE, AB_MISC), (gb, 'const', 0, 0),
             (gain.reshape(1, B_V_DIM), 'const', 0, 0), (c0, 'batch', 0, 0), (n0, 'batch', 0, 0),
             (m0, 'batch', 0, 0)],
        outs=[(jax.ShapeDtypeStruct((bsz, t, wv), F32), 'rows', wv, 0),
              (jax.ShapeDtypeStruct(c0.shape, F32), 'batch', 0, 0),
              (jax.ShapeDtypeStruct(n0.shape, F32), 'batch', 0, 0),
              (jax.ShapeDtypeStruct(m0.shape, F32), 'batch', 0, 0)],
        scratch=[pltpu.VMEM((nb, wq, B_V_DIM), F32), pltpu.VMEM((nb, 1, wq), F32), pltpu.VMEM((nb, 1, LANE), F32)],
        name="mlstm_mixer")
    return (h.reshape(bsz * t, wv), c.reshape(bsz, B_HEADS, B_QK_DIM, B_V_DIM), n.reshape(bsz, B_HEADS, B_QK_DIM),
            m.reshape(bsz, LANE)[:, :B_HEADS])


CD_QKV, CD_ZC, CD_QD, CD_FD, CD_VD, CD_GD, CD_MISC, CD_TOTAL = 0, 1536, 2048, 2560, 3072, 3584, 4096, 4224
MISC_BC, MISC_AC = 0, 4
TAIL = 8


def _permute_w_in_cd(w):
    qkv, bc, ac, zc, qd, fd, vd, gd = _split_cols(w, ODD_SPLITS)
    pad = jnp.zeros((w.shape[0], LANE - 2 * C_HEADS), w.dtype)
    return jnp.concatenate([qkv, zc, qd, fd, vd, gd, bc, ac, pad], axis=1)


def _gdn_init(s0_ref, tail0_ref, st_scr, tail_scr):
    for h in range(C_HEADS):
        st_scr[:, h * C_DIM:(h + 1) * C_DIM] = s0_ref[h].T
    tail_scr[...] = tail0_ref[...]


def _gdn_final(s_out_ref, st_scr):
    for h in range(C_HEADS):
        s_out_ref[h] = st_scr[:, h * C_DIM:(h + 1) * C_DIM].T


def _gdn_chunk(qkv_ref, z_ref, misc_ref, cw_ref, alog_ref, dt_ref, gain_ref, o_ref, st_scr, tail_scr):
    rows = qkv_ref.shape[0]
    width = qkv_ref.shape[1]
    x = qkv_ref[...]
    tail = tail_scr[...]
    row8 = lax.broadcasted_iota(jnp.int32, (TAIL, width), 0)
    acc = x * cw_ref[CONV_W - 1:CONV_W, :]
    for back in range(1, CONV_W):
        rolled = pltpu.roll(x, back, 0)
        first = jnp.where(row8 < back, pltpu.roll(tail, back, 0), rolled[0:TAIL])
        shifted = first if rows == TAIL else jnp.concatenate([first, rolled[TAIL:]], axis=0)
        acc = acc + shifted * cw_ref[CONV_W - 1 - back:CONV_W - back, :]
    tail_scr[...] = x[rows - TAIL:rows]
    conv = acc * jax.nn.sigmoid(acc)

    tril16 = jnp.where(_tri_mask(rows), 1.0, 0.0).astype(BF16)
    misc = misc_ref[...]
    beta_t = jax.nn.sigmoid(misc)
    g_t = -jnp.exp(alog_ref[...]) * jax.nn.softplus(misc + dt_ref[...])
    gcum = _cumsum_rows(g_t, tril16)
    yield

    hd = C_HEADS * C_DIM
    hr = C_HEADS * rows

    def stack(f):
        return jnp.concatenate([f(h) for h in range(C_HEADS)], axis=0)

    def l2n(v):
        return v * lax.rsqrt(jnp.sum(v * v, axis=-1, keepdims=True) + EPS)

    q_all = stack(lambda h: l2n(conv[:, h * C_DIM:(h + 1) * C_DIM])) * C_DIM ** -0.5
    k_all = stack(lambda h: l2n(conv[:, hd + h * C_DIM:hd + (h + 1) * C_DIM]))
    v_all = stack(lambda h: conv[:, 2 * hd + h * C_DIM:2 * hd + (h + 1) * C_DIM])
    beta = stack(lambda h: beta_t[:, MISC_BC + h:MISC_BC + h + 1])
    gc = stack(lambda h: gcum[:, MISC_AC + h:MISC_AC + h + 1])
    g_end = stack(lambda h: jnp.broadcast_to(gcum[rows - 1:rows, MISC_AC + h:MISC_AC + h + 1], (rows, 1)))
    gc_row = jnp.broadcast_to(gc, (hr, LANE)).T[0:1, :]

    r = lax.broadcasted_iota(jnp.int32, (hr, hr), 0)
    c = lax.broadcasted_iota(jnp.int32, (hr, hr), 1)
    same = (r // rows) == (c // rows)
    incl = same & (r >= c)
    strict = same & (r > c)
    decay = jnp.exp(jnp.where(incl, gc - gc_row, -jnp.inf))
    k16 = k_all.astype(BF16)
    a_mat = jnp.where(strict, beta * _dot_nt(k16, k16) * decay, 0.0)
    yield
    power = -a_mat
    inv = jnp.where(r == c, 1.0, 0.0) + power
    for _ in range(int(math.log2(rows)) - 1):
        power = _dot_f32(power, power)
        yield
        inv = inv + _dot_f32(inv, power)
        yield
    inv_hi, inv_lo = _split2(inv)
    rhs = jnp.concatenate([beta * v_all, beta * jnp.exp(gc) * k_all], axis=1).astype(BF16)
    w = _dot(inv_hi, rhs) + _dot(inv_lo, rhs)
    yield
    w_v, w_k = w[:, 0:C_DIM], w[:, C_DIM:2 * C_DIM]
    qk = _dot_nt(q_all.astype(BF16), k16) * decay
    yield

    head_of_row = lax.broadcasted_iota(jnp.int32, (hr, C_DIM), 0) // rows

    def per_head_lanes(m):
        return jnp.concatenate([jnp.where(head_of_row == h, m, 0.0) for h in range(C_HEADS)], axis=1).astype(BF16)

    st = st_scr[...]
    st16 = st.astype(BF16)
    delta = w_v - _dot_nt(per_head_lanes(w_k), st16)
    yield
    d16 = delta.astype(BF16)
    out = _dot_nt(per_head_lanes(q_all * jnp.exp(gc)), st16) + _dot(qk.astype(BF16), d16)
    keep = jnp.concatenate([jnp.broadcast_to(jnp.exp(gcum[rows - 1:rows, MISC_AC + h:MISC_AC + h + 1]), (1, C_DIM))
                            for h in range(C_HEADS)], axis=1)
    st_scr[...] = keep * st + _dot_tn(d16, per_head_lanes(k_all * jnp.exp(g_end - gc)))
    yield
    gain = gain_ref[...]
    for h in range(C_HEADS):
        z = z_ref[:, h * C_DIM:(h + 1) * C_DIM]
        o_ref[:, h * C_DIM:(h + 1) * C_DIM] = _rms_rows(out[h * rows:(h + 1) * rows], gain) * (z * jax.nn.sigmoid(z))


def _hgrn2_init(s0_ref, st_scr):
    for h in range(D_HEADS):
        st_scr[h] = s0_ref[h].T


def _hgrn2_final(s_out_ref, st_scr):
    for h in range(D_HEADS):
        s_out_ref[h] = st_scr[h].T


def _hgrn2_chunk(q_ref, f_ref, v_ref, g_ref, lb_ref, gain_ref, o_ref, st_scr):
    rows = q_ref.shape[0]
    tril16 = jnp.where(_tri_mask(rows), 1.0, 0.0).astype(BF16)
    lb = lb_ref[...]
    zf = f_ref[...]
    logf = jnp.logaddexp(jnp.log(lb), jnp.log1p(-lb) + jax.nn.log_sigmoid(zf))
    kd = (1.0 - lb) * jax.nn.sigmoid(-zf)
    qx = q_ref[...]
    qd = qx * jax.nn.sigmoid(qx)
    bcum = _cumsum_rows(logf, tril16)
    yield
    gain = gain_ref[...]
    row_sub = lax.broadcasted_iota(jnp.int32, (SUB, 1), 0)
    for h in range(D_HEADS):
        sl = slice(h * D_EXPAND, (h + 1) * D_EXPAND)
        q, k, b = qd[:, sl], kd[:, sl], bcum[:, sl]
        v = v_ref[:, h * D_V_DIM:(h + 1) * D_V_DIM]
        v16 = v.astype(BF16)
        st = st_scr[h]
        inter = _dot_nt((q * jnp.exp(b)).astype(BF16), st.astype(BF16))
        blocks = []
        for i in range(rows // SUB):
            r0 = i * SUB
            qi, bi = q[r0:r0 + SUB], b[r0:r0 + SUB]
            oi = inter[r0:r0 + SUB]
            if i > 0:
                ref = b[r0 - 1:r0]
                att = _dot_nt((qi * jnp.exp(bi - ref)).astype(BF16),
                              (k[0:r0] * jnp.exp(ref - b[0:r0])).astype(BF16))
                oi = oi + _dot(att.astype(BF16), v16[0:r0])
            for s in range(SUB):
                r = r0 + s
                a = jnp.sum(qi * jnp.exp(bi - b[r:r + 1]) * k[r:r + 1], axis=1, keepdims=True)
                oi = oi + jnp.where(row_sub >= s, a, 0.0) * v[r:r + 1]
            blocks.append(oi)
        out = blocks[0] if len(blocks) == 1 else jnp.concatenate(blocks, axis=0)
        b_end = b[rows - 1:rows]
        st_scr[h] = jnp.exp(b_end) * st + _dot_tn(v16, (k * jnp.exp(b_end - b)).astype(BF16))
        g = g_ref[:, h * D_V_DIM:(h + 1) * D_V_DIM]
        o_ref[:, h * D_V_DIM:(h + 1) * D_V_DIM] = _rms_rows(out, gain) * (g * jax.nn.sigmoid(g))
        yield


def _cd_kernel(qkv_ref, z_ref, misc_ref, cw_ref, alog_ref, dt_ref, cgain_ref, sc0_ref, tail0_ref,
               qd_ref, fd_ref, vd_ref, gd_ref, lb_ref, dgain_ref, sd0_ref,
               oc_ref, sc_out_ref, od_ref, sd_out_ref, stc_scr, tail_scr, std_scr):
    ci = pl.program_id(1)
    seqs = range(qkv_ref.shape[0])

    @pl.when(ci == 0)
    def _():
        for i in seqs:
            _gdn_init(sc0_ref.at[i], tail0_ref.at[i], stc_scr.at[i], tail_scr.at[i])
            _hgrn2_init(sd0_ref.at[i], std_scr.at[i])

    gdn = [_gdn_chunk(qkv_ref.at[i], z_ref.at[i], misc_ref.at[i], cw_ref, alog_ref, dt_ref, cgain_ref,
                      oc_ref.at[i], stc_scr.at[i], tail_scr.at[i]) for i in seqs]
    hgrn2 = [_hgrn2_chunk(qd_ref.at[i], fd_ref.at[i], vd_ref.at[i], gd_ref.at[i], lb_ref, dgain_ref, od_ref.at[i],
                          std_scr.at[i]) for i in seqs]
    _lockstep(gdn + hgrn2)

    @pl.when(ci == pl.num_programs(1) - 1)
    def _():
        for i in seqs:
            _gdn_final(sc_out_ref.at[i], stc_scr.at[i])
            _hgrn2_final(sd_out_ref.at[i], std_scr.at[i])


def cd_mixers(proj, conv_w, a_log, dt_bias, c_gain, sc0, conv_prev, lower_bound, d_gain, sd0, *, bsz, t, chunk):
    nc = t // chunk
    nb = CD_SEQS_PER_STEP
    hd = C_HEADS * C_DIM
    wk = D_HEADS * D_EXPAND
    wv = D_HEADS * D_V_DIM
    lanes = jnp.zeros((1, LANE), F32)
    alog = lanes.at[0, MISC_AC:MISC_AC + C_HEADS].set(a_log)
    dt = lanes.at[0, MISC_AC:MISC_AC + C_HEADS].set(dt_bias)
    tail0 = jnp.pad(conv_prev, ((0, 0), (TAIL - (CONV_W - 1), 0), (0, 0)))
    oc, sc, od, sd = _chunk_call(
        _cd_kernel, bsz=bsz, nb=nb, nc=nc, rows=chunk,
        ins=[(proj, 'rows', 3 * hd, CD_QKV), (proj, 'rows', hd, CD_ZC), (proj, 'rows', LANE, CD_MISC),
             (conv_w, 'const', 0, 0), (alog, 'const', 0, 0), (dt, 'const', 0, 0),
             (c_gain.reshape(1, C_DIM), 'const', 0, 0), (sc0, 'batch', 0, 0), (tail0, 'batch', 0, 0),
             (proj, 'rows', wk, CD_QD), (proj, 'rows', wk, CD_FD), (proj, 'rows', wv, CD_VD),
             (proj, 'rows', wv, CD_GD), (lower_bound.reshape(1, wk), 'const', 0, 0),
             (d_gain.reshape(1, D_V_DIM), 'const', 0, 0), (sd0, 'batch', 0, 0)],
        outs=[(jax.ShapeDtypeStruct((bsz, t, hd), F32), 'rows', hd, 0),
              (jax.ShapeDtypeStruct(sc0.shape, F32), 'batch', 0, 0),
              (jax.ShapeDtypeStruct((bsz, t, wv), F32), 'rows', wv, 0),
              (jax.ShapeDtypeStruct(sd0.shape, F32), 'batch', 0, 0)],
        scratch=[pltpu.VMEM((nb, C_DIM, hd), F32), pltpu.VMEM((nb, TAIL, 3 * hd), F32),
                 pltpu.VMEM((nb, D_HEADS, D_V_DIM, D_EXPAND), F32)],
        name="cd_mixers")
    return oc.reshape(bsz * t, hd), sc, od.reshape(bsz * t, wv), sd


def _pad_cols(w, mult=LANE):
    pad = (-w.shape[-1]) % mult
    return jnp.pad(w, [(0, 0)] * (w.ndim - 1) + [(0, pad)])


def _mixer_ab(proj, bsz, t, pos, prm, cache):
    n = bsz * t
    qpad, qipad, k16, v16, ki16, k32, v32, ki32 = dsa_prep(proj, pos, prm['a_q_gain'][0], prm['a_k_gain'][0], t)
    if cache is None:
        limit = jnp.tile((pos // CHUNK + 1) * CHUNK, bsz).reshape(n, 1)
        a_out = dsa_attention(qpad, qipad, proj, limit, k16, v16, ki16, bsz=bsz, tq=t, tk=t,
                              causal=True, n_sel=min(TOPK_MAX, t // 4))
        c0 = jnp.zeros((bsz, B_HEADS, B_QK_DIM, B_V_DIM), F32)
        n0 = jnp.zeros((bsz, B_HEADS, B_QK_DIM), F32)
        m0 = jnp.zeros((bsz, B_HEADS), F32)
        chunk = CHUNK
    else:
        k_c, v_c, ki_c, c0, n0, m0 = cache
        past = k_c.shape[1]
        n_keys = past + t
        tk = -(-n_keys // KEY_CHUNK) * KEY_CHUNK

        def with_cache(c, new):
            c = c.reshape(bsz, past, -1).astype(BF16)
            c = jnp.pad(c, ((0, 0), (0, 0), (0, LANE - c.shape[-1])))
            return jnp.concatenate([c, new.reshape(bsz, t, LANE),
                                    jnp.zeros((bsz, tk - n_keys, LANE), BF16)], axis=1).reshape(bsz * tk, LANE)

        limit = jnp.full((n, 1), n_keys, jnp.int32)
        a_out = dsa_attention(qpad, qipad, proj, limit, with_cache(k_c, k16), with_cache(v_c, v16),
                              with_cache(ki_c, ki16), bsz=bsz, tq=t, tk=tk, causal=False,
                              n_sel=min(TOPK_MAX, n_keys // 4))
        chunk = t
    h, c, n_, m = mlstm_mixer(proj.reshape(bsz, t, -1), prm['b_gate_bias'][0], prm['b_norm_gain'][0], c0, n0, m0,
                              bsz=bsz, t=t, chunk=chunk)
    st = (k32.reshape(bsz, t, A_KV_HEADS, HEAD_DIM), v32.reshape(bsz, t, A_KV_HEADS, HEAD_DIM),
          ki32.reshape(bsz, t, IDX_DIM), c, n_, m)
    return a_out, h, st


def _mixer_cd(proj, bsz, t, prm, lower_bound, cache):
    hd = C_HEADS * C_DIM
    if cache is None:
        sc0 = jnp.zeros((bsz, C_HEADS, C_DIM, C_DIM), F32)
        conv_prev = jnp.zeros((bsz, CONV_W - 1, 3 * hd), F32)
        sd0 = jnp.zeros((bsz, D_HEADS, D_EXPAND, D_V_DIM), F32)
        chunk = CHUNK
    else:
        sc0, conv_prev, sd0 = cache
        chunk = t
    oc, sc, od, sd = cd_mixers(proj.reshape(bsz, t, -1), prm['c_conv_w'][0], prm['c_a_log'][0], prm['c_dt_bias'][0],
                               prm['c_norm_gain'][0], sc0, conv_prev, lower_bound, prm['d_norm_gain'][0], sd0,
                               bsz=bsz, t=t, chunk=chunk)
    qkv = proj.reshape(bsz, t, -1)[:, :, CD_QKV:CD_QKV + 3 * hd]
    conv_new = jnp.concatenate([conv_prev, qkv[:, t - (CONV_W - 1):]], axis=1)[:, -(CONV_W - 1):]
    return oc, od, (sc, conv_new, sd)


def _trunk(x, pos_offset, cache, prm, wts):
    bsz, t, d = x.shape
    n = bsz * t
    pos = pos_offset + jnp.arange(t, dtype=jnp.int32)
    probs = jax.nn.softmax(prm['d_lb_logits'], axis=0)
    lower_bounds = jnp.cumsum(probs, axis=0) - probs[0]
    xf = x.reshape(n, d)

    lc = None if cache is None else tuple(c[0] for c in cache[:6])
    proj = norm_matmul(xf, prm['norm_mix'][0], wts['w_in_ab'])
    a_out, b_out, st_even = _mixer_ab(proj, bsz, t, pos, prm, lc)
    xf = matmul_residual(a_out, b_out, wts['w_out_ab'], xf)
    xf = ffn_residual(xf, prm['norm_ffn'][0], wts['ffn_w1'], wts['ffn_w3'], wts['ffn_w2'])

    lc = None if cache is None else tuple(c[0] for c in cache[6:])
    proj = norm_matmul(xf, prm['norm_mix'][1], wts['w_in_cd'])
    c_out, d_out, st_odd = _mixer_cd(proj, bsz, t, prm, lower_bounds[1], lc)
    xf = matmul_residual(c_out, d_out, wts['w_out_cd'], xf)
    xf = moe_residual(xf, prm['norm_ffn'][1], wts['moe_router'], wts['moe_w1'], wts['moe_w3'], wts['moe_w2'])

    new_state = tuple(s[None] for s in st_even + st_odd)
    return xf.reshape(bsz, t, d), new_state


def kernel(x_prompt, x_sample, cache_a_k, cache_a_v, cache_a_kidx, state_b_c, state_b_n, state_b_m,
           state_c_s, state_c_conv, state_d_s, norm_mix, norm_ffn, w_in_ab, w_out_ab, a_q_gain, a_k_gain,
           b_gate_bias, b_norm_gain, w_in_cd, w_out_cd, c_conv_w, c_a_log, c_dt_bias, c_norm_gain,
           d_lb_logits, d_norm_gain, ffn_w1, ffn_w3, ffn_w2, moe_router, moe_w1, moe_w3, moe_w2):
    prm = dict(norm_mix=norm_mix, norm_ffn=norm_ffn, a_q_gain=a_q_gain, a_k_gain=a_k_gain,
               b_gate_bias=b_gate_bias, b_norm_gain=b_norm_gain, c_conv_w=c_conv_w, c_a_log=c_a_log,
               c_dt_bias=c_dt_bias, c_norm_gain=c_norm_gain, d_lb_logits=d_lb_logits, d_norm_gain=d_norm_gain)
    wts = dict(w_in_ab=_permute_w_in_ab(w_in_ab[0]).astype(BF16), w_out_ab=w_out_ab[0].astype(BF16),
               w_in_cd=_permute_w_in_cd(w_in_cd[0]).astype(BF16), w_out_cd=w_out_cd[0].astype(BF16),
               ffn_w1=ffn_w1[0].astype(BF16), ffn_w3=ffn_w3[0].astype(BF16), ffn_w2=ffn_w2[0].astype(BF16),
               moe_router=_pad_cols(moe_router[0]),
               moe_w1=moe_w1[0].astype(BF16), moe_w3=moe_w3[0].astype(BF16), moe_w2=moe_w2[0].astype(BF16))
    cache = (cache_a_k, cache_a_v, cache_a_kidx, state_b_c, state_b_n, state_b_m, state_c_s, state_c_conv, state_d_s)
    y_prompt, st_p = _trunk(x_prompt, 0, None, prm, wts)
    y_sample, st_s = _trunk(x_sample, cache_a_k.shape[2], cache, prm, wts)
    return (y_prompt, y_sample) + st_p + st_s
```

```python
import functools
import math

import jax
import jax.numpy as jnp
import numpy as np
from jax import lax
from jax.experimental import pallas as pl
from jax.experimental.pallas import tpu as pltpu

F32 = jnp.float32
BF16 = jnp.bfloat16

EPS = 1e-6
ROPE_THETA = 500000.0
ROT_FRACTION = 4
CHUNK = 64
A_HEADS, A_KV_HEADS, HEAD_DIM = 8, 2, 64
IDX_HEADS, IDX_DIM = 4, 64
TOPK_MAX, Q_BLOCK = 256, 128
B_HEADS, B_QK_DIM, B_V_DIM = 4, 64, 128
C_HEADS, C_DIM, CONV_W = 4, 128, 4
D_HEADS, D_EXPAND, D_V_DIM = 4, 128, 128
N_EXPERTS, TOP_K_EXPERTS = 8, 2

LANE = 128
VMEM_LIMIT = 48 * 1024 * 1024

EVEN_SPLITS = (A_HEADS * HEAD_DIM, A_KV_HEADS * HEAD_DIM, A_KV_HEADS * HEAD_DIM,
               IDX_HEADS * IDX_DIM, IDX_DIM, IDX_HEADS,
               B_HEADS * B_QK_DIM, B_HEADS * B_QK_DIM, B_HEADS * B_V_DIM,
               B_HEADS, B_HEADS, B_HEADS * B_V_DIM)
ODD_SPLITS = (3 * C_HEADS * C_DIM, C_HEADS, C_HEADS, C_HEADS * C_DIM,
              D_HEADS * D_EXPAND, D_HEADS * D_EXPAND, D_HEADS * D_V_DIM, D_HEADS * D_V_DIM)


def _split_cols(p, widths):
    cuts = [int(c) for c in np.cumsum(widths)[:-1]]
    return jnp.split(p, cuts, axis=-1)


def _row_tile(n, target):
    t = min(n, target)
    while n % t:
        t //= 2
    return t


def _col_tile(n, target):
    best = LANE
    for k in range(1, n // LANE + 1):
        c = k * LANE
        if n % c == 0 and c <= target:
            best = c
    return best


def _rms_rows(x, gain):
    return x * lax.rsqrt(jnp.mean(x * x, axis=-1, keepdims=True) + EPS) * gain


def _norm_matmul_kernel(x_ref, g_ref, w_ref, o_ref, xn_ref):
    @pl.when(pl.program_id(1) == 0)
    def _():
        xn_ref[...] = _rms_rows(x_ref[...], g_ref[...]).astype(BF16)

    o_ref[...] = jnp.dot(xn_ref[...], w_ref[...], preferred_element_type=F32)


def norm_matmul(x, gain, w):
    n, d = x.shape
    m = w.shape[1]
    tm = _row_tile(n, 1024)
    tn = _col_tile(m, 1536)
    return pl.pallas_call(
        _norm_matmul_kernel,
        grid=(n // tm, m // tn),
        in_specs=[pl.BlockSpec((tm, d), lambda i, j: (i, 0)),
                  pl.BlockSpec((1, d), lambda i, j: (0, 0)),
                  pl.BlockSpec((d, tn), lambda i, j: (0, j))],
        out_specs=pl.BlockSpec((tm, tn), lambda i, j: (i, j)),
        out_shape=jax.ShapeDtypeStruct((n, m), F32),
        scratch_shapes=[pltpu.VMEM((tm, d), BF16)],
        compiler_params=pltpu.CompilerParams(
            dimension_semantics=("parallel", "arbitrary"), vmem_limit_bytes=VMEM_LIMIT),
        name="norm_matmul",
    )(x, gain.reshape(1, d), w)


def _matmul_res_kernel(a_ref, b_ref, w_ref, r_ref, o_ref):
    ka = a_ref.shape[1]
    o_ref[...] = (r_ref[...] + jnp.dot(a_ref[...].astype(BF16), w_ref[0:ka, :], preferred_element_type=F32)
                  + jnp.dot(b_ref[...].astype(BF16), w_ref[ka:, :], preferred_element_type=F32))


def matmul_residual(a, b, w, res):
    n, ka = a.shape
    kb = b.shape[1]
    m = w.shape[1]
    tm = _row_tile(n, 1024)
    return pl.pallas_call(
        _matmul_res_kernel,
        grid=(n // tm,),
        in_specs=[pl.BlockSpec((tm, ka), lambda i: (i, 0)),
                  pl.BlockSpec((tm, kb), lambda i: (i, 0)),
                  pl.BlockSpec((ka + kb, m), lambda i: (0, 0)),
                  pl.BlockSpec((tm, m), lambda i: (i, 0))],
        out_specs=pl.BlockSpec((tm, m), lambda i: (i, 0)),
        out_shape=jax.ShapeDtypeStruct((n, m), F32),
        compiler_params=pltpu.CompilerParams(
            dimension_semantics=("parallel",), vmem_limit_bytes=VMEM_LIMIT),
        name="matmul_residual",
    )(a, b, w, res)


def _swiglu_tile(xn, w1, w3):
    h1 = jnp.dot(xn, w1, preferred_element_type=F32)
    h3 = jnp.dot(xn, w3, preferred_element_type=F32)
    return h1 * jax.nn.sigmoid(h1) * h3


def _ffn_kernel(x_ref, g_ref, w1_ref, w3_ref, w2_ref, o_ref, xn_ref):
    @pl.when(pl.program_id(1) == 0)
    def _():
        x = x_ref[...]
        xn_ref[...] = _rms_rows(x, g_ref[...]).astype(BF16)
        o_ref[...] = x

    act = _swiglu_tile(xn_ref[...], w1_ref[...], w3_ref[...])
    o_ref[...] += jnp.dot(act.astype(BF16), w2_ref[...], preferred_element_type=F32)


def ffn_residual(x, gain, w1, w3, w2):
    n, d = x.shape
    f = w1.shape[1]
    tm = _row_tile(n, 1024)
    tf = _col_tile(f, 512)
    return pl.pallas_call(
        _ffn_kernel,
        grid=(n // tm, f // tf),
        in_specs=[pl.BlockSpec((tm, d), lambda i, j: (i, 0)),
                  pl.BlockSpec((1, d), lambda i, j: (0, 0)),
                  pl.BlockSpec((d, tf), lambda i, j: (0, j)),
                  pl.BlockSpec((d, tf), lambda i, j: (0, j)),
                  pl.BlockSpec((tf, d), lambda i, j: (j, 0))],
        out_specs=pl.BlockSpec((tm, d), lambda i, j: (i, 0)),
        out_shape=jax.ShapeDtypeStruct((n, d), F32),
        scratch_shapes=[pltpu.VMEM((tm, d), BF16)],
        compiler_params=pltpu.CompilerParams(
            dimension_semantics=("parallel", "arbitrary"), vmem_limit_bytes=VMEM_LIMIT),
        name="ffn_residual",
    )(x, gain.reshape(1, d), w1, w3, w2)


MOE_VMEM_LIMIT = 58 * 1024 * 1024
MOE_CAPS = (256, 288, 320, 384, 512)


def _moe_route_kernel(x_ref, g_ref, r_ref, xn_ref, comb_ref, post_ref, cnt_ref):
    x = x_ref[...]
    tm = x.shape[0]
    xn = _rms_rows(x, g_ref[...])
    xn_ref[...] = xn.astype(BF16)
    logits = jnp.dot(xn, r_ref[...], preferred_element_type=F32, precision=lax.Precision.HIGHEST)
    lane = lax.broadcasted_iota(jnp.int32, logits.shape, 1)
    logits = jnp.where(lane < N_EXPERTS, logits, -jnp.inf)
    m1 = jnp.max(logits, axis=-1, keepdims=True)
    i1 = jnp.min(jnp.where(logits == m1, lane, LANE), axis=-1, keepdims=True)
    rest = jnp.where(lane == i1, -jnp.inf, logits)
    m2 = jnp.max(rest, axis=-1, keepdims=True)
    i2 = jnp.min(jnp.where(rest == m2, lane, LANE), axis=-1, keepdims=True)
    e2 = jnp.exp(m2 - m1)
    den = 1.0 + e2
    comb_ref[...] = jnp.where(lane == i1, 1.0 / den, 0.0) + jnp.where(lane == i2, e2 / den, 0.0)
    chosen = (lane == i1) | (lane == i2)
    sel = jnp.where(chosen, 1.0, 0.0)
    tril16 = jnp.where(_tri_mask(LANE), 1.0, 0.0).astype(BF16)
    seen = jnp.zeros((1, LANE), F32)
    ranks = []
    for blk in range(tm // LANE):
        sb = sel[blk * LANE:(blk + 1) * LANE]
        ranks.append(_dot(tril16, sb.astype(BF16)) + seen - 1.0)
        seen = seen + jnp.sum(sb, axis=0, keepdims=True)
    rank = jnp.where(chosen, jnp.concatenate(ranks, axis=0), -1.0)
    post_ref[0] = rank.T[0:N_EXPERTS, :]
    cnt_ref[0] = seen


def _moe_expert_kernel(cnt_ref, xn_ref, comb_ref, post_ref, x_ref, w1_ref, w3_ref, w2_ref, o_ref,
                       xe_scr, y_scr, *, caps):
    i = pl.program_id(0)
    e = pl.program_id(1)
    j = pl.program_id(2)
    last = pl.num_programs(2) - 1
    tm = x_ref.shape[0]

    @pl.when(jnp.logical_and(e == 0, j == 0))
    def _():
        o_ref[...] = x_ref[...]

    cnt = cnt_ref[i * N_EXPERTS + e]

    def expert_step(cap):
        def pick():
            rank_row = post_ref[0, pl.ds(e, 1), :].astype(jnp.int32)
            slot = lax.broadcasted_iota(jnp.int32, (cap, tm), 0)
            return jnp.where(rank_row == slot, 1.0, 0.0).astype(BF16)

        @pl.when(j == 0)
        def _():
            xe_scr[0:cap, :] = _dot(pick(), xn_ref[...]).astype(BF16)

        act = _swiglu_tile(xe_scr[0:cap, :], w1_ref[0], w3_ref[0])
        yj = _dot(act.astype(BF16), w2_ref[0])

        @pl.when(j == 0)
        def _():
            y_scr[0:cap, :] = yj

        @pl.when(j > 0)
        def _():
            y_scr[0:cap, :] += yj

        @pl.when(j == last)
        def _():
            comb = comb_ref[...]
            lane = lax.broadcasted_iota(jnp.int32, comb.shape, 1)
            gate = jnp.sum(jnp.where(lane == e, comb, 0.0), axis=-1, keepdims=True)
            o_ref[...] += gate * _dot_tn(pick(), y_scr[0:cap, :].astype(BF16))

    lo = 0
    for cap in caps:
        @pl.when(jnp.logical_and(cnt > lo, cnt <= cap))
        def _(cap=cap):
            expert_step(cap)
        lo = cap


def moe_residual(x, gain, router, w1, w3, w2):
    n, d = x.shape
    ne, _, f = w1.shape
    tm = _row_tile(n, 1024)
    tf = _col_tile(f, 896)
    nt = n // tm
    xn, comb, post, cnt = pl.pallas_call(
        _moe_route_kernel,
        grid=(nt,),
        in_specs=[pl.BlockSpec((tm, d), lambda i: (i, 0)),
                  pl.BlockSpec((1, d), lambda i: (0, 0)),
                  pl.BlockSpec((d, LANE), lambda i: (0, 0))],
        out_specs=[pl.BlockSpec((tm, d), lambda i: (i, 0)),
                   pl.BlockSpec((tm, LANE), lambda i: (i, 0)),
                   pl.BlockSpec((1, ne, tm), lambda i: (i, 0, 0)),
                   pl.BlockSpec((1, 1, LANE), lambda i: (i, 0, 0))],
        out_shape=[jax.ShapeDtypeStruct((n, d), BF16), jax.ShapeDtypeStruct((n, LANE), F32),
                   jax.ShapeDtypeStruct((nt, ne, tm), F32), jax.ShapeDtypeStruct((nt, 1, LANE), F32)],
        compiler_params=pltpu.CompilerParams(dimension_semantics=("parallel",), vmem_limit_bytes=VMEM_LIMIT),
        name="moe_route",
    )(x, gain.reshape(1, d), router)
    counts = cnt[:, 0, :ne].astype(jnp.int32).reshape(nt * ne)
    caps = tuple(c for c in MOE_CAPS if c < tm) + (tm,)
    grid_spec = pltpu.PrefetchScalarGridSpec(
        num_scalar_prefetch=1,
        grid=(nt, ne, f // tf),
        in_specs=[pl.BlockSpec((tm, d), lambda i, e, j, c: (i, 0)),
                  pl.BlockSpec((tm, LANE), lambda i, e, j, c: (i, 0)),
                  pl.BlockSpec((1, ne, tm), lambda i, e, j, c: (i, 0, 0)),
                  pl.BlockSpec((tm, d), lambda i, e, j, c: (i, 0)),
                  pl.BlockSpec((1, d, tf), lambda i, e, j, c: (e, 0, j)),
                  pl.BlockSpec((1, d, tf), lambda i, e, j, c: (e, 0, j)),
                  pl.BlockSpec((1, tf, d), lambda i, e, j, c: (e, j, 0))],
        out_specs=pl.BlockSpec((tm, d), lambda i, e, j, c: (i, 0)),
        scratch_shapes=[pltpu.VMEM((tm, d), BF16), pltpu.VMEM((tm, d), F32)])
    return pl.pallas_call(
        functools.partial(_moe_expert_kernel, caps=caps),
        grid_spec=grid_spec,
        out_shape=jax.ShapeDtypeStruct((n, d), F32),
        compiler_params=pltpu.CompilerParams(
            dimension_semantics=("parallel", "arbitrary", "arbitrary"), vmem_limit_bytes=MOE_VMEM_LIMIT),
        name="moe_experts",
    )(counts, xn, comb, post, x, w1, w3, w2)


AB_QA, AB_VB, AB_OB, AB_QI, AB_QB, AB_KB, AB_KA, AB_VA, AB_MISC, AB_TOTAL = (
    0, 512, 1024, 1536, 1792, 2048, 2304, 2432, 2560, 2688)
MISC_WI, MISC_IB, MISC_FB = 64, 68, 72
HALF = LANE // 2
KEY_CHUNK = 512
MASKED = -1e30
KEY_OF_NEG_INF = -2139095041
I16_MIN, I16_MAX = -32768, 32767


def _permute_w_in_ab(w):
    qa, ka, va, qi, ki, wi, qb, kb, vb, ib, fb, ob = _split_cols(w, EVEN_SPLITS)
    pad = jnp.zeros((w.shape[0], LANE - IDX_DIM - 3 * IDX_HEADS), w.dtype)
    return jnp.concatenate([qa, vb, ob, qi, qb, kb, ka, va, ki, wi, ib, fb, pad], axis=1)


def _rope_tables(pos):
    rot = HEAD_DIM // ROT_FRACTION
    half = rot // 2
    inv_freq = ROPE_THETA ** (-jnp.arange(half, dtype=F32) * 2.0 / rot)
    ang = pos.astype(F32)[:, None] * inv_freq[None, :]
    cos, sin = jnp.cos(ang), jnp.sin(ang)
    t = pos.shape[0]
    one = jnp.ones((t, HEAD_DIM - rot), F32)
    zero_r = jnp.zeros((t, HEAD_DIM - rot), F32)
    zero_h = jnp.zeros((t, half), F32)
    c = jnp.concatenate([cos, cos, one], axis=1)
    s_up = jnp.concatenate([-sin, zero_h, zero_r], axis=1)
    s_dn = jnp.concatenate([zero_h, sin, zero_r], axis=1)
    return tuple(jnp.concatenate([a, a], axis=1) for a in (c, s_up, s_dn))


def _rope_tile(x, c, s_up, s_dn):
    half = HEAD_DIM // ROT_FRACTION // 2
    return x * c + pltpu.roll(x, LANE - half, 1) * s_up + pltpu.roll(x, half, 1) * s_dn


def _head_norm_tile(x, gain, same_head):
    sq = x * x
    hi = sq.astype(BF16)
    lo = (sq - hi.astype(F32)).astype(BF16)
    ss = (jnp.dot(hi, same_head, preferred_element_type=F32)
          + jnp.dot(lo, same_head, preferred_element_type=F32))
    return x * lax.rsqrt(ss * (1.0 / HEAD_DIM) + EPS) * gain


def _aprep_kernel(qa_ref, ka_ref, va_ref, qi_ref, misc_ref, c_ref, su_ref, sd_ref, qg_ref, kg_ref,
                  qpad_ref, qipad_ref, k16_ref, v16_ref, ki16_ref, k32_ref, v32_ref, ki32_ref):
    c, su, sd = c_ref[...], su_ref[...], sd_ref[...]
    tm = c.shape[0]
    row = lax.broadcasted_iota(jnp.int32, (LANE, LANE), 0)
    col = lax.broadcasted_iota(jnp.int32, (LANE, LANE), 1)
    same_head = jnp.where(row // HALF == col // HALF, 1.0, 0.0).astype(BF16)
    lane = lax.broadcasted_iota(jnp.int32, (tm, LANE), 1)
    low = lane < HALF

    heads_per_group = A_HEADS // A_KV_HEADS
    for p in range(A_HEADS // 2):
        y = _rope_tile(_head_norm_tile(qa_ref[:, p * LANE:(p + 1) * LANE], qg_ref[...], same_head), c, su, sd)
        y = y * HEAD_DIM ** -0.5
        y_sw = pltpu.roll(y, HALF, 1)
        for o in range(2):
            h = 2 * p + o
            g = h // heads_per_group
            src = y if o == g else y_sw
            qpad_ref[:, h * LANE:(h + 1) * LANE] = jnp.where(low if g == 0 else ~low, src, 0.0).astype(BF16)
    k = _rope_tile(_head_norm_tile(ka_ref[...], kg_ref[...], same_head), c, su, sd)
    k32_ref[...] = k
    k16_ref[...] = k.astype(BF16)
    v = va_ref[...]
    v32_ref[...] = v
    v16_ref[...] = v.astype(BF16)
    for p in range(IDX_HEADS // 2):
        y = _rope_tile(qi_ref[:, p * LANE:(p + 1) * LANE], c, su, sd)
        y_sw = pltpu.roll(y, HALF, 1)
        qipad_ref[:, (2 * p) * LANE:(2 * p + 1) * LANE] = jnp.where(low, y, 0.0).astype(BF16)
        qipad_ref[:, (2 * p + 1) * LANE:(2 * p + 2) * LANE] = jnp.where(low, y_sw, 0.0).astype(BF16)
    ki = _rope_tile(misc_ref[...], c, su, sd)
    ki32_ref[...] = ki[:, :IDX_DIM]
    ki16_ref[...] = jnp.where(low, ki, 0.0).astype(BF16)


def dsa_prep(proj, pos, q_gain, k_gain, t):
    n = proj.shape[0]
    tm = _row_tile(n, 512)
    tabs = _rope_tables(pos)
    if t < tm:
        tabs = tuple(jnp.tile(a, (tm // t, 1)) for a in tabs)
    nt = tabs[0].shape[0] // tm
    tab_spec = pl.BlockSpec((tm, LANE), lambda i: (i % nt, 0))
    gain_spec = pl.BlockSpec((1, LANE), lambda i: (0, 0))

    def col(width, offset):
        return pl.BlockSpec((tm, width), lambda i: (i, offset // width))

    def out(width, dtype):
        return (jax.ShapeDtypeStruct((n, width), dtype), pl.BlockSpec((tm, width), lambda i: (i, 0)))

    outs = [out(A_HEADS * LANE, BF16), out(IDX_HEADS * LANE, BF16), out(LANE, BF16), out(LANE, BF16),
            out(LANE, BF16), out(LANE, F32), out(LANE, F32), out(IDX_DIM, F32)]
    return pl.pallas_call(
        _aprep_kernel,
        grid=(n // tm,),
        in_specs=[col(A_HEADS * HEAD_DIM, AB_QA), col(LANE, AB_KA), col(LANE, AB_VA),
                  col(IDX_HEADS * IDX_DIM, AB_QI), col(LANE, AB_MISC), tab_spec, tab_spec, tab_spec,
                  gain_spec, gain_spec],
        out_specs=[o[1] for o in outs],
        out_shape=[o[0] for o in outs],
        compiler_params=pltpu.CompilerParams(dimension_semantics=("parallel",), vmem_limit_bytes=VMEM_LIMIT),
        name="dsa_prep",
    )(proj, proj, proj, proj, proj, *tabs, jnp.tile(q_gain, 2).reshape(1, LANE), jnp.tile(k_gain, 2).reshape(1, LANE))


N_PARTIAL = 4


def _add_tiles(accs, m, sub):
    accs = list(accs)
    for t in range(m.shape[0] // sub):
        accs[t % len(accs)] = accs[t % len(accs)] + m[t * sub:(t + 1) * sub]
    return tuple(accs)


def _dsa_kernel(q_ref, qi_ref, misc_ref, lim_ref, k_ref, v_ref, ki_ref, o_ref, key_ref, bias_ref, hi_ref, lo_ref,
                *, nch, n_sel):
    qb = q_ref.shape[0]
    kc = KEY_CHUNK
    n_idx = IDX_HEADS
    hpg = A_HEADS // A_KV_HEADS
    nt = (((1,), (1,)), ((), ()))

    limit = lim_ref[0]
    misc_t = misc_ref[...].T
    wscale = IDX_HEADS ** -0.5 * IDX_DIM ** -0.5
    w = [misc_t[MISC_WI + j:MISC_WI + j + 1, :] * wscale for j in range(n_idx)]
    qis = [qi_ref[:, j * LANE:(j + 1) * LANE] for j in range(n_idx)]

    def score_body(c, carry):
        off = pl.multiple_of(c * kc, kc)
        kic = ki_ref[pl.ds(off, kc), :]
        lgs = [lax.dot_general(kic, qis[j], nt, preferred_element_type=F32) for j in range(n_idx)]
        s = w[0] * jnp.maximum(lgs[0], 0.0)
        for j in range(1, n_idx):
            s = s + w[j] * jnp.maximum(lgs[j], 0.0)
        kidx = off + lax.broadcasted_iota(jnp.int32, (kc, qb), 0)
        s = jnp.where(kidx < limit, s, -jnp.inf)
        bits = lax.bitcast_convert_type(s, jnp.int32)
        key = jnp.where(bits < 0, bits ^ 0x7FFFFFFF, bits)
        key_ref[pl.ds(off, kc), :] = key
        hi_ref[pl.ds(off, kc), :] = jnp.right_shift(key, 16).astype(jnp.int16)
        return carry

    lax.fori_loop(0, nch, score_body, 0)

    def count_ge(cand):
        sub = 8

        def body(c, accs):
            off = pl.multiple_of(c * kc, kc)
            m = jnp.where(key_ref[pl.ds(off, kc), :] >= cand, 1.0, 0.0)
            return _add_tiles(accs, m, sub)

        accs = lax.fori_loop(0, nch, body, (jnp.zeros((sub, qb), F32),) * N_PARTIAL, unroll=True)
        return jnp.sum(sum(accs), axis=0, keepdims=True)

    def count_ge16(ref, cand32):
        cand = cand32.astype(jnp.int16)
        sub = 16

        def body(c, accs):
            off = pl.multiple_of(c * kc, kc)
            m = jnp.where(ref[pl.ds(off, kc), :] >= cand, jnp.int16(1), jnp.int16(0))
            return _add_tiles(accs, m, sub)

        accs = lax.fori_loop(0, nch, body, (jnp.zeros((sub, qb), jnp.int16),) * N_PARTIAL, unroll=True)
        return jnp.sum(sum(accs).astype(F32), axis=0, keepdims=True)

    def kth_largest16(ref, want):
        tau = jnp.where(count_ge16(ref, jnp.zeros((1, qb), jnp.int32)) >= want, 0, I16_MIN).astype(jnp.int32)

        def bisect(i, tau):
            cand = tau | jnp.left_shift(jnp.int32(1), 14 - i)
            return jnp.where(count_ge16(ref, cand) >= want, cand, tau)

        return lax.fori_loop(0, 15, bisect, tau)

    want = float(n_sel)
    tau_hi = kth_largest16(hi_ref, want)
    above = jnp.where(tau_hi < I16_MAX, count_ge16(hi_ref, jnp.minimum(tau_hi + 1, I16_MAX)), 0.0)

    def low_body(c, carry):
        off = pl.multiple_of(c * kc, kc)
        key = key_ref[pl.ds(off, kc), :]
        low = (key & 0xFFFF) + I16_MIN
        lo_ref[pl.ds(off, kc), :] = jnp.where(jnp.right_shift(key, 16) == tau_hi, low, I16_MIN).astype(jnp.int16)
        return carry

    lax.fori_loop(0, nch, low_body, 0)
    tau_lo = kth_largest16(lo_ref, want - above)
    tau = jnp.left_shift(tau_hi, 16) + (tau_lo - I16_MIN)

    room = want - count_ge(tau + 1)
    r_i = lax.broadcasted_iota(jnp.int32, (LANE, LANE), 0)
    c_i = lax.broadcasted_iota(jnp.int32, (LANE, LANE), 1)
    prefix_ones = jnp.where(r_i >= c_i, 1.0, 0.0).astype(BF16)
    identity = jnp.where(r_i == c_i, 1.0, 0.0).astype(BF16)

    def bias_body(c, seen):
        off = pl.multiple_of(c * kc, kc)
        tiles = range(kc // LANE)
        xs = [key_ref[pl.ds(off + t * LANE, LANE), :] for t in tiles]
        eqs = [x == tau for x in xs]
        eqfs = [jnp.where(eq, 1.0, 0.0) for eq in eqs]
        ranks = [jnp.dot(prefix_ones, eqf.astype(BF16), preferred_element_type=F32) for eqf in eqfs]
        sels = []
        for t in tiles:
            sel = ((xs[t] > tau) | (eqs[t] & (ranks[t] + seen <= room))) & (xs[t] != KEY_OF_NEG_INF)
            sels.append(sel)
            seen = seen + jnp.sum(eqfs[t], axis=0, keepdims=True)
        if qb == LANE:
            for t in tiles:
                bias_ref[:, pl.ds(off + t * LANE, LANE)] = jnp.where(sels[t], 0.0, MASKED).T
        else:
            sel_ts = [lax.dot_general(jnp.where(sel, 1.0, 0.0).astype(BF16), identity, (((0,), (0,)), ((), ())),
                                      preferred_element_type=F32) for sel in sels]
            for t in tiles:
                bias_ref[:, pl.ds(off + t * LANE, LANE)] = jnp.where(sel_ts[t] > 0.5, 0.0, MASKED)
        return seen

    lax.fori_loop(0, nch, bias_body, jnp.zeros((1, qb), F32))

    lane = lax.broadcasted_iota(jnp.int32, (qb, LANE), 1)
    qgs = [jnp.concatenate([q_ref[:, (hpg * g + h) * LANE:(hpg * g + h + 1) * LANE] for h in range(hpg)], axis=0)
           for g in range(A_KV_HEADS)]

    def att_body(c, carry):
        off = pl.multiple_of(c * kc, kc)
        kch = k_ref[pl.ds(off, kc), :]
        vch = v_ref[pl.ds(off, kc), :]
        bias = bias_ref[:, pl.ds(off, kc)][None]
        new = [None] * A_KV_HEADS

        def group(g):
            m, l, acc = carry[g]
            s = lax.dot_general(qgs[g], kch, nt, preferred_element_type=F32)
            yield
            s = (s.reshape(hpg, qb, kc) + bias).reshape(hpg * qb, kc)
            m_new = jnp.maximum(m, jnp.max(s, axis=1, keepdims=True))
            alpha = jnp.exp(m - m_new)
            p = jnp.exp(s - m_new)
            l = alpha * l + jnp.sum(p, axis=1, keepdims=True)
            p16 = p.astype(BF16)
            yield
            new[g] = (m_new, l, alpha * acc + jnp.dot(p16, vch, preferred_element_type=F32))

        _lockstep(group(g) for g in range(A_KV_HEADS))
        return tuple(new)

    init = tuple((jnp.full((hpg * qb, 1), MASKED, F32), jnp.zeros((hpg * qb, 1), F32),
                  jnp.zeros((hpg * qb, LANE), F32)) for _ in range(A_KV_HEADS))
    res = lax.fori_loop(0, nch, att_body, init)
    outs = []
    for g in range(A_KV_HEADS):
        _, l, acc = res[g]
        og = acc / l
        for h in range(hpg):
            oh = og[h * qb:(h + 1) * qb]
            outs.append(oh if (h % 2) == g else pltpu.roll(oh, HALF, 1))
    for p in range(A_HEADS // 2):
        o_ref[:, p * LANE:(p + 1) * LANE] = jnp.where(lane < HALF, outs[2 * p], outs[2 * p + 1])


def dsa_attention(qpad, qipad, proj, limit, k16, v16, ki16, *, bsz, tq, tk, causal, n_sel):
    qb = min(Q_BLOCK, tq)
    nqb = tq // qb
    assert tk % KEY_CHUNK == 0 and tk >= n_sel
    if causal:
        per_group = KEY_CHUNK // qb
        groups = [(g * per_group, per_group, g + 1) for g in range(nqb // per_group)]
    else:
        groups = [(0, nqb, tk // KEY_CHUNK)]
    lim3 = limit.reshape(bsz * nqb, 1, qb)
    out = None
    for first, count, nch in groups:
        def qspec(width, col=0, first=first):
            return pl.BlockSpec((qb, width), lambda b, i: (b * nqb + first + i, col))

        kspec = pl.BlockSpec((tk, LANE), lambda b, i: (b, 0))
        in_specs = [qspec(A_HEADS * LANE), qspec(IDX_HEADS * LANE), qspec(LANE, AB_MISC // LANE),
                    pl.BlockSpec((1, 1, qb), lambda b, i, first=first: (b * nqb + first + i, 0, 0)),
                    kspec, kspec, kspec]
        args = [qpad, qipad, proj, lim3, k16, v16, ki16]
        kern = functools.partial(_dsa_kernel, nch=nch, n_sel=n_sel)
        aliases = {}
        if out is not None:
            in_specs.append(pl.BlockSpec(memory_space=pl.ANY))
            args.append(out)
            aliases = {len(args) - 1: 0}
            kern = functools.partial(_dsa_kernel_with_carry, nch=nch, n_sel=n_sel)
        out = pl.pallas_call(
            kern,
            grid=(bsz, count),
            in_specs=in_specs,
            out_specs=qspec(A_HEADS * HEAD_DIM),
            out_shape=jax.ShapeDtypeStruct((bsz * tq, A_HEADS * HEAD_DIM), F32),
            scratch_shapes=[pltpu.VMEM((tk, qb), jnp.int32), pltpu.VMEM((qb, tk), F32),
                            pltpu.VMEM((tk, qb), jnp.int16), pltpu.VMEM((tk, qb), jnp.int16)],
            input_output_aliases=aliases,
            compiler_params=pltpu.CompilerParams(
                dimension_semantics=("parallel", "arbitrary"), vmem_limit_bytes=VMEM_LIMIT),
            name="dsa_attention",
        )(*args)
    return out


def _dsa_kernel_with_carry(q_ref, qi_ref, misc_ref, lim_ref, k_ref, v_ref, ki_ref, carry_ref, o_ref, *scratch,
                           nch, n_sel):
    del carry_ref
    _dsa_kernel(q_ref, qi_ref, misc_ref, lim_ref, k_ref, v_ref, ki_ref, o_ref, *scratch, nch=nch, n_sel=n_sel)


SUB = 16


def _dot(a, b):
    return jnp.dot(a, b, preferred_element_type=F32)


def _dot_nt(a, b):
    return lax.dot_general(a, b, (((1,), (1,)), ((), ())), preferred_element_type=F32)


def _dot_tn(a, b):
    return lax.dot_general(a, b, (((0,), (0,)), ((), ())), preferred_element_type=F32)


def _split2(x):
    hi = x.astype(BF16)
    return hi, (x - hi.astype(F32)).astype(BF16)


def _split3(x):
    hi = x.astype(BF16)
    r = x - hi.astype(F32)
    mid = r.astype(BF16)
    return hi, mid, (r - mid.astype(F32)).astype(BF16)


def _cumsum_rows(x, tril16):
    hi, mid, lo = _split3(x)
    return _dot(tril16, hi) + _dot(tril16, mid) + _dot(tril16, lo)


def _dot_f32(a, b):
    ah, al = _split2(a)
    bh, bl = _split2(b)
    return _dot(ah, bh) + (_dot(ah, bl) + _dot(al, bh))


def _tri_mask(n, strict=False):
    r = lax.broadcasted_iota(jnp.int32, (n, n), 0)
    c = lax.broadcasted_iota(jnp.int32, (n, n), 1)
    return r > c if strict else r >= c


MLSTM_SEQS_PER_STEP = 2
CD_SEQS_PER_STEP = 4


def _lockstep(stages):
    stages = list(stages)
    while stages:
        for g in list(stages):
            if next(g, StopIteration) is StopIteration:
                stages.remove(g)


def _chunk_call(kern, *, bsz, nb, nc, rows, ins, outs, scratch, name):
    def spec(a, kind, width, offset):
        if kind == 'rows':
            return pl.BlockSpec((nb, rows, width), lambda b, c: (b, c, offset // width))
        if kind == 'batch':
            return pl.BlockSpec((nb,) + tuple(a.shape[1:]), lambda b, c: (b,) + (0,) * (len(a.shape) - 1))
        return pl.BlockSpec(tuple(a.shape), lambda b, c: (0,) * len(a.shape))

    return pl.pallas_call(
        kern,
        grid=(bsz // nb, nc),
        in_specs=[spec(*i) for i in ins],
        out_specs=[spec(*o) for o in outs],
        out_shape=[o[0] for o in outs],
        scratch_shapes=scratch,
        compiler_params=pltpu.CompilerParams(
            dimension_semantics=("parallel", "arbitrary"), vmem_limit_bytes=VMEM_LIMIT),
        name=name,
    )(*[i[0] for i in ins])


def _mlstm_kernel(q_ref, k_ref, v_ref, og_ref, misc_ref, gb_ref, gain_ref, c0_ref, n0_ref, m0_ref,
                  h_ref, c_out_ref, n_out_ref, m_out_ref, c_scr, n_scr, m_scr):
    ci = pl.program_id(1)
    seqs = range(q_ref.shape[0])

    @pl.when(ci == 0)
    def _():
        c_scr[...] = c0_ref[...]
        n_scr[...] = n0_ref[...]
        m_scr[...] = m0_ref[...]

    _lockstep(_mlstm_chunk(q_ref.at[i], k_ref.at[i], v_ref.at[i], og_ref.at[i], misc_ref.at[i], gb_ref, gain_ref,
                           h_ref.at[i], c_scr.at[i], n_scr.at[i], m_scr.at[i]) for i in seqs)

    @pl.when(ci == pl.num_programs(1) - 1)
    def _():
        c_out_ref[...] = c_scr[...]
        n_out_ref[...] = n_scr[...]
        m_out_ref[...] = m_scr[...]


def _mlstm_chunk(q_ref, k_ref, v_ref, og_ref, misc_ref, gb_ref, gain_ref, h_ref, c_scr, n_scr, m_scr):
    rows = q_ref.shape[0]
    hr = B_HEADS * rows
    wq = B_HEADS * B_QK_DIM
    tril16 = jnp.where(_tri_mask(rows), 1.0, 0.0).astype(BF16)
    gates = misc_ref[...] + gb_ref[...]
    bcum = _cumsum_rows(jax.nn.log_sigmoid(gates), tril16)
    yield
    m_all = m_scr[...]

    def stack(f):
        return jnp.concatenate([f(h) for h in range(B_HEADS)], axis=0)

    lane_q = lax.broadcasted_iota(jnp.int32, (rows, wq), 1)
    qx, kx = q_ref[...], k_ref[...]
    q_all = stack(lambda h: jnp.where(lane_q // B_QK_DIM == h, qx, 0.0))
    k_all = stack(lambda h: jnp.where(lane_q // B_QK_DIM == h, kx, 0.0)) * B_QK_DIM ** -0.5
    v_all = stack(lambda h: v_ref[:, h * B_V_DIM:(h + 1) * B_V_DIM])
    b_col = stack(lambda h: bcum[:, MISC_FB + h:MISC_FB + h + 1])
    i_col = stack(lambda h: gates[:, MISC_IB + h:MISC_IB + h + 1])
    m_col = stack(lambda h: jnp.broadcast_to(m_all[:, h:h + 1], (rows, 1)))
    b_end = stack(lambda h: jnp.broadcast_to(bcum[rows - 1:rows, MISC_FB + h:MISC_FB + h + 1], (rows, 1)))
    bi_row = jnp.broadcast_to(b_col - i_col, (hr, LANE)).T[0:1, :]

    r = lax.broadcasted_iota(jnp.int32, (hr, hr), 0)
    c = lax.broadcasted_iota(jnp.int32, (hr, hr), 1)
    incl = ((r // rows) == (c // rows)) & (r >= c)
    dmat = jnp.where(incl, b_col - bi_row, -jnp.inf)
    inter = b_col + m_col
    mrow = jnp.maximum(inter, jnp.max(dmat, axis=1, keepdims=True))
    w_state = jnp.exp(inter - mrow)
    q16 = q_all.astype(BF16)
    v16 = v_all.astype(BF16)
    scores = _dot_nt(q16, k_all.astype(BF16)) * jnp.exp(dmat - mrow)
    yield
    cs = c_scr[...]
    n_row = n_scr[...]
    num = _dot(scores.astype(BF16), v16) + w_state * _dot(q16, cs.astype(BF16))
    den = jnp.sum(scores, axis=1, keepdims=True) + w_state * jnp.sum(q_all * n_row, axis=1, keepdims=True)
    hh = num / jnp.maximum(jnp.abs(den), jnp.exp(-mrow))
    yield
    gain = gain_ref[...]
    for h in range(B_HEADS):
        h_ref[:, h * B_V_DIM:(h + 1) * B_V_DIM] = (_rms_rows(hh[h * rows:(h + 1) * rows], gain)
                                                   * jax.nn.sigmoid(og_ref[:, h * B_V_DIM:(h + 1) * B_V_DIM]))

    g_col = b_end - b_col + i_col
    lane1 = lax.broadcasted_iota(jnp.int32, (1, LANE), 1)
    m_next = m_all
    m_new_rows, keep_rows, keep_lanes = [], [], []
    for h in range(B_HEADS):
        m_h = m_all[:, h:h + 1]
        be = bcum[rows - 1:rows, MISC_FB + h:MISC_FB + h + 1]
        m_new = jnp.maximum(be + m_h, jnp.max(g_col[h * rows:(h + 1) * rows], axis=0, keepdims=True))
        keep = jnp.exp(be + m_h - m_new)
        m_next = jnp.where(lane1 == h, m_new, m_next)
        m_new_rows.append(jnp.broadcast_to(m_new, (rows, 1)))
        keep_rows.append(jnp.broadcast_to(keep, (B_QK_DIM, 1)))
        keep_lanes.append(jnp.broadcast_to(keep, (1, B_QK_DIM)))
    kw = k_all * jnp.exp(g_col - jnp.concatenate(m_new_rows, axis=0))
    c_scr[...] = jnp.concatenate(keep_rows, axis=0) * cs + _dot_tn(kw.astype(BF16), v16)
    n_scr[...] = jnp.concatenate(keep_lanes, axis=1) * n_row + jnp.sum(kw, axis=0, keepdims=True)
    m_scr[...] = m_next
    yield


def mlstm_mixer(proj, gate_bias, gain, c0, n0, m0, *, bsz, t, chunk):
    nc = t // chunk
    nb = MLSTM_SEQS_PER_STEP
    gb = jnp.zeros((1, LANE), F32)
    gb = gb.at[0, MISC_IB:MISC_IB + B_HEADS].set(gate_bias[0]).at[0, MISC_FB:MISC_FB + B_HEADS].set(gate_bias[1])
    c0 = c0.reshape(bsz, B_HEADS * B_QK_DIM, B_V_DIM)
    n0 = n0.reshape(bsz, 1, B_HEADS * B_QK_DIM)
    m0 = jnp.pad(m0, ((0, 0), (0, LANE - B_HEADS))).reshape(bsz, 1, LANE)
    wq = B_HEADS * B_QK_DIM
    wv = B_HEADS * B_V_DIM
    h, c, n, m = _chunk_call(
        _mlstm_kernel, bsz=bsz, nb=nb, nc=nc, rows=chunk,
        ins=[(proj, 'rows', wq, AB_QB), (proj, 'rows', wq, AB_KB), (proj, 'rows', wv, AB_VB),
             (proj, 'rows', wv, AB_OB), (proj, 'rows', LANE, AB_MISC), (gb, 'const', 0, 0),
             (gain.reshape(1, B_V_DIM), 'const', 0, 0), (c0, 'batch', 0, 0), (n0, 'batch', 0, 0),
             (m0, 'batch', 0, 0)],
        outs=[(jax.ShapeDtypeStruct((bsz, t, wv), F32), 'rows', wv, 0),
              (jax.ShapeDtypeStruct(c0.shape, F32), 'batch', 0, 0),
              (jax.ShapeDtypeStruct(n0.shape, F32), 'batch', 0, 0),
              (jax.ShapeDtypeStruct(m0.shape, F32), 'batch', 0, 0)],
        scratch=[pltpu.VMEM((nb, wq, B_V_DIM), F32), pltpu.VMEM((nb, 1, wq), F32), pltpu.VMEM((nb, 1, LANE), F32)],
        name="mlstm_mixer")
    return (h.reshape(bsz * t, wv), c.reshape(bsz, B_HEADS, B_QK_DIM, B_V_DIM), n.reshape(bsz, B_HEADS, B_QK_DIM),
            m.reshape(bsz, LANE)[:, :B_HEADS])


CD_QKV, CD_ZC, CD_QD, CD_FD, CD_VD, CD_GD, CD_MISC, CD_TOTAL = 0, 1536, 2048, 2560, 3072, 3584, 4096, 4224
MISC_BC, MISC_AC = 0, 4
TAIL = 8


def _permute_w_in_cd(w):
    qkv, bc, ac, zc, qd, fd, vd, gd = _split_cols(w, ODD_SPLITS)
    pad = jnp.zeros((w.shape[0], LANE - 2 * C_HEADS), w.dtype)
    return jnp.concatenate([qkv, zc, qd, fd, vd, gd, bc, ac, pad], axis=1)


def _gdn_init(s0_ref, tail0_ref, st_scr, tail_scr):
    for h in range(C_HEADS):
        st_scr[:, h * C_DIM:(h + 1) * C_DIM] = s0_ref[h].T
    tail_scr[...] = tail0_ref[...]


def _gdn_final(s_out_ref, st_scr):
    for h in range(C_HEADS):
        s_out_ref[h] = st_scr[:, h * C_DIM:(h + 1) * C_DIM].T


def _gdn_chunk(qkv_ref, z_ref, misc_ref, cw_ref, alog_ref, dt_ref, gain_ref, o_ref, st_scr, tail_scr):
    rows = qkv_ref.shape[0]
    width = qkv_ref.shape[1]
    x = qkv_ref[...]
    tail = tail_scr[...]
    row8 = lax.broadcasted_iota(jnp.int32, (TAIL, width), 0)
    acc = x * cw_ref[CONV_W - 1:CONV_W, :]
    for back in range(1, CONV_W):
        rolled = pltpu.roll(x, back, 0)
        first = jnp.where(row8 < back, pltpu.roll(tail, back, 0), rolled[0:TAIL])
        shifted = first if rows == TAIL else jnp.concatenate([first, rolled[TAIL:]], axis=0)
        acc = acc + shifted * cw_ref[CONV_W - 1 - back:CONV_W - back, :]
    tail_scr[...] = x[rows - TAIL:rows]
    conv = acc * jax.nn.sigmoid(acc)

    tril16 = jnp.where(_tri_mask(rows), 1.0, 0.0).astype(BF16)
    misc = misc_ref[...]
    beta_t = jax.nn.sigmoid(misc)
    g_t = -jnp.exp(alog_ref[...]) * jax.nn.softplus(misc + dt_ref[...])
    gcum = _cumsum_rows(g_t, tril16)
    yield

    hd = C_HEADS * C_DIM
    hr = C_HEADS * rows

    def stack(f):
        return jnp.concatenate([f(h) for h in range(C_HEADS)], axis=0)

    def l2n(v):
        return v * lax.rsqrt(jnp.sum(v * v, axis=-1, keepdims=True) + EPS)

    q_all = stack(lambda h: l2n(conv[:, h * C_DIM:(h + 1) * C_DIM])) * C_DIM ** -0.5
    k_all = stack(lambda h: l2n(conv[:, hd + h * C_DIM:hd + (h + 1) * C_DIM]))
    v_all = stack(lambda h: conv[:, 2 * hd + h * C_DIM:2 * hd + (h + 1) * C_DIM])
    beta = stack(lambda h: beta_t[:, MISC_BC + h:MISC_BC + h + 1])
    gc = stack(lambda h: gcum[:, MISC_AC + h:MISC_AC + h + 1])
    g_end = stack(lambda h: jnp.broadcast_to(gcum[rows - 1:rows, MISC_AC + h:MISC_AC + h + 1], (rows, 1)))
    gc_row = jnp.broadcast_to(gc, (hr, LANE)).T[0:1, :]

    r = lax.broadcasted_iota(jnp.int32, (hr, hr), 0)
    c = lax.broadcasted_iota(jnp.int32, (hr, hr), 1)
    same = (r // rows) == (c // rows)
    incl = same & (r >= c)
    strict = same & (r > c)
    decay = jnp.exp(jnp.where(incl, gc - gc_row, -jnp.inf))
    k16 = k_all.astype(BF16)
    a_mat = jnp.where(strict, beta * _dot_nt(k16, k16) * decay, 0.0)
    yield
    power = -a_mat
    inv = jnp.where(r == c, 1.0, 0.0) + power
    for _ in range(int(math.log2(rows)) - 1):
        power = _dot_f32(power, power)
        yield
        inv = inv + _dot_f32(inv, power)
        yield
    inv_hi, inv_lo = _split2(inv)
    rhs = jnp.concatenate([beta * v_all, beta * jnp.exp(gc) * k_all], axis=1).astype(BF16)
    w = _dot(inv_hi, rhs) + _dot(inv_lo, rhs)
    yield
    w_v, w_k = w[:, 0:C_DIM], w[:, C_DIM:2 * C_DIM]
    qk = _dot_nt(q_all.astype(BF16), k16) * decay
    yield

    head_of_row = lax.broadcasted_iota(jnp.int32, (hr, C_DIM), 0) // rows

    def per_head_lanes(m):
        return jnp.concatenate([jnp.where(head_of_row == h, m, 0.0) for h in range(C_HEADS)], axis=1).astype(BF16)

    st = st_scr[...]
    st16 = st.astype(BF16)
    delta = w_v - _dot_nt(per_head_lanes(w_k), st16)
    yield
    d16 = delta.astype(BF16)
    out = _dot_nt(per_head_lanes(q_all * jnp.exp(gc)), st16) + _dot(qk.astype(BF16), d16)
    keep = jnp.concatenate([jnp.broadcast_to(jnp.exp(gcum[rows - 1:rows, MISC_AC + h:MISC_AC + h + 1]), (1, C_DIM))
                            for h in range(C_HEADS)], axis=1)
    st_scr[...] = keep * st + _dot_tn(d16, per_head_lanes(k_all * jnp.exp(g_end - gc)))
    yield
    gain = gain_ref[...]
    for h in range(C_HEADS):
        z = z_ref[:, h * C_DIM:(h + 1) * C_DIM]
        o_ref[:, h * C_DIM:(h + 1) * C_DIM] = _rms_rows(out[h * rows:(h + 1) * rows], gain) * (z * jax.nn.sigmoid(z))


def _hgrn2_init(s0_ref, st_scr):
    for h in range(D_HEADS):
        st_scr[h] = s0_ref[h].T


def _hgrn2_final(s_out_ref, st_scr):
    for h in range(D_HEADS):
        s_out_ref[h] = st_scr[h].T


def _hgrn2_chunk(q_ref, f_ref, v_ref, g_ref, lb_ref, gain_ref, o_ref, st_scr):
    rows = q_ref.shape[0]
    tril16 = jnp.where(_tri_mask(rows), 1.0, 0.0).astype(BF16)
    lb = lb_ref[...]
    zf = f_ref[...]
    logf = jnp.logaddexp(jnp.log(lb), jnp.log1p(-lb) + jax.nn.log_sigmoid(zf))
    kd = (1.0 - lb) * jax.nn.sigmoid(-zf)
    qx = q_ref[...]
    qd = qx * jax.nn.sigmoid(qx)
    bcum = _cumsum_rows(logf, tril16)
    yield
    gain = gain_ref[...]
    row_sub = lax.broadcasted_iota(jnp.int32, (SUB, 1), 0)
    for h in range(D_HEADS):
        sl = slice(h * D_EXPAND, (h + 1) * D_EXPAND)
        q, k, b = qd[:, sl], kd[:, sl], bcum[:, sl]
        v = v_ref[:, h * D_V_DIM:(h + 1) * D_V_DIM]
        v16 = v.astype(BF16)
        st = st_scr[h]
        inter = _dot_nt((q * jnp.exp(b)).astype(BF16), st.astype(BF16))
        blocks = []
        for i in range(rows // SUB):
            r0 = i * SUB
            qi, bi = q[r0:r0 + SUB], b[r0:r0 + SUB]
            oi = inter[r0:r0 + SUB]
            if i > 0:
                ref = b[r0 - 1:r0]
                att = _dot_nt((qi * jnp.exp(bi - ref)).astype(BF16),
                              (k[0:r0] * jnp.exp(ref - b[0:r0])).astype(BF16))
                oi = oi + _dot(att.astype(BF16), v16[0:r0])
            for s in range(SUB):
                r = r0 + s
                a = jnp.sum(qi * jnp.exp(bi - b[r:r + 1]) * k[r:r + 1], axis=1, keepdims=True)
                oi = oi + jnp.where(row_sub >= s, a, 0.0) * v[r:r + 1]
            blocks.append(oi)
        out = blocks[0] if len(blocks) == 1 else jnp.concatenate(blocks, axis=0)
        b_end = b[rows - 1:rows]
        st_scr[h] = jnp.exp(b_end) * st + _dot_tn(v16, (k * jnp.exp(b_end - b)).astype(BF16))
        g = g_ref[:, h * D_V_DIM:(h + 1) * D_V_DIM]
        o_ref[:, h * D_V_DIM:(h + 1) * D_V_DIM] = _rms_rows(out, gain) * (g * jax.nn.sigmoid(g))
        yield


def _cd_kernel(qkv_ref, z_ref, misc_ref, cw_ref, alog_ref, dt_ref, cgain_ref, sc0_ref, tail0_ref,
               qd_ref, fd_ref, vd_ref, gd_ref, lb_ref, dgain_ref, sd0_ref,
               oc_ref, sc_out_ref, od_ref, sd_out_ref, stc_scr, tail_scr, std_scr):
    ci = pl.program_id(1)
    seqs = range(qkv_ref.shape[0])

    @pl.when(ci == 0)
    def _():
        for i in seqs:
            _gdn_init(sc0_ref.at[i], tail0_ref.at[i], stc_scr.at[i], tail_scr.at[i])
            _hgrn2_init(sd0_ref.at[i], std_scr.at[i])

    gdn = [_gdn_chunk(qkv_ref.at[i], z_ref.at[i], misc_ref.at[i], cw_ref, alog_ref, dt_ref, cgain_ref,
                      oc_ref.at[i], stc_scr.at[i], tail_scr.at[i]) for i in seqs]
    hgrn2 = [_hgrn2_chunk(qd_ref.at[i], fd_ref.at[i], vd_ref.at[i], gd_ref.at[i], lb_ref, dgain_ref, od_ref.at[i],
                          std_scr.at[i]) for i in seqs]
    _lockstep(gdn + hgrn2)

    @pl.when(ci == pl.num_programs(1) - 1)
    def _():
        for i in seqs:
            _gdn_final(sc_out_ref.at[i], stc_scr.at[i])
            _hgrn2_final(sd_out_ref.at[i], std_scr.at[i])


def cd_mixers(proj, conv_w, a_log, dt_bias, c_gain, sc0, conv_prev, lower_bound, d_gain, sd0, *, bsz, t, chunk):
    nc = t // chunk
    nb = CD_SEQS_PER_STEP
    hd = C_HEADS * C_DIM
    wk = D_HEADS * D_EXPAND
    wv = D_HEADS * D_V_DIM
    lanes = jnp.zeros((1, LANE), F32)
    alog = lanes.at[0, MISC_AC:MISC_AC + C_HEADS].set(a_log)
    dt = lanes.at[0, MISC_AC:MISC_AC + C_HEADS].set(dt_bias)
    tail0 = jnp.pad(conv_prev, ((0, 0), (TAIL - (CONV_W - 1), 0), (0, 0)))
    oc, sc, od, sd = _chunk_call(
        _cd_kernel, bsz=bsz, nb=nb, nc=nc, rows=chunk,
        ins=[(proj, 'rows', 3 * hd, CD_QKV), (proj, 'rows', hd, CD_ZC), (proj, 'rows', LANE, CD_MISC),
             (conv_w, 'const', 0, 0), (alog, 'const', 0, 0), (dt, 'const', 0, 0),
             (c_gain.reshape(1, C_DIM), 'const', 0, 0), (sc0, 'batch', 0, 0), (tail0, 'batch', 0, 0),
             (proj, 'rows', wk, CD_QD), (proj, 'rows', wk, CD_FD), (proj, 'rows', wv, CD_VD),
             (proj, 'rows', wv, CD_GD), (lower_bound.reshape(1, wk), 'const', 0, 0),
             (d_gain.reshape(1, D_V_DIM), 'const', 0, 0), (sd0, 'batch', 0, 0)],
        outs=[(jax.ShapeDtypeStruct((bsz, t, hd), F32), 'rows', hd, 0),
              (jax.ShapeDtypeStruct(sc0.shape, F32), 'batch', 0, 0),
              (jax.ShapeDtypeStruct((bsz, t, wv), F32), 'rows', wv, 0),
              (jax.ShapeDtypeStruct(sd0.shape, F32), 'batch', 0, 0)],
        scratch=[pltpu.VMEM((nb, C_DIM, hd), F32), pltpu.VMEM((nb, TAIL, 3 * hd), F32),
                 pltpu.VMEM((nb, D_HEADS, D_V_DIM, D_EXPAND), F32)],
        name="cd_mixers")
    return oc.reshape(bsz * t, hd), sc, od.reshape(bsz * t, wv), sd


def _pad_cols(w, mult=LANE):
    pad = (-w.shape[-1]) % mult
    return jnp.pad(w, [(0, 0)] * (w.ndim - 1) + [(0, pad)])


def _mixer_ab(proj, bsz, t, pos, prm, cache):
    n = bsz * t
    qpad, qipad, k16, v16, ki16, k32, v32, ki32 = dsa_prep(proj, pos, prm['a_q_gain'][0], prm['a_k_gain'][0], t)
    if cache is None:
        limit = jnp.tile((pos // CHUNK + 1) * CHUNK, bsz).reshape(n, 1)
        a_out = dsa_attention(qpad, qipad, proj, limit, k16, v16, ki16, bsz=bsz, tq=t, tk=t,
                              causal=True, n_sel=min(TOPK_MAX, t // 4))
        c0 = jnp.zeros((bsz, B_HEADS, B_QK_DIM, B_V_DIM), F32)
        n0 = jnp.zeros((bsz, B_HEADS, B_QK_DIM), F32)
        m0 = jnp.zeros((bsz, B_HEADS), F32)
        chunk = CHUNK
    else:
        k_c, v_c, ki_c, c0, n0, m0 = cache
        past = k_c.shape[1]
        n_keys = past + t
        tk = -(-n_keys // KEY_CHUNK) * KEY_CHUNK

        def with_cache(c, new):
            c = c.reshape(bsz, past, -1).astype(BF16)
            c = jnp.pad(c, ((0, 0), (0, 0), (0, LANE - c.shape[-1])))
            return jnp.concatenate([c, new.reshape(bsz, t, LANE),
                                    jnp.zeros((bsz, tk - n_keys, LANE), BF16)], axis=1).reshape(bsz * tk, LANE)

        limit = jnp.full((n, 1), n_keys, jnp.int32)
        a_out = dsa_attention(qpad, qipad, proj, limit, with_cache(k_c, k16), with_cache(v_c, v16),
                              with_cache(ki_c, ki16), bsz=bsz, tq=t, tk=tk, causal=False,
                              n_sel=min(TOPK_MAX, n_keys // 4))
        chunk = t
    h, c, n_, m = mlstm_mixer(proj.reshape(bsz, t, -1), prm['b_gate_bias'][0], prm['b_norm_gain'][0], c0, n0, m0,
                              bsz=bsz, t=t, chunk=chunk)
    st = (k32.reshape(bsz, t, A_KV_HEADS, HEAD_DIM), v32.reshape(bsz, t, A_KV_HEADS, HEAD_DIM),
          ki32.reshape(bsz, t, IDX_DIM), c, n_, m)
    return a_out, h, st


def _mixer_cd(proj, bsz, t, prm, lower_bound, cache):
    hd = C_HEADS * C_DIM
    if cache is None:
        sc0 = jnp.zeros((bsz, C_HEADS, C_DIM, C_DIM), F32)
        conv_prev = jnp.zeros((bsz, CONV_W - 1, 3 * hd), F32)
        sd0 = jnp.zeros((bsz, D_HEADS, D_EXPAND, D_V_DIM), F32)
        chunk = CHUNK
    else:
        sc0, conv_prev, sd0 = cache
        chunk = t
    oc, sc, od, sd = cd_mixers(proj.reshape(bsz, t, -1), prm['c_conv_w'][0], prm['c_a_log'][0], prm['c_dt_bias'][0],
                               prm['c_norm_gain'][0], sc0, conv_prev, lower_bound, prm['d_norm_gain'][0], sd0,
                               bsz=bsz, t=t, chunk=chunk)
    qkv = proj.reshape(bsz, t, -1)[:, :, CD_QKV:CD_QKV + 3 * hd]
    conv_new = jnp.concatenate([conv_prev, qkv[:, t - (CONV_W - 1):]], axis=1)[:, -(CONV_W - 1):]
    return oc, od, (sc, conv_new, sd)


def _trunk(x, pos_offset, cache, prm, wts):
    bsz, t, d = x.shape
    n = bsz * t
    pos = pos_offset + jnp.arange(t, dtype=jnp.int32)
    probs = jax.nn.softmax(prm['d_lb_logits'], axis=0)
    lower_bounds = jnp.cumsum(probs, axis=0) - probs[0]
    xf = x.reshape(n, d)

    lc = None if cache is None else tuple(c[0] for c in cache[:6])
    proj = norm_matmul(xf, prm['norm_mix'][0], wts['w_in_ab'])
    a_out, b_out, st_even = _mixer_ab(proj, bsz, t, pos, prm, lc)
    xf = matmul_residual(a_out, b_out, wts['w_out_ab'], xf)
    xf = ffn_residual(xf, prm['norm_ffn'][0], wts['ffn_w1'], wts['ffn_w3'], wts['ffn_w2'])

    lc = None if cache is None else tuple(c[0] for c in cache[6:])
    proj = norm_matmul(xf, prm['norm_mix'][1], wts['w_in_cd'])
    c_out, d_out, st_odd = _mixer_cd(proj, bsz, t, prm, lower_bounds[1], lc)
    xf = matmul_residual(c_out, d_out, wts['w_out_cd'], xf)
    xf = moe_residual(xf, prm['norm_ffn'][1], wts['moe_router'], wts['moe_w1'], wts['moe_w3'], wts['moe_w2'])

    new_state = tuple(s[None] for s in st_even + st_odd)
    return xf.reshape(bsz, t, d), new_state


def kernel(x_prompt, x_sample, cache_a_k, cache_a_v, cache_a_kidx, state_b_c, state_b_n, state_b_m,
           state_c_s, state_c_conv, state_d_s, norm_mix, norm_ffn, w_in_ab, w_out_ab, a_q_gain, a_k_gain,
           b_gate_bias, b_norm_gain, w_in_cd, w_out_cd, c_conv_w, c_a_log, c_dt_bias, c_norm_gain,
           d_lb_logits, d_norm_gain, ffn_w1, ffn_w3, ffn_w2, moe_router, moe_w1, moe_w3, moe_w2):
    prm = dict(norm_mix=norm_mix, norm_ffn=norm_ffn, a_q_gain=a_q_gain, a_k_gain=a_k_gain,
               b_gate_bias=b_gate_bias, b_norm_gain=b_norm_gain, c_conv_w=c_conv_w, c_a_log=c_a_log,
               c_dt_bias=c_dt_bias, c_norm_gain=c_norm_gain, d_lb_logits=d_lb_logits, d_norm_gain=d_norm_gain)
    wts = dict(w_in_ab=_permute_w_in_ab(w_in_ab[0]).astype(BF16), w_out_ab=w_out_ab[0].astype(BF16),
               w_in_cd=_permute_w_in_cd(w_in_cd[0]).astype(BF16), w_out_cd=w_out_cd[0].astype(BF16),
               ffn_w1=ffn_w1[0].astype(BF16), ffn_w3=ffn_w3[0].astype(BF16), ffn_w2=ffn_w2[0].astype(BF16),
               moe_router=_pad_cols(moe_router[0]),
               moe_w1=moe_w1[0].astype(BF16), moe_w3=moe_w3[0].astype(BF16), moe_w2=moe_w2[0].astype(BF16))
    cache = (cache_a_k, cache_a_v, cache_a_kidx, state_b_c, state_b_n, state_b_m, state_c_s, state_c_conv, state_d_s)
    y_prompt, st_p = _trunk(x_prompt, 0, None, prm, wts)
    y_sample, st_s = _trunk(x_sample, cache_a_k.shape[2], cache, prm, wts)
    return (y_prompt, y_sample) + st_p + st_s
```

```python
import functools
import math

import jax
import jax.numpy as jnp
import numpy as np
from jax import lax
from jax.experimental import pallas as pl
from jax.experimental.pallas import tpu as pltpu

F32 = jnp.float32
BF16 = jnp.bfloat16

EPS = 1e-6
ROPE_THETA = 500000.0
ROT_FRACTION = 4
CHUNK = 64
A_HEADS, A_KV_HEADS, HEAD_DIM = 8, 2, 64
IDX_HEADS, IDX_DIM = 4, 64
TOPK_MAX, Q_BLOCK = 256, 128
B_HEADS, B_QK_DIM, B_V_DIM = 4, 64, 128
C_HEADS, C_DIM, CONV_W = 4, 128, 4
D_HEADS, D_EXPAND, D_V_DIM = 4, 128, 128
N_EXPERTS, TOP_K_EXPERTS = 8, 2

LANE = 128
VMEM_LIMIT = 48 * 1024 * 1024

EVEN_SPLITS = (A_HEADS * HEAD_DIM, A_KV_HEADS * HEAD_DIM, A_KV_HEADS * HEAD_DIM,
               IDX_HEADS * IDX_DIM, IDX_DIM, IDX_HEADS,
               B_HEADS * B_QK_DIM, B_HEADS * B_QK_DIM, B_HEADS * B_V_DIM,
               B_HEADS, B_HEADS, B_HEADS * B_V_DIM)
ODD_SPLITS = (3 * C_HEADS * C_DIM, C_HEADS, C_HEADS, C_HEADS * C_DIM,
              D_HEADS * D_EXPAND, D_HEADS * D_EXPAND, D_HEADS * D_V_DIM, D_HEADS * D_V_DIM)


def _split_cols(p, widths):
    cuts = [int(c) for c in np.cumsum(widths)[:-1]]
    return jnp.split(p, cuts, axis=-1)


def _row_tile(n, target):
    t = min(n, target)
    while n % t:
        t //= 2
    return t


def _col_tile(n, target):
    best = LANE
    for k in range(1, n // LANE + 1):
        c = k * LANE
        if n % c == 0 and c <= target:
            best = c
    return best


def _rms_rows(x, gain):
    return x * lax.rsqrt(jnp.mean(x * x, axis=-1, keepdims=True) + EPS) * gain


def _norm_matmul_kernel(x_ref, g_ref, w_ref, o_ref, xn_ref):
    @pl.when(pl.program_id(1) == 0)
    def _():
        xn_ref[...] = _rms_rows(x_ref[...], g_ref[...]).astype(BF16)

    o_ref[...] = jnp.dot(xn_ref[...], w_ref[...], preferred_element_type=F32)


def norm_matmul(x, gain, w):
    n, d = x.shape
    m = w.shape[1]
    tm = _row_tile(n, 1024)
    tn = _col_tile(m, 1536)
    return pl.pallas_call(
        _norm_matmul_kernel,
        grid=(n // tm, m // tn),
        in_specs=[pl.BlockSpec((tm, d), lambda i, j: (i, 0)),
                  pl.BlockSpec((1, d), lambda i, j: (0, 0)),
                  pl.BlockSpec((d, tn), lambda i, j: (0, j))],
        out_specs=pl.BlockSpec((tm, tn), lambda i, j: (i, j)),
        out_shape=jax.ShapeDtypeStruct((n, m), F32),
        scratch_shapes=[pltpu.VMEM((tm, d), BF16)],
        compiler_params=pltpu.CompilerParams(
            dimension_semantics=("parallel", "arbitrary"), vmem_limit_bytes=VMEM_LIMIT),
        name="norm_matmul",
    )(x, gain.reshape(1, d), w)


def _matmul_res_kernel(a_ref, b_ref, w_ref, r_ref, o_ref):
    ka = a_ref.shape[1]
    o_ref[...] = (r_ref[...] + jnp.dot(a_ref[...].astype(BF16), w_ref[0:ka, :], preferred_element_type=F32)
                  + jnp.dot(b_ref[...].astype(BF16), w_ref[ka:, :], preferred_element_type=F32))


def matmul_residual(a, b, w, res):
    n, ka = a.shape
    kb = b.shape[1]
    m = w.shape[1]
    tm = _row_tile(n, 1024)
    return pl.pallas_call(
        _matmul_res_kernel,
        grid=(n // tm,),
        in_specs=[pl.BlockSpec((tm, ka), lambda i: (i, 0)),
                  pl.BlockSpec((tm, kb), lambda i: (i, 0)),
                  pl.BlockSpec((ka + kb, m), lambda i: (0, 0)),
                  pl.BlockSpec((tm, m), lambda i: (i, 0))],
        out_specs=pl.BlockSpec((tm, m), lambda i: (i, 0)),
        out_shape=jax.ShapeDtypeStruct((n, m), F32),
        compiler_params=pltpu.CompilerParams(
            dimension_semantics=("parallel",), vmem_limit_bytes=VMEM_LIMIT),
        name="matmul_residual",
    )(a, b, w, res)


def _swiglu_tile(xn, w1, w3):
    h1 = jnp.dot(xn, w1, preferred_element_type=F32)
    h3 = jnp.dot(xn, w3, preferred_element_type=F32)
    return h1 * jax.nn.sigmoid(h1) * h3


def _ffn_kernel(x_ref, g_ref, w1_ref, w3_ref, w2_ref, o_ref, xn_ref):
    @pl.when(pl.program_id(1) == 0)
    def _():
        x = x_ref[...]
        xn_ref[...] = _rms_rows(x, g_ref[...]).astype(BF16)
        o_ref[...] = x

    act = _swiglu_tile(xn_ref[...], w1_ref[...], w3_ref[...])
    o_ref[...] += jnp.dot(act.astype(BF16), w2_ref[...], preferred_element_type=F32)


def ffn_residual(x, gain, w1, w3, w2):
    n, d = x.shape
    f = w1.shape[1]
    tm = _row_tile(n, 1024)
    tf = _col_tile(f, 512)
    return pl.pallas_call(
        _ffn_kernel,
        grid=(n // tm, f // tf),
        in_specs=[pl.BlockSpec((tm, d), lambda i, j: (i, 0)),
                  pl.BlockSpec((1, d), lambda i, j: (0, 0)),
                  pl.BlockSpec((d, tf), lambda i, j: (0, j)),
                  pl.BlockSpec((d, tf), lambda i, j: (0, j)),
                  pl.BlockSpec((tf, d), lambda i, j: (j, 0))],
        out_specs=pl.BlockSpec((tm, d), lambda i, j: (i, 0)),
        out_shape=jax.ShapeDtypeStruct((n, d), F32),
        scratch_shapes=[pltpu.VMEM((tm, d), BF16)],
        compiler_params=pltpu.CompilerParams(
            dimension_semantics=("parallel", "arbitrary"), vmem_limit_bytes=VMEM_LIMIT),
        name="ffn_residual",
    )(x, gain.reshape(1, d), w1, w3, w2)


MOE_VMEM_LIMIT = 58 * 1024 * 1024
MOE_CAPS = (256, 288, 320, 384, 512)


def _moe_route_kernel(x_ref, g_ref, r_ref, xn_ref, comb_ref, post_ref, cnt_ref):
    x = x_ref[...]
    tm = x.shape[0]
    xn = _rms_rows(x, g_ref[...])
    xn_ref[...] = xn.astype(BF16)
    logits = jnp.dot(xn, r_ref[...], preferred_element_type=F32, precision=lax.Precision.HIGHEST)
    lane = lax.broadcasted_iota(jnp.int32, logits.shape, 1)
    logits = jnp.where(lane < N_EXPERTS, logits, -jnp.inf)
    m1 = jnp.max(logits, axis=-1, keepdims=True)
    i1 = jnp.min(jnp.where(logits == m1, lane, LANE), axis=-1, keepdims=True)
    rest = jnp.where(lane == i1, -jnp.inf, logits)
    m2 = jnp.max(rest, axis=-1, keepdims=True)
    i2 = jnp.min(jnp.where(rest == m2, lane, LANE), axis=-1, keepdims=True)
    e2 = jnp.exp(m2 - m1)
    den = 1.0 + e2
    comb_ref[...] = jnp.where(lane == i1, 1.0 / den, 0.0) + jnp.where(lane == i2, e2 / den, 0.0)
    chosen = (lane == i1) | (lane == i2)
    sel = jnp.where(chosen, 1.0, 0.0)
    tril16 = jnp.where(_tri_mask(LANE), 1.0, 0.0).astype(BF16)
    seen = jnp.zeros((1, LANE), F32)
    ranks = []
    for blk in range(tm // LANE):
        sb = sel[blk * LANE:(blk + 1) * LANE]
        ranks.append(_dot(tril16, sb.astype(BF16)) + seen - 1.0)
        seen = seen + jnp.sum(sb, axis=0, keepdims=True)
    rank = jnp.where(chosen, jnp.concatenate(ranks, axis=0), -1.0)
    post_ref[0] = rank.T[0:N_EXPERTS, :]
    cnt_ref[0] = seen


def _moe_expert_kernel(cnt_ref, xn_ref, comb_ref, post_ref, x_ref, w1_ref, w3_ref, w2_ref, o_ref,
                       xe_scr, y_scr, *, caps):
    i = pl.program_id(0)
    e = pl.program_id(1)
    j = pl.program_id(2)
    last = pl.num_programs(2) - 1
    tm = x_ref.shape[0]

    @pl.when(jnp.logical_and(e == 0, j == 0))
    def _():
        o_ref[...] = x_ref[...]

    cnt = cnt_ref[i * N_EXPERTS + e]

    def expert_step(cap):
        def pick():
            rank_row = post_ref[0, pl.ds(e, 1), :].astype(jnp.int32)
            slot = lax.broadcasted_iota(jnp.int32, (cap, tm), 0)
            return jnp.where(rank_row == slot, 1.0, 0.0).astype(BF16)

        @pl.when(j == 0)
        def _():
            xe_scr[0:cap, :] = _dot(pick(), xn_ref[...]).astype(BF16)

        act = _swiglu_tile(xe_scr[0:cap, :], w1_ref[0], w3_ref[0])
        yj = _dot(act.astype(BF16), w2_ref[0])

        @pl.when(j == 0)
        def _():
            y_scr[0:cap, :] = yj

        @pl.when(j > 0)
        def _():
            y_scr[0:cap, :] += yj

        @pl.when(j == last)
        def _():
            comb = comb_ref[...]
            lane = lax.broadcasted_iota(jnp.int32, comb.shape, 1)
            gate = jnp.sum(jnp.where(lane == e, comb, 0.0), axis=-1, keepdims=True)
            o_ref[...] += gate * _dot_tn(pick(), y_scr[0:cap, :].astype(BF16))

    lo = 0
    for cap in caps:
        @pl.when(jnp.logical_and(cnt > lo, cnt <= cap))
        def _(cap=cap):
            expert_step(cap)
        lo = cap


def moe_residual(x, gain, router, w1, w3, w2):
    n, d = x.shape
    ne, _, f = w1.shape
    tm = _row_tile(n, 1024)
    tf = _col_tile(f, 1792)
    nt = n // tm
    xn, comb, post, cnt = pl.pallas_call(
        _moe_route_kernel,
        grid=(nt,),
        in_specs=[pl.BlockSpec((tm, d), lambda i: (i, 0)),
                  pl.BlockSpec((1, d), lambda i: (0, 0)),
                  pl.BlockSpec((d, LANE), lambda i: (0, 0))],
        out_specs=[pl.BlockSpec((tm, d), lambda i: (i, 0)),
                   pl.BlockSpec((tm, LANE), lambda i: (i, 0)),
                   pl.BlockSpec((1, ne, tm), lambda i: (i, 0, 0)),
                   pl.BlockSpec((1, 1, LANE), lambda i: (i, 0, 0))],
        out_shape=[jax.ShapeDtypeStruct((n, d), BF16), jax.ShapeDtypeStruct((n, LANE), F32),
                   jax.ShapeDtypeStruct((nt, ne, tm), F32), jax.ShapeDtypeStruct((nt, 1, LANE), F32)],
        compiler_params=pltpu.CompilerParams(dimension_semantics=("parallel",), vmem_limit_bytes=VMEM_LIMIT),
        name="moe_route",
    )(x, gain.reshape(1, d), router)
    counts = cnt[:, 0, :ne].astype(jnp.int32).reshape(nt * ne)
    caps = tuple(c for c in MOE_CAPS if c < tm) + (tm,)
    grid_spec = pltpu.PrefetchScalarGridSpec(
        num_scalar_prefetch=1,
        grid=(nt, ne, f // tf),
        in_specs=[pl.BlockSpec((tm, d), lambda i, e, j, c: (i, 0)),
                  pl.BlockSpec((tm, LANE), lambda i, e, j, c: (i, 0)),
                  pl.BlockSpec((1, ne, tm), lambda i, e, j, c: (i, 0, 0)),
                  pl.BlockSpec((tm, d), lambda i, e, j, c: (i, 0)),
                  pl.BlockSpec((1, d, tf), lambda i, e, j, c: (e, 0, j)),
                  pl.BlockSpec((1, d, tf), lambda i, e, j, c: (e, 0, j)),
                  pl.BlockSpec((1, tf, d), lambda i, e, j, c: (e, j, 0))],
        out_specs=pl.BlockSpec((tm, d), lambda i, e, j, c: (i, 0), pipeline_mode=pl.Buffered(1)),
        scratch_shapes=[pltpu.VMEM((tm, d), BF16), pltpu.VMEM((tm, d), F32)])
    return pl.pallas_call(
        functools.partial(_moe_expert_kernel, caps=caps),
        grid_spec=grid_spec,
        out_shape=jax.ShapeDtypeStruct((n, d), F32),
        compiler_params=pltpu.CompilerParams(
            dimension_semantics=("parallel", "arbitrary", "arbitrary"), vmem_limit_bytes=MOE_VMEM_LIMIT),
        name="moe_experts",
    )(counts, xn, comb, post, x, w1, w3, w2)


AB_QA, AB_VB, AB_OB, AB_QI, AB_QB, AB_KB, AB_KA, AB_VA, AB_MISC, AB_TOTAL = (
    0, 512, 1024, 1536, 1792, 2048, 2304, 2432, 2560, 2688)
MISC_WI, MISC_IB, MISC_FB = 64, 68, 72
HALF = LANE // 2
KEY_CHUNK = 512
MASKED = -1e30
KEY_OF_NEG_INF = -2139095041
I16_MIN, I16_MAX = -32768, 32767


def _permute_w_in_ab(w):
    qa, ka, va, qi, ki, wi, qb, kb, vb, ib, fb, ob = _split_cols(w, EVEN_SPLITS)
    pad = jnp.zeros((w.shape[0], LANE - IDX_DIM - 3 * IDX_HEADS), w.dtype)
    return jnp.concatenate([qa, vb, ob, qi, qb, kb, ka, va, ki, wi, ib, fb, pad], axis=1)


def _rope_tables(pos):
    rot = HEAD_DIM // ROT_FRACTION
    half = rot // 2
    inv_freq = ROPE_THETA ** (-jnp.arange(half, dtype=F32) * 2.0 / rot)
    ang = pos.astype(F32)[:, None] * inv_freq[None, :]
    cos, sin = jnp.cos(ang), jnp.sin(ang)
    t = pos.shape[0]
    one = jnp.ones((t, HEAD_DIM - rot), F32)
    zero_r = jnp.zeros((t, HEAD_DIM - rot), F32)
    zero_h = jnp.zeros((t, half), F32)
    c = jnp.concatenate([cos, cos, one], axis=1)
    s_up = jnp.concatenate([-sin, zero_h, zero_r], axis=1)
    s_dn = jnp.concatenate([zero_h, sin, zero_r], axis=1)
    return tuple(jnp.concatenate([a, a], axis=1) for a in (c, s_up, s_dn))


def _rope_tile(x, c, s_up, s_dn):
    half = HEAD_DIM // ROT_FRACTION // 2
    return x * c + pltpu.roll(x, LANE - half, 1) * s_up + pltpu.roll(x, half, 1) * s_dn


def _head_norm_tile(x, gain, same_head):
    sq = x * x
    hi = sq.astype(BF16)
    lo = (sq - hi.astype(F32)).astype(BF16)
    ss = (jnp.dot(hi, same_head, preferred_element_type=F32)
          + jnp.dot(lo, same_head, preferred_element_type=F32))
    return x * lax.rsqrt(ss * (1.0 / HEAD_DIM) + EPS) * gain


def _aprep_kernel(qa_ref, ka_ref, va_ref, qi_ref, misc_ref, c_ref, su_ref, sd_ref, qg_ref, kg_ref,
                  qpad_ref, qipad_ref, k16_ref, v16_ref, ki16_ref, k32_ref, v32_ref, ki32_ref):
    c, su, sd = c_ref[...], su_ref[...], sd_ref[...]
    tm = c.shape[0]
    row = lax.broadcasted_iota(jnp.int32, (LANE, LANE), 0)
    col = lax.broadcasted_iota(jnp.int32, (LANE, LANE), 1)
    same_head = jnp.where(row // HALF == col // HALF, 1.0, 0.0).astype(BF16)
    lane = lax.broadcasted_iota(jnp.int32, (tm, LANE), 1)
    low = lane < HALF

    heads_per_group = A_HEADS // A_KV_HEADS
    for p in range(A_HEADS // 2):
        y = _rope_tile(_head_norm_tile(qa_ref[:, p * LANE:(p + 1) * LANE], qg_ref[...], same_head), c, su, sd)
        y = y * HEAD_DIM ** -0.5
        y_sw = pltpu.roll(y, HALF, 1)
        for o in range(2):
            h = 2 * p + o
            g = h // heads_per_group
            src = y if o == g else y_sw
            qpad_ref[:, h * LANE:(h + 1) * LANE] = jnp.where(low if g == 0 else ~low, src, 0.0).astype(BF16)
    k = _rope_tile(_head_norm_tile(ka_ref[...], kg_ref[...], same_head), c, su, sd)
    k32_ref[...] = k
    k16_ref[...] = k.astype(BF16)
    v = va_ref[...]
    v32_ref[...] = v
    v16_ref[...] = v.astype(BF16)
    for p in range(IDX_HEADS // 2):
        y = _rope_tile(qi_ref[:, p * LANE:(p + 1) * LANE], c, su, sd)
        y_sw = pltpu.roll(y, HALF, 1)
        qipad_ref[:, (2 * p) * LANE:(2 * p + 1) * LANE] = jnp.where(low, y, 0.0).astype(BF16)
        qipad_ref[:, (2 * p + 1) * LANE:(2 * p + 2) * LANE] = jnp.where(low, y_sw, 0.0).astype(BF16)
    ki = _rope_tile(misc_ref[...], c, su, sd)
    ki32_ref[...] = ki[:, :IDX_DIM]
    ki16_ref[...] = jnp.where(low, ki, 0.0).astype(BF16)


def dsa_prep(proj, pos, q_gain, k_gain, t):
    n = proj.shape[0]
    tm = _row_tile(n, 512)
    tabs = _rope_tables(pos)
    if t < tm:
        tabs = tuple(jnp.tile(a, (tm // t, 1)) for a in tabs)
    nt = tabs[0].shape[0] // tm
    tab_spec = pl.BlockSpec((tm, LANE), lambda i: (i % nt, 0))
    gain_spec = pl.BlockSpec((1, LANE), lambda i: (0, 0))

    def col(width, offset):
        return pl.BlockSpec((tm, width), lambda i: (i, offset // width))

    def out(width, dtype):
        return (jax.ShapeDtypeStruct((n, width), dtype), pl.BlockSpec((tm, width), lambda i: (i, 0)))

    outs = [out(A_HEADS * LANE, BF16), out(IDX_HEADS * LANE, BF16), out(LANE, BF16), out(LANE, BF16),
            out(LANE, BF16), out(LANE, F32), out(LANE, F32), out(IDX_DIM, F32)]
    return pl.pallas_call(
        _aprep_kernel,
        grid=(n // tm,),
        in_specs=[col(A_HEADS * HEAD_DIM, AB_QA), col(LANE, AB_KA), col(LANE, AB_VA),
                  col(IDX_HEADS * IDX_DIM, AB_QI), col(LANE, AB_MISC), tab_spec, tab_spec, tab_spec,
                  gain_spec, gain_spec],
        out_specs=[o[1] for o in outs],
        out_shape=[o[0] for o in outs],
        compiler_params=pltpu.CompilerParams(dimension_semantics=("parallel",), vmem_limit_bytes=VMEM_LIMIT),
        name="dsa_prep",
    )(proj, proj, proj, proj, proj, *tabs, jnp.tile(q_gain, 2).reshape(1, LANE), jnp.tile(k_gain, 2).reshape(1, LANE))


N_PARTIAL = 4


def _add_tiles(accs, m, sub):
    accs = list(accs)
    for t in range(m.shape[0] // sub):
        accs[t % len(accs)] = accs[t % len(accs)] + m[t * sub:(t + 1) * sub]
    return tuple(accs)


def _dsa_kernel(q_ref, qi_ref, misc_ref, lim_ref, k_ref, v_ref, ki_ref, o_ref, key_ref, bias_ref, hi_ref, lo_ref,
                *, nch, n_sel):
    qb = q_ref.shape[0]
    kc = KEY_CHUNK
    n_idx = IDX_HEADS
    hpg = A_HEADS // A_KV_HEADS
    nt = (((1,), (1,)), ((), ()))

    limit = lim_ref[0]
    misc_t = misc_ref[...].T
    wscale = IDX_HEADS ** -0.5 * IDX_DIM ** -0.5
    w = [misc_t[MISC_WI + j:MISC_WI + j + 1, :] * wscale for j in range(n_idx)]
    qis = [qi_ref[:, j * LANE:(j + 1) * LANE] for j in range(n_idx)]

    def score_body(c, carry):
        off = pl.multiple_of(c * kc, kc)
        kic = ki_ref[pl.ds(off, kc), :]
        lgs = [lax.dot_general(kic, qis[j], nt, preferred_element_type=F32) for j in range(n_idx)]
        s = w[0] * jnp.maximum(lgs[0], 0.0)
        for j in range(1, n_idx):
            s = s + w[j] * jnp.maximum(lgs[j], 0.0)
        kidx = off + lax.broadcasted_iota(jnp.int32, (kc, qb), 0)
        s = jnp.where(kidx < limit, s, -jnp.inf)
        bits = lax.bitcast_convert_type(s, jnp.int32)
        key = jnp.where(bits < 0, bits ^ 0x7FFFFFFF, bits)
        key_ref[pl.ds(off, kc), :] = key
        hi_ref[pl.ds(off, kc), :] = jnp.right_shift(key, 16).astype(jnp.int16)
        return carry

    lax.fori_loop(0, nch, score_body, 0)

    def count_ge(cand):
        sub = 8

        def body(c, accs):
            off = pl.multiple_of(c * kc, kc)
            m = jnp.where(key_ref[pl.ds(off, kc), :] >= cand, 1.0, 0.0)
            return _add_tiles(accs, m, sub)

        accs = lax.fori_loop(0, nch, body, (jnp.zeros((sub, qb), F32),) * N_PARTIAL, unroll=True)
        return jnp.sum(sum(accs), axis=0, keepdims=True)

    def count_ge16(ref, cand32):
        cand = cand32.astype(jnp.int16)
        sub = 16

        def body(c, accs):
            off = pl.multiple_of(c * kc, kc)
            m = jnp.where(ref[pl.ds(off, kc), :] >= cand, jnp.int16(1), jnp.int16(0))
            return _add_tiles(accs, m, sub)

        accs = lax.fori_loop(0, nch, body, (jnp.zeros((sub, qb), jnp.int16),) * N_PARTIAL, unroll=True)
        return jnp.sum(sum(accs).astype(F32), axis=0, keepdims=True)

    def kth_largest16(ref, want):
        tau = jnp.where(count_ge16(ref, jnp.zeros((1, qb), jnp.int32)) >= want, 0, I16_MIN).astype(jnp.int32)

        def bisect(i, tau):
            cand = tau | jnp.left_shift(jnp.int32(1), 14 - i)
            return jnp.where(count_ge16(ref, cand) >= want, cand, tau)

        return lax.fori_loop(0, 15, bisect, tau)

    want = float(n_sel)
    tau_hi = kth_largest16(hi_ref, want)
    above = jnp.where(tau_hi < I16_MAX, count_ge16(hi_ref, jnp.minimum(tau_hi + 1, I16_MAX)), 0.0)

    def low_body(c, carry):
        off = pl.multiple_of(c * kc, kc)
        key = key_ref[pl.ds(off, kc), :]
        low = (key & 0xFFFF) + I16_MIN
        lo_ref[pl.ds(off, kc), :] = jnp.where(jnp.right_shift(key, 16) == tau_hi, low, I16_MIN).astype(jnp.int16)
        return carry

    lax.fori_loop(0, nch, low_body, 0)
    tau_lo = kth_largest16(lo_ref, want - above)
    tau = jnp.left_shift(tau_hi, 16) + (tau_lo - I16_MIN)

    room = want - count_ge(tau + 1)
    r_i = lax.broadcasted_iota(jnp.int32, (LANE, LANE), 0)
    c_i = lax.broadcasted_iota(jnp.int32, (LANE, LANE), 1)
    prefix_ones = jnp.where(r_i >= c_i, 1.0, 0.0).astype(BF16)
    identity = jnp.where(r_i == c_i, 1.0, 0.0).astype(BF16)

    def bias_body(c, seen):
        off = pl.multiple_of(c * kc, kc)
        tiles = range(kc // LANE)
        xs = [key_ref[pl.ds(off + t * LANE, LANE), :] for t in tiles]
        eqs = [x == tau for x in xs]
        eqfs = [jnp.where(eq, 1.0, 0.0) for eq in eqs]
        ranks = [jnp.dot(prefix_ones, eqf.astype(BF16), preferred_element_type=F32) for eqf in eqfs]
        sels = []
        for t in tiles:
            sel = ((xs[t] > tau) | (eqs[t] & (ranks[t] + seen <= room))) & (xs[t] != KEY_OF_NEG_INF)
            sels.append(sel)
            seen = seen + jnp.sum(eqfs[t], axis=0, keepdims=True)
        if qb == LANE:
            for t in tiles:
                bias_ref[:, pl.ds(off + t * LANE, LANE)] = jnp.where(sels[t], 0.0, MASKED).T
        else:
            sel_ts = [lax.dot_general(jnp.where(sel, 1.0, 0.0).astype(BF16), identity, (((0,), (0,)), ((), ())),
                                      preferred_element_type=F32) for sel in sels]
            for t in tiles:
                bias_ref[:, pl.ds(off + t * LANE, LANE)] = jnp.where(sel_ts[t] > 0.5, 0.0, MASKED)
        return seen

    lax.fori_loop(0, nch, bias_body, jnp.zeros((1, qb), F32))

    lane = lax.broadcasted_iota(jnp.int32, (qb, LANE), 1)
    qgs = [jnp.concatenate([q_ref[:, (hpg * g + h) * LANE:(hpg * g + h + 1) * LANE] for h in range(hpg)], axis=0)
           for g in range(A_KV_HEADS)]

    def att_body(c, carry):
        off = pl.multiple_of(c * kc, kc)
        kch = k_ref[pl.ds(off, kc), :]
        vch = v_ref[pl.ds(off, kc), :]
        bias = bias_ref[:, pl.ds(off, kc)][None]
        new = [None] * A_KV_HEADS

        def group(g):
            m, l, acc = carry[g]
            s = lax.dot_general(qgs[g], kch, nt, preferred_element_type=F32)
            yield
            s = (s.reshape(hpg, qb, kc) + bias).reshape(hpg * qb, kc)
            m_new = jnp.maximum(m, jnp.max(s, axis=1, keepdims=True))
            alpha = jnp.exp(m - m_new)
            p = jnp.exp(s - m_new)
            l = alpha * l + jnp.sum(p, axis=1, keepdims=True)
            p16 = p.astype(BF16)
            yield
            new[g] = (m_new, l, alpha * acc + jnp.dot(p16, vch, preferred_element_type=F32))

        _lockstep(group(g) for g in range(A_KV_HEADS))
        return tuple(new)

    init = tuple((jnp.full((hpg * qb, 1), MASKED, F32), jnp.zeros((hpg * qb, 1), F32),
                  jnp.zeros((hpg * qb, LANE), F32)) for _ in range(A_KV_HEADS))
    res = lax.fori_loop(0, nch, att_body, init)
    outs = []
    for g in range(A_KV_HEADS):
        _, l, acc = res[g]
        og = acc / l
        for h in range(hpg):
            oh = og[h * qb:(h + 1) * qb]
            outs.append(oh if (h % 2) == g else pltpu.roll(oh, HALF, 1))
    for p in range(A_HEADS // 2):
        o_ref[:, p * LANE:(p + 1) * LANE] = jnp.where(lane < HALF, outs[2 * p], outs[2 * p + 1])


def dsa_attention(qpad, qipad, proj, limit, k16, v16, ki16, *, bsz, tq, tk, causal, n_sel):
    qb = min(Q_BLOCK, tq)
    nqb = tq // qb
    assert tk % KEY_CHUNK == 0 and tk >= n_sel
    if causal:
        per_group = KEY_CHUNK // qb
        groups = [(g * per_group, per_group, g + 1) for g in range(nqb // per_group)]
    else:
        groups = [(0, nqb, tk // KEY_CHUNK)]
    lim3 = limit.reshape(bsz * nqb, 1, qb)
    out = None
    for first, count, nch in groups:
        def qspec(width, col=0, first=first):
            return pl.BlockSpec((qb, width), lambda b, i: (b * nqb + first + i, col))

        kspec = pl.BlockSpec((tk, LANE), lambda b, i: (b, 0))
        in_specs = [qspec(A_HEADS * LANE), qspec(IDX_HEADS * LANE), qspec(LANE, AB_MISC // LANE),
                    pl.BlockSpec((1, 1, qb), lambda b, i, first=first: (b * nqb + first + i, 0, 0)),
                    kspec, kspec, kspec]
        args = [qpad, qipad, proj, lim3, k16, v16, ki16]
        kern = functools.partial(_dsa_kernel, nch=nch, n_sel=n_sel)
        aliases = {}
        if out is not None:
            in_specs.append(pl.BlockSpec(memory_space=pl.ANY))
            args.append(out)
            aliases = {len(args) - 1: 0}
            kern = functools.partial(_dsa_kernel_with_carry, nch=nch, n_sel=n_sel)
        out = pl.pallas_call(
            kern,
            grid=(bsz, count),
            in_specs=in_specs,
            out_specs=qspec(A_HEADS * HEAD_DIM),
            out_shape=jax.ShapeDtypeStruct((bsz * tq, A_HEADS * HEAD_DIM), F32),
            scratch_shapes=[pltpu.VMEM((tk, qb), jnp.int32), pltpu.VMEM((qb, tk), F32),
                            pltpu.VMEM((tk, qb), jnp.int16), pltpu.VMEM((tk, qb), jnp.int16)],
            input_output_aliases=aliases,
            compiler_params=pltpu.CompilerParams(
                dimension_semantics=("parallel", "arbitrary"), vmem_limit_bytes=VMEM_LIMIT),
            name="dsa_attention",
        )(*args)
    return out


def _dsa_kernel_with_carry(q_ref, qi_ref, misc_ref, lim_ref, k_ref, v_ref, ki_ref, carry_ref, o_ref, *scratch,
                           nch, n_sel):
    del carry_ref
    _dsa_kernel(q_ref, qi_ref, misc_ref, lim_ref, k_ref, v_ref, ki_ref, o_ref, *scratch, nch=nch, n_sel=n_sel)


SUB = 16


def _dot(a, b):
    return jnp.dot(a, b, preferred_element_type=F32)


def _dot_nt(a, b):
    return lax.dot_general(a, b, (((1,), (1,)), ((), ())), preferred_element_type=F32)


def _dot_tn(a, b):
    return lax.dot_general(a, b, (((0,), (0,)), ((), ())), preferred_element_type=F32)


def _split2(x):
    hi = x.astype(BF16)
    return hi, (x - hi.astype(F32)).astype(BF16)


def _split3(x):
    hi = x.astype(BF16)
    r = x - hi.astype(F32)
    mid = r.astype(BF16)
    return hi, mid, (r - mid.astype(F32)).astype(BF16)


def _cumsum_rows(x, tril16):
    hi, mid, lo = _split3(x)
    return _dot(tril16, hi) + _dot(tril16, mid) + _dot(tril16, lo)


def _dot_f32(a, b):
    ah, al = _split2(a)
    bh, bl = _split2(b)
    return _dot(ah, bh) + (_dot(ah, bl) + _dot(al, bh))


def _tri_mask(n, strict=False):
    r = lax.broadcasted_iota(jnp.int32, (n, n), 0)
    c = lax.broadcasted_iota(jnp.int32, (n, n), 1)
    return r > c if strict else r >= c


MLSTM_SEQS_PER_STEP = 2
CD_SEQS_PER_STEP = 4


def _lockstep(stages):
    stages = list(stages)
    while stages:
        for g in list(stages):
            if next(g, StopIteration) is StopIteration:
                stages.remove(g)


def _chunk_call(kern, *, bsz, nb, nc, rows, ins, outs, scratch, name):
    def spec(a, kind, width, offset):
        if kind == 'rows':
            return pl.BlockSpec((nb, rows, width), lambda b, c: (b, c, offset // width))
        if kind == 'batch':
            return pl.BlockSpec((nb,) + tuple(a.shape[1:]), lambda b, c: (b,) + (0,) * (len(a.shape) - 1))
        return pl.BlockSpec(tuple(a.shape), lambda b, c: (0,) * len(a.shape))

    return pl.pallas_call(
        kern,
        grid=(bsz // nb, nc),
        in_specs=[spec(*i) for i in ins],
        out_specs=[spec(*o) for o in outs],
        out_shape=[o[0] for o in outs],
        scratch_shapes=scratch,
        compiler_params=pltpu.CompilerParams(
            dimension_semantics=("parallel", "arbitrary"), vmem_limit_bytes=VMEM_LIMIT),
        name=name,
    )(*[i[0] for i in ins])


def _mlstm_kernel(q_ref, k_ref, v_ref, og_ref, misc_ref, gb_ref, gain_ref, c0_ref, n0_ref, m0_ref,
                  h_ref, c_out_ref, n_out_ref, m_out_ref, c_scr, n_scr, m_scr):
    ci = pl.program_id(1)
    seqs = range(q_ref.shape[0])

    @pl.when(ci == 0)
    def _():
        c_scr[...] = c0_ref[...]
        n_scr[...] = n0_ref[...]
        m_scr[...] = m0_ref[...]

    _lockstep(_mlstm_chunk(q_ref.at[i], k_ref.at[i], v_ref.at[i], og_ref.at[i], misc_ref.at[i], gb_ref, gain_ref,
                           h_ref.at[i], c_scr.at[i], n_scr.at[i], m_scr.at[i]) for i in seqs)

    @pl.when(ci == pl.num_programs(1) - 1)
    def _():
        c_out_ref[...] = c_scr[...]
        n_out_ref[...] = n_scr[...]
        m_out_ref[...] = m_scr[...]


def _mlstm_chunk(q_ref, k_ref, v_ref, og_ref, misc_ref, gb_ref, gain_ref, h_ref, c_scr, n_scr, m_scr):
    rows = q_ref.shape[0]
    hr = B_HEADS * rows
    wq = B_HEADS * B_QK_DIM
    tril16 = jnp.where(_tri_mask(rows), 1.0, 0.0).astype(BF16)
    gates = misc_ref[...] + gb_ref[...]
    bcum = _cumsum_rows(jax.nn.log_sigmoid(gates), tril16)
    yield
    m_all = m_scr[...]

    def stack(f):
        return jnp.concatenate([f(h) for h in range(B_HEADS)], axis=0)

    lane_q = lax.broadcasted_iota(jnp.int32, (rows, wq), 1)
    qx, kx = q_ref[...], k_ref[...]
    q_all = stack(lambda h: jnp.where(lane_q // B_QK_DIM == h, qx, 0.0))
    k_all = stack(lambda h: jnp.where(lane_q // B_QK_DIM == h, kx, 0.0)) * B_QK_DIM ** -0.5
    v_all = stack(lambda h: v_ref[:, h * B_V_DIM:(h + 1) * B_V_DIM])
    b_col = stack(lambda h: bcum[:, MISC_FB + h:MISC_FB + h + 1])
    i_col = stack(lambda h: gates[:, MISC_IB + h:MISC_IB + h + 1])
    m_col = stack(lambda h: jnp.broadcast_to(m_all[:, h:h + 1], (rows, 1)))
    b_end = stack(lambda h: jnp.broadcast_to(bcum[rows - 1:rows, MISC_FB + h:MISC_FB + h + 1], (rows, 1)))
    bi_row = jnp.broadcast_to(b_col - i_col, (hr, LANE)).T[0:1, :]

    r = lax.broadcasted_iota(jnp.int32, (hr, hr), 0)
    c = lax.broadcasted_iota(jnp.int32, (hr, hr), 1)
    incl = ((r // rows) == (c // rows)) & (r >= c)
    dmat = jnp.where(incl, b_col - bi_row, -jnp.inf)
    inter = b_col + m_col
    mrow = jnp.maximum(inter, jnp.max(dmat, axis=1, keepdims=True))
    w_state = jnp.exp(inter - mrow)
    q16 = q_all.astype(BF16)
    v16 = v_all.astype(BF16)
    scores = _dot_nt(q16, k_all.astype(BF16)) * jnp.exp(dmat - mrow)
    yield
    cs = c_scr[...]
    n_row = n_scr[...]
    num = _dot(scores.astype(BF16), v16) + w_state * _dot(q16, cs.astype(BF16))
    den = jnp.sum(scores, axis=1, keepdims=True) + w_state * jnp.sum(q_all * n_row, axis=1, keepdims=True)
    hh = num / jnp.maximum(jnp.abs(den), jnp.exp(-mrow))
    yield
    gain = gain_ref[...]
    for h in range(B_HEADS):
        h_ref[:, h * B_V_DIM:(h + 1) * B_V_DIM] = (_rms_rows(hh[h * rows:(h + 1) * rows], gain)
                                                   * jax.nn.sigmoid(og_ref[:, h * B_V_DIM:(h + 1) * B_V_DIM]))

    g_col = b_end - b_col + i_col
    lane1 = lax.broadcasted_iota(jnp.int32, (1, LANE), 1)
    m_next = m_all
    m_new_rows, keep_rows, keep_lanes = [], [], []
    for h in range(B_HEADS):
        m_h = m_all[:, h:h + 1]
        be = bcum[rows - 1:rows, MISC_FB + h:MISC_FB + h + 1]
        m_new = jnp.maximum(be + m_h, jnp.max(g_col[h * rows:(h + 1) * rows], axis=0, keepdims=True))
        keep = jnp.exp(be + m_h - m_new)
        m_next = jnp.where(lane1 == h, m_new, m_next)
        m_new_rows.append(jnp.broadcast_to(m_new, (rows, 1)))
        keep_rows.append(jnp.broadcast_to(keep, (B_QK_DIM, 1)))
        keep_lanes.append(jnp.broadcast_to(keep, (1, B_QK_DIM)))
    kw = k_all * jnp.exp(g_col - jnp.concatenate(m_new_rows, axis=0))
    c_scr[...] = jnp.concatenate(keep_rows, axis=0) * cs + _dot_tn(kw.astype(BF16), v16)
    n_scr[...] = jnp.concatenate(keep_lanes, axis=1) * n_row + jnp.sum(kw, axis=0, keepdims=True)
    m_scr[...] = m_next
    yield


def mlstm_mixer(proj, gate_bias, gain, c0, n0, m0, *, bsz, t, chunk):
    nc = t // chunk
    nb = MLSTM_SEQS_PER_STEP
    gb = jnp.zeros((1, LANE), F32)
    gb = gb.at[0, MISC_IB:MISC_IB + B_HEADS].set(gate_bias[0]).at[0, MISC_FB:MISC_FB + B_HEADS].set(gate_bias[1])
    c0 = c0.reshape(bsz, B_HEADS * B_QK_DIM, B_V_DIM)
    n0 = n0.reshape(bsz, 1, B_HEADS * B_QK_DIM)
    m0 = jnp.pad(m0, ((0, 0), (0, LANE - B_HEADS))).reshape(bsz, 1, LANE)
    wq = B_HEADS * B_QK_DIM
    wv = B_HEADS * B_V_DIM
    h, c, n, m = _chunk_call(
        _mlstm_kernel, bsz=bsz, nb=nb, nc=nc, rows=chunk,
        ins=[(proj, 'rows', wq, AB_QB), (proj, 'rows', wq, AB_KB), (proj, 'rows', wv, AB_VB),
             (proj, 'rows', wv, AB_OB), (proj, 'rows', LANE, AB_MISC), (gb, 'const', 0, 0),
             (gain.reshape(1, B_V_DIM), 'const', 0, 0), (c0, 'batch', 0, 0), (n0, 'batch', 0, 0),
             (m0, 'batch', 0, 0)],
        outs=[(jax.ShapeDtypeStruct((bsz, t, wv), F32), 'rows', wv, 0),
              (jax.ShapeDtypeStruct(c0.shape, F32), 'batch', 0, 0),
              (jax.ShapeDtypeStruct(n0.shape, F32), 'batch', 0, 0),
              (jax.ShapeDtypeStruct(m0.shape, F32), 'batch', 0, 0)],
        scratch=[pltpu.VMEM((nb, wq, B_V_DIM), F32), pltpu.VMEM((nb, 1, wq), F32), pltpu.VMEM((nb, 1, LANE), F32)],
        name="mlstm_mixer")
    return (h.reshape(bsz * t, wv), c.reshape(bsz, B_HEADS, B_QK_DIM, B_V_DIM), n.reshape(bsz, B_HEADS, B_QK_DIM),
            m.reshape(bsz, LANE)[:, :B_HEADS])


CD_QKV, CD_ZC, CD_QD, CD_FD, CD_VD, CD_GD, CD_MISC, CD_TOTAL = 0, 1536, 2048, 2560, 3072, 3584, 4096, 4224
MISC_BC, MISC_AC = 0, 4
TAIL = 8


def _permute_w_in_cd(w):
    qkv, bc, ac, zc, qd, fd, vd, gd = _split_cols(w, ODD_SPLITS)
    pad = jnp.zeros((w.shape[0], LANE - 2 * C_HEADS), w.dtype)
    return jnp.concatenate([qkv, zc, qd, fd, vd, gd, bc, ac, pad], axis=1)


def _gdn_init(s0_ref, tail0_ref, st_scr, tail_scr):
    for h in range(C_HEADS):
        st_scr[:, h * C_DIM:(h + 1) * C_DIM] = s0_ref[h].T
    tail_scr[...] = tail0_ref[...]


def _gdn_final(s_out_ref, st_scr):
    for h in range(C_HEADS):
        s_out_ref[h] = st_scr[:, h * C_DIM:(h + 1) * C_DIM].T


def _gdn_chunk(qkv_ref, z_ref, misc_ref, cw_ref, alog_ref, dt_ref, gain_ref, o_ref, st_scr, tail_scr):
    rows = qkv_ref.shape[0]
    width = qkv_ref.shape[1]
    x = qkv_ref[...]
    tail = tail_scr[...]
    row8 = lax.broadcasted_iota(jnp.int32, (TAIL, width), 0)
    acc = x * cw_ref[CONV_W - 1:CONV_W, :]
    for back in range(1, CONV_W):
        rolled = pltpu.roll(x, back, 0)
        first = jnp.where(row8 < back, pltpu.roll(tail, back, 0), rolled[0:TAIL])
        shifted = first if rows == TAIL else jnp.concatenate([first, rolled[TAIL:]], axis=0)
        acc = acc + shifted * cw_ref[CONV_W - 1 - back:CONV_W - back, :]
    tail_scr[...] = x[rows - TAIL:rows]
    conv = acc * jax.nn.sigmoid(acc)

    tril16 = jnp.where(_tri_mask(rows), 1.0, 0.0).astype(BF16)
    misc = misc_ref[...]
    beta_t = jax.nn.sigmoid(misc)
    g_t = -jnp.exp(alog_ref[...]) * jax.nn.softplus(misc + dt_ref[...])
    gcum = _cumsum_rows(g_t, tril16)
    yield

    hd = C_HEADS * C_DIM
    hr = C_HEADS * rows

    def stack(f):
        return jnp.concatenate([f(h) for h in range(C_HEADS)], axis=0)

    def l2n(v):
        return v * lax.rsqrt(jnp.sum(v * v, axis=-1, keepdims=True) + EPS)

    q_all = stack(lambda h: l2n(conv[:, h * C_DIM:(h + 1) * C_DIM])) * C_DIM ** -0.5
    k_all = stack(lambda h: l2n(conv[:, hd + h * C_DIM:hd + (h + 1) * C_DIM]))
    v_all = stack(lambda h: conv[:, 2 * hd + h * C_DIM:2 * hd + (h + 1) * C_DIM])
    beta = stack(lambda h: beta_t[:, MISC_BC + h:MISC_BC + h + 1])
    gc = stack(lambda h: gcum[:, MISC_AC + h:MISC_AC + h + 1])
    g_end = stack(lambda h: jnp.broadcast_to(gcum[rows - 1:rows, MISC_AC + h:MISC_AC + h + 1], (rows, 1)))
    gc_row = jnp.broadcast_to(gc, (hr, LANE)).T[0:1, :]

    r = lax.broadcasted_iota(jnp.int32, (hr, hr), 0)
    c = lax.broadcasted_iota(jnp.int32, (hr, hr), 1)
    same = (r // rows) == (c // rows)
    incl = same & (r >= c)
    strict = same & (r > c)
    decay = jnp.exp(jnp.where(incl, gc - gc_row, -jnp.inf))
    k16 = k_all.astype(BF16)
    a_mat = jnp.where(strict, beta * _dot_nt(k16, k16) * decay, 0.0)
    yield
    power = -a_mat
    inv = jnp.where(r == c, 1.0, 0.0) + power
    for _ in range(int(math.log2(rows)) - 1):
        power = _dot_f32(power, power)
        yield
        inv = inv + _dot_f32(inv, power)
        yield
    inv_hi, inv_lo = _split2(inv)
    rhs = jnp.concatenate([beta * v_all, beta * jnp.exp(gc) * k_all], axis=1).astype(BF16)
    w = _dot(inv_hi, rhs) + _dot(inv_lo, rhs)
    yield
    w_v, w_k = w[:, 0:C_DIM], w[:, C_DIM:2 * C_DIM]
    qk = _dot_nt(q_all.astype(BF16), k16) * decay
    yield

    head_of_row = lax.broadcasted_iota(jnp.int32, (hr, C_DIM), 0) // rows

    def per_head_lanes(m):
        return jnp.concatenate([jnp.where(head_of_row == h, m, 0.0) for h in range(C_HEADS)], axis=1).astype(BF16)

    st = st_scr[...]
    st16 = st.astype(BF16)
    delta = w_v - _dot_nt(per_head_lanes(w_k), st16)
    yield
    d16 = delta.astype(BF16)
    out = _dot_nt(per_head_lanes(q_all * jnp.exp(gc)), st16) + _dot(qk.astype(BF16), d16)
    keep = jnp.concatenate([jnp.broadcast_to(jnp.exp(gcum[rows - 1:rows, MISC_AC + h:MISC_AC + h + 1]), (1, C_DIM))
                            for h in range(C_HEADS)], axis=1)
    st_scr[...] = keep * st + _dot_tn(d16, per_head_lanes(k_all * jnp.exp(g_end - gc)))
    yield
    gain = gain_ref[...]
    for h in range(C_HEADS):
        z = z_ref[:, h * C_DIM:(h + 1) * C_DIM]
        o_ref[:, h * C_DIM:(h + 1) * C_DIM] = _rms_rows(out[h * rows:(h + 1) * rows], gain) * (z * jax.nn.sigmoid(z))


def _hgrn2_init(s0_ref, st_scr):
    for h in range(D_HEADS):
        st_scr[h] = s0_ref[h].T


def _hgrn2_final(s_out_ref, st_scr):
    for h in range(D_HEADS):
        s_out_ref[h] = st_scr[h].T


def _hgrn2_chunk(q_ref, f_ref, v_ref, g_ref, lb_ref, gain_ref, o_ref, st_scr):
    rows = q_ref.shape[0]
    tril16 = jnp.where(_tri_mask(rows), 1.0, 0.0).astype(BF16)
    lb = lb_ref[...]
    zf = f_ref[...]
    logf = jnp.logaddexp(jnp.log(lb), jnp.log1p(-lb) + jax.nn.log_sigmoid(zf))
    kd = (1.0 - lb) * jax.nn.sigmoid(-zf)
    qx = q_ref[...]
    qd = qx * jax.nn.sigmoid(qx)
    bcum = _cumsum_rows(logf, tril16)
    yield
    gain = gain_ref[...]
    row_sub = lax.broadcasted_iota(jnp.int32, (SUB, 1), 0)
    for h in range(D_HEADS):
        sl = slice(h * D_EXPAND, (h + 1) * D_EXPAND)
        q, k, b = qd[:, sl], kd[:, sl], bcum[:, sl]
        v = v_ref[:, h * D_V_DIM:(h + 1) * D_V_DIM]
        v16 = v.astype(BF16)
        st = st_scr[h]
        inter = _dot_nt((q * jnp.exp(b)).astype(BF16), st.astype(BF16))
        blocks = []
        for i in range(rows // SUB):
            r0 = i * SUB
            qi, bi = q[r0:r0 + SUB], b[r0:r0 + SUB]
            oi = inter[r0:r0 + SUB]
            if i > 0:
                ref = b[r0 - 1:r0]
                att = _dot_nt((qi * jnp.exp(bi - ref)).astype(BF16),
                              (k[0:r0] * jnp.exp(ref - b[0:r0])).astype(BF16))
                oi = oi + _dot(att.astype(BF16), v16[0:r0])
            for s in range(SUB):
                r = r0 + s
                a = jnp.sum(qi * jnp.exp(bi - b[r:r + 1]) * k[r:r + 1], axis=1, keepdims=True)
                oi = oi + jnp.where(row_sub >= s, a, 0.0) * v[r:r + 1]
            blocks.append(oi)
        out = blocks[0] if len(blocks) == 1 else jnp.concatenate(blocks, axis=0)
        b_end = b[rows - 1:rows]
        st_scr[h] = jnp.exp(b_end) * st + _dot_tn(v16, (k * jnp.exp(b_end - b)).astype(BF16))
        g = g_ref[:, h * D_V_DIM:(h + 1) * D_V_DIM]
        o_ref[:, h * D_V_DIM:(h + 1) * D_V_DIM] = _rms_rows(out, gain) * (g * jax.nn.sigmoid(g))
        yield


def _cd_kernel(qkv_ref, z_ref, misc_ref, cw_ref, alog_ref, dt_ref, cgain_ref, sc0_ref, tail0_ref,
               qd_ref, fd_ref, vd_ref, gd_ref, lb_ref, dgain_ref, sd0_ref,
               oc_ref, sc_out_ref, od_ref, sd_out_ref, stc_scr, tail_scr, std_scr):
    ci = pl.program_id(1)
    seqs = range(qkv_ref.shape[0])

    @pl.when(ci == 0)
    def _():
        for i in seqs:
            _gdn_init(sc0_ref.at[i], tail0_ref.at[i], stc_scr.at[i], tail_scr.at[i])
            _hgrn2_init(sd0_ref.at[i], std_scr.at[i])

    gdn = [_gdn_chunk(qkv_ref.at[i], z_ref.at[i], misc_ref.at[i], cw_ref, alog_ref, dt_ref, cgain_ref,
                      oc_ref.at[i], stc_scr.at[i], tail_scr.at[i]) for i in seqs]
    hgrn2 = [_hgrn2_chunk(qd_ref.at[i], fd_ref.at[i], vd_ref.at[i], gd_ref.at[i], lb_ref, dgain_ref, od_ref.at[i],
                          std_scr.at[i]) for i in seqs]
    _lockstep(gdn + hgrn2)

    @pl.when(ci == pl.num_programs(1) - 1)
    def _():
        for i in seqs:
            _gdn_final(sc_out_ref.at[i], stc_scr.at[i])
            _hgrn2_final(sd_out_ref.at[i], std_scr.at[i])


def cd_mixers(proj, conv_w, a_log, dt_bias, c_gain, sc0, conv_prev, lower_bound, d_gain, sd0, *, bsz, t, chunk):
    nc = t // chunk
    nb = CD_SEQS_PER_STEP
    hd = C_HEADS * C_DIM
    wk = D_HEADS * D_EXPAND
    wv = D_HEADS * D_V_DIM
    lanes = jnp.zeros((1, LANE), F32)
    alog = lanes.at[0, MISC_AC:MISC_AC + C_HEADS].set(a_log)
    dt = lanes.at[0, MISC_AC:MISC_AC + C_HEADS].set(dt_bias)
    tail0 = jnp.pad(conv_prev, ((0, 0), (TAIL - (CONV_W - 1), 0), (0, 0)))
    oc, sc, od, sd = _chunk_call(
        _cd_kernel, bsz=bsz, nb=nb, nc=nc, rows=chunk,
        ins=[(proj, 'rows', 3 * hd, CD_QKV), (proj, 'rows', hd, CD_ZC), (proj, 'rows', LANE, CD_MISC),
             (conv_w, 'const', 0, 0), (alog, 'const', 0, 0), (dt, 'const', 0, 0),
             (c_gain.reshape(1, C_DIM), 'const', 0, 0), (sc0, 'batch', 0, 0), (tail0, 'batch', 0, 0),
             (proj, 'rows', wk, CD_QD), (proj, 'rows', wk, CD_FD), (proj, 'rows', wv, CD_VD),
             (proj, 'rows', wv, CD_GD), (lower_bound.reshape(1, wk), 'const', 0, 0),
             (d_gain.reshape(1, D_V_DIM), 'const', 0, 0), (sd0, 'batch', 0, 0)],
        outs=[(jax.ShapeDtypeStruct((bsz, t, hd), F32), 'rows', hd, 0),
              (jax.ShapeDtypeStruct(sc0.shape, F32), 'batch', 0, 0),
              (jax.ShapeDtypeStruct((bsz, t, wv), F32), 'rows', wv, 0),
              (jax.ShapeDtypeStruct(sd0.shape, F32), 'batch', 0, 0)],
        scratch=[pltpu.VMEM((nb, C_DIM, hd), F32), pltpu.VMEM((nb, TAIL, 3 * hd), F32),
                 pltpu.VMEM((nb, D_HEADS, D_V_DIM, D_EXPAND), F32)],
        name="cd_mixers")
    return oc.reshape(bsz * t, hd), sc, od.reshape(bsz * t, wv), sd


def _pad_cols(w, mult=LANE):
    pad = (-w.shape[-1]) % mult
    return jnp.pad(w, [(0, 0)] * (w.ndim - 1) + [(0, pad)])


def _mixer_ab(proj, bsz, t, pos, prm, cache):
    n = bsz * t
    qpad, qipad, k16, v16, ki16, k32, v32, ki32 = dsa_prep(proj, pos, prm['a_q_gain'][0], prm['a_k_gain'][0], t)
    if cache is None:
        limit = jnp.tile((pos // CHUNK + 1) * CHUNK, bsz).reshape(n, 1)
        a_out = dsa_attention(qpad, qipad, proj, limit, k16, v16, ki16, bsz=bsz, tq=t, tk=t,
                              causal=True, n_sel=min(TOPK_MAX, t // 4))
        c0 = jnp.zeros((bsz, B_HEADS, B_QK_DIM, B_V_DIM), F32)
        n0 = jnp.zeros((bsz, B_HEADS, B_QK_DIM), F32)
        m0 = jnp.zeros((bsz, B_HEADS), F32)
        chunk = CHUNK
    else:
        k_c, v_c, ki_c, c0, n0, m0 = cache
        past = k_c.shape[1]
        n_keys = past + t
        tk = -(-n_keys // KEY_CHUNK) * KEY_CHUNK

        def with_cache(c, new):
            c = c.reshape(bsz, past, -1).astype(BF16)
            c = jnp.pad(c, ((0, 0), (0, 0), (0, LANE - c.shape[-1])))
            return jnp.concatenate([c, new.reshape(bsz, t, LANE),
                                    jnp.zeros((bsz, tk - n_keys, LANE), BF16)], axis=1).reshape(bsz * tk, LANE)

        limit = jnp.full((n, 1), n_keys, jnp.int32)
        a_out = dsa_attention(qpad, qipad, proj, limit, with_cache(k_c, k16), with_cache(v_c, v16),
                              with_cache(ki_c, ki16), bsz=bsz, tq=t, tk=tk, causal=False,
                              n_sel=min(TOPK_MAX, n_keys // 4))
        chunk = t
    h, c, n_, m = mlstm_mixer(proj.reshape(bsz, t, -1), prm['b_gate_bias'][0], prm['b_norm_gain'][0], c0, n0, m0,
                              bsz=bsz, t=t, chunk=chunk)
    st = (k32.reshape(bsz, t, A_KV_HEADS, HEAD_DIM), v32.reshape(bsz, t, A_KV_HEADS, HEAD_DIM),
          ki32.reshape(bsz, t, IDX_DIM), c, n_, m)
    return a_out, h, st


def _mixer_cd(proj, bsz, t, prm, lower_bound, cache):
    hd = C_HEADS * C_DIM
    if cache is None:
        sc0 = jnp.zeros((bsz, C_HEADS, C_DIM, C_DIM), F32)
        conv_prev = jnp.zeros((bsz, CONV_W - 1, 3 * hd), F32)
        sd0 = jnp.zeros((bsz, D_HEADS, D_EXPAND, D_V_DIM), F32)
        chunk = CHUNK
    else:
        sc0, conv_prev, sd0 = cache
        chunk = t
    oc, sc, od, sd = cd_mixers(proj.reshape(bsz, t, -1), prm['c_conv_w'][0], prm['c_a_log'][0], prm['c_dt_bias'][0],
                               prm['c_norm_gain'][0], sc0, conv_prev, lower_bound, prm['d_norm_gain'][0], sd0,
                               bsz=bsz, t=t, chunk=chunk)
    qkv = proj.reshape(bsz, t, -1)[:, :, CD_QKV:CD_QKV + 3 * hd]
    conv_new = jnp.concatenate([conv_prev, qkv[:, t - (CONV_W - 1):]], axis=1)[:, -(CONV_W - 1):]
    return oc, od, (sc, conv_new, sd)


def _trunk(x, pos_offset, cache, prm, wts):
    bsz, t, d = x.shape
    n = bsz * t
    pos = pos_offset + jnp.arange(t, dtype=jnp.int32)
    probs = jax.nn.softmax(prm['d_lb_logits'], axis=0)
    lower_bounds = jnp.cumsum(probs, axis=0) - probs[0]
    xf = x.reshape(n, d)

    lc = None if cache is None else tuple(c[0] for c in cache[:6])
    proj = norm_matmul(xf, prm['norm_mix'][0], wts['w_in_ab'])
    a_out, b_out, st_even = _mixer_ab(proj, bsz, t, pos, prm, lc)
    xf = matmul_residual(a_out, b_out, wts['w_out_ab'], xf)
    xf = ffn_residual(xf, prm['norm_ffn'][0], wts['ffn_w1'], wts['ffn_w3'], wts['ffn_w2'])

    lc = None if cache is None else tuple(c[0] for c in cache[6:])
    proj = norm_matmul(xf, prm['norm_mix'][1], wts['w_in_cd'])
    c_out, d_out, st_odd = _mixer_cd(proj, bsz, t, prm, lower_bounds[1], lc)
    xf = matmul_residual(c_out, d_out, wts['w_out_cd'], xf)
    xf = moe_residual(xf, prm['norm_ffn'][1], wts['moe_router'], wts['moe_w1'], wts['moe_w3'], wts['moe_w2'])

    new_state = tuple(s[None] for s in st_even + st_odd)
    return xf.reshape(bsz, t, d), new_state


def kernel(x_prompt, x_sample, cache_a_k, cache_a_v, cache_a_kidx, state_b_c, state_b_n, state_b_m,
           state_c_s, state_c_conv, state_d_s, norm_mix, norm_ffn, w_in_ab, w_out_ab, a_q_gain, a_k_gain,
           b_gate_bias, b_norm_gain, w_in_cd, w_out_cd, c_conv_w, c_a_log, c_dt_bias, c_norm_gain,
           d_lb_logits, d_norm_gain, ffn_w1, ffn_w3, ffn_w2, moe_router, moe_w1, moe_w3, moe_w2):
    prm = dict(norm_mix=norm_mix, norm_ffn=norm_ffn, a_q_gain=a_q_gain, a_k_gain=a_k_gain,
               b_gate_bias=b_gate_bias, b_norm_gain=b_norm_gain, c_conv_w=c_conv_w, c_a_log=c_a_log,
               c_dt_bias=c_dt_bias, c_norm_gain=c_norm_gain, d_lb_logits=d_lb_logits, d_norm_gain=d_norm_gain)
    wts = dict(w_in_ab=_permute_w_in_ab(w_in_ab[0]).astype(BF16), w_out_ab=w_out_ab[0].astype(BF16),
               w_in_cd=_permute_w_in_cd(w_in_cd[0]).astype(BF16), w_out_cd=w_out_cd[0].astype(BF16),
               ffn_w1=ffn_w1[0].astype(BF16), ffn_w3=ffn_w3[0].astype(BF16), ffn_w2=ffn_w2[0].astype(BF16),
               moe_router=_pad_cols(moe_router[0]),
               moe_w1=moe_w1[0].astype(BF16), moe_w3=moe_w3[0].astype(BF16), moe_w2=moe_w2[0].astype(BF16))
    cache = (cache_a_k, cache_a_v, cache_a_kidx, state_b_c, state_b_n, state_b_m, state_c_s, state_c_conv, state_d_s)
    y_prompt, st_p = _trunk(x_prompt, 0, None, prm, wts)
    y_sample, st_s = _trunk(x_sample, cache_a_k.shape[2], cache, prm, wts)
    return (y_prompt, y_sample) + st_p + st_s
```
